```python
import jax, jax.numpy as jnp
from jax import lax
import numpy as np

D_MODEL = 1024
BATCH = 16
SEQ = 2048
DEPTH = 1

CHUNK = 64
D_MIX = D_MODEL
CONV_WIDTH = D_MIX // 2
CONV_GROUPS = 8
CONV_K = 3
HGRN_WIDTH = D_MIX - CONV_WIDTH
HGRN_HEADS = 4
HGRN_DK = HGRN_WIDTH // HGRN_HEADS
HGRN_DV = HGRN_WIDTH // HGRN_HEADS
N_PROJ_SLOTS = 7
PROJ_WIDTH = 3 * CONV_WIDTH + 4 * HGRN_WIDTH
MEM_LEN = 256
XATTN_HEADS = 4
XATTN_HEAD_DIM = D_MODEL // XATTN_HEADS
N_GROUPS = 4
EXPERTS_PER_GROUP = 4
N_EXPERTS = N_GROUPS * EXPERTS_PER_GROUP
TOPK_IN_GROUP = 2
D_EXPERT = D_MODEL // 2
EPS = 1e-6

kernel_name = "hymba_conv_hgrn2_xattn_hiermoe_block"


def rms_norm(x, g):
    xf = x.astype(jnp.float32)
    y = xf * lax.rsqrt(jnp.mean(xf * xf, axis=-1, keepdims=True) + EPS)
    return (y * g.astype(jnp.float32)).astype(x.dtype)


def hgrn2_chunkwise(q, k, v, log_f):
    bsz, seq, heads, dk = q.shape
    dv = v.shape[-1]
    n_chunks = seq // CHUNK

    def to_chunks(t):
        return t.reshape(bsz, n_chunks, CHUNK, heads, t.shape[-1]).transpose(1, 0, 3, 2, 4)

    tri = jnp.tril(jnp.ones((CHUNK, CHUNK), dtype=bool))

    def step(state, inp):
        qc, kc, vc, ac = inp
        b = jnp.cumsum(ac, axis=2)
        diff = b[:, :, :, None, :] - b[:, :, None, :, :]
        decay = jnp.exp(jnp.where(tri[None, None, :, :, None], diff, -jnp.inf))
        scores = jnp.einsum('bhtsk,bhsk->bhts', qc[:, :, :, None, :] * decay, kc)
        o = (jnp.einsum('bhts,bhsv->bhtv', scores, vc)
             + jnp.einsum('bhtk,bhkv->bhtv', qc * jnp.exp(b), state))
        b_last = b[:, :, -1:, :]
        state = (jnp.exp(b_last[:, :, 0, :])[..., None] * state
                 + jnp.einsum('bhsk,bhsv->bhkv', kc * jnp.exp(b_last - b), vc))
        return state, o

    s0 = jnp.zeros((bsz, heads, dk, dv), jnp.float32)
    _, o = lax.scan(step, s0, (to_chunks(q), to_chunks(k), to_chunks(v), to_chunks(log_f)))
    return o.transpose(1, 0, 3, 2, 4).reshape(bsz, seq, heads, dv)


def parallel_mixer(h, w_in, conv_w, lb, hgrn_norm, w_out):
    bsz, seq, _ = h.shape
    proj = jnp.einsum('bsd,dp->bsp', h, w_in)
    cb, cc, cx, hq, hf, hi, hg = jnp.split(proj, N_PROJ_SLOTS, axis=-1)
    u = jnp.pad(cc * cx, ((0, 0), (CONV_K - 1, 0), (0, 0)))
    conv = sum(u[:, j:j + seq] * conv_w[j] for j in range(CONV_K))
    y_a = cb * conv
    z = hf.astype(jnp.float32).reshape(bsz, seq, HGRN_HEADS, HGRN_DK)
    lbh = lb.astype(jnp.float32).reshape(HGRN_HEADS, HGRN_DK)
    log_f = jnp.log(lbh + (1.0 - lbh) * jax.nn.sigmoid(z))
    k = (1.0 - lbh) * jax.nn.sigmoid(-z)
    q = hq.astype(jnp.float32).reshape(bsz, seq, HGRN_HEADS, HGRN_DK)
    v = hi.astype(jnp.float32).reshape(bsz, seq, HGRN_HEADS, HGRN_DV)
    o = hgrn2_chunkwise(q, k, v, log_f)
    o = o * lax.rsqrt(jnp.mean(o * o, axis=-1, keepdims=True) + EPS)
    o = o * hgrn_norm.astype(jnp.float32).reshape(HGRN_HEADS, HGRN_DV)
    g = hg.astype(jnp.float32).reshape(bsz, seq, HGRN_HEADS, HGRN_DV)
    y_b = (o * jax.nn.silu(g)).reshape(bsz, seq, HGRN_WIDTH).astype(h.dtype)
    y = jnp.concatenate([y_a, y_b], axis=-1)
    return jnp.einsum('bsm,md->bsd', y, w_out)


def memory_cross_attention(h, m, w_q, w_kv, w_o):
    bsz, seq, _ = h.shape
    q = jnp.einsum('bsd,de->bse', h, w_q).reshape(bsz, seq, XATTN_HEADS, XATTN_HEAD_DIM)
    kv = jnp.einsum('bmd,de->bme', m, w_kv)
    k, v = jnp.split(kv, 2, axis=-1)
    k = k.reshape(bsz, m.shape[1], XATTN_HEADS, XATTN_HEAD_DIM)
    v = v.reshape(bsz, m.shape[1], XATTN_HEADS, XATTN_HEAD_DIM)
    s = jnp.einsum('bshd,bmhd->bhsm', q, k).astype(jnp.float32) * (XATTN_HEAD_DIM ** -0.5)
    p = jax.nn.softmax(s, axis=-1).astype(v.dtype)
    o = jnp.einsum('bhsm,bmhd->bshd', p, v).reshape(bsz, seq, D_MODEL)
    return jnp.einsum('bse,ed->bsd', o, w_o)


def hierarchical_moe(h, w_group, b_group, w_expert, b_expert, w_gate, w_up, w_down):
    bsz, seq, d = h.shape
    t = h.reshape(-1, d)
    g_prob = jax.nn.softmax((t @ w_group + b_group).astype(jnp.float32), axis=-1)
    g_p, g_idx = lax.top_k(g_prob, 1)
    e_logits = (t @ w_expert + b_expert).astype(jnp.float32).reshape(-1, N_GROUPS, EXPERTS_PER_GROUP)
    e_logits = jnp.take_along_axis(e_logits, g_idx[:, :, None], axis=1)[:, 0]
    e_prob = jax.nn.softmax(e_logits, axis=-1)
    e_p, e_idx = lax.top_k(e_prob, TOPK_IN_GROUP)
    weights = g_p * e_p / jnp.sum(e_p, axis=-1, keepdims=True)
    expert_id = g_idx * EXPERTS_PER_GROUP + e_idx
    combine = jnp.sum(jax.nn.one_hot(expert_id, N_EXPERTS, dtype=jnp.float32) * weights[..., None], axis=1)
    out = jnp.zeros(t.shape, jnp.float32)
    for e in range(N_EXPERTS):
        hid = jax.nn.silu(t @ w_gate[e]) * (t @ w_up[e])
        out = out + combine[:, e:e + 1] * (hid @ w_down[e])
    return out.astype(h.dtype).reshape(bsz, seq, d)


def setup_inputs(seed: int = 0) -> dict:
    key = jax.random.key(seed)
    ks = jax.random.split(key, 24)
    f32 = jnp.float32

    def nrm(k, shape, scale):
        return jax.random.normal(k, shape, f32) * scale

    def gain(k, shape):
        return 1.0 + 0.02 * jax.random.normal(k, shape, f32)

    return {
        "x": jax.random.normal(ks[0], (BATCH, SEQ, D_MODEL), f32),
        "mem": jax.random.normal(ks[1], (BATCH, MEM_LEN, D_MODEL), f32),
        "mix_norm": gain(ks[2], (DEPTH, D_MODEL)),
        "w_in": nrm(ks[3], (DEPTH, D_MODEL, PROJ_WIDTH), D_MODEL ** -0.5),
        "conv_w": nrm(ks[4], (DEPTH, CONV_K, CONV_WIDTH), CONV_K ** -0.5),
        "hgrn_lb": jax.random.normal(ks[5], (DEPTH + 1, HGRN_WIDTH), f32),
        "hgrn_norm": gain(ks[6], (DEPTH, HGRN_WIDTH)),
        "w_out": nrm(ks[7], (DEPTH, D_MIX, D_MODEL), D_MIX ** -0.5),
        "xattn_norm": gain(ks[8], (DEPTH, D_MODEL)),
        "mem_norm": gain(ks[9], (DEPTH, D_MODEL)),
        "w_q": nrm(ks[10], (DEPTH, D_MODEL, D_MODEL), D_MODEL ** -0.5),
        "w_kv": nrm(ks[11], (DEPTH, D_MODEL, 2 * D_MODEL), D_MODEL ** -0.5),
        "w_o": nrm(ks[12], (DEPTH, D_MODEL, D_MODEL), D_MODEL ** -0.5),
        "ffn_norm": gain(ks[13], (DEPTH, D_MODEL)),
        "w_group": nrm(ks[14], (DEPTH, D_MODEL, N_GROUPS), D_MODEL ** -0.5),
        "b_group": nrm(ks[15], (DEPTH, N_GROUPS), 0.01),
        "w_expert": nrm(ks[16], (DEPTH, D_MODEL, N_EXPERTS), D_MODEL ** -0.5),
        "b_expert": nrm(ks[17], (DEPTH, N_EXPERTS), 0.01),
        "w_gate": nrm(ks[18], (DEPTH, N_EXPERTS, D_MODEL, D_EXPERT), D_MODEL ** -0.5),
        "w_up": nrm(ks[19], (DEPTH, N_EXPERTS, D_MODEL, D_EXPERT), D_MODEL ** -0.5),
        "w_down": nrm(ks[20], (DEPTH, N_EXPERTS, D_EXPERT, D_MODEL), D_EXPERT ** -0.5),
        "final_norm": gain(ks[21], (D_MODEL,)),
    }


def reference(x, mem, mix_norm, w_in, conv_w, hgrn_lb, hgrn_norm, w_out,
              xattn_norm, mem_norm, w_q, w_kv, w_o,
              ffn_norm, w_group, b_group, w_expert, b_expert, w_gate, w_up, w_down,
              final_norm):
    lb_all = jnp.cumsum(jax.nn.softmax(hgrn_lb.astype(jnp.float32), axis=0), axis=0)
    for l in range(DEPTH):
        x = x + parallel_mixer(rms_norm(x, mix_norm[l]), w_in[l], conv_w[l], lb_all[l],
                               hgrn_norm[l], w_out[l])
        x = x + memory_cross_attention(rms_norm(x, xattn_norm[l]), rms_norm(mem, mem_norm[l]),
                                       w_q[l], w_kv[l], w_o[l])
        x = x + hierarchical_moe(rms_norm(x, ffn_norm[l]), w_group[l], b_group[l], w_expert[l],
                                 b_expert[l], w_gate[l], w_up[l], w_down[l])
    return rms_norm(x, final_norm)
```

```python
import functools

import jax
import jax.numpy as jnp
from jax import lax
from jax.experimental import pallas as pl
from jax.experimental.pallas import tpu as pltpu

F32 = jnp.float32
BF16 = jnp.bfloat16

D_MODEL = 1024
CONV_WIDTH = 512
HGRN_WIDTH = 512
HGRN_HEADS = 4
HEAD_DIM = 128
N_PROJ_SLOTS = 7
PROJ_WIDTH = N_PROJ_SLOTS * 512
XATTN_HEADS = 4
XATTN_HEAD_DIM = 256
N_GROUPS = 4
EXPERTS_PER_GROUP = 4
N_EXPERTS = 16
D_EXPERT = 512
EPS = 1e-6

LANES = 128
CHUNK = 64
CHUNK_LEVELS = 6
SEQ_TILE = 256
ATTN_TILE = 256
MOE_TILE = 512
VMEM_LIMIT = 56 * 1024 * 1024


def _rms(x, g):
    return x * lax.rsqrt(jnp.mean(x * x, axis=-1, keepdims=True) + EPS) * g


def _dot(a, b):
    return jnp.dot(a, b, preferred_element_type=F32)


def _dot_nt(a, b):
    return lax.dot_general(a, b, (((1,), (1,)), ((), ())), preferred_element_type=F32)


def _dot_tn(a, b):
    return lax.dot_general(a, b, (((0,), (0,)), ((), ())), preferred_element_type=F32)


def _roll_rows(x, shift):
    return pltpu.roll(x, shift % x.shape[0], axis=0)


def _last_of_group(r, row, group):
    g = 1
    while g < group:
        r = jnp.where((row & g) != 0, r, _roll_rows(r, -g))
        g *= 2
    return r


def _first_half_total(r, row, half):
    n = r.shape[0]
    if half < 8:
        last = _last_of_group(r, row, half)
        return jnp.where((row & half) != 0, _roll_rows(last, half), last)
    pieces = []
    for start in range(0, n, 2 * half):
        mid = start + half - 1
        pieces.append(jnp.broadcast_to(r[mid:mid + 1, :], (2 * half, r.shape[1])))
    return pieces[0] if len(pieces) == 1 else jnp.concatenate(pieces, axis=0)


def _hgrn_chunk(q, z, v, lb, st):
    e = jnp.exp(-jnp.abs(z))
    inv = 1.0 / (1.0 + e)
    pos = z >= 0
    sig_p = jnp.where(pos, inv, e * inv)
    sig_n = jnp.where(pos, e * inv, inv)
    one_m_lb = 1.0 - lb
    logf = jnp.log(lb + one_m_lb * sig_p)
    k = one_m_lb * sig_n

    row = lax.broadcasted_iota(jnp.int32, (CHUNK, HEAD_DIM), 0)
    tt = lax.broadcasted_iota(jnp.int32, (CHUNK, CHUNK), 0)
    ss = lax.broadcasted_iota(jnp.int32, (CHUNK, CHUNK), 1)
    txs = tt ^ ss

    a = jnp.where(tt == ss, jnp.sum(q * k, axis=-1, keepdims=True), 0.0)
    r = logf
    for lvl in range(1, CHUNK_LEVELS + 1):
        half = 1 << (lvl - 1)
        second = (row & half) != 0
        tot = _first_half_total(r, row, half)
        decay = jnp.exp(jnp.where(second, r, tot - r))
        ql = (q * decay).astype(BF16)
        kl = (k * decay).astype(BF16)
        m = (ss < tt) & (txs >= half) & (txs < 2 * half)
        a = jnp.where(m, _dot_nt(ql, kl), a)
        r = jnp.where(second, r + tot, r)
    b = r
    b_last = b[CHUNK - 1:CHUNK, :]

    vb = v.astype(BF16)
    o = _dot(a.astype(BF16), vb)
    o = o + _dot_nt((q * jnp.exp(b)).astype(BF16), st.astype(BF16))
    kd = (k * jnp.exp(b_last - b)).astype(BF16)
    st_new = st * jnp.exp(b_last) + _dot_tn(vb, kd)
    return o, st_new


def _mixer_kernel(x_ref, gmix_ref, win_ref, convw_ref, lbraw_ref, hnorm_ref, wout_ref,
                  o_ref, p_scr, y_scr, st_scr, tail_scr):
    j = pl.program_id(1)

    @pl.when(j == 0)
    def _():
        st_scr[...] = jnp.zeros_like(st_scr)
        tail_scr[...] = jnp.zeros_like(tail_scr)

    x = x_ref[0]
    h = _rms(x, gmix_ref[...]).astype(BF16)
    p_scr[...] = _dot(h, win_ref[...])

    ts = x.shape[0]
    cb = p_scr[:, 0:CONV_WIDTH]
    u = p_scr[:, CONV_WIDTH:2 * CONV_WIDTH] * p_scr[:, 2 * CONV_WIDTH:3 * CONV_WIDTH]
    row = lax.broadcasted_iota(jnp.int32, u.shape, 0)
    prev1 = tail_scr[7:8, :]
    prev2 = tail_scr[6:7, :]
    u1 = jnp.where(row == 0, prev1, _roll_rows(u, 1))
    u2 = jnp.where(row == 0, prev2, jnp.where(row == 1, prev1, _roll_rows(u, 2)))
    cw = convw_ref[...]
    conv = u2 * cw[0:1, :] + u1 * cw[1:2, :] + u * cw[2:3, :]
    y_scr[:, 0:CONV_WIDTH] = (cb * conv).astype(BF16)
    tail_scr[...] = u[ts - 8:ts, :]

    raw = lbraw_ref[...]
    mx = jnp.max(raw, axis=0, keepdims=True)
    ex = jnp.exp(raw - mx)
    lb_all = ex[0:1, :] / jnp.sum(ex, axis=0, keepdims=True)
    hn = hnorm_ref[...]

    def chunk_body(c, carry):
        r0 = pl.multiple_of(c * CHUNK, CHUNK)
        rows = pl.ds(r0, CHUNK)
        for hd in range(HGRN_HEADS):
            lo = hd * HEAD_DIM
            sl = slice(lo, lo + HEAD_DIM)
            q = p_scr[rows, 3 * 512 + lo:3 * 512 + lo + HEAD_DIM]
            z = p_scr[rows, 4 * 512 + lo:4 * 512 + lo + HEAD_DIM]
            v = p_scr[rows, 5 * 512 + lo:5 * 512 + lo + HEAD_DIM]
            g = p_scr[rows, 6 * 512 + lo:6 * 512 + lo + HEAD_DIM]
            o, st_new = _hgrn_chunk(q, z, v, lb_all[:, sl], st_scr[hd])
            st_scr[hd] = st_new
            o = o * lax.rsqrt(jnp.mean(o * o, axis=-1, keepdims=True) + EPS) * hn[:, sl]
            eg = jnp.exp(-jnp.abs(g))
            sg = jnp.where(g >= 0, 1.0, eg) / (1.0 + eg)
            y_scr[rows, CONV_WIDTH + lo:CONV_WIDTH + lo + HEAD_DIM] = (o * (g * sg)).astype(BF16)
        return carry

    lax.fori_loop(0, ts // CHUNK, chunk_body, 0)

    o_ref[0] = x + _dot(y_scr[...], wout_ref[...])


def _mixer(x, mix_norm, w_in, conv_w, hgrn_lb, hgrn_norm, w_out):
    bsz, seq, d = x.shape
    ts = SEQ_TILE
    const = lambda b, j: (0, 0)
    return pl.pallas_call(
        _mixer_kernel,
        grid=(bsz, seq // ts),
        in_specs=[
            pl.BlockSpec((1, ts, d), lambda b, j: (b, j, 0)),
            pl.BlockSpec((1, d), const),
            pl.BlockSpec((d, PROJ_WIDTH), const),
            pl.BlockSpec((3, CONV_WIDTH), const),
            pl.BlockSpec((2, HGRN_WIDTH), const),
            pl.BlockSpec((1, HGRN_WIDTH), const),
            pl.BlockSpec((d, d), const),
        ],
        out_specs=pl.BlockSpec((1, ts, d), lambda b, j: (b, j, 0)),
        out_shape=jax.ShapeDtypeStruct((bsz, seq, d), F32),
        scratch_shapes=[
            pltpu.VMEM((ts, PROJ_WIDTH), F32),
            pltpu.VMEM((ts, d), BF16),
            pltpu.VMEM((HGRN_HEADS, HEAD_DIM, HEAD_DIM), F32),
            pltpu.VMEM((8, CONV_WIDTH), F32),
        ],
        compiler_params=pltpu.CompilerParams(
            dimension_semantics=("arbitrary", "arbitrary"), vmem_limit_bytes=VMEM_LIMIT),
        name="mixer",
    )(x, mix_norm, w_in, conv_w, hgrn_lb, hgrn_norm, w_out)


def _kv_kernel(m_ref, g_ref, w_ref, o_ref):
    h = _rms(m_ref[0], g_ref[...]).astype(BF16)
    o_ref[0] = _dot(h, w_ref[...]).astype(BF16)


def _kv_proj(mem, mem_norm, w_kv):
    bsz, mlen, d = mem.shape
    const = lambda b: (0, 0)
    return pl.pallas_call(
        _kv_kernel,
        grid=(bsz,),
        in_specs=[
            pl.BlockSpec((1, mlen, d), lambda b: (b, 0, 0)),
            pl.BlockSpec((1, d), const),
            pl.BlockSpec((d, 2 * d), const),
        ],
        out_specs=pl.BlockSpec((1, mlen, 2 * d), lambda b: (b, 0, 0)),
        out_shape=jax.ShapeDtypeStruct((bsz, mlen, 2 * d), BF16),
        compiler_params=pltpu.CompilerParams(
            dimension_semantics=("arbitrary",), vmem_limit_bytes=VMEM_LIMIT),
        name="kv_proj",
    )(mem, mem_norm, w_kv)


def _first_argmax(vals, lane, big):
    mx = jnp.max(vals, axis=-1, keepdims=True)
    idx = jnp.min(jnp.where(vals == mx, lane, big), axis=-1, keepdims=True)
    return mx, idx


def _route(logits):
    lane = lax.broadcasted_iota(jnp.int32, logits.shape, 1)
    neg = jnp.float32(-jnp.inf)
    gl = jnp.where(lane < N_GROUPS, logits, neg)
    gmax, gidx = _first_argmax(gl, lane, LANES)
    g_p = 1.0 / jnp.sum(jnp.exp(gl - gmax), axis=-1, keepdims=True)
    base = N_GROUPS + EXPERTS_PER_GROUP * gidx
    el = jnp.where((lane >= base) & (lane < base + EXPERTS_PER_GROUP), logits, neg)
    m1, i1 = _first_argmax(el, lane, LANES)
    el2 = jnp.where(lane == i1, neg, el)
    m2, i2 = _first_argmax(el2, lane, LANES)
    zsum = jnp.sum(jnp.exp(el - m1), axis=-1, keepdims=True)
    p1 = 1.0 / zsum
    p2 = jnp.exp(m2 - m1) / zsum
    w1 = g_p * p1 / (p1 + p2)
    w2 = g_p * p2 / (p1 + p2)
    e_lane = lane + N_GROUPS
    return jnp.where(e_lane == i1, w1, 0.0) + jnp.where(e_lane == i2, w2, 0.0)


def _attn_kernel(x_ref, kv_ref, gx_ref, wq_ref, wo_ref, gf_ref, wr_ref, br_ref,
                 x2_ref, h3_ref, comb_ref, o_scr):
    x = x_ref[0]
    h = _rms(x, gx_ref[...]).astype(BF16)
    q = _dot(h, wq_ref[...])
    scale = XATTN_HEAD_DIM ** -0.5
    for hd in range(XATTN_HEADS):
        lo = hd * XATTN_HEAD_DIM
        qh = q[:, lo:lo + XATTN_HEAD_DIM].astype(BF16)
        kh = kv_ref[0, :, lo:lo + XATTN_HEAD_DIM]
        vh = kv_ref[0, :, D_MODEL + lo:D_MODEL + lo + XATTN_HEAD_DIM]
        s = _dot_nt(qh, kh) * scale
        s = s - jnp.max(s, axis=-1, keepdims=True)
        p = jnp.exp(s)
        p = p / jnp.sum(p, axis=-1, keepdims=True)
        o_scr[:, lo:lo + XATTN_HEAD_DIM] = _dot(p.astype(BF16), vh).astype(BF16)
    x2 = x + _dot(o_scr[...], wo_ref[...])
    x2_ref[0] = x2
    h3 = _rms(x2, gf_ref[...])
    h3_ref[0] = h3.astype(BF16)
    logits = jnp.dot(h3, wr_ref[...], preferred_element_type=F32,
                     precision=lax.Precision.HIGHEST) + br_ref[...]
    comb_ref[0] = _route(logits)


def _attention(x1, kv, xattn_norm, w_q, w_o, ffn_norm, w_router, b_router):
    bsz, seq, d = x1.shape
    mlen = kv.shape[1]
    tq = ATTN_TILE
    const = lambda b, j: (0, 0)
    tile = lambda b, j: (b, j, 0)
    return pl.pallas_call(
        _attn_kernel,
        grid=(bsz, seq // tq),
        in_specs=[
            pl.BlockSpec((1, tq, d), tile),
            pl.BlockSpec((1, mlen, 2 * d), lambda b, j: (b, 0, 0)),
            pl.BlockSpec((1, d), const),
            pl.BlockSpec((d, d), const),
            pl.BlockSpec((d, d), const),
            pl.BlockSpec((1, d), const),
            pl.BlockSpec((d, LANES), const),
            pl.BlockSpec((1, LANES), const),
        ],
        out_specs=[
            pl.BlockSpec((1, tq, d), tile),
            pl.BlockSpec((1, tq, d), tile),
            pl.BlockSpec((1, tq, LANES), tile),
        ],
        out_shape=[
            jax.ShapeDtypeStruct((bsz, seq, d), F32),
            jax.ShapeDtypeStruct((bsz, seq, d), BF16),
            jax.ShapeDtypeStruct((bsz, seq, LANES), F32),
        ],
        scratch_shapes=[pltpu.VMEM((tq, d), BF16)],
        compiler_params=pltpu.CompilerParams(
            dimension_semantics=("arbitrary", "arbitrary"), vmem_limit_bytes=VMEM_LIMIT),
        name="xattn_router",
    )(x1, kv, xattn_norm, w_q, w_o, ffn_norm, w_router, b_router)


def _moe_kernel(h_ref, x2_ref, comb_ref, wg_ref, wu_ref, wd_ref, gfin_ref, o_ref, acc_scr):
    e = pl.program_id(1)

    @pl.when(e == 0)
    def _():
        acc_scr[...] = jnp.zeros_like(acc_scr)

    h = h_ref[...]
    gate = _dot(h, wg_ref[0])
    up = _dot(h, wu_ref[0])
    eg = jnp.exp(-jnp.abs(gate))
    sg = jnp.where(gate >= 0, 1.0, eg) / (1.0 + eg)
    hid = (gate * sg * up).astype(BF16)
    lane = lax.broadcasted_iota(jnp.int32, comb_ref.shape, 1)
    w = jnp.sum(jnp.where(lane == e, comb_ref[...], 0.0), axis=-1, keepdims=True)
    acc_scr[...] += w * _dot(hid, wd_ref[0])

    @pl.when(e == N_EXPERTS - 1)
    def _():
        o_ref[...] = _rms(x2_ref[...] + acc_scr[...], gfin_ref[...])


def _moe_dense(h3, x2, comb, w_gate, w_up, w_down, final_norm):
    t, d = x2.shape
    tm = MOE_TILE
    tile = lambda i, e: (i, 0)
    return pl.pallas_call(
        _moe_kernel,
        grid=(t // tm, N_EXPERTS),
        in_specs=[
            pl.BlockSpec((tm, d), tile),
            pl.BlockSpec((tm, d), tile),
            pl.BlockSpec((tm, LANES), tile),
            pl.BlockSpec((1, d, D_EXPERT), lambda i, e: (e, 0, 0)),
            pl.BlockSpec((1, d, D_EXPERT), lambda i, e: (e, 0, 0)),
            pl.BlockSpec((1, D_EXPERT, d), lambda i, e: (e, 0, 0)),
            pl.BlockSpec((1, d), lambda i, e: (0, 0)),
        ],
        out_specs=pl.BlockSpec((tm, d), tile),
        out_shape=jax.ShapeDtypeStruct((t, d), F32),
        scratch_shapes=[pltpu.VMEM((tm, d), F32)],
        compiler_params=pltpu.CompilerParams(
            dimension_semantics=("arbitrary", "arbitrary"), vmem_limit_bytes=VMEM_LIMIT),
        name="moe_dense",
    )(h3, x2, comb, w_gate, w_up, w_down, final_norm)


def kernel(x, mem, mix_norm, w_in, conv_w, hgrn_lb, hgrn_norm, w_out, xattn_norm, mem_norm,
           w_q, w_kv, w_o, ffn_norm, w_group, b_group, w_expert, b_expert, w_gate, w_up,
           w_down, final_norm):
    bsz, seq, d = x.shape
    assert d == D_MODEL and seq % SEQ_TILE == 0 and seq % ATTN_TILE == 0
    assert (bsz * seq) % MOE_TILE == 0 and mix_norm.shape[0] == 1
    bf = lambda w: w.astype(BF16)

    x1 = _mixer(x, mix_norm, bf(w_in[0]), conv_w[0], hgrn_lb, hgrn_norm, bf(w_out[0]))
    kv = _kv_proj(mem, mem_norm, bf(w_kv[0]))

    pad = LANES - N_GROUPS - N_EXPERTS
    w_router = jnp.concatenate(
        [w_group[0], w_expert[0], jnp.zeros((d, pad), F32)], axis=1)
    b_router = jnp.concatenate(
        [b_group[0], b_expert[0], jnp.zeros((pad,), F32)])[None, :]
    x2, h3, comb = _attention(x1, kv, xattn_norm, bf(w_q[0]), bf(w_o[0]), ffn_norm,
                              w_router, b_router)

    t = bsz * seq
    out = _moe_dense(h3.reshape(t, d), x2.reshape(t, d), comb.reshape(t, LANES),
                     bf(w_gate[0]), bf(w_up[0]), bf(w_down[0]), final_norm[None, :])
    return out.reshape(bsz, seq, d)
```

```python
import functools

import jax
import jax.numpy as jnp
from jax import lax
from jax.experimental import pallas as pl
from jax.experimental.pallas import tpu as pltpu

F32 = jnp.float32
BF16 = jnp.bfloat16

D_MODEL = 1024
CONV_WIDTH = 512
HGRN_WIDTH = 512
HGRN_HEADS = 4
HEAD_DIM = 128
N_PROJ_SLOTS = 7
PROJ_WIDTH = N_PROJ_SLOTS * 512
XATTN_HEADS = 4
XATTN_HEAD_DIM = 256
N_GROUPS = 4
EXPERTS_PER_GROUP = 4
N_EXPERTS = 16
D_EXPERT = 512
EPS = 1e-6

LANES = 128
CHUNK = 64
CHUNK_LEVELS = 6
SEQ_TILE = 256
ATTN_TILE = 256
MOE_TILE = 256
FINAL_TILE = 256
N_PAIRS = 6
N_BUCKETS = N_GROUPS * N_PAIRS
ROW_TILES = D_MODEL // LANES
VMEM_LIMIT = 56 * 1024 * 1024


def _rms(x, g):
    return x * lax.rsqrt(jnp.mean(x * x, axis=-1, keepdims=True) + EPS) * g


def _dot(a, b):
    return jnp.dot(a, b, preferred_element_type=F32)


def _dot_nt(a, b):
    return lax.dot_general(a, b, (((1,), (1,)), ((), ())), preferred_element_type=F32)


def _dot_tn(a, b):
    return lax.dot_general(a, b, (((0,), (0,)), ((), ())), preferred_element_type=F32)


def _roll_rows(x, shift):
    return pltpu.roll(x, shift % x.shape[0], axis=0)


def _last_of_group(r, row, group):
    g = 1
    while g < group:
        r = jnp.where((row & g) != 0, r, _roll_rows(r, -g))
        g *= 2
    return r


def _first_half_total(r, row, half):
    n = r.shape[0]
    if half < 8:
        last = _last_of_group(r, row, half)
        return jnp.where((row & half) != 0, _roll_rows(last, half), last)
    pieces = []
    for start in range(0, n, 2 * half):
        mid = start + half - 1
        pieces.append(jnp.broadcast_to(r[mid:mid + 1, :], (2 * half, r.shape[1])))
    return pieces[0] if len(pieces) == 1 else jnp.concatenate(pieces, axis=0)


def _hgrn_chunk(q, z, v, lb, st):
    e = jnp.exp(-jnp.abs(z))
    inv = 1.0 / (1.0 + e)
    pos = z >= 0
    sig_p = jnp.where(pos, inv, e * inv)
    sig_n = jnp.where(pos, e * inv, inv)
    one_m_lb = 1.0 - lb
    logf = jnp.log(lb + one_m_lb * sig_p)
    k = one_m_lb * sig_n

    row = lax.broadcasted_iota(jnp.int32, (CHUNK, HEAD_DIM), 0)
    tt = lax.broadcasted_iota(jnp.int32, (CHUNK, CHUNK), 0)
    ss = lax.broadcasted_iota(jnp.int32, (CHUNK, CHUNK), 1)
    txs = tt ^ ss

    a = jnp.where(tt == ss, jnp.sum(q * k, axis=-1, keepdims=True), 0.0)
    r = logf
    for lvl in range(1, CHUNK_LEVELS + 1):
        half = 1 << (lvl - 1)
        second = (row & half) != 0
        tot = _first_half_total(r, row, half)
        decay = jnp.exp(jnp.where(second, r, tot - r))
        ql = (q * decay).astype(BF16)
        kl = (k * decay).astype(BF16)
        m = (ss < tt) & (txs >= half) & (txs < 2 * half)
        a = jnp.where(m, _dot_nt(ql, kl), a)
        r = jnp.where(second, r + tot, r)
    b = r
    b_last = b[CHUNK - 1:CHUNK, :]

    vb = v.astype(BF16)
    o = _dot(a.astype(BF16), vb)
    o = o + _dot_nt((q * jnp.exp(b)).astype(BF16), st.astype(BF16))
    kd = (k * jnp.exp(b_last - b)).astype(BF16)
    st_new = st * jnp.exp(b_last) + _dot_tn(vb, kd)
    return o, st_new


def _mixer_kernel(x_ref, gmix_ref, win_ref, convw_ref, lbraw_ref, hnorm_ref, wout_ref,
                  o_ref, p_scr, y_scr, st_scr, tail_scr):
    j = pl.program_id(1)

    @pl.when(j == 0)
    def _():
        st_scr[...] = jnp.zeros_like(st_scr)
        tail_scr[...] = jnp.zeros_like(tail_scr)

    x = x_ref[0]
    h = _rms(x, gmix_ref[...]).astype(BF16)
    p_scr[...] = _dot(h, win_ref[...])

    ts = x.shape[0]
    cb = p_scr[:, 0:CONV_WIDTH]
    u = p_scr[:, CONV_WIDTH:2 * CONV_WIDTH] * p_scr[:, 2 * CONV_WIDTH:3 * CONV_WIDTH]
    row = lax.broadcasted_iota(jnp.int32, u.shape, 0)
    prev1 = tail_scr[7:8, :]
    prev2 = tail_scr[6:7, :]
    u1 = jnp.where(row == 0, prev1, _roll_rows(u, 1))
    u2 = jnp.where(row == 0, prev2, jnp.where(row == 1, prev1, _roll_rows(u, 2)))
    cw = convw_ref[...]
    conv = u2 * cw[0:1, :] + u1 * cw[1:2, :] + u * cw[2:3, :]
    y_scr[:, 0:CONV_WIDTH] = (cb * conv).astype(BF16)
    tail_scr[...] = u[ts - 8:ts, :]

    raw = lbraw_ref[...]
    mx = jnp.max(raw, axis=0, keepdims=True)
    ex = jnp.exp(raw - mx)
    lb_all = ex[0:1, :] / jnp.sum(ex, axis=0, keepdims=True)
    hn = hnorm_ref[...]

    def chunk_body(c, carry):
        r0 = pl.multiple_of(c * CHUNK, CHUNK)
        rows = pl.ds(r0, CHUNK)
        for hd in range(HGRN_HEADS):
            lo = hd * HEAD_DIM
            sl = slice(lo, lo + HEAD_DIM)
            q = p_scr[rows, 3 * 512 + lo:3 * 512 + lo + HEAD_DIM]
            z = p_scr[rows, 4 * 512 + lo:4 * 512 + lo + HEAD_DIM]
            v = p_scr[rows, 5 * 512 + lo:5 * 512 + lo + HEAD_DIM]
            g = p_scr[rows, 6 * 512 + lo:6 * 512 + lo + HEAD_DIM]
            o, st_new = _hgrn_chunk(q, z, v, lb_all[:, sl], st_scr[hd])
            st_scr[hd] = st_new
            o = o * lax.rsqrt(jnp.mean(o * o, axis=-1, keepdims=True) + EPS) * hn[:, sl]
            eg = jnp.exp(-jnp.abs(g))
            sg = jnp.where(g >= 0, 1.0, eg) / (1.0 + eg)
            y_scr[rows, CONV_WIDTH + lo:CONV_WIDTH + lo + HEAD_DIM] = (o * (g * sg)).astype(BF16)
        return carry

    lax.fori_loop(0, ts // CHUNK, chunk_body, 0)

    o_ref[0] = x + _dot(y_scr[...], wout_ref[...])


def _mixer(x, mix_norm, w_in, conv_w, hgrn_lb, hgrn_norm, w_out):
    bsz, seq, d = x.shape
    ts = SEQ_TILE
    const = lambda b, j: (0, 0)
    return pl.pallas_call(
        _mixer_kernel,
        grid=(bsz, seq // ts),
        in_specs=[
            pl.BlockSpec((1, ts, d), lambda b, j: (b, j, 0)),
            pl.BlockSpec((1, d), const),
            pl.BlockSpec((d, PROJ_WIDTH), const),
            pl.BlockSpec((3, CONV_WIDTH), const),
            pl.BlockSpec((2, HGRN_WIDTH), const),
            pl.BlockSpec((1, HGRN_WIDTH), const),
            pl.BlockSpec((d, d), const),
        ],
        out_specs=pl.BlockSpec((1, ts, d), lambda b, j: (b, j, 0)),
        out_shape=jax.ShapeDtypeStruct((bsz, seq, d), F32),
        scratch_shapes=[
            pltpu.VMEM((ts, PROJ_WIDTH), F32),
            pltpu.VMEM((ts, d), BF16),
            pltpu.VMEM((HGRN_HEADS, HEAD_DIM, HEAD_DIM), F32),
            pltpu.VMEM((8, CONV_WIDTH), F32),
        ],
        compiler_params=pltpu.CompilerParams(
            dimension_semantics=("arbitrary", "arbitrary"), vmem_limit_bytes=VMEM_LIMIT),
        name="mixer",
    )(x, mix_norm, w_in, conv_w, hgrn_lb, hgrn_norm, w_out)


def _kv_kernel(m_ref, g_ref, w_ref, o_ref):
    h = _rms(m_ref[0], g_ref[...]).astype(BF16)
    o_ref[0] = _dot(h, w_ref[...]).astype(BF16)


def _kv_proj(mem, mem_norm, w_kv):
    bsz, mlen, d = mem.shape
    const = lambda b: (0, 0)
    return pl.pallas_call(
        _kv_kernel,
        grid=(bsz,),
        in_specs=[
            pl.BlockSpec((1, mlen, d), lambda b: (b, 0, 0)),
            pl.BlockSpec((1, d), const),
            pl.BlockSpec((d, 2 * d), const),
        ],
        out_specs=pl.BlockSpec((1, mlen, 2 * d), lambda b: (b, 0, 0)),
        out_shape=jax.ShapeDtypeStruct((bsz, mlen, 2 * d), BF16),
        compiler_params=pltpu.CompilerParams(
            dimension_semantics=("arbitrary",), vmem_limit_bytes=VMEM_LIMIT),
        name="kv_proj",
    )(mem, mem_norm, w_kv)


def _first_argmax(vals, lane, big):
    mx = jnp.max(vals, axis=-1, keepdims=True)
    idx = jnp.min(jnp.where(vals == mx, lane, big), axis=-1, keepdims=True)
    return mx, idx


def _route(logits, running):
    rows = logits.shape[0]
    lane = lax.broadcasted_iota(jnp.int32, logits.shape, 1)
    neg = jnp.float32(-jnp.inf)
    gl = jnp.where(lane < N_GROUPS, logits, neg)
    gmax, gidx = _first_argmax(gl, lane, LANES)
    g_p = 1.0 / jnp.sum(jnp.exp(gl - gmax), axis=-1, keepdims=True)
    base = N_GROUPS + EXPERTS_PER_GROUP * gidx
    el = jnp.where((lane >= base) & (lane < base + EXPERTS_PER_GROUP), logits, neg)
    m1, i1 = _first_argmax(el, lane, LANES)
    el2 = jnp.where(lane == i1, neg, el)
    m2, i2 = _first_argmax(el2, lane, LANES)
    zsum = jnp.sum(jnp.exp(el - m1), axis=-1, keepdims=True)
    p1 = 1.0 / zsum
    p2 = jnp.exp(m2 - m1) / zsum
    w1 = g_p * p1 / (p1 + p2)
    w2 = g_p * p2 / (p1 + p2)
    first_is_lo = i1 < i2
    lo = jnp.minimum(i1, i2) - base
    hi = jnp.maximum(i1, i2) - base
    bucket = gidx * N_PAIRS + ((lo * (7 - lo)) >> 1) + hi - lo - 1
    w_lo = jnp.where(first_is_lo, w1, w2)
    w_hi = jnp.where(first_is_lo, w2, w1)

    onehot = lane == bucket
    tt = lax.broadcasted_iota(jnp.int32, (rows, rows), 0)
    ss = lax.broadcasted_iota(jnp.int32, (rows, rows), 1)
    before = _dot((ss < tt).astype(BF16), onehot.astype(BF16))
    rank = jnp.sum(jnp.where(onehot, before + running, 0.0), axis=-1, keepdims=True)
    running = running + jnp.sum(onehot.astype(F32), axis=0, keepdims=True)
    rinfo = jnp.where(lane == 0, bucket.astype(F32),
                      jnp.where(lane == 1, rank,
                                jnp.where(lane == 2, w_lo, jnp.where(lane == 3, w_hi, 0.0))))
    return rinfo, running


def _attn_kernel(x_ref, kv_ref, gx_ref, wq_ref, wo_ref, gf_ref, wr_ref, br_ref,
                 x2_ref, h3r_ref, rinfo_ref, counts_ref, o_scr, cnt_scr):
    @pl.when((pl.program_id(0) == 0) & (pl.program_id(1) == 0))
    def _():
        cnt_scr[...] = jnp.zeros_like(cnt_scr)

    x = x_ref[0]
    tq = x.shape[0]
    h = _rms(x, gx_ref[...]).astype(BF16)
    q = _dot(h, wq_ref[...])
    scale = XATTN_HEAD_DIM ** -0.5
    for hd in range(XATTN_HEADS):
        lo = hd * XATTN_HEAD_DIM
        qh = q[:, lo:lo + XATTN_HEAD_DIM].astype(BF16)
        kh = kv_ref[0, :, lo:lo + XATTN_HEAD_DIM]
        vh = kv_ref[0, :, D_MODEL + lo:D_MODEL + lo + XATTN_HEAD_DIM]
        s = _dot_nt(qh, kh) * scale
        s = s - jnp.max(s, axis=-1, keepdims=True)
        p = jnp.exp(s)
        p = p / jnp.sum(p, axis=-1, keepdims=True)
        o_scr[:, lo:lo + XATTN_HEAD_DIM] = _dot(p.astype(BF16), vh).astype(BF16)
    x2 = x + _dot(o_scr[...], wo_ref[...])
    x2_ref[0] = x2
    h3 = _rms(x2, gf_ref[...])
    for s in range(ROW_TILES):
        h3r_ref[pl.ds(s, tq, stride=ROW_TILES), :] = h3[:, s * LANES:(s + 1) * LANES]
    logits = jnp.dot(h3, wr_ref[...], preferred_element_type=F32,
                     precision=lax.Precision.HIGHEST) + br_ref[...]
    rinfo, running = _route(logits, cnt_scr[...])
    rinfo_ref[...] = rinfo
    cnt_scr[...] = running
    counts_ref[...] = running


def _attention(x1, kv, xattn_norm, w_q, w_o, ffn_norm, w_router, b_router):
    bsz, seq, d = x1.shape
    mlen = kv.shape[1]
    tq = ATTN_TILE
    nj = seq // tq
    const = lambda b, j: (0, 0)
    tile = lambda b, j: (b, j, 0)
    flat = lambda b, j: (b * nj + j, 0)
    return pl.pallas_call(
        _attn_kernel,
        grid=(bsz, seq // tq),
        in_specs=[
            pl.BlockSpec((1, tq, d), tile),
            pl.BlockSpec((1, mlen, 2 * d), lambda b, j: (b, 0, 0)),
            pl.BlockSpec((1, d), const),
            pl.BlockSpec((d, d), const),
            pl.BlockSpec((d, d), const),
            pl.BlockSpec((1, d), const),
            pl.BlockSpec((d, LANES), const),
            pl.BlockSpec((1, LANES), const),
        ],
        out_specs=[
            pl.BlockSpec((1, tq, d), tile),
            pl.BlockSpec((tq * ROW_TILES, LANES), flat),
            pl.BlockSpec((tq, LANES), flat),
            pl.BlockSpec((1, LANES), const),
        ],
        out_shape=[
            jax.ShapeDtypeStruct((bsz, seq, d), F32),
            jax.ShapeDtypeStruct((bsz * seq * ROW_TILES, LANES), F32),
            jax.ShapeDtypeStruct((bsz * seq, LANES), F32),
            jax.ShapeDtypeStruct((1, LANES), F32),
        ],
        scratch_shapes=[pltpu.VMEM((tq, d), BF16), pltpu.VMEM((1, LANES), F32)],
        compiler_params=pltpu.CompilerParams(
            dimension_semantics=("arbitrary", "arbitrary"), vmem_limit_bytes=VMEM_LIMIT),
        name="xattn_router",
    )(x1, kv, xattn_norm, w_q, w_o, ffn_norm, w_router, b_router)


def _row_gather(idx_ref, base, n_rows, rows_per_item, src_hbm, dst_buf, dst_base, sem):
    def body(r, carry):
        item = idx_ref[base + r]
        pltpu.make_async_copy(
            src_hbm.at[pl.ds(pl.multiple_of(item * rows_per_item, rows_per_item), rows_per_item)],
            dst_buf.at[pl.ds(pl.multiple_of(dst_base + r * rows_per_item, rows_per_item),
                             rows_per_item)],
            sem).start()
        return carry

    lax.fori_loop(0, n_rows, body, 0, unroll=8)


def _row_gather_wait(n_rows, rows_per_item, src_hbm, dst_buf, dst_base, sem):
    n = n_rows * rows_per_item
    pltpu.make_async_copy(
        src_hbm.at[pl.ds(0, n)],
        dst_buf.at[pl.ds(pl.multiple_of(dst_base, rows_per_item), n)], sem).wait()


def _silu(x):
    e = jnp.exp(-jnp.abs(x))
    return x * (jnp.where(x >= 0, 1.0, e) / (1.0 + e))


def _moe_kernel(src_ref, elo_ref, ehi_ref, used_ref,
                h3r_hbm, wg_lo, wu_lo, wd_lo, wg_hi, wu_hi, wd_hi,
                ys_ref, xbuf, sem):
    del elo_ref, ehi_ref
    i = pl.program_id(0)
    nt = pl.num_programs(0)
    tm = MOE_TILE
    buf_rows = tm * ROW_TILES

    def start(tile, slot):
        _row_gather(src_ref, tile * tm, tm, ROW_TILES, h3r_hbm, xbuf, slot * buf_rows,
                    sem.at[slot])

    @pl.when((i == 0) & (used_ref[0] == 1))
    def _():
        start(0, 0)

    nxt = jnp.minimum(i + 1, nt - 1)

    @pl.when((i + 1 < nt) & (used_ref[nxt] == 1))
    def _():
        start(i + 1, (i + 1) % 2)

    slot = i % 2
    base = pl.multiple_of(slot * buf_rows, buf_rows)

    @pl.when(used_ref[i] == 1)
    def _():
        _row_gather_wait(tm, ROW_TILES, h3r_hbm, xbuf, base, sem.at[slot])
        x = jnp.concatenate(
            [xbuf[pl.ds(base + s, tm, stride=ROW_TILES), :] for s in range(ROW_TILES)],
            axis=1).astype(BF16)

        def expert(wg, wu, wd, row0):
            hid = (_silu(_dot(x, wg[0])) * _dot(x, wu[0])).astype(BF16)
            y = _dot(hid, wd[0])
            for s in range(ROW_TILES):
                ys_ref[pl.ds(row0 + s, tm, stride=2 * ROW_TILES), :] = \
                    y[:, s * LANES:(s + 1) * LANES]

        expert(wg_lo, wu_lo, wd_lo, 0)
        expert(wg_hi, wu_hi, wd_hi, ROW_TILES)

    @pl.when(used_ref[i] == 0)
    def _():
        ys_ref[...] = jnp.zeros_like(ys_ref)


def _moe_sparse(src, e_lo, e_hi, used, h3r, w_gate, w_up, w_down):
    tm = MOE_TILE
    nt = used.shape[0]
    d = D_MODEL
    lo = lambda i, src, elo, ehi, used: (elo[i], 0, 0)
    hi = lambda i, src, elo, ehi, used: (ehi[i], 0, 0)
    grid_spec = pltpu.PrefetchScalarGridSpec(
        num_scalar_prefetch=4,
        grid=(nt,),
        in_specs=[
            pl.BlockSpec(memory_space=pl.ANY),
            pl.BlockSpec((1, d, D_EXPERT), lo),
            pl.BlockSpec((1, d, D_EXPERT), lo),
            pl.BlockSpec((1, D_EXPERT, d), lo),
            pl.BlockSpec((1, d, D_EXPERT), hi),
            pl.BlockSpec((1, d, D_EXPERT), hi),
            pl.BlockSpec((1, D_EXPERT, d), hi),
        ],
        out_specs=pl.BlockSpec((2 * ROW_TILES * tm, LANES), lambda i, *_: (i, 0)),
        scratch_shapes=[
            pltpu.VMEM((2 * tm * ROW_TILES, LANES), F32),
            pltpu.SemaphoreType.DMA((2,)),
        ],
    )
    return pl.pallas_call(
        _moe_kernel,
        grid_spec=grid_spec,
        out_shape=jax.ShapeDtypeStruct((nt * tm * 2 * ROW_TILES, LANES), F32),
        compiler_params=pltpu.CompilerParams(
            dimension_semantics=("arbitrary",), vmem_limit_bytes=VMEM_LIMIT),
        name="moe_sparse",
    )(src, e_lo, e_hi, used, h3r, w_gate, w_up, w_down, w_gate, w_up, w_down)


def _final_kernel(pos_ref, x2_ref, rinfo_ref, gfin_ref, ys_hbm, o_ref, ybuf, sem):
    i = pl.program_id(0)
    nt = pl.num_programs(0)
    tf = FINAL_TILE
    rpi = 2 * ROW_TILES
    buf_rows = tf * rpi

    def start(tile, slot):
        _row_gather(pos_ref, tile * tf, tf, rpi, ys_hbm, ybuf, slot * buf_rows, sem.at[slot])

    @pl.when(i == 0)
    def _():
        start(0, 0)

    @pl.when(i + 1 < nt)
    def _():
        start(i + 1, (i + 1) % 2)

    slot = i % 2
    base = pl.multiple_of(slot * buf_rows, buf_rows)
    _row_gather_wait(tf, rpi, ys_hbm, ybuf, base, sem.at[slot])
    y_lo = jnp.concatenate(
        [ybuf[pl.ds(base + s, tf, stride=rpi), :] for s in range(ROW_TILES)], axis=1)
    y_hi = jnp.concatenate(
        [ybuf[pl.ds(base + ROW_TILES + s, tf, stride=rpi), :] for s in range(ROW_TILES)], axis=1)
    rinfo = rinfo_ref[...]
    lane = lax.broadcasted_iota(jnp.int32, rinfo.shape, 1)
    w_lo = jnp.sum(jnp.where(lane == 2, rinfo, 0.0), axis=-1, keepdims=True)
    w_hi = jnp.sum(jnp.where(lane == 3, rinfo, 0.0), axis=-1, keepdims=True)
    o_ref[...] = _rms(x2_ref[...] + (w_lo * y_lo + w_hi * y_hi), gfin_ref[...])


def _final(pos, x2, rinfo, final_norm, ys):
    t, d = x2.shape
    tf = FINAL_TILE
    grid_spec = pltpu.PrefetchScalarGridSpec(
        num_scalar_prefetch=1,
        grid=(t // tf,),
        in_specs=[
            pl.BlockSpec((tf, d), lambda i, pos: (i, 0)),
            pl.BlockSpec((tf, LANES), lambda i, pos: (i, 0)),
            pl.BlockSpec((1, d), lambda i, pos: (0, 0)),
            pl.BlockSpec(memory_space=pl.ANY),
        ],
        out_specs=pl.BlockSpec((tf, d), lambda i, pos: (i, 0)),
        scratch_shapes=[
            pltpu.VMEM((2 * tf * 2 * ROW_TILES, LANES), F32),
            pltpu.SemaphoreType.DMA((2,)),
        ],
    )
    return pl.pallas_call(
        _final_kernel,
        grid_spec=grid_spec,
        out_shape=jax.ShapeDtypeStruct((t, d), F32),
        compiler_params=pltpu.CompilerParams(
            dimension_semantics=("arbitrary",), vmem_limit_bytes=VMEM_LIMIT),
        name="moe_combine_norm",
    )(pos, x2, rinfo, final_norm, ys)


def _routing_tables(rinfo, counts, t):
    tm = MOE_TILE
    nt = t // tm + N_BUCKETS
    cnt = counts[0, :N_BUCKETS].astype(jnp.int32)
    padded = ((cnt + tm - 1) // tm) * tm
    ends = jnp.cumsum(padded)
    starts = ends - padded
    bucket = rinfo[:, 0].astype(jnp.int32)
    rank = rinfo[:, 1].astype(jnp.int32)
    pos = starts[bucket] + rank
    src = jnp.zeros((nt * tm,), jnp.int32).at[pos].set(jnp.arange(t, dtype=jnp.int32))
    tile_start = jnp.arange(nt, dtype=jnp.int32) * tm
    tile_bucket = jnp.sum((ends[None, :] <= tile_start[:, None]).astype(jnp.int32), axis=1)
    used = (tile_bucket < N_BUCKETS).astype(jnp.int32)
    tile_bucket = jnp.minimum(tile_bucket, N_BUCKETS - 1)
    pair = tile_bucket % N_PAIRS
    group0 = (tile_bucket // N_PAIRS) * EXPERTS_PER_GROUP
    e_lo = group0 + jnp.array([0, 0, 0, 1, 1, 2], jnp.int32)[pair]
    e_hi = group0 + jnp.array([1, 2, 3, 2, 3, 3], jnp.int32)[pair]
    return pos, src, e_lo, e_hi, used


def kernel(x, mem, mix_norm, w_in, conv_w, hgrn_lb, hgrn_norm, w_out, xattn_norm, mem_norm,
           w_q, w_kv, w_o, ffn_norm, w_group, b_group, w_expert, b_expert, w_gate, w_up,
           w_down, final_norm):
    bsz, seq, d = x.shape
    assert d == D_MODEL and seq % SEQ_TILE == 0 and seq % ATTN_TILE == 0
    assert (bsz * seq) % MOE_TILE == 0 and (bsz * seq) % FINAL_TILE == 0
    assert mix_norm.shape[0] == 1
    bf = lambda w: w.astype(BF16)

    x1 = _mixer(x, mix_norm, bf(w_in[0]), conv_w[0], hgrn_lb, hgrn_norm, bf(w_out[0]))
    kv = _kv_proj(mem, mem_norm, bf(w_kv[0]))

    pad = LANES - N_GROUPS - N_EXPERTS
    w_router = jnp.concatenate(
        [w_group[0], w_expert[0], jnp.zeros((d, pad), F32)], axis=1)
    b_router = jnp.concatenate(
        [b_group[0], b_expert[0], jnp.zeros((pad,), F32)])[None, :]
    x2, h3r, rinfo, counts = _attention(x1, kv, xattn_norm, bf(w_q[0]), bf(w_o[0]), ffn_norm,
                                        w_router, b_router)

    t = bsz * seq
    pos, src, e_lo, e_hi, used = _routing_tables(rinfo, counts, t)
    ys = _moe_sparse(src, e_lo, e_hi, used, h3r, bf(w_gate[0]), bf(w_up[0]), bf(w_down[0]))
    out = _final(pos, x2.reshape(t, d), rinfo, final_norm[None, :], ys)
    return out.reshape(bsz, seq, d)
```

```python
import functools

import jax
import jax.numpy as jnp
from jax import lax
from jax.experimental import pallas as pl
from jax.experimental.pallas import tpu as pltpu

F32 = jnp.float32
BF16 = jnp.bfloat16

D_MODEL = 1024
CONV_WIDTH = 512
HGRN_WIDTH = 512
HGRN_HEADS = 4
HEAD_DIM = 128
N_PROJ_SLOTS = 7
PROJ_WIDTH = N_PROJ_SLOTS * 512
XATTN_HEADS = 4
XATTN_HEAD_DIM = 256
N_GROUPS = 4
EXPERTS_PER_GROUP = 4
N_EXPERTS = 16
D_EXPERT = 512
EPS = 1e-6

LANES = 128
CHUNK = 64
CHUNK_LEVELS = 6
SEQ_TILE = 256
ATTN_TILE = 512
DISPATCH_TILE = 256
MOE_TILE = 256
FINAL_TILE = 256
N_PAIRS = 6
N_BUCKETS = N_GROUPS * N_PAIRS
ROW_TILES = D_MODEL // LANES
VMEM_LIMIT = 56 * 1024 * 1024


def _rms(x, g):
    return x * lax.rsqrt(jnp.mean(x * x, axis=-1, keepdims=True) + EPS) * g


def _dot(a, b):
    return jnp.dot(a, b, preferred_element_type=F32)


def _dot_nt(a, b):
    return lax.dot_general(a, b, (((1,), (1,)), ((), ())), preferred_element_type=F32)


def _dot_tn(a, b):
    return lax.dot_general(a, b, (((0,), (0,)), ((), ())), preferred_element_type=F32)


def _roll_rows(x, shift):
    return pltpu.roll(x, shift % x.shape[0], axis=0)


def _last_of_group(r, row, group):
    g = 1
    while g < group:
        r = jnp.where((row & g) != 0, r, _roll_rows(r, -g))
        g *= 2
    return r


def _first_half_total(r, row, half):
    n = r.shape[0]
    if half < 8:
        last = _last_of_group(r, row, half)
        return jnp.where((row & half) != 0, _roll_rows(last, half), last)
    pieces = []
    for start in range(0, n, 2 * half):
        mid = start + half - 1
        pieces.append(jnp.broadcast_to(r[mid:mid + 1, :], (2 * half, r.shape[1])))
    return pieces[0] if len(pieces) == 1 else jnp.concatenate(pieces, axis=0)


def _hgrn_chunk(q, z, v, lb, st):
    e = jnp.exp(-jnp.abs(z))
    inv = 1.0 / (1.0 + e)
    pos = z >= 0
    sig_p = jnp.where(pos, inv, e * inv)
    sig_n = jnp.where(pos, e * inv, inv)
    one_m_lb = 1.0 - lb
    logf = jnp.log(lb + one_m_lb * sig_p)
    k = one_m_lb * sig_n

    row = lax.broadcasted_iota(jnp.int32, (CHUNK, HEAD_DIM), 0)
    tt = lax.broadcasted_iota(jnp.int32, (CHUNK, CHUNK), 0)
    ss = lax.broadcasted_iota(jnp.int32, (CHUNK, CHUNK), 1)
    txs = tt ^ ss

    a = jnp.where(tt == ss, jnp.sum(q * k, axis=-1, keepdims=True), 0.0)
    r = logf
    for lvl in range(1, CHUNK_LEVELS + 1):
        half = 1 << (lvl - 1)
        second = (row & half) != 0
        tot = _first_half_total(r, row, half)
        decay = jnp.exp(jnp.where(second, r, tot - r))
        ql = (q * decay).astype(BF16)
        kl = (k * decay).astype(BF16)
        m = (ss < tt) & (txs >= half) & (txs < 2 * half)
        a = jnp.where(m, _dot_nt(ql, kl), a)
        r = jnp.where(second, r + tot, r)
    b = r
    b_last = b[CHUNK - 1:CHUNK, :]

    vb = v.astype(BF16)
    o = _dot(a.astype(BF16), vb)
    o = o + _dot_nt((q * jnp.exp(b)).astype(BF16), st.astype(BF16))
    kd = (k * jnp.exp(b_last - b)).astype(BF16)
    st_new = st * jnp.exp(b_last) + _dot_tn(vb, kd)
    return o, st_new


def _mixer_kernel(x_ref, gmix_ref, win_ref, convw_ref, lbraw_ref, hnorm_ref, wout_ref,
                  o_ref, p_scr, y_scr, st_scr, tail_scr):
    j = pl.program_id(1)

    @pl.when(j == 0)
    def _():
        st_scr[...] = jnp.zeros_like(st_scr)
        tail_scr[...] = jnp.zeros_like(tail_scr)

    x = x_ref[0]
    h = _rms(x, gmix_ref[...]).astype(BF16)
    p_scr[...] = _dot(h, win_ref[...])

    ts = x.shape[0]
    cb = p_scr[:, 0:CONV_WIDTH]
    u = p_scr[:, CONV_WIDTH:2 * CONV_WIDTH] * p_scr[:, 2 * CONV_WIDTH:3 * CONV_WIDTH]
    row = lax.broadcasted_iota(jnp.int32, u.shape, 0)
    prev1 = tail_scr[7:8, :]
    prev2 = tail_scr[6:7, :]
    u1 = jnp.where(row == 0, prev1, _roll_rows(u, 1))
    u2 = jnp.where(row == 0, prev2, jnp.where(row == 1, prev1, _roll_rows(u, 2)))
    cw = convw_ref[...]
    conv = u2 * cw[0:1, :] + u1 * cw[1:2, :] + u * cw[2:3, :]
    y_scr[:, 0:CONV_WIDTH] = (cb * conv).astype(BF16)
    tail_scr[...] = u[ts - 8:ts, :]

    raw = lbraw_ref[...]
    mx = jnp.max(raw, axis=0, keepdims=True)
    ex = jnp.exp(raw - mx)
    lb_all = ex[0:1, :] / jnp.sum(ex, axis=0, keepdims=True)
    hn = hnorm_ref[...]

    def chunk_body(c, carry):
        r0 = pl.multiple_of(c * CHUNK, CHUNK)
        rows = pl.ds(r0, CHUNK)
        for hd in range(HGRN_HEADS):
            lo = hd * HEAD_DIM
            sl = slice(lo, lo + HEAD_DIM)
            q = p_scr[rows, 3 * 512 + lo:3 * 512 + lo + HEAD_DIM]
            z = p_scr[rows, 4 * 512 + lo:4 * 512 + lo + HEAD_DIM]
            v = p_scr[rows, 5 * 512 + lo:5 * 512 + lo + HEAD_DIM]
            g = p_scr[rows, 6 * 512 + lo:6 * 512 + lo + HEAD_DIM]
            o, st_new = _hgrn_chunk(q, z, v, lb_all[:, sl], st_scr[hd])
            st_scr[hd] = st_new
            o = o * lax.rsqrt(jnp.mean(o * o, axis=-1, keepdims=True) + EPS) * hn[:, sl]
            eg = jnp.exp(-jnp.abs(g))
            sg = jnp.where(g >= 0, 1.0, eg) / (1.0 + eg)
            y_scr[rows, CONV_WIDTH + lo:CONV_WIDTH + lo + HEAD_DIM] = (o * (g * sg)).astype(BF16)
        return carry

    lax.fori_loop(0, ts // CHUNK, chunk_body, 0)

    o_ref[0] = x + _dot(y_scr[...], wout_ref[...])


def _mixer(x, mix_norm, w_in, conv_w, hgrn_lb, hgrn_norm, w_out):
    bsz, seq, d = x.shape
    ts = SEQ_TILE
    const = lambda b, j: (0, 0)
    return pl.pallas_call(
        _mixer_kernel,
        grid=(bsz, seq // ts),
        in_specs=[
            pl.BlockSpec((1, ts, d), lambda b, j: (b, j, 0)),
            pl.BlockSpec((1, d), const),
            pl.BlockSpec((d, PROJ_WIDTH), const),
            pl.BlockSpec((3, CONV_WIDTH), const),
            pl.BlockSpec((2, HGRN_WIDTH), const),
            pl.BlockSpec((1, HGRN_WIDTH), const),
            pl.BlockSpec((d, d), const),
        ],
        out_specs=pl.BlockSpec((1, ts, d), lambda b, j: (b, j, 0)),
        out_shape=jax.ShapeDtypeStruct((bsz, seq, d), F32),
        scratch_shapes=[
            pltpu.VMEM((ts, PROJ_WIDTH), F32),
            pltpu.VMEM((ts, d), BF16),
            pltpu.VMEM((HGRN_HEADS, HEAD_DIM, HEAD_DIM), F32),
            pltpu.VMEM((8, CONV_WIDTH), F32),
        ],
        compiler_params=pltpu.CompilerParams(
            dimension_semantics=("arbitrary", "arbitrary"), vmem_limit_bytes=VMEM_LIMIT),
        name="mixer",
    )(x, mix_norm, w_in, conv_w, hgrn_lb, hgrn_norm, w_out)


def _kv_kernel(m_ref, g_ref, w_ref, o_ref):
    h = _rms(m_ref[0], g_ref[...]).astype(BF16)
    o_ref[0] = _dot(h, w_ref[...]).astype(BF16)


def _kv_proj(mem, mem_norm, w_kv):
    bsz, mlen, d = mem.shape
    const = lambda b: (0, 0)
    return pl.pallas_call(
        _kv_kernel,
        grid=(bsz,),
        in_specs=[
            pl.BlockSpec((1, mlen, d), lambda b: (b, 0, 0)),
            pl.BlockSpec((1, d), const),
            pl.BlockSpec((d, 2 * d), const),
        ],
        out_specs=pl.BlockSpec((1, mlen, 2 * d), lambda b: (b, 0, 0)),
        out_shape=jax.ShapeDtypeStruct((bsz, mlen, 2 * d), BF16),
        compiler_params=pltpu.CompilerParams(
            dimension_semantics=("arbitrary",), vmem_limit_bytes=VMEM_LIMIT),
        name="kv_proj",
    )(mem, mem_norm, w_kv)


def _first_argmax(vals, lane):
    mx = jnp.max(vals, axis=-1, keepdims=True)
    idx = jnp.min(jnp.where(vals == mx, lane, float(LANES)), axis=-1, keepdims=True)
    return mx, idx


def _route(logits, running, tri):
    lane = lax.broadcasted_iota(jnp.int32, logits.shape, 1).astype(F32)
    neg = jnp.float32(-jnp.inf)
    gl = jnp.where(lane < N_GROUPS, logits, neg)
    gmax, gidx = _first_argmax(gl, lane)
    g_p = 1.0 / jnp.sum(jnp.exp(gl - gmax), axis=-1, keepdims=True)
    base = N_GROUPS + EXPERTS_PER_GROUP * gidx
    el = jnp.where((lane >= base) & (lane < base + EXPERTS_PER_GROUP), logits, neg)
    m1, i1 = _first_argmax(el, lane)
    el2 = jnp.where(lane == i1, neg, el)
    m2, i2 = _first_argmax(el2, lane)
    zsum = jnp.sum(jnp.exp(el - m1), axis=-1, keepdims=True)
    p1 = 1.0 / zsum
    p2 = jnp.exp(m2 - m1) / zsum
    w1 = g_p * p1 / (p1 + p2)
    w2 = g_p * p2 / (p1 + p2)
    first_is_lo = i1 < i2
    lo = jnp.minimum(i1, i2) - base
    hi = jnp.maximum(i1, i2) - base
    bucket = gidx * N_PAIRS + (lo * (7.0 - lo)) * 0.5 + hi - lo - 1.0
    w_lo = jnp.where(first_is_lo, w1, w2)
    w_hi = jnp.where(first_is_lo, w2, w1)

    onehot = lane == bucket
    before = _dot(tri, onehot.astype(BF16))
    rank = jnp.sum(jnp.where(onehot, before + running, 0.0), axis=-1, keepdims=True)
    running = running + jnp.sum(onehot.astype(F32), axis=0, keepdims=True)
    rank_hi = jnp.floor(rank * (1.0 / 256.0))
    rank_lo = rank - 256.0 * rank_hi
    rinfo = jnp.where(lane == 0, bucket, 0.0)
    for k, val in enumerate((rank_hi, rank_lo, w_lo, w_hi)):
        rinfo = jnp.where(lane == k + 1, val, rinfo)
    return rinfo, running


def _attn_kernel(x_ref, kv_ref, gx_ref, wq_ref, wo_ref, gf_ref, wr_ref, br_ref, tri_ref, eye_ref,
                 x2_ref, rinfo_ref, brk_ref, counts_ref, o_scr, cnt_scr):
    @pl.when((pl.program_id(0) == 0) & (pl.program_id(1) == 0))
    def _():
        cnt_scr[...] = jnp.zeros_like(cnt_scr)

    x = x_ref[0]
    h = _rms(x, gx_ref[...]).astype(BF16)
    q = _dot(h, wq_ref[...])
    for hd in range(XATTN_HEADS):
        lo = hd * XATTN_HEAD_DIM
        qh = q[:, lo:lo + XATTN_HEAD_DIM].astype(BF16)
        kh = kv_ref[0, :, lo:lo + XATTN_HEAD_DIM]
        vh = kv_ref[0, :, D_MODEL + lo:D_MODEL + lo + XATTN_HEAD_DIM]
        s = _dot_nt(qh, kh)
        p = jnp.exp(s - jnp.max(s, axis=-1, keepdims=True))
        inv = 1.0 / jnp.sum(p, axis=-1, keepdims=True)
        o_scr[:, lo:lo + XATTN_HEAD_DIM] = (_dot(p.astype(BF16), vh) * inv).astype(BF16)
    x2 = x + _dot(o_scr[...], wo_ref[...])
    x2_ref[0] = x2
    h3 = _rms(x2, gf_ref[...])
    h3_hi = h3.astype(BF16)
    h3_lo = (h3 - h3_hi.astype(F32)).astype(BF16)
    two = _dot(h3_hi, wr_ref[...])
    logits = (two[:, :LANES] + two[:, LANES:]) + _dot(h3_lo, wr_ref[:, :LANES]) + br_ref[...]
    rinfo, running = _route(logits, cnt_scr[...], tri_ref[...])
    rinfo_ref[...] = rinfo
    cnt_scr[...] = running
    counts_ref[...] = running
    brk_ref[...] = _dot_tn(rinfo.astype(BF16), eye_ref[...])[0:8, :].astype(jnp.int32)


def _attention(x1, kv, xattn_norm, w_q, w_o, ffn_norm, w_router, b_router):
    bsz, seq, d = x1.shape
    mlen = kv.shape[1]
    tq = ATTN_TILE
    nj = seq // tq
    const = lambda b, j: (0, 0)
    tile = lambda b, j: (b, j, 0)
    tri = jnp.tri(tq, tq, -1, dtype=BF16)
    eye = jnp.eye(tq, dtype=BF16)
    return pl.pallas_call(
        _attn_kernel,
        grid=(bsz, nj),
        in_specs=[
            pl.BlockSpec((1, tq, d), tile),
            pl.BlockSpec((1, mlen, 2 * d), lambda b, j: (b, 0, 0)),
            pl.BlockSpec((1, d), const),
            pl.BlockSpec((d, d), const),
            pl.BlockSpec((d, d), const),
            pl.BlockSpec((1, d), const),
            pl.BlockSpec((d, 2 * LANES), const),
            pl.BlockSpec((1, LANES), const),
            pl.BlockSpec((tq, tq), const),
            pl.BlockSpec((tq, tq), const),
        ],
        out_specs=[
            pl.BlockSpec((1, tq, d), tile),
            pl.BlockSpec((tq, LANES), lambda b, j: (b * nj + j, 0)),
            pl.BlockSpec((8, tq), lambda b, j: (0, b * nj + j)),
            pl.BlockSpec((1, LANES), const),
        ],
        out_shape=[
            jax.ShapeDtypeStruct((bsz, seq, d), F32),
            jax.ShapeDtypeStruct((bsz * seq, LANES), F32),
            jax.ShapeDtypeStruct((8, bsz * seq), jnp.int32),
            jax.ShapeDtypeStruct((1, LANES), F32),
        ],
        scratch_shapes=[pltpu.VMEM((tq, d), BF16), pltpu.VMEM((1, LANES), F32)],
        compiler_params=pltpu.CompilerParams(
            dimension_semantics=("arbitrary", "arbitrary"), vmem_limit_bytes=VMEM_LIMIT),
        name="xattn_router",
    )(x1, kv, xattn_norm, w_q, w_o, ffn_norm, w_router, b_router, tri, eye)


def _sorted_pos(bucket_ref, rank_ref, starts_ref, t):
    return starts_ref[bucket_ref[t]] + rank_ref[t]


def _item_copy(hbm, hbm_item, buf, buf_item, rows_per_item, sem, to_hbm):
    h = hbm.at[pl.ds(pl.multiple_of(hbm_item * rows_per_item, rows_per_item), rows_per_item)]
    b = buf.at[pl.ds(pl.multiple_of(buf_item * rows_per_item, rows_per_item), rows_per_item)]
    return pltpu.make_async_copy(b, h, sem) if to_hbm else pltpu.make_async_copy(h, b, sem)


def _items_wait(hbm, buf, buf_item, n_items, rows_per_item, sem, to_hbm):
    n = n_items * rows_per_item
    h = hbm.at[pl.ds(0, n)]
    b = buf.at[pl.ds(pl.multiple_of(buf_item * rows_per_item, rows_per_item), n)]
    (pltpu.make_async_copy(b, h, sem) if to_hbm else pltpu.make_async_copy(h, b, sem)).wait()


def _dispatch_kernel(bucket_ref, rank_ref, starts_ref, fill_ref, ntiles_ref,
                     x2_ref, rinfo_ref, gf_ref, xs_hbm, buf, sem):
    i = pl.program_id(0)
    nt = pl.num_programs(0)
    td = DISPATCH_TILE
    rpi = 2 * ROW_TILES
    slot = i % 2
    base = slot * td

    @pl.when(i == 0)
    def _():
        buf[...] = jnp.zeros_like(buf)

    @pl.when(i >= 2)
    def _():
        _items_wait(xs_hbm, buf, base, td, rpi, sem.at[slot], True)

    h3 = _rms(x2_ref[...], gf_ref[...])
    row0 = pl.multiple_of(base * rpi, rpi)
    for s in range(ROW_TILES):
        buf[pl.ds(row0 + s, td, stride=rpi), :] = h3[:, s * LANES:(s + 1) * LANES]
    rinfo = rinfo_ref[...]
    lane = lax.broadcasted_iota(jnp.int32, rinfo.shape, 1)
    for k in range(2):
        w = jnp.sum(jnp.where(lane == 3 + k, rinfo, 0.0), axis=-1, keepdims=True)
        buf[pl.ds(row0 + ROW_TILES + k, td, stride=rpi), :] = jnp.broadcast_to(w, rinfo.shape)

    def send(r, carry):
        p = _sorted_pos(bucket_ref, rank_ref, starts_ref, i * td + r)
        _item_copy(xs_hbm, p, buf, base + r, rpi, sem.at[slot], True).start()
        return carry

    lax.fori_loop(0, td, send, 0, unroll=8)

    @pl.when(i == nt - 1)
    def _():
        _items_wait(xs_hbm, buf, base, td, rpi, sem.at[slot], True)
        _items_wait(xs_hbm, buf, (1 - slot) * td, td, rpi, sem.at[1 - slot], True)
        buf[...] = jnp.zeros_like(buf)
        zsem = sem.at[2]
        for b in range(N_BUCKETS):
            def fill(p, carry):
                _item_copy(xs_hbm, p, buf, 0, rpi, zsem, True).start()
                return carry

            def drain(p, carry):
                _item_copy(xs_hbm, p, buf, 0, rpi, zsem, True).wait()
                return carry

            lax.fori_loop(fill_ref[b], fill_ref[N_BUCKETS + b], fill, 0)
            lax.fori_loop(fill_ref[b], fill_ref[N_BUCKETS + b], drain, 0)

        n_tiles_total = xs_hbm.shape[0] // (MOE_TILE * rpi)

        def fill_tile(tile, carry):
            cp = pltpu.make_async_copy(
                buf.at[pl.ds(0, MOE_TILE * rpi)],
                xs_hbm.at[pl.ds(pl.multiple_of(tile * MOE_TILE * rpi, MOE_TILE * rpi),
                                MOE_TILE * rpi)], zsem)
            cp.start()
            cp.wait()
            return carry

        lax.fori_loop(ntiles_ref[0], n_tiles_total, fill_tile, 0)


def _dispatch(bucket, rank, starts, fill, ntiles, x2, rinfo, ffn_norm, n_sorted):
    t, d = x2.shape
    td = DISPATCH_TILE
    rpi = 2 * ROW_TILES
    grid_spec = pltpu.PrefetchScalarGridSpec(
        num_scalar_prefetch=5,
        grid=(t // td,),
        in_specs=[
            pl.BlockSpec((td, d), lambda i, *_: (i, 0)),
            pl.BlockSpec((td, LANES), lambda i, *_: (i, 0)),
            pl.BlockSpec((1, d), lambda i, *_: (0, 0)),
        ],
        out_specs=pl.BlockSpec(memory_space=pl.ANY),
        scratch_shapes=[
            pltpu.VMEM((2 * td * rpi, LANES), F32),
            pltpu.SemaphoreType.DMA((3,)),
        ],
    )
    return pl.pallas_call(
        _dispatch_kernel,
        grid_spec=grid_spec,
        out_shape=jax.ShapeDtypeStruct((n_sorted * rpi, LANES), F32),
        compiler_params=pltpu.CompilerParams(
            dimension_semantics=("arbitrary",), vmem_limit_bytes=VMEM_LIMIT),
        name="moe_dispatch",
    )(bucket, rank, starts, fill, ntiles, x2, rinfo, ffn_norm)


def _silu(x):
    e = jnp.exp(-jnp.abs(x))
    return x * (jnp.where(x >= 0, 1.0, e) / (1.0 + e))


def _moe_kernel(elo_ref, ehi_ref, used_ref,
                xs_ref, wg_lo, wu_lo, wd_lo, wg_hi, wu_hi, wd_hi, y_ref):
    del elo_ref, ehi_ref
    i = pl.program_id(0)
    tm = MOE_TILE
    rpi = 2 * ROW_TILES

    @pl.when(used_ref[i] == 1)
    def _():
        x = jnp.concatenate(
            [xs_ref[pl.ds(s, tm, stride=rpi), :] for s in range(ROW_TILES)],
            axis=1).astype(BF16)

        def expert(wg, wu, wd):
            hid = (_silu(_dot(x, wg[0])) * _dot(x, wu[0])).astype(BF16)
            return _dot(hid, wd[0])

        y_lo = expert(wg_lo, wu_lo, wd_lo)
        y_hi = expert(wg_hi, wu_hi, wd_hi)
        g_lo = xs_ref[pl.ds(ROW_TILES, tm, stride=rpi), :]
        g_hi = xs_ref[pl.ds(ROW_TILES + 1, tm, stride=rpi), :]
        for s in range(ROW_TILES):
            blk = slice(s * LANES, (s + 1) * LANES)
            y_ref[pl.ds(s, tm, stride=ROW_TILES), :] = g_lo * y_lo[:, blk] + g_hi * y_hi[:, blk]

    @pl.when(used_ref[i] == 0)
    def _():
        y_ref[...] = jnp.zeros_like(y_ref)


def _moe_sparse(e_lo, e_hi, used, xs, w_gate, w_up, w_down):
    tm = MOE_TILE
    nt = used.shape[0]
    d = D_MODEL
    rpi = 2 * ROW_TILES
    lo = lambda i, elo, ehi, used: (elo[i], 0, 0)
    hi = lambda i, elo, ehi, used: (ehi[i], 0, 0)
    grid_spec = pltpu.PrefetchScalarGridSpec(
        num_scalar_prefetch=3,
        grid=(nt,),
        in_specs=[
            pl.BlockSpec((tm * rpi, LANES), lambda i, *_: (i, 0)),
            pl.BlockSpec((1, d, D_EXPERT), lo),
            pl.BlockSpec((1, d, D_EXPERT), lo),
            pl.BlockSpec((1, D_EXPERT, d), lo),
            pl.BlockSpec((1, d, D_EXPERT), hi),
            pl.BlockSpec((1, d, D_EXPERT), hi),
            pl.BlockSpec((1, D_EXPERT, d), hi),
        ],
        out_specs=pl.BlockSpec((tm * ROW_TILES, LANES), lambda i, *_: (i, 0)),
    )
    return pl.pallas_call(
        _moe_kernel,
        grid_spec=grid_spec,
        out_shape=jax.ShapeDtypeStruct((nt * tm * ROW_TILES, LANES), F32),
        compiler_params=pltpu.CompilerParams(
            dimension_semantics=("arbitrary",), vmem_limit_bytes=VMEM_LIMIT),
        name="moe_sparse",
    )(e_lo, e_hi, used, xs, w_gate, w_up, w_down, w_gate, w_up, w_down)


def _final_kernel(bucket_ref, rank_ref, starts_ref, x2_ref, gfin_ref, y_hbm, o_ref, ybuf, sem):
    i = pl.program_id(0)
    nt = pl.num_programs(0)
    tf = FINAL_TILE

    def start(tile, slot):
        def fetch(r, carry):
            p = _sorted_pos(bucket_ref, rank_ref, starts_ref, tile * tf + r)
            _item_copy(y_hbm, p, ybuf, slot * tf + r, ROW_TILES, sem.at[slot], False).start()
            return carry

        lax.fori_loop(0, tf, fetch, 0, unroll=8)

    @pl.when(i == 0)
    def _():
        start(0, 0)

    @pl.when(i + 1 < nt)
    def _():
        start(i + 1, (i + 1) % 2)

    slot = i % 2
    _items_wait(y_hbm, ybuf, slot * tf, tf, ROW_TILES, sem.at[slot], False)
    row0 = pl.multiple_of(slot * tf * ROW_TILES, ROW_TILES)
    y = jnp.concatenate(
        [ybuf[pl.ds(row0 + s, tf, stride=ROW_TILES), :] for s in range(ROW_TILES)], axis=1)
    o_ref[...] = _rms(x2_ref[...] + y, gfin_ref[...])


def _final(bucket, rank, starts, x2, final_norm, y):
    t, d = x2.shape
    tf = FINAL_TILE
    grid_spec = pltpu.PrefetchScalarGridSpec(
        num_scalar_prefetch=3,
        grid=(t // tf,),
        in_specs=[
            pl.BlockSpec((tf, d), lambda i, *_: (i, 0)),
            pl.BlockSpec((1, d), lambda i, *_: (0, 0)),
            pl.BlockSpec(memory_space=pl.ANY),
        ],
        out_specs=pl.BlockSpec((tf, d), lambda i, *_: (i, 0)),
        scratch_shapes=[
            pltpu.VMEM((2 * tf * ROW_TILES, LANES), F32),
            pltpu.SemaphoreType.DMA((2,)),
        ],
    )
    return pl.pallas_call(
        _final_kernel,
        grid_spec=grid_spec,
        out_shape=jax.ShapeDtypeStruct((t, d), F32),
        compiler_params=pltpu.CompilerParams(
            dimension_semantics=("arbitrary",), vmem_limit_bytes=VMEM_LIMIT),
        name="moe_combine_norm",
    )(bucket, rank, starts, x2, final_norm, y)


def _routing_tables(counts, t):
    tm = MOE_TILE
    nt = t // tm + N_BUCKETS
    cnt = counts[0, :N_BUCKETS].astype(jnp.int32)
    padded = ((cnt + tm - 1) // tm) * tm
    ends = jnp.cumsum(padded)
    starts = ends - padded
    fill = jnp.concatenate([starts + cnt, ends])
    ntiles = ends[-1:] // tm
    tile_start = jnp.arange(nt, dtype=jnp.int32) * tm
    tile_bucket = jnp.sum((ends[None, :] <= tile_start[:, None]).astype(jnp.int32), axis=1)
    used = (tile_bucket < N_BUCKETS).astype(jnp.int32)
    tile_bucket = jnp.minimum(tile_bucket, N_BUCKETS - 1)
    pair = tile_bucket % N_PAIRS
    group0 = (tile_bucket // N_PAIRS) * EXPERTS_PER_GROUP
    e_lo = group0 + jnp.array([0, 0, 0, 1, 1, 2], jnp.int32)[pair]
    e_hi = group0 + jnp.array([1, 2, 3, 2, 3, 3], jnp.int32)[pair]
    return starts, fill, ntiles, e_lo, e_hi, used, nt


def kernel(x, mem, mix_norm, w_in, conv_w, hgrn_lb, hgrn_norm, w_out, xattn_norm, mem_norm,
           w_q, w_kv, w_o, ffn_norm, w_group, b_group, w_expert, b_expert, w_gate, w_up,
           w_down, final_norm):
    bsz, seq, d = x.shape
    assert d == D_MODEL and seq % SEQ_TILE == 0 and seq % ATTN_TILE == 0
    assert (bsz * seq) % MOE_TILE == 0 and (bsz * seq) % FINAL_TILE == 0
    assert (bsz * seq) % DISPATCH_TILE == 0 and bsz * seq >= 2 * DISPATCH_TILE
    assert 2 * DISPATCH_TILE >= MOE_TILE and mix_norm.shape[0] == 1
    bf = lambda w: w.astype(BF16)

    x1 = _mixer(x, mix_norm, bf(w_in[0]), conv_w[0], hgrn_lb, hgrn_norm, bf(w_out[0]))
    kv = _kv_proj(mem, mem_norm, bf(w_kv[0]))

    pad = LANES - N_GROUPS - N_EXPERTS
    w_router = jnp.concatenate(
        [w_group[0], w_expert[0], jnp.zeros((d, pad), F32)], axis=1)
    b_router = jnp.concatenate(
        [b_group[0], b_expert[0], jnp.zeros((pad,), F32)])[None, :]
    w_router_hi = bf(w_router)
    w_router_lo = bf(w_router - w_router_hi.astype(F32))
    w_router2 = jnp.concatenate([w_router_hi, w_router_lo], axis=1)
    w_q_scaled = bf(w_q[0] * (XATTN_HEAD_DIM ** -0.5))
    x2, rinfo, brk, counts = _attention(x1, kv, xattn_norm, w_q_scaled, bf(w_o[0]), ffn_norm,
                                        w_router2, b_router)

    t = bsz * seq
    x2 = x2.reshape(t, d)
    bucket = brk[0]
    rank = brk[1] * 256 + brk[2]
    starts, fill, ntiles, e_lo, e_hi, used, nt = _routing_tables(counts, t)
    xs = _dispatch(bucket, rank, starts, fill, ntiles, x2, rinfo, ffn_norm, nt * MOE_TILE)
    y = _moe_sparse(e_lo, e_hi, used, xs, bf(w_gate[0]), bf(w_up[0]), bf(w_down[0]))
    out = _final(bucket, rank, starts, x2, final_norm[None, :], y)
    return out.reshape(bsz, seq, d)
```

```python
import functools

import jax
import jax.numpy as jnp
from jax import lax
from jax.experimental import pallas as pl
from jax.experimental.pallas import tpu as pltpu

F32 = jnp.float32
BF16 = jnp.bfloat16

D_MODEL = 1024
CONV_WIDTH = 512
HGRN_WIDTH = 512
HGRN_HEADS = 4
HEAD_DIM = 128
N_PROJ_SLOTS = 7
PROJ_WIDTH = N_PROJ_SLOTS * 512
XATTN_HEADS = 4
XATTN_HEAD_DIM = 256
N_GROUPS = 4
EXPERTS_PER_GROUP = 4
N_EXPERTS = 16
D_EXPERT = 512
EPS = 1e-6

LANES = 128
SUBLANES = 8
CHUNK = 64
CHUNK_LEVELS = 6
SEQ_TILE = 256
ATTN_TILE = 512
DISPATCH_TILE = 256
MOE_TILE = 256
FINAL_TILE = 256
N_PAIRS = 6
N_BUCKETS = N_GROUPS * N_PAIRS
ROW_TILES = D_MODEL // LANES
VMEM_LIMIT = 56 * 1024 * 1024


def _rms(x, g):
    return x * lax.rsqrt(jnp.mean(x * x, axis=-1, keepdims=True) + EPS) * g


def _dot(a, b):
    return jnp.dot(a, b, preferred_element_type=F32)


def _dot_nt(a, b):
    return lax.dot_general(a, b, (((1,), (1,)), ((), ())), preferred_element_type=F32)


def _dot_tn(a, b):
    return lax.dot_general(a, b, (((0,), (0,)), ((), ())), preferred_element_type=F32)


def _roll_rows(x, shift):
    return pltpu.roll(x, shift % x.shape[0], axis=0)


def _level_exponents(logf2, use_level):
    n = CHUNK // SUBLANES
    sub = lax.broadcasted_iota(jnp.int32, (SUBLANES, logf2.shape[1]), 0)
    roll = lambda x, s: pltpu.roll(x, s % SUBLANES, axis=0)
    r = [logf2[SUBLANES * j:SUBLANES * (j + 1), :] for j in range(n)]
    out = []
    for lvl in range(1, CHUNK_LEVELS + 1):
        half = 1 << (lvl - 1)
        g = [None] * n
        if half < SUBLANES:
            second = (sub & half) != 0
            for j in range(n):
                last = r[j]
                w = 1
                while w < half:
                    last = jnp.where((sub & w) != 0, last, roll(last, -w))
                    w *= 2
                tot = jnp.where(second, roll(last, half), last)
                g[j] = jnp.where(second, r[j], tot - r[j])
                r[j] = jnp.where(second, r[j] + tot, r[j])
        else:
            hv = half // SUBLANES
            for j0 in range(0, n, 2 * hv):
                mid = r[j0 + hv - 1]
                tot = jnp.broadcast_to(mid[SUBLANES - 1:SUBLANES, :], mid.shape)
                for j in range(j0, j0 + hv):
                    g[j] = tot - r[j]
                for j in range(j0 + hv, j0 + 2 * hv):
                    g[j] = r[j]
                    r[j] = r[j] + tot
        use_level(lvl, jnp.concatenate(g, axis=0))
    return jnp.concatenate(r, axis=0)


def _sigmoid(x):
    return 0.5 * jnp.tanh(0.5 * x) + 0.5


def _split_levels():
    import numpy as np
    t = np.arange(CHUNK)[:, None]
    s = np.arange(CHUNK)[None, :]
    msb = np.floor(np.log2(np.maximum(t ^ s, 1))).astype(np.int32) + 1
    return np.where(s < t, msb, np.where(s == t, 0, -1)).astype(np.int32)


def _hgrn_chunk(q, z, v, lb, st, levels):
    half_th = 0.5 * jnp.tanh(0.5 * z)
    one_m_lb = 1.0 - lb
    logf2 = jnp.log2(lb + one_m_lb * (0.5 + half_th))
    k = one_m_lb * (0.5 - half_th)

    qb = q.astype(BF16)
    kb = k.astype(BF16)
    scores = [jnp.where(levels == 0, jnp.sum(q * k, axis=-1, keepdims=True), 0.0)]

    def use_level(lvl, g):
        decay = jnp.exp2(g).astype(BF16)
        scores[0] = jnp.where(levels == lvl, _dot_nt(qb * decay, kb * decay), scores[0])

    b2 = _level_exponents(logf2, use_level)
    a = scores[0]
    b2_last = b2[CHUNK - 1:CHUNK, :]

    vb = v.astype(BF16)
    o = _dot(a.astype(BF16), vb)
    o = o + _dot_nt((q * jnp.exp2(b2)).astype(BF16), st.astype(BF16))
    kd = (k * jnp.exp2(b2_last - b2)).astype(BF16)
    st_new = st * jnp.exp2(b2_last) + _dot_tn(vb, kd)
    return o, st_new


def _mixer_kernel(x_ref, gmix_ref, win_ref, convw_ref, lbraw_ref, hnorm_ref, wout_ref, lvl_ref,
                  o_ref, p_scr, y_scr, st_scr, tail_scr):
    j = pl.program_id(1)

    @pl.when(j == 0)
    def _():
        st_scr[...] = jnp.zeros_like(st_scr)
        tail_scr[...] = jnp.zeros_like(tail_scr)

    x = x_ref[0]
    h = _rms(x, gmix_ref[...]).astype(BF16)
    p_scr[...] = _dot(h, win_ref[...])

    ts = x.shape[0]
    cb = p_scr[:, 0:CONV_WIDTH]
    u = p_scr[:, CONV_WIDTH:2 * CONV_WIDTH] * p_scr[:, 2 * CONV_WIDTH:3 * CONV_WIDTH]
    row = lax.broadcasted_iota(jnp.int32, u.shape, 0)
    prev1 = tail_scr[7:8, :]
    prev2 = tail_scr[6:7, :]
    u1 = jnp.where(row == 0, prev1, _roll_rows(u, 1))
    u2 = jnp.where(row == 0, prev2, jnp.where(row == 1, prev1, _roll_rows(u, 2)))
    cw = convw_ref[...]
    conv = u2 * cw[0:1, :] + u1 * cw[1:2, :] + u * cw[2:3, :]
    y_scr[:, 0:CONV_WIDTH] = (cb * conv).astype(BF16)
    tail_scr[...] = u[ts - 8:ts, :]

    raw = lbraw_ref[...]
    mx = jnp.max(raw, axis=0, keepdims=True)
    ex = jnp.exp(raw - mx)
    lb_all = ex[0:1, :] / jnp.sum(ex, axis=0, keepdims=True)
    hn = hnorm_ref[...]

    def chunk_body(c, carry):
        r0 = pl.multiple_of(c * CHUNK, CHUNK)
        rows = pl.ds(r0, CHUNK)
        for hd in range(HGRN_HEADS):
            lo = hd * HEAD_DIM
            sl = slice(lo, lo + HEAD_DIM)
            q = p_scr[rows, 3 * 512 + lo:3 * 512 + lo + HEAD_DIM]
            z = p_scr[rows, 4 * 512 + lo:4 * 512 + lo + HEAD_DIM]
            v = p_scr[rows, 5 * 512 + lo:5 * 512 + lo + HEAD_DIM]
            g = p_scr[rows, 6 * 512 + lo:6 * 512 + lo + HEAD_DIM]
            o, st_new = _hgrn_chunk(q, z, v, lb_all[:, sl], st_scr[hd], lvl_ref[...])
            st_scr[hd] = st_new
            o = o * lax.rsqrt(jnp.mean(o * o, axis=-1, keepdims=True) + EPS) * hn[:, sl]
            y_scr[rows, CONV_WIDTH + lo:CONV_WIDTH + lo + HEAD_DIM] = \
                (o * (g * _sigmoid(g))).astype(BF16)
        return carry

    lax.fori_loop(0, ts // CHUNK, chunk_body, 0, unroll=2)

    o_ref[0] = x + _dot(y_scr[...], wout_ref[...])


def _mixer(x, mix_norm, w_in, conv_w, hgrn_lb, hgrn_norm, w_out):
    bsz, seq, d = x.shape
    ts = SEQ_TILE
    const = lambda b, j: (0, 0)
    return pl.pallas_call(
        _mixer_kernel,
        grid=(bsz, seq // ts),
        in_specs=[
            pl.BlockSpec((1, ts, d), lambda b, j: (b, j, 0)),
            pl.BlockSpec((1, d), const),
            pl.BlockSpec((d, PROJ_WIDTH), const),
            pl.BlockSpec((3, CONV_WIDTH), const),
            pl.BlockSpec((2, HGRN_WIDTH), const),
            pl.BlockSpec((1, HGRN_WIDTH), const),
            pl.BlockSpec((d, d), const),
            pl.BlockSpec((CHUNK, CHUNK), const),
        ],
        out_specs=pl.BlockSpec((1, ts, d), lambda b, j: (b, j, 0)),
        out_shape=jax.ShapeDtypeStruct((bsz, seq, d), F32),
        scratch_shapes=[
            pltpu.VMEM((ts, PROJ_WIDTH), F32),
            pltpu.VMEM((ts, d), BF16),
            pltpu.VMEM((HGRN_HEADS, HEAD_DIM, HEAD_DIM), F32),
            pltpu.VMEM((8, CONV_WIDTH), F32),
        ],
        compiler_params=pltpu.CompilerParams(
            dimension_semantics=("arbitrary", "arbitrary"), vmem_limit_bytes=VMEM_LIMIT),
        name="mixer",
    )(x, mix_norm, w_in, conv_w, hgrn_lb, hgrn_norm, w_out, jnp.asarray(_split_levels()))


def _kv_kernel(m_ref, g_ref, w_ref, o_ref):
    h = _rms(m_ref[0], g_ref[...]).astype(BF16)
    o_ref[0] = _dot(h, w_ref[...]).astype(BF16)


def _kv_proj(mem, mem_norm, w_kv):
    bsz, mlen, d = mem.shape
    const = lambda b: (0, 0)
    return pl.pallas_call(
        _kv_kernel,
        grid=(bsz,),
        in_specs=[
            pl.BlockSpec((1, mlen, d), lambda b: (b, 0, 0)),
            pl.BlockSpec((1, d), const),
            pl.BlockSpec((d, 2 * d), const),
        ],
        out_specs=pl.BlockSpec((1, mlen, 2 * d), lambda b: (b, 0, 0)),
        out_shape=jax.ShapeDtypeStruct((bsz, mlen, 2 * d), BF16),
        compiler_params=pltpu.CompilerParams(
            dimension_semantics=("arbitrary",), vmem_limit_bytes=VMEM_LIMIT),
        name="kv_proj",
    )(mem, mem_norm, w_kv)


def _first_argmax(vals, lane):
    mx = jnp.max(vals, axis=-1, keepdims=True)
    idx = jnp.min(jnp.where(vals == mx, lane, float(LANES)), axis=-1, keepdims=True)
    return mx, idx


def _router_logits(h3, wr_ref, br_ref):
    h3_hi = h3.astype(BF16)
    h3_lo = (h3 - h3_hi.astype(F32)).astype(BF16)
    two = _dot(h3_hi, wr_ref[...])
    return (two[:, :LANES] + two[:, LANES:]) + _dot(h3_lo, wr_ref[:, :LANES]) + br_ref[...]


def _route(logits, running, tri):
    lane = lax.broadcasted_iota(jnp.int32, logits.shape, 1).astype(F32)
    neg = jnp.float32(-jnp.inf)
    gl = jnp.where(lane < N_GROUPS, logits, neg)
    _, gidx = _first_argmax(gl, lane)
    base = N_GROUPS + EXPERTS_PER_GROUP * gidx
    el = jnp.where((lane >= base) & (lane < base + EXPERTS_PER_GROUP), logits, neg)
    _, i1 = _first_argmax(el, lane)
    _, i2 = _first_argmax(jnp.where(lane == i1, neg, el), lane)
    lo = jnp.minimum(i1, i2) - base
    hi = jnp.maximum(i1, i2) - base
    bucket = gidx * N_PAIRS + (lo * (7.0 - lo)) * 0.5 + hi - lo - 1.0

    onehot = lane == bucket
    before = _dot(tri, onehot.astype(BF16))
    rank = jnp.sum(jnp.where(onehot, before + running, 0.0), axis=-1, keepdims=True)
    running = running + jnp.sum(onehot.astype(F32), axis=0, keepdims=True)
    rank_hi = jnp.floor(rank * (1.0 / 256.0))
    rank_lo = rank - 256.0 * rank_hi
    info = jnp.where(lane == 0, bucket, jnp.where(lane == 1, rank_hi,
                                                  jnp.where(lane == 2, rank_lo, 0.0)))
    return info, running


def _gates(logits, e_lo, e_hi):
    lane = lax.broadcasted_iota(jnp.int32, logits.shape, 1)
    neg = jnp.float32(-jnp.inf)
    gl = jnp.where(lane < N_GROUPS, logits, neg)
    g_p = 1.0 / jnp.sum(jnp.exp(gl - jnp.max(gl, axis=-1, keepdims=True)), axis=-1, keepdims=True)
    l_lo = jnp.sum(jnp.where(lane == N_GROUPS + e_lo, logits, 0.0), axis=-1, keepdims=True)
    l_hi = jnp.sum(jnp.where(lane == N_GROUPS + e_hi, logits, 0.0), axis=-1, keepdims=True)
    m = jnp.maximum(l_lo, l_hi)
    p_lo = jnp.exp(l_lo - m)
    p_hi = jnp.exp(l_hi - m)
    inv = g_p / (p_lo + p_hi)
    return p_lo * inv, p_hi * inv


def _attn_kernel(x_ref, kv_ref, gx_ref, wq_ref, wo_ref, gf_ref, wr_ref, br_ref, tri_ref, eye_ref,
                 x2r_ref, brk_ref, counts_ref, o_scr, cnt_scr):
    @pl.when((pl.program_id(0) == 0) & (pl.program_id(1) == 0))
    def _():
        cnt_scr[...] = jnp.zeros_like(cnt_scr)

    x = x_ref[0]
    tq = x.shape[0]
    h = _rms(x, gx_ref[...]).astype(BF16)
    q = _dot(h, wq_ref[...])
    for hd in range(XATTN_HEADS):
        lo = hd * XATTN_HEAD_DIM
        qh = q[:, lo:lo + XATTN_HEAD_DIM].astype(BF16)
        kh = kv_ref[0, :, lo:lo + XATTN_HEAD_DIM]
        vh = kv_ref[0, :, D_MODEL + lo:D_MODEL + lo + XATTN_HEAD_DIM]
        s = _dot_nt(qh, kh)
        p = jnp.exp(s - jnp.max(s, axis=-1, keepdims=True))
        inv = 1.0 / jnp.sum(p, axis=-1, keepdims=True)
        o_scr[:, lo:lo + XATTN_HEAD_DIM] = (_dot(p.astype(BF16), vh) * inv).astype(BF16)
    x2 = x + _dot(o_scr[...], wo_ref[...])
    for s in range(ROW_TILES):
        x2r_ref[pl.ds(s, tq, stride=ROW_TILES), :] = x2[:, s * LANES:(s + 1) * LANES]
    logits = _router_logits(_rms(x2, gf_ref[...]), wr_ref, br_ref)
    info, running = _route(logits, cnt_scr[...], tri_ref[...])
    cnt_scr[...] = running
    counts_ref[...] = running
    brk_ref[...] = _dot_tn(info.astype(BF16), eye_ref[...])[0:8, :].astype(jnp.int32)


def _attention(x1, kv, xattn_norm, w_q, w_o, ffn_norm, w_router, b_router):
    bsz, seq, d = x1.shape
    mlen = kv.shape[1]
    tq = ATTN_TILE
    nj = seq // tq
    const = lambda b, j: (0, 0)
    tile = lambda b, j: (b, j, 0)
    tri = jnp.tri(tq, tq, -1, dtype=BF16)
    eye = jnp.eye(tq, dtype=BF16)
    return pl.pallas_call(
        _attn_kernel,
        grid=(bsz, nj),
        in_specs=[
            pl.BlockSpec((1, tq, d), tile),
            pl.BlockSpec((1, mlen, 2 * d), lambda b, j: (b, 0, 0)),
            pl.BlockSpec((1, d), const),
            pl.BlockSpec((d, d), const),
            pl.BlockSpec((d, d), const),
            pl.BlockSpec((1, d), const),
            pl.BlockSpec((d, 2 * LANES), const),
            pl.BlockSpec((1, LANES), const),
            pl.BlockSpec((tq, tq), const),
            pl.BlockSpec((tq, tq), const),
        ],
        out_specs=[
            pl.BlockSpec((tq * ROW_TILES, LANES), lambda b, j: (b * nj + j, 0)),
            pl.BlockSpec((8, tq), lambda b, j: (0, b * nj + j)),
            pl.BlockSpec((1, LANES), const),
        ],
        out_shape=[
            jax.ShapeDtypeStruct((bsz * seq * ROW_TILES, LANES), F32),
            jax.ShapeDtypeStruct((8, bsz * seq), jnp.int32),
            jax.ShapeDtypeStruct((1, LANES), F32),
        ],
        scratch_shapes=[pltpu.VMEM((tq, d), BF16), pltpu.VMEM((1, LANES), F32)],
        compiler_params=pltpu.CompilerParams(
            dimension_semantics=("arbitrary", "arbitrary"), vmem_limit_bytes=VMEM_LIMIT),
        name="xattn_router",
    )(x1, kv, xattn_norm, w_q, w_o, ffn_norm, w_router, b_router, tri, eye)


def _sorted_pos(bucket_ref, rank_ref, starts_ref, t):
    return starts_ref[bucket_ref[t]] + rank_ref[t]


def _item_copy(hbm, hbm_item, buf, buf_item, rows_per_item, sem, to_hbm):
    h = hbm.at[pl.ds(pl.multiple_of(hbm_item * rows_per_item, rows_per_item), rows_per_item)]
    b = buf.at[pl.ds(pl.multiple_of(buf_item * rows_per_item, rows_per_item), rows_per_item)]
    return pltpu.make_async_copy(b, h, sem) if to_hbm else pltpu.make_async_copy(h, b, sem)


def _items_wait(hbm, buf, buf_item, n_items, rows_per_item, sem, to_hbm):
    n = n_items * rows_per_item
    h = hbm.at[pl.ds(0, n)]
    b = buf.at[pl.ds(pl.multiple_of(buf_item * rows_per_item, rows_per_item), n)]
    (pltpu.make_async_copy(b, h, sem) if to_hbm else pltpu.make_async_copy(h, b, sem)).wait()


def _dispatch_kernel(bucket_ref, rank_ref, starts_ref, fill_ref, ntiles_ref,
                     x2r_hbm, xs_hbm, buf, sem):
    i = pl.program_id(0)
    nt = pl.num_programs(0)
    td = DISPATCH_TILE
    rpi = ROW_TILES
    slot = i % 2

    def row_copy(t, p, s):
        return pltpu.make_async_copy(
            x2r_hbm.at[pl.ds(pl.multiple_of(t * rpi, rpi), rpi)],
            xs_hbm.at[pl.ds(pl.multiple_of(p * rpi, rpi), rpi)], sem.at[s])

    def wait_step(s):
        pltpu.make_async_copy(x2r_hbm.at[pl.ds(0, td * rpi)], xs_hbm.at[pl.ds(0, td * rpi)],
                              sem.at[s]).wait()

    def send(r, carry):
        t = i * td + r
        row_copy(t, _sorted_pos(bucket_ref, rank_ref, starts_ref, t), slot).start()
        return carry

    lax.fori_loop(0, td, send, 0, unroll=8)

    @pl.when(i >= 1)
    def _():
        wait_step(1 - slot)

    @pl.when(i == nt - 1)
    def _():
        wait_step(slot)
        buf[...] = jnp.zeros_like(buf)
        zsem = sem.at[2]
        for b in range(N_BUCKETS):
            def fill(p, carry):
                _item_copy(xs_hbm, p, buf, 0, rpi, zsem, True).start()
                return carry

            def drain(p, carry):
                _item_copy(xs_hbm, p, buf, 0, rpi, zsem, True).wait()
                return carry

            lax.fori_loop(fill_ref[b], fill_ref[N_BUCKETS + b], fill, 0)
            lax.fori_loop(fill_ref[b], fill_ref[N_BUCKETS + b], drain, 0)

        n_tiles_total = xs_hbm.shape[0] // (MOE_TILE * rpi)

        def fill_tile(tile, carry):
            cp = pltpu.make_async_copy(
                buf.at[pl.ds(0, MOE_TILE * rpi)],
                xs_hbm.at[pl.ds(pl.multiple_of(tile * MOE_TILE * rpi, MOE_TILE * rpi),
                                MOE_TILE * rpi)], zsem)
            cp.start()
            cp.wait()
            return carry

        lax.fori_loop(ntiles_ref[0], n_tiles_total, fill_tile, 0)


def _dispatch(bucket, rank, starts, fill, ntiles, x2r, n_sorted):
    t = x2r.shape[0] // ROW_TILES
    td = DISPATCH_TILE
    grid_spec = pltpu.PrefetchScalarGridSpec(
        num_scalar_prefetch=5,
        grid=(t // td,),
        in_specs=[pl.BlockSpec(memory_space=pl.ANY)],
        out_specs=pl.BlockSpec(memory_space=pl.ANY),
        scratch_shapes=[
            pltpu.VMEM((MOE_TILE * ROW_TILES, LANES), F32),
            pltpu.SemaphoreType.DMA((3,)),
        ],
    )
    return pl.pallas_call(
        _dispatch_kernel,
        grid_spec=grid_spec,
        out_shape=jax.ShapeDtypeStruct((n_sorted * ROW_TILES, LANES), F32),
        compiler_params=pltpu.CompilerParams(
            dimension_semantics=("arbitrary",), vmem_limit_bytes=VMEM_LIMIT),
        name="moe_dispatch",
    )(bucket, rank, starts, fill, ntiles, x2r)


def _silu(x):
    return x * _sigmoid(x)


def _moe_kernel(elo_ref, ehi_ref, used_ref,
                xs_ref, gf_ref, wr_ref, br_ref, gfin_ref,
                wg_lo, wu_lo, wd_lo, wg_hi, wu_hi, wd_hi, y_ref):
    i = pl.program_id(0)
    tm = MOE_TILE

    @pl.when(used_ref[i] == 1)
    def _():
        x2 = jnp.concatenate(
            [xs_ref[pl.ds(s, tm, stride=ROW_TILES), :] for s in range(ROW_TILES)], axis=1)
        h3 = _rms(x2, gf_ref[...])
        g_lo, g_hi = _gates(_router_logits(h3, wr_ref, br_ref), elo_ref[i], ehi_ref[i])
        x = h3.astype(BF16)

        def expert(wg, wu, wd):
            hid = (_silu(_dot(x, wg[0])) * _dot(x, wu[0])).astype(BF16)
            return _dot(hid, wd[0])

        moe = g_lo * expert(wg_lo, wu_lo, wd_lo) + g_hi * expert(wg_hi, wu_hi, wd_hi)
        out = _rms(x2 + moe, gfin_ref[...])
        for s in range(ROW_TILES):
            y_ref[pl.ds(s, tm, stride=ROW_TILES), :] = out[:, s * LANES:(s + 1) * LANES]

    @pl.when(used_ref[i] == 0)
    def _():
        y_ref[...] = jnp.zeros_like(y_ref)


def _moe_sparse(e_lo, e_hi, used, xs, ffn_norm, w_router, b_router, final_norm,
                w_gate, w_up, w_down):
    tm = MOE_TILE
    nt = used.shape[0]
    d = D_MODEL
    lo = lambda i, elo, ehi, used: (elo[i], 0, 0)
    hi = lambda i, elo, ehi, used: (ehi[i], 0, 0)
    const = lambda i, *_: (0, 0)
    grid_spec = pltpu.PrefetchScalarGridSpec(
        num_scalar_prefetch=3,
        grid=(nt,),
        in_specs=[
            pl.BlockSpec((tm * ROW_TILES, LANES), lambda i, *_: (i, 0)),
            pl.BlockSpec((1, d), const),
            pl.BlockSpec((d, 2 * LANES), const),
            pl.BlockSpec((1, LANES), const),
            pl.BlockSpec((1, d), const),
            pl.BlockSpec((1, d, D_EXPERT), lo),
            pl.BlockSpec((1, d, D_EXPERT), lo),
            pl.BlockSpec((1, D_EXPERT, d), lo),
            pl.BlockSpec((1, d, D_EXPERT), hi),
            pl.BlockSpec((1, d, D_EXPERT), hi),
            pl.BlockSpec((1, D_EXPERT, d), hi),
        ],
        out_specs=pl.BlockSpec((tm * ROW_TILES, LANES), lambda i, *_: (i, 0)),
    )
    return pl.pallas_call(
        _moe_kernel,
        grid_spec=grid_spec,
        out_shape=jax.ShapeDtypeStruct((nt * tm * ROW_TILES, LANES), F32),
        compiler_params=pltpu.CompilerParams(
            dimension_semantics=("arbitrary",), vmem_limit_bytes=VMEM_LIMIT),
        name="moe_sparse",
    )(e_lo, e_hi, used, xs, ffn_norm, w_router, b_router, final_norm,
      w_gate, w_up, w_down, w_gate, w_up, w_down)


def _final_kernel(bucket_ref, rank_ref, starts_ref, y_hbm, o_ref, ybuf, sem):
    i = pl.program_id(0)
    nt = pl.num_programs(0)
    tf = FINAL_TILE

    def start(tile, slot):
        def fetch(r, carry):
            p = _sorted_pos(bucket_ref, rank_ref, starts_ref, tile * tf + r)
            _item_copy(y_hbm, p, ybuf, slot * tf + r, ROW_TILES, sem.at[slot], False).start()
            return carry

        lax.fori_loop(0, tf, fetch, 0, unroll=8)

    @pl.when(i == 0)
    def _():
        start(0, 0)

    @pl.when(i + 1 < nt)
    def _():
        start(i + 1, (i + 1) % 2)

    slot = i % 2
    _items_wait(y_hbm, ybuf, slot * tf, tf, ROW_TILES, sem.at[slot], False)
    row0 = pl.multiple_of(slot * tf * ROW_TILES, ROW_TILES)
    for s in range(ROW_TILES):
        o_ref[:, s * LANES:(s + 1) * LANES] = ybuf[pl.ds(row0 + s, tf, stride=ROW_TILES), :]


def _final(bucket, rank, starts, y, t):
    d = D_MODEL
    tf = FINAL_TILE
    grid_spec = pltpu.PrefetchScalarGridSpec(
        num_scalar_prefetch=3,
        grid=(t // tf,),
        in_specs=[pl.BlockSpec(memory_space=pl.ANY)],
        out_specs=pl.BlockSpec((tf, d), lambda i, *_: (i, 0)),
        scratch_shapes=[
            pltpu.VMEM((2 * tf * ROW_TILES, LANES), F32),
            pltpu.SemaphoreType.DMA((2,)),
        ],
    )
    return pl.pallas_call(
        _final_kernel,
        grid_spec=grid_spec,
        out_shape=jax.ShapeDtypeStruct((t, d), F32),
        compiler_params=pltpu.CompilerParams(
            dimension_semantics=("arbitrary",), vmem_limit_bytes=VMEM_LIMIT),
        name="moe_unpermute",
    )(bucket, rank, starts, y)


def _routing_tables(counts, t):
    tm = MOE_TILE
    nt = t // tm + N_BUCKETS
    cnt = counts[0, :N_BUCKETS].astype(jnp.int32)
    padded = ((cnt + tm - 1) // tm) * tm
    ends = jnp.cumsum(padded)
    starts = ends - padded
    fill = jnp.concatenate([starts + cnt, ends])
    ntiles = ends[-1:] // tm
    tile_start = jnp.arange(nt, dtype=jnp.int32) * tm
    tile_bucket = jnp.sum((ends[None, :] <= tile_start[:, None]).astype(jnp.int32), axis=1)
    used = (tile_bucket < N_BUCKETS).astype(jnp.int32)
    tile_bucket = jnp.minimum(tile_bucket, N_BUCKETS - 1)
    pair = tile_bucket % N_PAIRS
    group0 = (tile_bucket // N_PAIRS) * EXPERTS_PER_GROUP
    e_lo = group0 + jnp.array([0, 0, 0, 1, 1, 2], jnp.int32)[pair]
    e_hi = group0 + jnp.array([1, 2, 3, 2, 3, 3], jnp.int32)[pair]
    return starts, fill, ntiles, e_lo, e_hi, used, nt


def kernel(x, mem, mix_norm, w_in, conv_w, hgrn_lb, hgrn_norm, w_out, xattn_norm, mem_norm,
           w_q, w_kv, w_o, ffn_norm, w_group, b_group, w_expert, b_expert, w_gate, w_up,
           w_down, final_norm):
    bsz, seq, d = x.shape
    assert d == D_MODEL and seq % SEQ_TILE == 0 and seq % ATTN_TILE == 0
    assert (bsz * seq) % MOE_TILE == 0 and (bsz * seq) % FINAL_TILE == 0
    assert (bsz * seq) % DISPATCH_TILE == 0 and bsz * seq >= 2 * DISPATCH_TILE
    assert 2 * DISPATCH_TILE >= MOE_TILE and mix_norm.shape[0] == 1
    bf = lambda w: w.astype(BF16)

    x1 = _mixer(x, mix_norm, bf(w_in[0]), conv_w[0], hgrn_lb, hgrn_norm, bf(w_out[0]))
    kv = _kv_proj(mem, mem_norm, bf(w_kv[0]))

    pad = LANES - N_GROUPS - N_EXPERTS
    w_router = jnp.concatenate(
        [w_group[0], w_expert[0], jnp.zeros((d, pad), F32)], axis=1)
    b_router = jnp.concatenate(
        [b_group[0], b_expert[0], jnp.zeros((pad,), F32)])[None, :]
    w_router_hi = bf(w_router)
    w_router_lo = bf(w_router - w_router_hi.astype(F32))
    w_router2 = jnp.concatenate([w_router_hi, w_router_lo], axis=1)
    w_q_scaled = bf(w_q[0] * (XATTN_HEAD_DIM ** -0.5))
    x2r, brk, counts = _attention(x1, kv, xattn_norm, w_q_scaled, bf(w_o[0]), ffn_norm,
                                  w_router2, b_router)

    t = bsz * seq
    bucket = brk[0]
    rank = brk[1] * 256 + brk[2]
    starts, fill, ntiles, e_lo, e_hi, used, nt = _routing_tables(counts, t)
    xs = _dispatch(bucket, rank, starts, fill, ntiles, x2r, nt * MOE_TILE)
    y = _moe_sparse(e_lo, e_hi, used, xs, ffn_norm, w_router2, b_router, final_norm[None, :],
                    bf(w_gate[0]), bf(w_up[0]), bf(w_down[0]))
    out = _final(bucket, rank, starts, y, t)
    return out.reshape(bsz, seq, d)
```

```python
import functools

import jax
import jax.numpy as jnp
from jax import lax
from jax.experimental import pallas as pl
from jax.experimental.pallas import tpu as pltpu

F32 = jnp.float32
BF16 = jnp.bfloat16

D_MODEL = 1024
CONV_WIDTH = 512
HGRN_WIDTH = 512
HGRN_HEADS = 4
HEAD_DIM = 128
N_PROJ_SLOTS = 7
PROJ_WIDTH = N_PROJ_SLOTS * 512
XATTN_HEADS = 4
XATTN_HEAD_DIM = 256
N_GROUPS = 4
EXPERTS_PER_GROUP = 4
N_EXPERTS = 16
D_EXPERT = 512
EPS = 1e-6

LANES = 128
SUBLANES = 8
CHUNK = 64
CHUNK_LEVELS = 6
SEQ_TILE = 256
ATTN_TILE = 512
DISPATCH_TILE = 256
MOE_TILE = 256
FINAL_TILE = 256
N_PAIRS = 6
N_BUCKETS = N_GROUPS * N_PAIRS
ROW_TILES = D_MODEL // LANES
VMEM_LIMIT = 56 * 1024 * 1024


def _rms(x, g):
    return x * lax.rsqrt(jnp.mean(x * x, axis=-1, keepdims=True) + EPS) * g


def _dot(a, b):
    return jnp.dot(a, b, preferred_element_type=F32)


def _dot_nt(a, b):
    return lax.dot_general(a, b, (((1,), (1,)), ((), ())), preferred_element_type=F32)


def _dot_tn(a, b):
    return lax.dot_general(a, b, (((0,), (0,)), ((), ())), preferred_element_type=F32)


def _roll_rows(x, shift):
    return pltpu.roll(x, shift % x.shape[0], axis=0)


def _level_exponents(logf2, use_level):
    n = CHUNK // SUBLANES
    sub = lax.broadcasted_iota(jnp.int32, (SUBLANES, logf2.shape[1]), 0)
    roll = lambda x, s: pltpu.roll(x, s % SUBLANES, axis=0)
    r = [logf2[SUBLANES * j:SUBLANES * (j + 1), :] for j in range(n)]
    out = []
    for lvl in range(1, CHUNK_LEVELS + 1):
        half = 1 << (lvl - 1)
        g = [None] * n
        if half < SUBLANES:
            second = (sub & half) != 0
            for j in range(n):
                last = r[j]
                w = 1
                while w < half:
                    last = jnp.where((sub & w) != 0, last, roll(last, -w))
                    w *= 2
                tot = jnp.where(second, roll(last, half), last)
                g[j] = jnp.where(second, r[j], tot - r[j])
                r[j] = jnp.where(second, r[j] + tot, r[j])
        else:
            hv = half // SUBLANES
            for j0 in range(0, n, 2 * hv):
                mid = r[j0 + hv - 1]
                tot = jnp.broadcast_to(mid[SUBLANES - 1:SUBLANES, :], mid.shape)
                for j in range(j0, j0 + hv):
                    g[j] = tot - r[j]
                for j in range(j0 + hv, j0 + 2 * hv):
                    g[j] = r[j]
                    r[j] = r[j] + tot
        use_level(lvl, jnp.concatenate(g, axis=0))
    return jnp.concatenate(r, axis=0)


def _sigmoid(x):
    return 0.5 * jnp.tanh(0.5 * x) + 0.5


def _split_levels():
    import numpy as np
    t = np.arange(CHUNK)[:, None]
    s = np.arange(CHUNK)[None, :]
    msb = np.floor(np.log2(np.maximum(t ^ s, 1))).astype(np.int32) + 1
    return np.where(s < t, msb, np.where(s == t, 0, -1)).astype(np.int32)


def _hgrn_chunk(q, z, v, lb, st, levels):
    half_th = 0.5 * jnp.tanh(0.5 * z)
    one_m_lb = 1.0 - lb
    logf2 = jnp.log2(lb + one_m_lb * (0.5 + half_th))
    k = one_m_lb * (0.5 - half_th)

    qb = q.astype(BF16)
    kb = k.astype(BF16)
    scores = [jnp.where(levels == 0, jnp.sum(q * k, axis=-1, keepdims=True), 0.0)]

    def use_level(lvl, g):
        decay = jnp.exp2(g).astype(BF16)
        scores[0] = jnp.where(levels == lvl, _dot_nt(qb * decay, kb * decay), scores[0])

    b2 = _level_exponents(logf2, use_level)
    a = scores[0]
    b2_last = b2[CHUNK - 1:CHUNK, :]

    vb = v.astype(BF16)
    o = _dot(a.astype(BF16), vb)
    o = o + _dot_nt((q * jnp.exp2(b2)).astype(BF16), st.astype(BF16))
    kd = (k * jnp.exp2(b2_last - b2)).astype(BF16)
    st_new = st * jnp.exp2(b2_last) + _dot_tn(vb, kd)
    return o, st_new


def _mixer_kernel(x_ref, gmix_ref, win_ref, convw_ref, lbraw_ref, hnorm_ref, wout_ref, lvl_ref,
                  o_ref, p_scr, y_scr, st_scr, tail_scr):
    j = pl.program_id(1)

    @pl.when(j == 0)
    def _():
        st_scr[...] = jnp.zeros_like(st_scr)
        tail_scr[...] = jnp.zeros_like(tail_scr)

    x = x_ref[0]
    h = _rms(x, gmix_ref[...]).astype(BF16)
    p_scr[...] = _dot(h, win_ref[...])

    ts = x.shape[0]
    cb = p_scr[:, 0:CONV_WIDTH]
    u = p_scr[:, CONV_WIDTH:2 * CONV_WIDTH] * p_scr[:, 2 * CONV_WIDTH:3 * CONV_WIDTH]
    row = lax.broadcasted_iota(jnp.int32, u.shape, 0)
    prev1 = tail_scr[7:8, :]
    prev2 = tail_scr[6:7, :]
    u1 = jnp.where(row == 0, prev1, _roll_rows(u, 1))
    u2 = jnp.where(row == 0, prev2, jnp.where(row == 1, prev1, _roll_rows(u, 2)))
    cw = convw_ref[...]
    conv = u2 * cw[0:1, :] + u1 * cw[1:2, :] + u * cw[2:3, :]
    y_scr[:, 0:CONV_WIDTH] = (cb * conv).astype(BF16)
    tail_scr[...] = u[ts - 8:ts, :]

    raw = lbraw_ref[...]
    mx = jnp.max(raw, axis=0, keepdims=True)
    ex = jnp.exp(raw - mx)
    lb_all = ex[0:1, :] / jnp.sum(ex, axis=0, keepdims=True)
    hn = hnorm_ref[...]

    def chunk_body(c, carry):
        r0 = pl.multiple_of(c * CHUNK, CHUNK)
        rows = pl.ds(r0, CHUNK)
        for hd in range(HGRN_HEADS):
            lo = hd * HEAD_DIM
            sl = slice(lo, lo + HEAD_DIM)
            q = p_scr[rows, 3 * 512 + lo:3 * 512 + lo + HEAD_DIM]
            z = p_scr[rows, 4 * 512 + lo:4 * 512 + lo + HEAD_DIM]
            v = p_scr[rows, 5 * 512 + lo:5 * 512 + lo + HEAD_DIM]
            g = p_scr[rows, 6 * 512 + lo:6 * 512 + lo + HEAD_DIM]
            o, st_new = _hgrn_chunk(q, z, v, lb_all[:, sl], st_scr[hd], lvl_ref[...])
            st_scr[hd] = st_new
            o = o * lax.rsqrt(jnp.mean(o * o, axis=-1, keepdims=True) + EPS) * hn[:, sl]
            y_scr[rows, CONV_WIDTH + lo:CONV_WIDTH + lo + HEAD_DIM] = \
                (o * (g * _sigmoid(g))).astype(BF16)
        return carry

    lax.fori_loop(0, ts // CHUNK, chunk_body, 0, unroll=2)

    o_ref[0] = x + _dot(y_scr[...], wout_ref[...])


def _mixer(x, mix_norm, w_in, conv_w, hgrn_lb, hgrn_norm, w_out):
    bsz, seq, d = x.shape
    ts = SEQ_TILE
    const = lambda b, j: (0, 0)
    return pl.pallas_call(
        _mixer_kernel,
        grid=(bsz, seq // ts),
        in_specs=[
            pl.BlockSpec((1, ts, d), lambda b, j: (b, j, 0)),
            pl.BlockSpec((1, d), const),
            pl.BlockSpec((d, PROJ_WIDTH), const),
            pl.BlockSpec((3, CONV_WIDTH), const),
            pl.BlockSpec((2, HGRN_WIDTH), const),
            pl.BlockSpec((1, HGRN_WIDTH), const),
            pl.BlockSpec((d, d), const),
            pl.BlockSpec((CHUNK, CHUNK), const),
        ],
        out_specs=pl.BlockSpec((1, ts, d), lambda b, j: (b, j, 0)),
        out_shape=jax.ShapeDtypeStruct((bsz, seq, d), F32),
        scratch_shapes=[
            pltpu.VMEM((ts, PROJ_WIDTH), F32),
            pltpu.VMEM((ts, d), BF16),
            pltpu.VMEM((HGRN_HEADS, HEAD_DIM, HEAD_DIM), F32),
            pltpu.VMEM((8, CONV_WIDTH), F32),
        ],
        compiler_params=pltpu.CompilerParams(
            dimension_semantics=("arbitrary", "arbitrary"), vmem_limit_bytes=VMEM_LIMIT),
        name="mixer",
    )(x, mix_norm, w_in, conv_w, hgrn_lb, hgrn_norm, w_out, jnp.asarray(_split_levels()))


def _kv_kernel(m_ref, g_ref, w_ref, o_ref):
    h = _rms(m_ref[0], g_ref[...]).astype(BF16)
    o_ref[0] = _dot(h, w_ref[...]).astype(BF16)


def _kv_proj(mem, mem_norm, w_kv):
    bsz, mlen, d = mem.shape
    const = lambda b: (0, 0)
    return pl.pallas_call(
        _kv_kernel,
        grid=(bsz,),
        in_specs=[
            pl.BlockSpec((1, mlen, d), lambda b: (b, 0, 0)),
            pl.BlockSpec((1, d), const),
            pl.BlockSpec((d, 2 * d), const),
        ],
        out_specs=pl.BlockSpec((1, mlen, 2 * d), lambda b: (b, 0, 0)),
        out_shape=jax.ShapeDtypeStruct((bsz, mlen, 2 * d), BF16),
        compiler_params=pltpu.CompilerParams(
            dimension_semantics=("arbitrary",), vmem_limit_bytes=VMEM_LIMIT),
        name="kv_proj",
    )(mem, mem_norm, w_kv)


def _first_argmax(vals, lane):
    mx = jnp.max(vals, axis=-1, keepdims=True)
    idx = jnp.min(jnp.where(vals == mx, lane, float(LANES)), axis=-1, keepdims=True)
    return mx, idx


def _router_logits(h3, wr_ref, br_ref):
    h3_hi = h3.astype(BF16)
    h3_lo = (h3 - h3_hi.astype(F32)).astype(BF16)
    two = _dot(h3_hi, wr_ref[...])
    return (two[:, :LANES] + two[:, LANES:]) + _dot(h3_lo, wr_ref[:, :LANES]) + br_ref[...]


def _route(logits, running, tri):
    lane = lax.broadcasted_iota(jnp.int32, logits.shape, 1).astype(F32)
    neg = jnp.float32(-jnp.inf)
    gl = jnp.where(lane < N_GROUPS, logits, neg)
    _, gidx = _first_argmax(gl, lane)
    base = N_GROUPS + EXPERTS_PER_GROUP * gidx
    el = jnp.where((lane >= base) & (lane < base + EXPERTS_PER_GROUP), logits, neg)
    _, i1 = _first_argmax(el, lane)
    _, i2 = _first_argmax(jnp.where(lane == i1, neg, el), lane)
    lo = jnp.minimum(i1, i2) - base
    hi = jnp.maximum(i1, i2) - base
    bucket = gidx * N_PAIRS + (lo * (7.0 - lo)) * 0.5 + hi - lo - 1.0

    onehot = lane == bucket
    before = _dot(tri, onehot.astype(BF16))
    rank = jnp.sum(jnp.where(onehot, before + running, 0.0), axis=-1, keepdims=True)
    running = running + jnp.sum(onehot.astype(F32), axis=0, keepdims=True)
    rank_hi = jnp.floor(rank * (1.0 / 256.0))
    rank_lo = rank - 256.0 * rank_hi
    info = jnp.where(lane == 0, bucket, jnp.where(lane == 1, rank_hi,
                                                  jnp.where(lane == 2, rank_lo, 0.0)))
    return info, running


def _gates(logits, e_lo, e_hi):
    lane = lax.broadcasted_iota(jnp.int32, logits.shape, 1)
    neg = jnp.float32(-jnp.inf)
    gl = jnp.where(lane < N_GROUPS, logits, neg)
    g_p = 1.0 / jnp.sum(jnp.exp(gl - jnp.max(gl, axis=-1, keepdims=True)), axis=-1, keepdims=True)
    l_lo = jnp.sum(jnp.where(lane == N_GROUPS + e_lo, logits, 0.0), axis=-1, keepdims=True)
    l_hi = jnp.sum(jnp.where(lane == N_GROUPS + e_hi, logits, 0.0), axis=-1, keepdims=True)
    m = jnp.maximum(l_lo, l_hi)
    p_lo = jnp.exp(l_lo - m)
    p_hi = jnp.exp(l_hi - m)
    inv = g_p / (p_lo + p_hi)
    return p_lo * inv, p_hi * inv


def _attn_kernel(x_ref, kv_ref, gx_ref, wq_ref, wo_ref, gf_ref, wr_ref, br_ref, tri_ref, eye_ref,
                 x2r_ref, brk_ref, counts_ref, o_scr, cnt_scr):
    @pl.when((pl.program_id(0) == 0) & (pl.program_id(1) == 0))
    def _():
        cnt_scr[...] = jnp.zeros_like(cnt_scr)

    x = x_ref[0]
    tq = x.shape[0]
    h = _rms(x, gx_ref[...]).astype(BF16)
    q = _dot(h, wq_ref[...])
    for hd in range(XATTN_HEADS):
        lo = hd * XATTN_HEAD_DIM
        qh = q[:, lo:lo + XATTN_HEAD_DIM].astype(BF16)
        kh = kv_ref[0, :, lo:lo + XATTN_HEAD_DIM]
        vh = kv_ref[0, :, D_MODEL + lo:D_MODEL + lo + XATTN_HEAD_DIM]
        s = _dot_nt(qh, kh)
        p = jnp.exp(s - jnp.max(s, axis=-1, keepdims=True))
        inv = 1.0 / jnp.sum(p, axis=-1, keepdims=True)
        o_scr[:, lo:lo + XATTN_HEAD_DIM] = (_dot(p.astype(BF16), vh) * inv).astype(BF16)
    x2 = x + _dot(o_scr[...], wo_ref[...])
    for s in range(ROW_TILES):
        x2r_ref[pl.ds(s, tq, stride=ROW_TILES), :] = x2[:, s * LANES:(s + 1) * LANES]
    logits = _router_logits(_rms(x2, gf_ref[...]), wr_ref, br_ref)
    info, running = _route(logits, cnt_scr[...], tri_ref[...])
    cnt_scr[...] = running
    counts_ref[...] = running
    brk_ref[...] = _dot_tn(info.astype(BF16), eye_ref[...])[0:8, :].astype(jnp.int32)


def _attention(x1, kv, xattn_norm, w_q, w_o, ffn_norm, w_router, b_router):
    bsz, seq, d = x1.shape
    mlen = kv.shape[1]
    tq = ATTN_TILE
    nj = seq // tq
    const = lambda b, j: (0, 0)
    tile = lambda b, j: (b, j, 0)
    tri = jnp.tri(tq, tq, -1, dtype=BF16)
    eye = jnp.eye(tq, dtype=BF16)
    return pl.pallas_call(
        _attn_kernel,
        grid=(bsz, nj),
        in_specs=[
            pl.BlockSpec((1, tq, d), tile),
            pl.BlockSpec((1, mlen, 2 * d), lambda b, j: (b, 0, 0)),
            pl.BlockSpec((1, d), const),
            pl.BlockSpec((d, d), const),
            pl.BlockSpec((d, d), const),
            pl.BlockSpec((1, d), const),
            pl.BlockSpec((d, 2 * LANES), const),
            pl.BlockSpec((1, LANES), const),
            pl.BlockSpec((tq, tq), const),
            pl.BlockSpec((tq, tq), const),
        ],
        out_specs=[
            pl.BlockSpec((tq * ROW_TILES, LANES), lambda b, j: (b * nj + j, 0)),
            pl.BlockSpec((8, tq), lambda b, j: (0, b * nj + j)),
            pl.BlockSpec((1, LANES), const),
        ],
        out_shape=[
            jax.ShapeDtypeStruct((bsz * seq * ROW_TILES, LANES), F32),
            jax.ShapeDtypeStruct((8, bsz * seq), jnp.int32),
            jax.ShapeDtypeStruct((1, LANES), F32),
        ],
        scratch_shapes=[pltpu.VMEM((tq, d), BF16), pltpu.VMEM((1, LANES), F32)],
        compiler_params=pltpu.CompilerParams(
            dimension_semantics=("arbitrary", "arbitrary"), vmem_limit_bytes=VMEM_LIMIT),
        name="xattn_router",
    )(x1, kv, xattn_norm, w_q, w_o, ffn_norm, w_router, b_router, tri, eye)


def _sorted_pos(bucket_ref, rank_ref, starts_ref, t):
    return starts_ref[bucket_ref[t]] + rank_ref[t]


def _item_copy(hbm, hbm_item, buf, buf_item, rows_per_item, sem, to_hbm):
    h = hbm.at[pl.ds(pl.multiple_of(hbm_item * rows_per_item, rows_per_item), rows_per_item)]
    b = buf.at[pl.ds(pl.multiple_of(buf_item * rows_per_item, rows_per_item), rows_per_item)]
    return pltpu.make_async_copy(b, h, sem) if to_hbm else pltpu.make_async_copy(h, b, sem)


def _items_wait(hbm, buf, buf_item, n_items, rows_per_item, sem, to_hbm):
    n = n_items * rows_per_item
    h = hbm.at[pl.ds(0, n)]
    b = buf.at[pl.ds(pl.multiple_of(buf_item * rows_per_item, rows_per_item), n)]
    (pltpu.make_async_copy(b, h, sem) if to_hbm else pltpu.make_async_copy(h, b, sem)).wait()


def _dispatch_kernel(bucket_ref, rank_ref, starts_ref, fill_ref, ntiles_ref,
                     x2r_ref, xs_hbm, buf, sem):
    i = pl.program_id(0)
    nt = pl.num_programs(0)
    td = DISPATCH_TILE
    rpi = ROW_TILES
    slot = i % 2
    base = slot * td

    @pl.when(i >= 2)
    def _():
        _items_wait(xs_hbm, buf, base, td, rpi, sem.at[slot], True)

    buf[pl.ds(pl.multiple_of(base * rpi, td * rpi), td * rpi), :] = x2r_ref[...]

    def send(r, carry):
        p = _sorted_pos(bucket_ref, rank_ref, starts_ref, i * td + r)
        _item_copy(xs_hbm, p, buf, base + r, rpi, sem.at[slot], True).start()
        return carry

    lax.fori_loop(0, td, send, 0, unroll=8)

    @pl.when(i == nt - 1)
    def _():
        _items_wait(xs_hbm, buf, base, td, rpi, sem.at[slot], True)
        _items_wait(xs_hbm, buf, (1 - slot) * td, td, rpi, sem.at[1 - slot], True)
        buf[...] = jnp.zeros_like(buf)
        zsem = sem.at[2]
        for b in range(N_BUCKETS):
            def fill(p, carry):
                _item_copy(xs_hbm, p, buf, 0, rpi, zsem, True).start()
                return carry

            def drain(p, carry):
                _item_copy(xs_hbm, p, buf, 0, rpi, zsem, True).wait()
                return carry

            lax.fori_loop(fill_ref[b], fill_ref[N_BUCKETS + b], fill, 0)
            lax.fori_loop(fill_ref[b], fill_ref[N_BUCKETS + b], drain, 0)

        n_tiles_total = xs_hbm.shape[0] // (MOE_TILE * rpi)

        def fill_tile(tile, carry):
            cp = pltpu.make_async_copy(
                buf.at[pl.ds(0, MOE_TILE * rpi)],
                xs_hbm.at[pl.ds(pl.multiple_of(tile * MOE_TILE * rpi, MOE_TILE * rpi),
                                MOE_TILE * rpi)], zsem)
            cp.start()
            cp.wait()
            return carry

        lax.fori_loop(ntiles_ref[0], n_tiles_total, fill_tile, 0)


def _dispatch(bucket, rank, starts, fill, ntiles, x2r, n_sorted):
    t = x2r.shape[0] // ROW_TILES
    td = DISPATCH_TILE
    grid_spec = pltpu.PrefetchScalarGridSpec(
        num_scalar_prefetch=5,
        grid=(t // td,),
        in_specs=[pl.BlockSpec((td * ROW_TILES, LANES), lambda i, *_: (i, 0))],
        out_specs=pl.BlockSpec(memory_space=pl.ANY),
        scratch_shapes=[
            pltpu.VMEM((2 * td * ROW_TILES, LANES), F32),
            pltpu.SemaphoreType.DMA((3,)),
        ],
    )
    return pl.pallas_call(
        _dispatch_kernel,
        grid_spec=grid_spec,
        out_shape=jax.ShapeDtypeStruct((n_sorted * ROW_TILES, LANES), F32),
        compiler_params=pltpu.CompilerParams(
            dimension_semantics=("arbitrary",), vmem_limit_bytes=VMEM_LIMIT),
        name="moe_dispatch",
    )(bucket, rank, starts, fill, ntiles, x2r)


def _silu(x):
    return x * _sigmoid(x)


def _moe_kernel(elo_ref, ehi_ref, used_ref,
                xs_ref, gf_ref, wr_ref, br_ref, gfin_ref,
                wg_lo, wu_lo, wd_lo, wg_hi, wu_hi, wd_hi, y_ref):
    i = pl.program_id(0)
    tm = MOE_TILE

    @pl.when(used_ref[i] == 1)
    def _():
        x2 = jnp.concatenate(
            [xs_ref[pl.ds(s, tm, stride=ROW_TILES), :] for s in range(ROW_TILES)], axis=1)
        h3 = _rms(x2, gf_ref[...])
        g_lo, g_hi = _gates(_router_logits(h3, wr_ref, br_ref), elo_ref[i], ehi_ref[i])
        x = h3.astype(BF16)

        def expert(wg, wu, wd):
            hid = (_silu(_dot(x, wg[0])) * _dot(x, wu[0])).astype(BF16)
            return _dot(hid, wd[0])

        moe = g_lo * expert(wg_lo, wu_lo, wd_lo) + g_hi * expert(wg_hi, wu_hi, wd_hi)
        out = _rms(x2 + moe, gfin_ref[...])
        for s in range(ROW_TILES):
            y_ref[pl.ds(s, tm, stride=ROW_TILES), :] = out[:, s * LANES:(s + 1) * LANES]

    @pl.when(used_ref[i] == 0)
    def _():
        y_ref[...] = jnp.zeros_like(y_ref)


def _moe_sparse(e_lo, e_hi, used, xs, ffn_norm, w_router, b_router, final_norm,
                w_gate, w_up, w_down):
    tm = MOE_TILE
    nt = used.shape[0]
    d = D_MODEL
    lo = lambda i, elo, ehi, used: (elo[i], 0, 0)
    hi = lambda i, elo, ehi, used: (ehi[i], 0, 0)
    const = lambda i, *_: (0, 0)
    grid_spec = pltpu.PrefetchScalarGridSpec(
        num_scalar_prefetch=3,
        grid=(nt,),
        in_specs=[
            pl.BlockSpec((tm * ROW_TILES, LANES), lambda i, *_: (i, 0)),
            pl.BlockSpec((1, d), const),
            pl.BlockSpec((d, 2 * LANES), const),
            pl.BlockSpec((1, LANES), const),
            pl.BlockSpec((1, d), const),
            pl.BlockSpec((1, d, D_EXPERT), lo),
            pl.BlockSpec((1, d, D_EXPERT), lo),
            pl.BlockSpec((1, D_EXPERT, d), lo),
            pl.BlockSpec((1, d, D_EXPERT), hi),
            pl.BlockSpec((1, d, D_EXPERT), hi),
            pl.BlockSpec((1, D_EXPERT, d), hi),
        ],
        out_specs=pl.BlockSpec((tm * ROW_TILES, LANES), lambda i, *_: (i, 0)),
    )
    return pl.pallas_call(
        _moe_kernel,
        grid_spec=grid_spec,
        out_shape=jax.ShapeDtypeStruct((nt * tm * ROW_TILES, LANES), F32),
        compiler_params=pltpu.CompilerParams(
            dimension_semantics=("arbitrary",), vmem_limit_bytes=VMEM_LIMIT),
        name="moe_sparse",
    )(e_lo, e_hi, used, xs, ffn_norm, w_router, b_router, final_norm,
      w_gate, w_up, w_down, w_gate, w_up, w_down)


def _final_kernel(bucket_ref, rank_ref, starts_ref, y_hbm, o_ref, ybuf, sem):
    i = pl.program_id(0)
    nt = pl.num_programs(0)
    tf = FINAL_TILE

    def start(tile, slot):
        def fetch(r, carry):
            p = _sorted_pos(bucket_ref, rank_ref, starts_ref, tile * tf + r)
            _item_copy(y_hbm, p, ybuf, slot * tf + r, ROW_TILES, sem.at[slot], False).start()
            return carry

        lax.fori_loop(0, tf, fetch, 0, unroll=8)

    @pl.when(i == 0)
    def _():
        start(0, 0)

    @pl.when(i + 1 < nt)
    def _():
        start(i + 1, (i + 1) % 2)

    slot = i % 2
    _items_wait(y_hbm, ybuf, slot * tf, tf, ROW_TILES, sem.at[slot], False)
    row0 = pl.multiple_of(slot * tf * ROW_TILES, ROW_TILES)
    for s in range(ROW_TILES):
        o_ref[:, s * LANES:(s + 1) * LANES] = ybuf[pl.ds(row0 + s, tf, stride=ROW_TILES), :]


def _final(bucket, rank, starts, y, t):
    d = D_MODEL
    tf = FINAL_TILE
    grid_spec = pltpu.PrefetchScalarGridSpec(
        num_scalar_prefetch=3,
        grid=(t // tf,),
        in_specs=[pl.BlockSpec(memory_space=pl.ANY)],
        out_specs=pl.BlockSpec((tf, d), lambda i, *_: (i, 0)),
        scratch_shapes=[
            pltpu.VMEM((2 * tf * ROW_TILES, LANES), F32),
            pltpu.SemaphoreType.DMA((2,)),
        ],
    )
    return pl.pallas_call(
        _final_kernel,
        grid_spec=grid_spec,
        out_shape=jax.ShapeDtypeStruct((t, d), F32),
        compiler_params=pltpu.CompilerParams(
            dimension_semantics=("arbitrary",), vmem_limit_bytes=VMEM_LIMIT),
        name="moe_unpermute",
    )(bucket, rank, starts, y)


def _routing_tables(counts, t):
    tm = MOE_TILE
    nt = t // tm + N_BUCKETS
    cnt = counts[0, :N_BUCKETS].astype(jnp.int32)
    padded = ((cnt + tm - 1) // tm) * tm
    ends = jnp.cumsum(padded)
    starts = ends - padded
    fill = jnp.concatenate([starts + cnt, ends])
    ntiles = ends[-1:] // tm
    tile_start = jnp.arange(nt, dtype=jnp.int32) * tm
    tile_bucket = jnp.sum((ends[None, :] <= tile_start[:, None]).astype(jnp.int32), axis=1)
    used = (tile_bucket < N_BUCKETS).astype(jnp.int32)
    tile_bucket = jnp.minimum(tile_bucket, N_BUCKETS - 1)
    pair = tile_bucket % N_PAIRS
    group0 = (tile_bucket // N_PAIRS) * EXPERTS_PER_GROUP
    e_lo = group0 + jnp.array([0, 0, 0, 1, 1, 2], jnp.int32)[pair]
    e_hi = group0 + jnp.array([1, 2, 3, 2, 3, 3], jnp.int32)[pair]
    return starts, fill, ntiles, e_lo, e_hi, used, nt


def kernel(x, mem, mix_norm, w_in, conv_w, hgrn_lb, hgrn_norm, w_out, xattn_norm, mem_norm,
           w_q, w_kv, w_o, ffn_norm, w_group, b_group, w_expert, b_expert, w_gate, w_up,
           w_down, final_norm):
    bsz, seq, d = x.shape
    assert d == D_MODEL and seq % SEQ_TILE == 0 and seq % ATTN_TILE == 0
    assert (bsz * seq) % MOE_TILE == 0 and (bsz * seq) % FINAL_TILE == 0
    assert (bsz * seq) % DISPATCH_TILE == 0 and bsz * seq >= 2 * DISPATCH_TILE
    assert 2 * DISPATCH_TILE >= MOE_TILE and mix_norm.shape[0] == 1
    bf = lambda w: w.astype(BF16)

    x1 = _mixer(x, mix_norm, bf(w_in[0]), conv_w[0], hgrn_lb, hgrn_norm, bf(w_out[0]))
    kv = _kv_proj(mem, mem_norm, bf(w_kv[0]))

    pad = LANES - N_GROUPS - N_EXPERTS
    w_router = jnp.concatenate(
        [w_group[0], w_expert[0], jnp.zeros((d, pad), F32)], axis=1)
    b_router = jnp.concatenate(
        [b_group[0], b_expert[0], jnp.zeros((pad,), F32)])[None, :]
    w_router_hi = bf(w_router)
    w_router_lo = bf(w_router - w_router_hi.astype(F32))
    w_router2 = jnp.concatenate([w_router_hi, w_router_lo], axis=1)
    w_q_scaled = bf(w_q[0] * (XATTN_HEAD_DIM ** -0.5))
    x2r, brk, counts = _attention(x1, kv, xattn_norm, w_q_scaled, bf(w_o[0]), ffn_norm,
                                  w_router2, b_router)

    t = bsz * seq
    bucket = brk[0]
    rank = brk[1] * 256 + brk[2]
    starts, fill, ntiles, e_lo, e_hi, used, nt = _routing_tables(counts, t)
    xs = _dispatch(bucket, rank, starts, fill, ntiles, x2r, nt * MOE_TILE)
    y = _moe_sparse(e_lo, e_hi, used, xs, ffn_norm, w_router2, b_router, final_norm[None, :],
                    bf(w_gate[0]), bf(w_up[0]), bf(w_down[0]))
    out = _final(bucket, rank, starts, y, t)
    return out.reshape(bsz, seq, d)
```

```python
import functools

import jax
import jax.numpy as jnp
from jax import lax
from jax.experimental import pallas as pl
from jax.experimental.pallas import tpu as pltpu

F32 = jnp.float32
BF16 = jnp.bfloat16

D_MODEL = 1024
CONV_WIDTH = 512
HGRN_WIDTH = 512
HGRN_HEADS = 4
HEAD_DIM = 128
N_PROJ_SLOTS = 7
PROJ_WIDTH = N_PROJ_SLOTS * 512
XATTN_HEADS = 4
XATTN_HEAD_DIM = 256
N_GROUPS = 4
EXPERTS_PER_GROUP = 4
N_EXPERTS = 16
D_EXPERT = 512
EPS = 1e-6

LANES = 128
SUBLANES = 8
CHUNK = 64
CHUNK_LEVELS = 6
SEQ_TILE = 256
ATTN_TILE = 512
DISPATCH_TILE = 256
MOE_TILE = 256
FINAL_TILE = 256
N_PAIRS = 6
N_BUCKETS = N_GROUPS * N_PAIRS
ROW_TILES = D_MODEL // LANES
DMA_QUEUES = 2
VMEM_LIMIT = 56 * 1024 * 1024


def _rms(x, g):
    return x * lax.rsqrt(jnp.mean(x * x, axis=-1, keepdims=True) + EPS) * g


def _dot(a, b):
    return jnp.dot(a, b, preferred_element_type=F32)


def _dot_nt(a, b):
    return lax.dot_general(a, b, (((1,), (1,)), ((), ())), preferred_element_type=F32)


def _dot_tn(a, b):
    return lax.dot_general(a, b, (((0,), (0,)), ((), ())), preferred_element_type=F32)


def _roll_rows(x, shift):
    return pltpu.roll(x, shift % x.shape[0], axis=0)


def _level_exponents(logf2, use_level):
    n = CHUNK // SUBLANES
    sub = lax.broadcasted_iota(jnp.int32, (SUBLANES, logf2.shape[1]), 0)
    roll = lambda x, s: pltpu.roll(x, s % SUBLANES, axis=0)
    r = [logf2[SUBLANES * j:SUBLANES * (j + 1), :] for j in range(n)]
    out = []
    for lvl in range(1, CHUNK_LEVELS + 1):
        half = 1 << (lvl - 1)
        g = [None] * n
        if half < SUBLANES:
            second = (sub & half) != 0
            for j in range(n):
                last = r[j]
                w = 1
                while w < half:
                    last = jnp.where((sub & w) != 0, last, roll(last, -w))
                    w *= 2
                tot = jnp.where(second, roll(last, half), last)
                g[j] = jnp.where(second, r[j], tot - r[j])
                r[j] = jnp.where(second, r[j] + tot, r[j])
        else:
            hv = half // SUBLANES
            for j0 in range(0, n, 2 * hv):
                mid = r[j0 + hv - 1]
                tot = jnp.broadcast_to(mid[SUBLANES - 1:SUBLANES, :], mid.shape)
                for j in range(j0, j0 + hv):
                    g[j] = tot - r[j]
                for j in range(j0 + hv, j0 + 2 * hv):
                    g[j] = r[j]
                    r[j] = r[j] + tot
        use_level(lvl, jnp.concatenate(g, axis=0))
    return jnp.concatenate(r, axis=0)


def _sigmoid(x):
    return 0.5 * jnp.tanh(0.5 * x) + 0.5


def _split_levels():
    import numpy as np
    t = np.arange(CHUNK)[:, None]
    s = np.arange(CHUNK)[None, :]
    msb = np.floor(np.log2(np.maximum(t ^ s, 1))).astype(np.int32) + 1
    return np.where(s < t, msb, np.where(s == t, 0, -1)).astype(np.int32)


def _hgrn_chunk(q, z, v, lb, st, levels):
    half_th = 0.5 * jnp.tanh(0.5 * z)
    one_m_lb = 1.0 - lb
    logf2 = jnp.log2(lb + one_m_lb * (0.5 + half_th))
    k = one_m_lb * (0.5 - half_th)

    qb = q.astype(BF16)
    kb = k.astype(BF16)
    scores = [jnp.where(levels == 0, jnp.sum(q * k, axis=-1, keepdims=True), 0.0)]

    def use_level(lvl, g):
        decay = jnp.exp2(g).astype(BF16)
        scores[0] = jnp.where(levels == lvl, _dot_nt(qb * decay, kb * decay), scores[0])

    b2 = _level_exponents(logf2, use_level)
    a = scores[0]
    b2_last = b2[CHUNK - 1:CHUNK, :]

    vb = v.astype(BF16)
    o = _dot(a.astype(BF16), vb)
    o = o + _dot_nt((q * jnp.exp2(b2)).astype(BF16), st.astype(BF16))
    kd = (k * jnp.exp2(b2_last - b2)).astype(BF16)
    st_new = st * jnp.exp2(b2_last) + _dot_tn(vb, kd)
    return o, st_new


def _mixer_kernel(x_ref, gmix_ref, win_ref, convw_ref, lbraw_ref, hnorm_ref, wout_ref, lvl_ref,
                  o_ref, p_scr, y_scr, st_scr, tail_scr):
    j = pl.program_id(1)

    @pl.when(j == 0)
    def _():
        st_scr[...] = jnp.zeros_like(st_scr)
        tail_scr[...] = jnp.zeros_like(tail_scr)

    x = x_ref[0]
    h = _rms(x, gmix_ref[...]).astype(BF16)
    p_scr[...] = _dot(h, win_ref[...])

    ts = x.shape[0]
    cb = p_scr[:, 0:CONV_WIDTH]
    u = p_scr[:, CONV_WIDTH:2 * CONV_WIDTH] * p_scr[:, 2 * CONV_WIDTH:3 * CONV_WIDTH]
    row = lax.broadcasted_iota(jnp.int32, u.shape, 0)
    prev1 = tail_scr[7:8, :]
    prev2 = tail_scr[6:7, :]
    u1 = jnp.where(row == 0, prev1, _roll_rows(u, 1))
    u2 = jnp.where(row == 0, prev2, jnp.where(row == 1, prev1, _roll_rows(u, 2)))
    cw = convw_ref[...]
    conv = u2 * cw[0:1, :] + u1 * cw[1:2, :] + u * cw[2:3, :]
    y_scr[:, 0:CONV_WIDTH] = (cb * conv).astype(BF16)
    tail_scr[...] = u[ts - 8:ts, :]

    raw = lbraw_ref[...]
    mx = jnp.max(raw, axis=0, keepdims=True)
    ex = jnp.exp(raw - mx)
    lb_all = ex[0:1, :] / jnp.sum(ex, axis=0, keepdims=True)
    hn = hnorm_ref[...]

    def chunk_body(c, carry):
        r0 = pl.multiple_of(c * CHUNK, CHUNK)
        rows = pl.ds(r0, CHUNK)
        for hd in range(HGRN_HEADS):
            lo = hd * HEAD_DIM
            sl = slice(lo, lo + HEAD_DIM)
            q = p_scr[rows, 3 * 512 + lo:3 * 512 + lo + HEAD_DIM]
            z = p_scr[rows, 4 * 512 + lo:4 * 512 + lo + HEAD_DIM]
            v = p_scr[rows, 5 * 512 + lo:5 * 512 + lo + HEAD_DIM]
            g = p_scr[rows, 6 * 512 + lo:6 * 512 + lo + HEAD_DIM]
            o, st_new = _hgrn_chunk(q, z, v, lb_all[:, sl], st_scr[hd], lvl_ref[...])
            st_scr[hd] = st_new
            o = o * lax.rsqrt(jnp.mean(o * o, axis=-1, keepdims=True) + EPS) * hn[:, sl]
            y_scr[rows, CONV_WIDTH + lo:CONV_WIDTH + lo + HEAD_DIM] = \
                (o * (g * _sigmoid(g))).astype(BF16)
        return carry

    lax.fori_loop(0, ts // CHUNK, chunk_body, 0, unroll=True)

    o_ref[0] = x + _dot(y_scr[...], wout_ref[...])


def _mixer(x, mix_norm, w_in, conv_w, hgrn_lb, hgrn_norm, w_out):
    bsz, seq, d = x.shape
    ts = SEQ_TILE
    const = lambda b, j: (0, 0)
    return pl.pallas_call(
        _mixer_kernel,
        grid=(bsz, seq // ts),
        in_specs=[
            pl.BlockSpec((1, ts, d), lambda b, j: (b, j, 0)),
            pl.BlockSpec((1, d), const),
            pl.BlockSpec((d, PROJ_WIDTH), const),
            pl.BlockSpec((3, CONV_WIDTH), const),
            pl.BlockSpec((2, HGRN_WIDTH), const),
            pl.BlockSpec((1, HGRN_WIDTH), const),
            pl.BlockSpec((d, d), const),
            pl.BlockSpec((CHUNK, CHUNK), const),
        ],
        out_specs=pl.BlockSpec((1, ts, d), lambda b, j: (b, j, 0)),
        out_shape=jax.ShapeDtypeStruct((bsz, seq, d), F32),
        scratch_shapes=[
            pltpu.VMEM((ts, PROJ_WIDTH), F32),
            pltpu.VMEM((ts, d), BF16),
            pltpu.VMEM((HGRN_HEADS, HEAD_DIM, HEAD_DIM), F32),
            pltpu.VMEM((8, CONV_WIDTH), F32),
        ],
        compiler_params=pltpu.CompilerParams(
            dimension_semantics=("arbitrary", "arbitrary"), vmem_limit_bytes=VMEM_LIMIT),
        name="mixer",
    )(x, mix_norm, w_in, conv_w, hgrn_lb, hgrn_norm, w_out, jnp.asarray(_split_levels()))


def _kv_kernel(m_ref, g_ref, w_ref, o_ref):
    h = _rms(m_ref[0], g_ref[...]).astype(BF16)
    o_ref[0] = _dot(h, w_ref[...]).astype(BF16)


def _kv_proj(mem, mem_norm, w_kv):
    bsz, mlen, d = mem.shape
    const = lambda b: (0, 0)
    return pl.pallas_call(
        _kv_kernel,
        grid=(bsz,),
        in_specs=[
            pl.BlockSpec((1, mlen, d), lambda b: (b, 0, 0)),
            pl.BlockSpec((1, d), const),
            pl.BlockSpec((d, 2 * d), const),
        ],
        out_specs=pl.BlockSpec((1, mlen, 2 * d), lambda b: (b, 0, 0)),
        out_shape=jax.ShapeDtypeStruct((bsz, mlen, 2 * d), BF16),
        compiler_params=pltpu.CompilerParams(
            dimension_semantics=("arbitrary",), vmem_limit_bytes=VMEM_LIMIT),
        name="kv_proj",
    )(mem, mem_norm, w_kv)


def _first_argmax(vals, lane):
    mx = jnp.max(vals, axis=-1, keepdims=True)
    idx = jnp.min(jnp.where(vals == mx, lane, float(LANES)), axis=-1, keepdims=True)
    return mx, idx


def _router_logits(h3, wr_ref, br_ref):
    h3_hi = h3.astype(BF16)
    h3_lo = (h3 - h3_hi.astype(F32)).astype(BF16)
    two = _dot(h3_hi, wr_ref[...])
    return (two[:, :LANES] + two[:, LANES:]) + _dot(h3_lo, wr_ref[:, :LANES]) + br_ref[...]


def _route(logits, running, tri):
    lane = lax.broadcasted_iota(jnp.int32, logits.shape, 1).astype(F32)
    neg = jnp.float32(-jnp.inf)
    gl = jnp.where(lane < N_GROUPS, logits, neg)
    _, gidx = _first_argmax(gl, lane)
    base = N_GROUPS + EXPERTS_PER_GROUP * gidx
    el = jnp.where((lane >= base) & (lane < base + EXPERTS_PER_GROUP), logits, neg)
    _, i1 = _first_argmax(el, lane)
    _, i2 = _first_argmax(jnp.where(lane == i1, neg, el), lane)
    lo = jnp.minimum(i1, i2) - base
    hi = jnp.maximum(i1, i2) - base
    bucket = gidx * N_PAIRS + (lo * (7.0 - lo)) * 0.5 + hi - lo - 1.0

    onehot = lane == bucket
    before = _dot(tri, onehot.astype(BF16))
    rank = jnp.sum(jnp.where(onehot, before + running, 0.0), axis=-1, keepdims=True)
    running = running + jnp.sum(onehot.astype(F32), axis=0, keepdims=True)
    rank_hi = jnp.floor(rank * (1.0 / 256.0))
    rank_lo = rank - 256.0 * rank_hi
    info = jnp.where(lane == 0, bucket, jnp.where(lane == 1, rank_hi,
                                                  jnp.where(lane == 2, rank_lo, 0.0)))
    return info, running


def _gates(logits, e_lo, e_hi):
    lane = lax.broadcasted_iota(jnp.int32, logits.shape, 1)
    neg = jnp.float32(-jnp.inf)
    gl = jnp.where(lane < N_GROUPS, logits, neg)
    g_p = 1.0 / jnp.sum(jnp.exp(gl - jnp.max(gl, axis=-1, keepdims=True)), axis=-1, keepdims=True)
    l_lo = jnp.sum(jnp.where(lane == N_GROUPS + e_lo, logits, 0.0), axis=-1, keepdims=True)
    l_hi = jnp.sum(jnp.where(lane == N_GROUPS + e_hi, logits, 0.0), axis=-1, keepdims=True)
    m = jnp.maximum(l_lo, l_hi)
    p_lo = jnp.exp(l_lo - m)
    p_hi = jnp.exp(l_hi - m)
    inv = g_p / (p_lo + p_hi)
    return p_lo * inv, p_hi * inv


def _attn_kernel(x_ref, kv_ref, gx_ref, wq_ref, wo_ref, gf_ref, wr_ref, br_ref, tri_ref, eye_ref,
                 x2r_ref, brk_ref, counts_ref, o_scr, cnt_scr):
    @pl.when((pl.program_id(0) == 0) & (pl.program_id(1) == 0))
    def _():
        cnt_scr[...] = jnp.zeros_like(cnt_scr)

    x = x_ref[0]
    tq = x.shape[0]
    h = _rms(x, gx_ref[...]).astype(BF16)
    q = _dot(h, wq_ref[...])
    for hd in range(XATTN_HEADS):
        lo = hd * XATTN_HEAD_DIM
        qh = q[:, lo:lo + XATTN_HEAD_DIM].astype(BF16)
        kh = kv_ref[0, :, lo:lo + XATTN_HEAD_DIM]
        vh = kv_ref[0, :, D_MODEL + lo:D_MODEL + lo + XATTN_HEAD_DIM]
        s = _dot_nt(qh, kh)
        p = jnp.exp(s - jnp.max(s, axis=-1, keepdims=True))
        inv = 1.0 / jnp.sum(p, axis=-1, keepdims=True)
        o_scr[:, lo:lo + XATTN_HEAD_DIM] = (_dot(p.astype(BF16), vh) * inv).astype(BF16)
    x2 = x + _dot(o_scr[...], wo_ref[...])
    for s in range(ROW_TILES):
        x2r_ref[pl.ds(s, tq, stride=ROW_TILES), :] = x2[:, s * LANES:(s + 1) * LANES]
    logits = _router_logits(_rms(x2, gf_ref[...]), wr_ref, br_ref)
    info, running = _route(logits, cnt_scr[...], tri_ref[...])
    cnt_scr[...] = running
    counts_ref[...] = running
    brk_ref[...] = _dot_tn(info.astype(BF16), eye_ref[...])[0:8, :].astype(jnp.int32)


def _attention(x1, kv, xattn_norm, w_q, w_o, ffn_norm, w_router, b_router):
    bsz, seq, d = x1.shape
    mlen = kv.shape[1]
    tq = ATTN_TILE
    nj = seq // tq
    const = lambda b, j: (0, 0)
    tile = lambda b, j: (b, j, 0)
    tri = jnp.tri(tq, tq, -1, dtype=BF16)
    eye = jnp.eye(tq, dtype=BF16)
    return pl.pallas_call(
        _attn_kernel,
        grid=(bsz, nj),
        in_specs=[
            pl.BlockSpec((1, tq, d), tile),
            pl.BlockSpec((1, mlen, 2 * d), lambda b, j: (b, 0, 0)),
            pl.BlockSpec((1, d), const),
            pl.BlockSpec((d, d), const),
            pl.BlockSpec((d, d), const),
            pl.BlockSpec((1, d), const),
            pl.BlockSpec((d, 2 * LANES), const),
            pl.BlockSpec((1, LANES), const),
            pl.BlockSpec((tq, tq), const),
            pl.BlockSpec((tq, tq), const),
        ],
        out_specs=[
            pl.BlockSpec((tq * ROW_TILES, LANES), lambda b, j: (b * nj + j, 0)),
            pl.BlockSpec((8, tq), lambda b, j: (0, b * nj + j)),
            pl.BlockSpec((1, LANES), const),
        ],
        out_shape=[
            jax.ShapeDtypeStruct((bsz * seq * ROW_TILES, LANES), F32),
            jax.ShapeDtypeStruct((8, bsz * seq), jnp.int32),
            jax.ShapeDtypeStruct((1, LANES), F32),
        ],
        scratch_shapes=[pltpu.VMEM((tq, d), BF16), pltpu.VMEM((1, LANES), F32)],
        compiler_params=pltpu.CompilerParams(
            dimension_semantics=("arbitrary", "arbitrary"), vmem_limit_bytes=VMEM_LIMIT),
        name="xattn_router",
    )(x1, kv, xattn_norm, w_q, w_o, ffn_norm, w_router, b_router, tri, eye)


def _sorted_pos(bucket_ref, rank_ref, starts_ref, t):
    return starts_ref[bucket_ref[t]] + rank_ref[t]


def _item_copy(hbm, hbm_item, buf, buf_item, rows_per_item, sem, to_hbm):
    h = hbm.at[pl.ds(pl.multiple_of(hbm_item * rows_per_item, rows_per_item), rows_per_item)]
    b = buf.at[pl.ds(pl.multiple_of(buf_item * rows_per_item, rows_per_item), rows_per_item)]
    return pltpu.make_async_copy(b, h, sem) if to_hbm else pltpu.make_async_copy(h, b, sem)


def _items_wait(hbm, buf, buf_item, n_items, rows_per_item, sem, to_hbm):
    n = n_items * rows_per_item
    h = hbm.at[pl.ds(0, n)]
    b = buf.at[pl.ds(pl.multiple_of(buf_item * rows_per_item, rows_per_item), n)]
    (pltpu.make_async_copy(b, h, sem) if to_hbm else pltpu.make_async_copy(h, b, sem)).wait()


def _dispatch_kernel(bucket_ref, rank_ref, starts_ref, fill_ref, ntiles_ref,
                     x2r_ref, xs_hbm, buf, sem):
    i = pl.program_id(0)
    nt = pl.num_programs(0)
    td = DISPATCH_TILE
    rpi = ROW_TILES
    slot = i % 2
    base = slot * td

    @pl.when(i >= 2)
    def _():
        _items_wait(xs_hbm, buf, base, td, rpi, sem.at[slot], True)

    buf[pl.ds(pl.multiple_of(base * rpi, td * rpi), td * rpi), :] = x2r_ref[...]

    def send(r2, carry):
        for k in range(DMA_QUEUES):
            r = r2 * DMA_QUEUES + k
            p = _sorted_pos(bucket_ref, rank_ref, starts_ref, i * td + r)
            _item_copy(xs_hbm, p, buf, base + r, rpi, sem.at[slot], True).start(priority=k)
        return carry

    lax.fori_loop(0, td // DMA_QUEUES, send, 0, unroll=4)

    @pl.when(i == nt - 1)
    def _():
        _items_wait(xs_hbm, buf, base, td, rpi, sem.at[slot], True)
        _items_wait(xs_hbm, buf, (1 - slot) * td, td, rpi, sem.at[1 - slot], True)
        buf[...] = jnp.zeros_like(buf)
        zsem = sem.at[2]
        for b in range(N_BUCKETS):
            def fill(p, carry):
                _item_copy(xs_hbm, p, buf, 0, rpi, zsem, True).start()
                return carry

            def drain(p, carry):
                _item_copy(xs_hbm, p, buf, 0, rpi, zsem, True).wait()
                return carry

            lax.fori_loop(fill_ref[b], fill_ref[N_BUCKETS + b], fill, 0)
            lax.fori_loop(fill_ref[b], fill_ref[N_BUCKETS + b], drain, 0)

        n_tiles_total = xs_hbm.shape[0] // (MOE_TILE * rpi)

        def fill_tile(tile, carry):
            cp = pltpu.make_async_copy(
                buf.at[pl.ds(0, MOE_TILE * rpi)],
                xs_hbm.at[pl.ds(pl.multiple_of(tile * MOE_TILE * rpi, MOE_TILE * rpi),
                                MOE_TILE * rpi)], zsem)
            cp.start()
            cp.wait()
            return carry

        lax.fori_loop(ntiles_ref[0], n_tiles_total, fill_tile, 0)


def _dispatch(bucket, rank, starts, fill, ntiles, x2r, n_sorted):
    t = x2r.shape[0] // ROW_TILES
    td = DISPATCH_TILE
    grid_spec = pltpu.PrefetchScalarGridSpec(
        num_scalar_prefetch=5,
        grid=(t // td,),
        in_specs=[pl.BlockSpec((td * ROW_TILES, LANES), lambda i, *_: (i, 0))],
        out_specs=pl.BlockSpec(memory_space=pl.ANY),
        scratch_shapes=[
            pltpu.VMEM((2 * td * ROW_TILES, LANES), F32),
            pltpu.SemaphoreType.DMA((3,)),
        ],
    )
    return pl.pallas_call(
        _dispatch_kernel,
        grid_spec=grid_spec,
        out_shape=jax.ShapeDtypeStruct((n_sorted * ROW_TILES, LANES), F32),
        compiler_params=pltpu.CompilerParams(
            dimension_semantics=("arbitrary",), vmem_limit_bytes=VMEM_LIMIT),
        name="moe_dispatch",
    )(bucket, rank, starts, fill, ntiles, x2r)


def _silu(x):
    return x * _sigmoid(x)


def _moe_kernel(elo_ref, ehi_ref, used_ref,
                xs_ref, gf_ref, wr_ref, br_ref, gfin_ref,
                wg_lo, wu_lo, wd_lo, wg_hi, wu_hi, wd_hi, y_ref):
    i = pl.program_id(0)
    tm = MOE_TILE

    @pl.when(used_ref[i] == 1)
    def _():
        x2 = jnp.concatenate(
            [xs_ref[pl.ds(s, tm, stride=ROW_TILES), :] for s in range(ROW_TILES)], axis=1)
        h3 = _rms(x2, gf_ref[...])
        g_lo, g_hi = _gates(_router_logits(h3, wr_ref, br_ref), elo_ref[i], ehi_ref[i])
        x = h3.astype(BF16)

        def expert(wg, wu, wd):
            hid = (_silu(_dot(x, wg[0])) * _dot(x, wu[0])).astype(BF16)
            return _dot(hid, wd[0])

        moe = g_lo * expert(wg_lo, wu_lo, wd_lo) + g_hi * expert(wg_hi, wu_hi, wd_hi)
        out = _rms(x2 + moe, gfin_ref[...])
        for s in range(ROW_TILES):
            y_ref[pl.ds(s, tm, stride=ROW_TILES), :] = out[:, s * LANES:(s + 1) * LANES]

    @pl.when(used_ref[i] == 0)
    def _():
        y_ref[...] = jnp.zeros_like(y_ref)


def _moe_sparse(e_lo, e_hi, used, xs, ffn_norm, w_router, b_router, final_norm,
                w_gate, w_up, w_down):
    tm = MOE_TILE
    nt = used.shape[0]
    d = D_MODEL
    lo = lambda i, elo, ehi, used: (elo[i], 0, 0)
    hi = lambda i, elo, ehi, used: (ehi[i], 0, 0)
    const = lambda i, *_: (0, 0)
    grid_spec = pltpu.PrefetchScalarGridSpec(
        num_scalar_prefetch=3,
        grid=(nt,),
        in_specs=[
            pl.BlockSpec((tm * ROW_TILES, LANES), lambda i, *_: (i, 0)),
            pl.BlockSpec((1, d), const),
            pl.BlockSpec((d, 2 * LANES), const),
            pl.BlockSpec((1, LANES), const),
            pl.BlockSpec((1, d), const),
            pl.BlockSpec((1, d, D_EXPERT), lo),
            pl.BlockSpec((1, d, D_EXPERT), lo),
            pl.BlockSpec((1, D_EXPERT, d), lo),
            pl.BlockSpec((1, d, D_EXPERT), hi),
            pl.BlockSpec((1, d, D_EXPERT), hi),
            pl.BlockSpec((1, D_EXPERT, d), hi),
        ],
        out_specs=pl.BlockSpec((tm * ROW_TILES, LANES), lambda i, *_: (i, 0)),
    )
    return pl.pallas_call(
        _moe_kernel,
        grid_spec=grid_spec,
        out_shape=jax.ShapeDtypeStruct((nt * tm * ROW_TILES, LANES), F32),
        compiler_params=pltpu.CompilerParams(
            dimension_semantics=("arbitrary",), vmem_limit_bytes=VMEM_LIMIT),
        name="moe_sparse",
    )(e_lo, e_hi, used, xs, ffn_norm, w_router, b_router, final_norm,
      w_gate, w_up, w_down, w_gate, w_up, w_down)


def _final_kernel(bucket_ref, rank_ref, starts_ref, y_hbm, o_ref, ybuf, sem):
    i = pl.program_id(0)
    nt = pl.num_programs(0)
    tf = FINAL_TILE

    def start(tile, slot):
        def fetch(r2, carry):
            for k in range(DMA_QUEUES):
                r = r2 * DMA_QUEUES + k
                p = _sorted_pos(bucket_ref, rank_ref, starts_ref, tile * tf + r)
                _item_copy(y_hbm, p, ybuf, slot * tf + r, ROW_TILES, sem.at[slot],
                           False).start(priority=k)
            return carry

        lax.fori_loop(0, tf // DMA_QUEUES, fetch, 0, unroll=4)

    @pl.when(i == 0)
    def _():
        start(0, 0)

    @pl.when(i + 1 < nt)
    def _():
        start(i + 1, (i + 1) % 2)

    slot = i % 2
    _items_wait(y_hbm, ybuf, slot * tf, tf, ROW_TILES, sem.at[slot], False)
    row0 = pl.multiple_of(slot * tf * ROW_TILES, ROW_TILES)
    for s in range(ROW_TILES):
        o_ref[:, s * LANES:(s + 1) * LANES] = ybuf[pl.ds(row0 + s, tf, stride=ROW_TILES), :]


def _final(bucket, rank, starts, y, t):
    d = D_MODEL
    tf = FINAL_TILE
    grid_spec = pltpu.PrefetchScalarGridSpec(
        num_scalar_prefetch=3,
        grid=(t // tf,),
        in_specs=[pl.BlockSpec(memory_space=pl.ANY)],
        out_specs=pl.BlockSpec((tf, d), lambda i, *_: (i, 0)),
        scratch_shapes=[
            pltpu.VMEM((2 * tf * ROW_TILES, LANES), F32),
            pltpu.SemaphoreType.DMA((2,)),
        ],
    )
    return pl.pallas_call(
        _final_kernel,
        grid_spec=grid_spec,
        out_shape=jax.ShapeDtypeStruct((t, d), F32),
        compiler_params=pltpu.CompilerParams(
            dimension_semantics=("arbitrary",), vmem_limit_bytes=VMEM_LIMIT),
        name="moe_unpermute",
    )(bucket, rank, starts, y)


def _routing_tables(counts, t):
    tm = MOE_TILE
    nt = t // tm + N_BUCKETS
    cnt = counts[0, :N_BUCKETS].astype(jnp.int32)
    padded = ((cnt + tm - 1) // tm) * tm
    ends = jnp.cumsum(padded)
    starts = ends - padded
    fill = jnp.concatenate([starts + cnt, ends])
    ntiles = ends[-1:] // tm
    tile_start = jnp.arange(nt, dtype=jnp.int32) * tm
    tile_bucket = jnp.sum((ends[None, :] <= tile_start[:, None]).astype(jnp.int32), axis=1)
    used = (tile_bucket < N_BUCKETS).astype(jnp.int32)
    tile_bucket = jnp.minimum(tile_bucket, N_BUCKETS - 1)
    pair = tile_bucket % N_PAIRS
    group0 = (tile_bucket // N_PAIRS) * EXPERTS_PER_GROUP
    e_lo = group0 + jnp.array([0, 0, 0, 1, 1, 2], jnp.int32)[pair]
    e_hi = group0 + jnp.array([1, 2, 3, 2, 3, 3], jnp.int32)[pair]
    return starts, fill, ntiles, e_lo, e_hi, used, nt


def kernel(x, mem, mix_norm, w_in, conv_w, hgrn_lb, hgrn_norm, w_out, xattn_norm, mem_norm,
           w_q, w_kv, w_o, ffn_norm, w_group, b_group, w_expert, b_expert, w_gate, w_up,
           w_down, final_norm):
    bsz, seq, d = x.shape
    assert d == D_MODEL and seq % SEQ_TILE == 0 and seq % ATTN_TILE == 0
    assert (bsz * seq) % MOE_TILE == 0 and (bsz * seq) % FINAL_TILE == 0
    assert (bsz * seq) % DISPATCH_TILE == 0 and bsz * seq >= 2 * DISPATCH_TILE
    assert 2 * DISPATCH_TILE >= MOE_TILE and mix_norm.shape[0] == 1
    bf = lambda w: w.astype(BF16)

    x1 = _mixer(x, mix_norm, bf(w_in[0]), conv_w[0], hgrn_lb, hgrn_norm, bf(w_out[0]))
    kv = _kv_proj(mem, mem_norm, bf(w_kv[0]))

    pad = LANES - N_GROUPS - N_EXPERTS
    w_router = jnp.concatenate(
        [w_group[0], w_expert[0], jnp.zeros((d, pad), F32)], axis=1)
    b_router = jnp.concatenate(
        [b_group[0], b_expert[0], jnp.zeros((pad,), F32)])[None, :]
    w_router_hi = bf(w_router)
    w_router_lo = bf(w_router - w_router_hi.astype(F32))
    w_router2 = jnp.concatenate([w_router_hi, w_router_lo], axis=1)
    w_q_scaled = bf(w_q[0] * (XATTN_HEAD_DIM ** -0.5))
    x2r, brk, counts = _attention(x1, kv, xattn_norm, w_q_scaled, bf(w_o[0]), ffn_norm,
                                  w_router2, b_router)

    t = bsz * seq
    bucket = brk[0]
    rank = brk[1] * 256 + brk[2]
    starts, fill, ntiles, e_lo, e_hi, used, nt = _routing_tables(counts, t)
    xs = _dispatch(bucket, rank, starts, fill, ntiles, x2r, nt * MOE_TILE)
    y = _moe_sparse(e_lo, e_hi, used, xs, ffn_norm, w_router2, b_router, final_norm[None, :],
                    bf(w_gate[0]), bf(w_up[0]), bf(w_down[0]))
    out = _final(bucket, rank, starts, y, t)
    return out.reshape(bsz, seq, d)
```

```python
import functools

import jax
import jax.numpy as jnp
from jax import lax
from jax.experimental import pallas as pl
from jax.experimental.pallas import tpu as pltpu

F32 = jnp.float32
BF16 = jnp.bfloat16

D_MODEL = 1024
CONV_WIDTH = 512
HGRN_WIDTH = 512
HGRN_HEADS = 4
HEAD_DIM = 128
N_PROJ_SLOTS = 7
PROJ_WIDTH = N_PROJ_SLOTS * 512
XATTN_HEADS = 4
XATTN_HEAD_DIM = 256
N_GROUPS = 4
EXPERTS_PER_GROUP = 4
N_EXPERTS = 16
D_EXPERT = 512
EPS = 1e-6

LANES = 128
SUBLANES = 8
CHUNK = 64
CHUNK_LEVELS = 6
SEQ_TILE = 256
ATTN_TILE = 512
DISPATCH_TILE = 256
MOE_TILE = 256
FINAL_TILE = 256
N_PAIRS = 6
N_BUCKETS = N_GROUPS * N_PAIRS
ROW_TILES = D_MODEL // LANES
DMA_QUEUES = 2
VMEM_LIMIT = 56 * 1024 * 1024


def _rms(x, g):
    return x * lax.rsqrt(jnp.mean(x * x, axis=-1, keepdims=True) + EPS) * g


def _dot(a, b):
    return jnp.dot(a, b, preferred_element_type=F32)


def _dot_nt(a, b):
    return lax.dot_general(a, b, (((1,), (1,)), ((), ())), preferred_element_type=F32)


def _dot_tn(a, b):
    return lax.dot_general(a, b, (((0,), (0,)), ((), ())), preferred_element_type=F32)


def _roll_rows(x, shift):
    return pltpu.roll(x, shift % x.shape[0], axis=0)


def _level_exponents(logf2, use_level):
    n = CHUNK // SUBLANES
    sub = lax.broadcasted_iota(jnp.int32, (SUBLANES, logf2.shape[1]), 0)
    roll = lambda x, s: pltpu.roll(x, s % SUBLANES, axis=0)
    r = [logf2[SUBLANES * j:SUBLANES * (j + 1), :] for j in range(n)]
    out = []
    for lvl in range(1, CHUNK_LEVELS + 1):
        half = 1 << (lvl - 1)
        g = [None] * n
        if half < SUBLANES:
            second = (sub & half) != 0
            for j in range(n):
                last = r[j]
                w = 1
                while w < half:
                    last = jnp.where((sub & w) != 0, last, roll(last, -w))
                    w *= 2
                tot = jnp.where(second, roll(last, half), last)
                g[j] = jnp.where(second, r[j], tot - r[j])
                r[j] = jnp.where(second, r[j] + tot, r[j])
        else:
            hv = half // SUBLANES
            for j0 in range(0, n, 2 * hv):
                mid = r[j0 + hv - 1]
                tot = jnp.broadcast_to(mid[SUBLANES - 1:SUBLANES, :], mid.shape)
                for j in range(j0, j0 + hv):
                    g[j] = tot - r[j]
                for j in range(j0 + hv, j0 + 2 * hv):
                    g[j] = r[j]
                    r[j] = r[j] + tot
        use_level(lvl, jnp.concatenate(g, axis=0))
    return jnp.concatenate(r, axis=0)


def _sigmoid(x):
    return 0.5 * jnp.tanh(0.5 * x) + 0.5


def _split_levels():
    import numpy as np
    t = np.arange(CHUNK)[:, None]
    s = np.arange(CHUNK)[None, :]
    msb = np.floor(np.log2(np.maximum(t ^ s, 1))).astype(np.int32) + 1
    return np.where(s < t, msb, np.where(s == t, 0, -1)).astype(np.int32)


def _hgrn_chunk(q, z, v, lb, levels):
    half_th = 0.5 * jnp.tanh(0.5 * z)
    one_m_lb = 1.0 - lb
    logf2 = jnp.log2(lb + one_m_lb * (0.5 + half_th))
    k = one_m_lb * (0.5 - half_th)

    qb = q.astype(BF16)
    kb = k.astype(BF16)
    scores = [jnp.where(levels == 0, jnp.sum(q * k, axis=-1, keepdims=True), 0.0)]

    def use_level(lvl, g):
        decay = jnp.exp2(g).astype(BF16)
        scores[0] = jnp.where(levels == lvl, _dot_nt(qb * decay, kb * decay), scores[0])

    b2 = _level_exponents(logf2, use_level)
    a = scores[0]
    b2_last = b2[CHUNK - 1:CHUNK, :]

    qe = (q * jnp.exp2(b2)).astype(BF16)
    kd = (k * jnp.exp2(b2_last - b2)).astype(BF16)
    return a, qe, kd, v.astype(BF16), b2_last


def _hgrn_tile(chunks, st):
    (a0, qe0, kd0, v0, bl0), (a1, qe1, kd1, v1, bl1), (a2, qe2, kd2, v2, bl2), \
        (a3, qe3, kd3, v3, bl3) = chunks
    scale = lambda x, log2_decay: x * jnp.exp2(log2_decay).astype(BF16)
    bf = lambda x: x.astype(BF16)
    cum1 = bl0
    cum2 = cum1 + bl1
    cum3 = cum2 + bl2
    cum4 = cum3 + bl3
    s10 = bf(_dot_nt(qe1, kd0))
    s32 = bf(_dot_nt(qe3, kd2))
    s8 = bf(_dot_nt(jnp.concatenate([qe2, scale(qe3, bl2)], axis=0),
                    jnp.concatenate([scale(kd0, bl1), kd1], axis=0)))
    v01 = jnp.concatenate([v0, v1], axis=0)
    stb = bf(st)
    o0 = _dot(bf(a0), v0) + _dot_nt(qe0, stb)
    o1 = _dot(bf(a1), v1) + _dot(s10, v0) + _dot_nt(scale(qe1, cum1), stb)
    o2 = _dot(bf(a2), v2) + _dot(s8[:CHUNK], v01) + _dot_nt(scale(qe2, cum2), stb)
    o3 = (_dot(bf(a3), v3) + _dot(s32, v2) + _dot(s8[CHUNK:], v01)
          + _dot_nt(scale(qe3, cum3), stb))
    kd_all = jnp.concatenate(
        [scale(kd0, cum4 - cum1), scale(kd1, cum4 - cum2), scale(kd2, bl3), kd3], axis=0)
    v_all = jnp.concatenate([v0, v1, v2, v3], axis=0)
    st_new = st * jnp.exp2(cum4) + _dot_tn(v_all, kd_all)
    return [o0, o1, o2, o3], st_new


def _mixer_kernel(x_ref, gmix_ref, win_ref, convw_ref, lbraw_ref, hnorm_ref, wout_ref, lvl_ref,
                  o_ref, p_scr, y_scr, st_scr, tail_scr):
    j = pl.program_id(1)

    @pl.when(j == 0)
    def _():
        st_scr[...] = jnp.zeros_like(st_scr)
        tail_scr[...] = jnp.zeros_like(tail_scr)

    x = x_ref[0]
    h = _rms(x, gmix_ref[...]).astype(BF16)
    p_scr[...] = _dot(h, win_ref[...])

    ts = x.shape[0]
    cb = p_scr[:, 0:CONV_WIDTH]
    u = p_scr[:, CONV_WIDTH:2 * CONV_WIDTH] * p_scr[:, 2 * CONV_WIDTH:3 * CONV_WIDTH]
    row = lax.broadcasted_iota(jnp.int32, u.shape, 0)
    prev1 = tail_scr[7:8, :]
    prev2 = tail_scr[6:7, :]
    u1 = jnp.where(row == 0, prev1, _roll_rows(u, 1))
    u2 = jnp.where(row == 0, prev2, jnp.where(row == 1, prev1, _roll_rows(u, 2)))
    cw = convw_ref[...]
    conv = u2 * cw[0:1, :] + u1 * cw[1:2, :] + u * cw[2:3, :]
    y_scr[:, 0:CONV_WIDTH] = (cb * conv).astype(BF16)
    tail_scr[...] = u[ts - 8:ts, :]

    raw = lbraw_ref[...]
    mx = jnp.max(raw, axis=0, keepdims=True)
    ex = jnp.exp(raw - mx)
    lb_all = ex[0:1, :] / jnp.sum(ex, axis=0, keepdims=True)
    hn = hnorm_ref[...]

    for hd in range(HGRN_HEADS):
        lo = hd * HEAD_DIM
        sl = slice(lo, lo + HEAD_DIM)
        col = lambda slot: slice(slot * 512 + lo, slot * 512 + lo + HEAD_DIM)
        chunks = []
        for c in range(ts // CHUNK):
            rows = slice(c * CHUNK, (c + 1) * CHUNK)
            chunks.append(_hgrn_chunk(p_scr[rows, col(3)], p_scr[rows, col(4)],
                                      p_scr[rows, col(5)], lb_all[:, sl], lvl_ref[...]))
        outs, st_new = _hgrn_tile(chunks, st_scr[hd])
        st_scr[hd] = st_new
        for c, o in enumerate(outs):
            rows = slice(c * CHUNK, (c + 1) * CHUNK)
            g = p_scr[rows, col(6)]
            o = o * lax.rsqrt(jnp.mean(o * o, axis=-1, keepdims=True) + EPS) * hn[:, sl]
            y_scr[rows, CONV_WIDTH + lo:CONV_WIDTH + lo + HEAD_DIM] = \
                (o * (g * _sigmoid(g))).astype(BF16)

    o_ref[0] = x + _dot(y_scr[...], wout_ref[...])


def _mixer(x, mix_norm, w_in, conv_w, hgrn_lb, hgrn_norm, w_out):
    bsz, seq, d = x.shape
    ts = SEQ_TILE
    const = lambda b, j: (0, 0)
    return pl.pallas_call(
        _mixer_kernel,
        grid=(bsz, seq // ts),
        in_specs=[
            pl.BlockSpec((1, ts, d), lambda b, j: (b, j, 0)),
            pl.BlockSpec((1, d), const),
            pl.BlockSpec((d, PROJ_WIDTH), const),
            pl.BlockSpec((3, CONV_WIDTH), const),
            pl.BlockSpec((2, HGRN_WIDTH), const),
            pl.BlockSpec((1, HGRN_WIDTH), const),
            pl.BlockSpec((d, d), const),
            pl.BlockSpec((CHUNK, CHUNK), const),
        ],
        out_specs=pl.BlockSpec((1, ts, d), lambda b, j: (b, j, 0)),
        out_shape=jax.ShapeDtypeStruct((bsz, seq, d), F32),
        scratch_shapes=[
            pltpu.VMEM((ts, PROJ_WIDTH), F32),
            pltpu.VMEM((ts, d), BF16),
            pltpu.VMEM((HGRN_HEADS, HEAD_DIM, HEAD_DIM), F32),
            pltpu.VMEM((8, CONV_WIDTH), F32),
        ],
        compiler_params=pltpu.CompilerParams(
            dimension_semantics=("arbitrary", "arbitrary"), vmem_limit_bytes=VMEM_LIMIT),
        name="mixer",
    )(x, mix_norm, w_in, conv_w, hgrn_lb, hgrn_norm, w_out, jnp.asarray(_split_levels()))


def _kv_kernel(m_ref, g_ref, w_ref, o_ref):
    h = _rms(m_ref[0], g_ref[...]).astype(BF16)
    o_ref[0] = _dot(h, w_ref[...]).astype(BF16)


def _kv_proj(mem, mem_norm, w_kv):
    bsz, mlen, d = mem.shape
    const = lambda b: (0, 0)
    return pl.pallas_call(
        _kv_kernel,
        grid=(bsz,),
        in_specs=[
            pl.BlockSpec((1, mlen, d), lambda b: (b, 0, 0)),
            pl.BlockSpec((1, d), const),
            pl.BlockSpec((d, 2 * d), const),
        ],
        out_specs=pl.BlockSpec((1, mlen, 2 * d), lambda b: (b, 0, 0)),
        out_shape=jax.ShapeDtypeStruct((bsz, mlen, 2 * d), BF16),
        compiler_params=pltpu.CompilerParams(
            dimension_semantics=("arbitrary",), vmem_limit_bytes=VMEM_LIMIT),
        name="kv_proj",
    )(mem, mem_norm, w_kv)


def _first_argmax(vals, lane):
    mx = jnp.max(vals, axis=-1, keepdims=True)
    idx = jnp.min(jnp.where(vals == mx, lane, float(LANES)), axis=-1, keepdims=True)
    return mx, idx


def _router_logits(h3, wr_ref, br_ref):
    h3_hi = h3.astype(BF16)
    h3_lo = (h3 - h3_hi.astype(F32)).astype(BF16)
    two = _dot(h3_hi, wr_ref[...])
    return (two[:, :LANES] + two[:, LANES:]) + _dot(h3_lo, wr_ref[:, :LANES]) + br_ref[...]


def _route(logits, running, tri):
    lane = lax.broadcasted_iota(jnp.int32, logits.shape, 1).astype(F32)
    neg = jnp.float32(-jnp.inf)
    gl = jnp.where(lane < N_GROUPS, logits, neg)
    _, gidx = _first_argmax(gl, lane)
    base = N_GROUPS + EXPERTS_PER_GROUP * gidx
    el = jnp.where((lane >= base) & (lane < base + EXPERTS_PER_GROUP), logits, neg)
    _, i1 = _first_argmax(el, lane)
    _, i2 = _first_argmax(jnp.where(lane == i1, neg, el), lane)
    lo = jnp.minimum(i1, i2) - base
    hi = jnp.maximum(i1, i2) - base
    bucket = gidx * N_PAIRS + (lo * (7.0 - lo)) * 0.5 + hi - lo - 1.0

    onehot = lane == bucket
    before = _dot(tri, onehot.astype(BF16))
    rank = jnp.sum(jnp.where(onehot, before + running, 0.0), axis=-1, keepdims=True)
    running = running + jnp.sum(onehot.astype(F32), axis=0, keepdims=True)
    rank_hi = jnp.floor(rank * (1.0 / 256.0))
    rank_lo = rank - 256.0 * rank_hi
    info = jnp.where(lane == 0, bucket, jnp.where(lane == 1, rank_hi,
                                                  jnp.where(lane == 2, rank_lo, 0.0)))
    return info, running


def _gates(logits, e_lo, e_hi):
    lane = lax.broadcasted_iota(jnp.int32, logits.shape, 1)
    neg = jnp.float32(-jnp.inf)
    gl = jnp.where(lane < N_GROUPS, logits, neg)
    g_p = 1.0 / jnp.sum(jnp.exp(gl - jnp.max(gl, axis=-1, keepdims=True)), axis=-1, keepdims=True)
    l_lo = jnp.sum(jnp.where(lane == N_GROUPS + e_lo, logits, 0.0), axis=-1, keepdims=True)
    l_hi = jnp.sum(jnp.where(lane == N_GROUPS + e_hi, logits, 0.0), axis=-1, keepdims=True)
    m = jnp.maximum(l_lo, l_hi)
    p_lo = jnp.exp(l_lo - m)
    p_hi = jnp.exp(l_hi - m)
    inv = g_p / (p_lo + p_hi)
    return p_lo * inv, p_hi * inv


def _attn_kernel(x_ref, kv_ref, gx_ref, wq_ref, wo_ref, gf_ref, wr_ref, br_ref, tri_ref, eye_ref,
                 x2r_ref, brk_ref, counts_ref, o_scr, cnt_scr):
    @pl.when((pl.program_id(0) == 0) & (pl.program_id(1) == 0))
    def _():
        cnt_scr[...] = jnp.zeros_like(cnt_scr)

    x = x_ref[0]
    tq = x.shape[0]
    h = _rms(x, gx_ref[...]).astype(BF16)
    q = _dot(h, wq_ref[...])
    for hd in range(XATTN_HEADS):
        lo = hd * XATTN_HEAD_DIM
        qh = q[:, lo:lo + XATTN_HEAD_DIM].astype(BF16)
        kh = kv_ref[0, :, lo:lo + XATTN_HEAD_DIM]
        vh = kv_ref[0, :, D_MODEL + lo:D_MODEL + lo + XATTN_HEAD_DIM]
        s = _dot_nt(qh, kh)
        p = jnp.exp(s - jnp.max(s, axis=-1, keepdims=True))
        inv = 1.0 / jnp.sum(p, axis=-1, keepdims=True)
        o_scr[:, lo:lo + XATTN_HEAD_DIM] = (_dot(p.astype(BF16), vh) * inv).astype(BF16)
    x2 = x + _dot(o_scr[...], wo_ref[...])
    for s in range(ROW_TILES):
        x2r_ref[pl.ds(s, tq, stride=ROW_TILES), :] = x2[:, s * LANES:(s + 1) * LANES]
    logits = _router_logits(_rms(x2, gf_ref[...]), wr_ref, br_ref)
    info, running = _route(logits, cnt_scr[...], tri_ref[...])
    cnt_scr[...] = running
    counts_ref[...] = running
    brk_ref[...] = _dot_tn(info.astype(BF16), eye_ref[...])[0:8, :].astype(jnp.int32)


def _attention(x1, kv, xattn_norm, w_q, w_o, ffn_norm, w_router, b_router):
    bsz, seq, d = x1.shape
    mlen = kv.shape[1]
    tq = ATTN_TILE
    nj = seq // tq
    const = lambda b, j: (0, 0)
    tile = lambda b, j: (b, j, 0)
    tri = jnp.tri(tq, tq, -1, dtype=BF16)
    eye = jnp.eye(tq, dtype=BF16)
    return pl.pallas_call(
        _attn_kernel,
        grid=(bsz, nj),
        in_specs=[
            pl.BlockSpec((1, tq, d), tile),
            pl.BlockSpec((1, mlen, 2 * d), lambda b, j: (b, 0, 0)),
            pl.BlockSpec((1, d), const),
            pl.BlockSpec((d, d), const),
            pl.BlockSpec((d, d), const),
            pl.BlockSpec((1, d), const),
            pl.BlockSpec((d, 2 * LANES), const),
            pl.BlockSpec((1, LANES), const),
            pl.BlockSpec((tq, tq), const),
            pl.BlockSpec((tq, tq), const),
        ],
        out_specs=[
            pl.BlockSpec((tq * ROW_TILES, LANES), lambda b, j: (b * nj + j, 0)),
            pl.BlockSpec((8, tq), lambda b, j: (0, b * nj + j)),
            pl.BlockSpec((1, LANES), const),
        ],
        out_shape=[
            jax.ShapeDtypeStruct((bsz * seq * ROW_TILES, LANES), F32),
            jax.ShapeDtypeStruct((8, bsz * seq), jnp.int32),
            jax.ShapeDtypeStruct((1, LANES), F32),
        ],
        scratch_shapes=[pltpu.VMEM((tq, d), BF16), pltpu.VMEM((1, LANES), F32)],
        compiler_params=pltpu.CompilerParams(
            dimension_semantics=("arbitrary", "arbitrary"), vmem_limit_bytes=VMEM_LIMIT),
        name="xattn_router",
    )(x1, kv, xattn_norm, w_q, w_o, ffn_norm, w_router, b_router, tri, eye)


def _sorted_pos(bucket_ref, rank_ref, starts_ref, t):
    return starts_ref[bucket_ref[t]] + rank_ref[t]


def _item_copy(hbm, hbm_item, buf, buf_item, rows_per_item, sem, to_hbm):
    h = hbm.at[pl.ds(pl.multiple_of(hbm_item * rows_per_item, rows_per_item), rows_per_item)]
    b = buf.at[pl.ds(pl.multiple_of(buf_item * rows_per_item, rows_per_item), rows_per_item)]
    return pltpu.make_async_copy(b, h, sem) if to_hbm else pltpu.make_async_copy(h, b, sem)


def _items_wait(hbm, buf, buf_item, n_items, rows_per_item, sem, to_hbm):
    n = n_items * rows_per_item
    h = hbm.at[pl.ds(0, n)]
    b = buf.at[pl.ds(pl.multiple_of(buf_item * rows_per_item, rows_per_item), n)]
    (pltpu.make_async_copy(b, h, sem) if to_hbm else pltpu.make_async_copy(h, b, sem)).wait()


def _dispatch_kernel(bucket_ref, rank_ref, starts_ref, fill_ref, ntiles_ref,
                     x2r_ref, xs_hbm, buf, sem):
    i = pl.program_id(0)
    nt = pl.num_programs(0)
    td = DISPATCH_TILE
    rpi = ROW_TILES
    slot = i % 2
    base = slot * td

    @pl.when(i >= 2)
    def _():
        _items_wait(xs_hbm, buf, base, td, rpi, sem.at[slot], True)

    buf[pl.ds(pl.multiple_of(base * rpi, td * rpi), td * rpi), :] = x2r_ref[...]

    def send(r2, carry):
        for k in range(DMA_QUEUES):
            r = r2 * DMA_QUEUES + k
            p = _sorted_pos(bucket_ref, rank_ref, starts_ref, i * td + r)
            _item_copy(xs_hbm, p, buf, base + r, rpi, sem.at[slot], True).start(priority=k)
        return carry

    lax.fori_loop(0, td // DMA_QUEUES, send, 0, unroll=4)

    @pl.when(i == nt - 1)
    def _():
        _items_wait(xs_hbm, buf, base, td, rpi, sem.at[slot], True)
        _items_wait(xs_hbm, buf, (1 - slot) * td, td, rpi, sem.at[1 - slot], True)
        buf[...] = jnp.zeros_like(buf)
        zsem = sem.at[2]
        for b in range(N_BUCKETS):
            def fill(p, carry):
                _item_copy(xs_hbm, p, buf, 0, rpi, zsem, True).start()
                return carry

            def drain(p, carry):
                _item_copy(xs_hbm, p, buf, 0, rpi, zsem, True).wait()
                return carry

            lax.fori_loop(fill_ref[b], fill_ref[N_BUCKETS + b], fill, 0)
            lax.fori_loop(fill_ref[b], fill_ref[N_BUCKETS + b], drain, 0)

        n_tiles_total = xs_hbm.shape[0] // (MOE_TILE * rpi)

        def fill_tile(tile, carry):
            cp = pltpu.make_async_copy(
                buf.at[pl.ds(0, MOE_TILE * rpi)],
                xs_hbm.at[pl.ds(pl.multiple_of(tile * MOE_TILE * rpi, MOE_TILE * rpi),
                                MOE_TILE * rpi)], zsem)
            cp.start()
            cp.wait()
            return carry

        lax.fori_loop(ntiles_ref[0], n_tiles_total, fill_tile, 0)


def _dispatch(bucket, rank, starts, fill, ntiles, x2r, n_sorted):
    t = x2r.shape[0] // ROW_TILES
    td = DISPATCH_TILE
    grid_spec = pltpu.PrefetchScalarGridSpec(
        num_scalar_prefetch=5,
        grid=(t // td,),
        in_specs=[pl.BlockSpec((td * ROW_TILES, LANES), lambda i, *_: (i, 0))],
        out_specs=pl.BlockSpec(memory_space=pl.ANY),
        scratch_shapes=[
            pltpu.VMEM((2 * td * ROW_TILES, LANES), F32),
            pltpu.SemaphoreType.DMA((3,)),
        ],
    )
    return pl.pallas_call(
        _dispatch_kernel,
        grid_spec=grid_spec,
        out_shape=jax.ShapeDtypeStruct((n_sorted * ROW_TILES, LANES), F32),
        compiler_params=pltpu.CompilerParams(
            dimension_semantics=("arbitrary",), vmem_limit_bytes=VMEM_LIMIT),
        name="moe_dispatch",
    )(bucket, rank, starts, fill, ntiles, x2r)


def _silu(x):
    return x * _sigmoid(x)


def _moe_kernel(elo_ref, ehi_ref, used_ref,
                xs_ref, gf_ref, wr_ref, br_ref, gfin_ref,
                wg_lo, wu_lo, wd_lo, wg_hi, wu_hi, wd_hi, y_ref):
    i = pl.program_id(0)
    tm = MOE_TILE

    @pl.when(used_ref[i] == 1)
    def _():
        x2 = jnp.concatenate(
            [xs_ref[pl.ds(s, tm, stride=ROW_TILES), :] for s in range(ROW_TILES)], axis=1)
        h3 = _rms(x2, gf_ref[...])
        g_lo, g_hi = _gates(_router_logits(h3, wr_ref, br_ref), elo_ref[i], ehi_ref[i])
        x = h3.astype(BF16)

        def expert(wg, wu, wd):
            hid = (_silu(_dot(x, wg[0])) * _dot(x, wu[0])).astype(BF16)
            return _dot(hid, wd[0])

        moe = g_lo * expert(wg_lo, wu_lo, wd_lo) + g_hi * expert(wg_hi, wu_hi, wd_hi)
        out = _rms(x2 + moe, gfin_ref[...])
        for s in range(ROW_TILES):
            y_ref[pl.ds(s, tm, stride=ROW_TILES), :] = out[:, s * LANES:(s + 1) * LANES]

    @pl.when(used_ref[i] == 0)
    def _():
        y_ref[...] = jnp.zeros_like(y_ref)


def _moe_sparse(e_lo, e_hi, used, xs, ffn_norm, w_router, b_router, final_norm,
                w_gate, w_up, w_down):
    tm = MOE_TILE
    nt = used.shape[0]
    d = D_MODEL
    lo = lambda i, elo, ehi, used: (elo[i], 0, 0)
    hi = lambda i, elo, ehi, used: (ehi[i], 0, 0)
    const = lambda i, *_: (0, 0)
    grid_spec = pltpu.PrefetchScalarGridSpec(
        num_scalar_prefetch=3,
        grid=(nt,),
        in_specs=[
            pl.BlockSpec((tm * ROW_TILES, LANES), lambda i, *_: (i, 0)),
            pl.BlockSpec((1, d), const),
            pl.BlockSpec((d, 2 * LANES), const),
            pl.BlockSpec((1, LANES), const),
            pl.BlockSpec((1, d), const),
            pl.BlockSpec((1, d, D_EXPERT), lo),
            pl.BlockSpec((1, d, D_EXPERT), lo),
            pl.BlockSpec((1, D_EXPERT, d), lo),
            pl.BlockSpec((1, d, D_EXPERT), hi),
            pl.BlockSpec((1, d, D_EXPERT), hi),
            pl.BlockSpec((1, D_EXPERT, d), hi),
        ],
        out_specs=pl.BlockSpec((tm * ROW_TILES, LANES), lambda i, *_: (i, 0)),
    )
    return pl.pallas_call(
        _moe_kernel,
        grid_spec=grid_spec,
        out_shape=jax.ShapeDtypeStruct((nt * tm * ROW_TILES, LANES), F32),
        compiler_params=pltpu.CompilerParams(
            dimension_semantics=("arbitrary",), vmem_limit_bytes=VMEM_LIMIT),
        name="moe_sparse",
    )(e_lo, e_hi, used, xs, ffn_norm, w_router, b_router, final_norm,
      w_gate, w_up, w_down, w_gate, w_up, w_down)


def _final_kernel(bucket_ref, rank_ref, starts_ref, y_hbm, o_ref, ybuf, sem):
    i = pl.program_id(0)
    nt = pl.num_programs(0)
    tf = FINAL_TILE

    def start(tile, slot):
        def fetch(r2, carry):
            for k in range(DMA_QUEUES):
                r = r2 * DMA_QUEUES + k
                p = _sorted_pos(bucket_ref, rank_ref, starts_ref, tile * tf + r)
                _item_copy(y_hbm, p, ybuf, slot * tf + r, ROW_TILES, sem.at[slot],
                           False).start(priority=k)
            return carry

        lax.fori_loop(0, tf // DMA_QUEUES, fetch, 0, unroll=4)

    @pl.when(i == 0)
    def _():
        start(0, 0)

    @pl.when(i + 1 < nt)
    def _():
        start(i + 1, (i + 1) % 2)

    slot = i % 2
    _items_wait(y_hbm, ybuf, slot * tf, tf, ROW_TILES, sem.at[slot], False)
    row0 = pl.multiple_of(slot * tf * ROW_TILES, ROW_TILES)
    for s in range(ROW_TILES):
        o_ref[:, s * LANES:(s + 1) * LANES] = ybuf[pl.ds(row0 + s, tf, stride=ROW_TILES), :]


def _final(bucket, rank, starts, y, t):
    d = D_MODEL
    tf = FINAL_TILE
    grid_spec = pltpu.PrefetchScalarGridSpec(
        num_scalar_prefetch=3,
        grid=(t // tf,),
        in_specs=[pl.BlockSpec(memory_space=pl.ANY)],
        out_specs=pl.BlockSpec((tf, d), lambda i, *_: (i, 0)),
        scratch_shapes=[
            pltpu.VMEM((2 * tf * ROW_TILES, LANES), F32),
            pltpu.SemaphoreType.DMA((2,)),
        ],
    )
    return pl.pallas_call(
        _final_kernel,
        grid_spec=grid_spec,
        out_shape=jax.ShapeDtypeStruct((t, d), F32),
        compiler_params=pltpu.CompilerParams(
            dimension_semantics=("arbitrary",), vmem_limit_bytes=VMEM_LIMIT),
        name="moe_unpermute",
    )(bucket, rank, starts, y)


def _routing_tables(counts, t):
    tm = MOE_TILE
    nt = t // tm + N_BUCKETS
    cnt = counts[0, :N_BUCKETS].astype(jnp.int32)
    padded = ((cnt + tm - 1) // tm) * tm
    ends = jnp.cumsum(padded)
    starts = ends - padded
    fill = jnp.concatenate([starts + cnt, ends])
    ntiles = ends[-1:] // tm
    tile_start = jnp.arange(nt, dtype=jnp.int32) * tm
    tile_bucket = jnp.sum((ends[None, :] <= tile_start[:, None]).astype(jnp.int32), axis=1)
    used = (tile_bucket < N_BUCKETS).astype(jnp.int32)
    tile_bucket = jnp.minimum(tile_bucket, N_BUCKETS - 1)
    pair = tile_bucket % N_PAIRS
    group0 = (tile_bucket // N_PAIRS) * EXPERTS_PER_GROUP
    e_lo = group0 + jnp.array([0, 0, 0, 1, 1, 2], jnp.int32)[pair]
    e_hi = group0 + jnp.array([1, 2, 3, 2, 3, 3], jnp.int32)[pair]
    return starts, fill, ntiles, e_lo, e_hi, used, nt


def kernel(x, mem, mix_norm, w_in, conv_w, hgrn_lb, hgrn_norm, w_out, xattn_norm, mem_norm,
           w_q, w_kv, w_o, ffn_norm, w_group, b_group, w_expert, b_expert, w_gate, w_up,
           w_down, final_norm):
    bsz, seq, d = x.shape
    assert d == D_MODEL and seq % SEQ_TILE == 0 and seq % ATTN_TILE == 0
    assert (bsz * seq) % MOE_TILE == 0 and (bsz * seq) % FINAL_TILE == 0
    assert (bsz * seq) % DISPATCH_TILE == 0 and bsz * seq >= 2 * DISPATCH_TILE
    assert 2 * DISPATCH_TILE >= MOE_TILE and mix_norm.shape[0] == 1
    bf = lambda w: w.astype(BF16)

    x1 = _mixer(x, mix_norm, bf(w_in[0]), conv_w[0], hgrn_lb, hgrn_norm, bf(w_out[0]))
    kv = _kv_proj(mem, mem_norm, bf(w_kv[0]))

    pad = LANES - N_GROUPS - N_EXPERTS
    w_router = jnp.concatenate(
        [w_group[0], w_expert[0], jnp.zeros((d, pad), F32)], axis=1)
    b_router = jnp.concatenate(
        [b_group[0], b_expert[0], jnp.zeros((pad,), F32)])[None, :]
    w_router_hi = bf(w_router)
    w_router_lo = bf(w_router - w_router_hi.astype(F32))
    w_router2 = jnp.concatenate([w_router_hi, w_router_lo], axis=1)
    w_q_scaled = bf(w_q[0] * (XATTN_HEAD_DIM ** -0.5))
    x2r, brk, counts = _attention(x1, kv, xattn_norm, w_q_scaled, bf(w_o[0]), ffn_norm,
                                  w_router2, b_router)

    t = bsz * seq
    bucket = brk[0]
    rank = brk[1] * 256 + brk[2]
    starts, fill, ntiles, e_lo, e_hi, used, nt = _routing_tables(counts, t)
    xs = _dispatch(bucket, rank, starts, fill, ntiles, x2r, nt * MOE_TILE)
    y = _moe_sparse(e_lo, e_hi, used, xs, ffn_norm, w_router2, b_router, final_norm[None, :],
                    bf(w_gate[0]), bf(w_up[0]), bf(w_down[0]))
    out = _final(bucket, rank, starts, y, t)
    return out.reshape(bsz, seq, d)
```

```python
import functools

import jax
import jax.numpy as jnp
from jax import lax
from jax.experimental import pallas as pl
from jax.experimental.pallas import tpu as pltpu

F32 = jnp.float32
BF16 = jnp.bfloat16

D_MODEL = 1024
CONV_WIDTH = 512
HGRN_WIDTH = 512
HGRN_HEADS = 4
HEAD_DIM = 128
N_PROJ_SLOTS = 7
PROJ_WIDTH = N_PROJ_SLOTS * 512
XATTN_HEADS = 4
XATTN_HEAD_DIM = 256
N_GROUPS = 4
EXPERTS_PER_GROUP = 4
N_EXPERTS = 16
D_EXPERT = 512
EPS = 1e-6

LANES = 128
SUBLANES = 8
CHUNK = 64
CHUNK_LEVELS = 6
HGRN_TILE = 4 * CHUNK
SEQ_TILE = 256
ATTN_TILE = 512
MOE_TILE = 256
MAX_CHUNKS = 256
META_ROWS = 16
FINAL_TILE = 256
N_PAIRS = 6
N_BUCKETS = N_GROUPS * N_PAIRS
ROW_TILES = D_MODEL // LANES
DMA_QUEUES = 2
VMEM_LIMIT = 56 * 1024 * 1024


def _rms(x, g):
    return x * lax.rsqrt(jnp.mean(x * x, axis=-1, keepdims=True) + EPS) * g


def _dot(a, b):
    return jnp.dot(a, b, preferred_element_type=F32)


def _dot_nt(a, b):
    return lax.dot_general(a, b, (((1,), (1,)), ((), ())), preferred_element_type=F32)


def _dot_tn(a, b):
    return lax.dot_general(a, b, (((0,), (0,)), ((), ())), preferred_element_type=F32)


def _roll_rows(x, shift):
    return pltpu.roll(x, shift % x.shape[0], axis=0)


def _level_exponents(logf2, use_level):
    n = CHUNK // SUBLANES
    sub = lax.broadcasted_iota(jnp.int32, (SUBLANES, logf2.shape[1]), 0)
    roll = lambda x, s: pltpu.roll(x, s % SUBLANES, axis=0)
    r = [logf2[SUBLANES * j:SUBLANES * (j + 1), :] for j in range(n)]
    out = []
    for lvl in range(1, CHUNK_LEVELS + 1):
        half = 1 << (lvl - 1)
        g = [None] * n
        if half < SUBLANES:
            second = (sub & half) != 0
            for j in range(n):
                last = r[j]
                w = 1
                while w < half:
                    last = jnp.where((sub & w) != 0, last, roll(last, -w))
                    w *= 2
                tot = jnp.where(second, roll(last, half), last)
                g[j] = jnp.where(second, r[j], tot - r[j])
                r[j] = jnp.where(second, r[j] + tot, r[j])
        else:
            hv = half // SUBLANES
            for j0 in range(0, n, 2 * hv):
                mid = r[j0 + hv - 1]
                tot = jnp.broadcast_to(mid[SUBLANES - 1:SUBLANES, :], mid.shape)
                for j in range(j0, j0 + hv):
                    g[j] = tot - r[j]
                for j in range(j0 + hv, j0 + 2 * hv):
                    g[j] = r[j]
                    r[j] = r[j] + tot
        use_level(lvl, jnp.concatenate(g, axis=0))
    return jnp.concatenate(r, axis=0)


def _sigmoid(x):
    return 0.5 * jnp.tanh(0.5 * x) + 0.5


def _split_levels():
    import numpy as np
    t = np.arange(CHUNK)[:, None]
    s = np.arange(CHUNK)[None, :]
    msb = np.floor(np.log2(np.maximum(t ^ s, 1))).astype(np.int32) + 1
    return np.where(s < t, msb, np.where(s == t, 0, -1)).astype(np.int32)


def _hgrn_chunk(q, z, v, lb, levels):
    half_th = 0.5 * jnp.tanh(0.5 * z)
    one_m_lb = 1.0 - lb
    logf2 = jnp.log2(lb + one_m_lb * (0.5 + half_th))
    k = one_m_lb * (0.5 - half_th)

    qb = q.astype(BF16)
    kb = k.astype(BF16)
    scores = [jnp.where(levels == 0, jnp.sum(q * k, axis=-1, keepdims=True), 0.0)]

    def use_level(lvl, g):
        decay = jnp.exp2(g).astype(BF16)
        scores[0] = jnp.where(levels == lvl, _dot_nt(qb * decay, kb * decay), scores[0])

    b2 = _level_exponents(logf2, use_level)
    a = scores[0]
    b2_last = b2[CHUNK - 1:CHUNK, :]

    qe = (q * jnp.exp2(b2)).astype(BF16)
    kd = (k * jnp.exp2(b2_last - b2)).astype(BF16)
    return a, qe, kd, v.astype(BF16), b2_last


def _hgrn_tile(chunks, st):
    (a0, qe0, kd0, v0, bl0), (a1, qe1, kd1, v1, bl1), (a2, qe2, kd2, v2, bl2), \
        (a3, qe3, kd3, v3, bl3) = chunks
    scale = lambda x, log2_decay: x * jnp.exp2(log2_decay).astype(BF16)
    bf = lambda x: x.astype(BF16)
    cum1 = bl0
    cum2 = cum1 + bl1
    cum3 = cum2 + bl2
    cum4 = cum3 + bl3
    s10 = bf(_dot_nt(qe1, kd0))
    s32 = bf(_dot_nt(qe3, kd2))
    s8 = bf(_dot_nt(jnp.concatenate([qe2, scale(qe3, bl2)], axis=0),
                    jnp.concatenate([scale(kd0, bl1), kd1], axis=0)))
    v01 = jnp.concatenate([v0, v1], axis=0)
    stb = bf(st)
    o0 = _dot(bf(a0), v0) + _dot_nt(qe0, stb)
    o1 = _dot(bf(a1), v1) + _dot(s10, v0) + _dot_nt(scale(qe1, cum1), stb)
    o2 = _dot(bf(a2), v2) + _dot(s8[:CHUNK], v01) + _dot_nt(scale(qe2, cum2), stb)
    o3 = (_dot(bf(a3), v3) + _dot(s32, v2) + _dot(s8[CHUNK:], v01)
          + _dot_nt(scale(qe3, cum3), stb))
    kd_all = jnp.concatenate(
        [scale(kd0, cum4 - cum1), scale(kd1, cum4 - cum2), scale(kd2, bl3), kd3], axis=0)
    v_all = jnp.concatenate([v0, v1, v2, v3], axis=0)
    st_new = st * jnp.exp2(cum4) + _dot_tn(v_all, kd_all)
    return [o0, o1, o2, o3], st_new


def _mixer_kernel(x_ref, gmix_ref, win_ref, convw_ref, lbraw_ref, hnorm_ref, wout_ref, lvl_ref,
                  o_ref, p_scr, y_scr, st_scr, tail_scr):
    j = pl.program_id(1)

    @pl.when(j == 0)
    def _():
        st_scr[...] = jnp.zeros_like(st_scr)
        tail_scr[...] = jnp.zeros_like(tail_scr)

    x = x_ref[0]
    h = _rms(x, gmix_ref[...]).astype(BF16)
    p_scr[...] = _dot(h, win_ref[...])

    ts = x.shape[0]
    cb = p_scr[:, 0:CONV_WIDTH]
    u = p_scr[:, CONV_WIDTH:2 * CONV_WIDTH] * p_scr[:, 2 * CONV_WIDTH:3 * CONV_WIDTH]
    row = lax.broadcasted_iota(jnp.int32, u.shape, 0)
    prev1 = tail_scr[7:8, :]
    prev2 = tail_scr[6:7, :]
    u1 = jnp.where(row == 0, prev1, _roll_rows(u, 1))
    u2 = jnp.where(row == 0, prev2, jnp.where(row == 1, prev1, _roll_rows(u, 2)))
    cw = convw_ref[...]
    conv = u2 * cw[0:1, :] + u1 * cw[1:2, :] + u * cw[2:3, :]
    y_scr[:, 0:CONV_WIDTH] = (cb * conv).astype(BF16)
    tail_scr[...] = u[ts - 8:ts, :]

    raw = lbraw_ref[...]
    mx = jnp.max(raw, axis=0, keepdims=True)
    ex = jnp.exp(raw - mx)
    lb_all = ex[0:1, :] / jnp.sum(ex, axis=0, keepdims=True)
    hn = hnorm_ref[...]

    for hd in range(HGRN_HEADS):
        lo = hd * HEAD_DIM
        sl = slice(lo, lo + HEAD_DIM)
        col = lambda slot: slice(slot * 512 + lo, slot * 512 + lo + HEAD_DIM)
        for t0 in range(0, ts, HGRN_TILE):
            chunks = []
            for r0 in range(t0, t0 + HGRN_TILE, CHUNK):
                rows = slice(r0, r0 + CHUNK)
                chunks.append(_hgrn_chunk(p_scr[rows, col(3)], p_scr[rows, col(4)],
                                          p_scr[rows, col(5)], lb_all[:, sl], lvl_ref[...]))
            outs, st_new = _hgrn_tile(chunks, st_scr[hd])
            st_scr[hd] = st_new
            for c, o in enumerate(outs):
                rows = slice(t0 + c * CHUNK, t0 + (c + 1) * CHUNK)
                g = p_scr[rows, col(6)]
                o = o * lax.rsqrt(jnp.mean(o * o, axis=-1, keepdims=True) + EPS) * hn[:, sl]
                y_scr[rows, CONV_WIDTH + lo:CONV_WIDTH + lo + HEAD_DIM] = \
                    (o * (g * _sigmoid(g))).astype(BF16)

    o_ref[0] = x + _dot(y_scr[...], wout_ref[...])


def _mixer(x, mix_norm, w_in, conv_w, hgrn_lb, hgrn_norm, w_out):
    bsz, seq, d = x.shape
    ts = SEQ_TILE
    const = lambda b, j: (0, 0)
    return pl.pallas_call(
        _mixer_kernel,
        grid=(bsz, seq // ts),
        in_specs=[
            pl.BlockSpec((1, ts, d), lambda b, j: (b, j, 0)),
            pl.BlockSpec((1, d), const),
            pl.BlockSpec((d, PROJ_WIDTH), const),
            pl.BlockSpec((3, CONV_WIDTH), const),
            pl.BlockSpec((2, HGRN_WIDTH), const),
            pl.BlockSpec((1, HGRN_WIDTH), const),
            pl.BlockSpec((d, d), const),
            pl.BlockSpec((CHUNK, CHUNK), const),
        ],
        out_specs=pl.BlockSpec((1, ts, d), lambda b, j: (b, j, 0)),
        out_shape=jax.ShapeDtypeStruct((bsz, seq, d), F32),
        scratch_shapes=[
            pltpu.VMEM((ts, PROJ_WIDTH), F32),
            pltpu.VMEM((ts, d), BF16),
            pltpu.VMEM((HGRN_HEADS, HEAD_DIM, HEAD_DIM), F32),
            pltpu.VMEM((8, CONV_WIDTH), F32),
        ],
        compiler_params=pltpu.CompilerParams(
            dimension_semantics=("arbitrary", "arbitrary"), vmem_limit_bytes=VMEM_LIMIT),
        name="mixer",
    )(x, mix_norm, w_in, conv_w, hgrn_lb, hgrn_norm, w_out, jnp.asarray(_split_levels()))


def _kv_kernel(m_ref, g_ref, w_ref, o_ref):
    h = _rms(m_ref[0], g_ref[...]).astype(BF16)
    o_ref[0] = _dot(h, w_ref[...]).astype(BF16)


def _kv_proj(mem, mem_norm, w_kv):
    bsz, mlen, d = mem.shape
    const = lambda b: (0, 0)
    return pl.pallas_call(
        _kv_kernel,
        grid=(bsz,),
        in_specs=[
            pl.BlockSpec((1, mlen, d), lambda b: (b, 0, 0)),
            pl.BlockSpec((1, d), const),
            pl.BlockSpec((d, 2 * d), const),
        ],
        out_specs=pl.BlockSpec((1, mlen, 2 * d), lambda b: (b, 0, 0)),
        out_shape=jax.ShapeDtypeStruct((bsz, mlen, 2 * d), BF16),
        compiler_params=pltpu.CompilerParams(
            dimension_semantics=("arbitrary",), vmem_limit_bytes=VMEM_LIMIT),
        name="kv_proj",
    )(mem, mem_norm, w_kv)


def _first_argmax(vals, lane):
    mx = jnp.max(vals, axis=-1, keepdims=True)
    idx = jnp.min(jnp.where(vals == mx, lane, float(LANES)), axis=-1, keepdims=True)
    return mx, idx


def _router_logits(h3, wr_ref, br_ref):
    h3_hi = h3.astype(BF16)
    h3_lo = (h3 - h3_hi.astype(F32)).astype(BF16)
    two = _dot(h3_hi, wr_ref[...])
    return (two[:, :LANES] + two[:, LANES:]) + _dot(h3_lo, wr_ref[:, :LANES]) + br_ref[...]


def _route(logits, tri):
    lane = lax.broadcasted_iota(jnp.int32, logits.shape, 1).astype(F32)
    neg = jnp.float32(-jnp.inf)
    gl = jnp.where(lane < N_GROUPS, logits, neg)
    _, gidx = _first_argmax(gl, lane)
    base = N_GROUPS + EXPERTS_PER_GROUP * gidx
    el = jnp.where((lane >= base) & (lane < base + EXPERTS_PER_GROUP), logits, neg)
    _, i1 = _first_argmax(el, lane)
    _, i2 = _first_argmax(jnp.where(lane == i1, neg, el), lane)
    lo = jnp.minimum(i1, i2) - base
    hi = jnp.maximum(i1, i2) - base
    bucket = gidx * N_PAIRS + (lo * (7.0 - lo)) * 0.5 + hi - lo - 1.0

    onehot = lane == bucket
    before = _dot(tri, onehot.astype(BF16))
    rank = jnp.sum(jnp.where(onehot, before, 0.0), axis=-1, keepdims=True)
    counts = jnp.sum(onehot.astype(F32), axis=0, keepdims=True)
    rank_hi = jnp.floor(rank * (1.0 / 256.0))
    rank_lo = rank - 256.0 * rank_hi
    info = jnp.where(lane == 0, bucket, jnp.where(lane == 1, rank_hi,
                                                  jnp.where(lane == 2, rank_lo, 0.0)))
    return info, counts


def _gates(logits, e_lo, e_hi):
    lane = lax.broadcasted_iota(jnp.int32, logits.shape, 1)
    neg = jnp.float32(-jnp.inf)
    gl = jnp.where(lane < N_GROUPS, logits, neg)
    g_p = 1.0 / jnp.sum(jnp.exp(gl - jnp.max(gl, axis=-1, keepdims=True)), axis=-1, keepdims=True)
    l_lo = jnp.sum(jnp.where(lane == N_GROUPS + e_lo, logits, 0.0), axis=-1, keepdims=True)
    l_hi = jnp.sum(jnp.where(lane == N_GROUPS + e_hi, logits, 0.0), axis=-1, keepdims=True)
    m = jnp.maximum(l_lo, l_hi)
    p_lo = jnp.exp(l_lo - m)
    p_hi = jnp.exp(l_hi - m)
    inv = g_p / (p_lo + p_hi)
    return p_lo * inv, p_hi * inv


def _chunk_no(n):
    return lax.shift_right_logical(n, MOE_TILE.bit_length() - 1)


def _chunk_row(n):
    return n & (MOE_TILE - 1)


def _send_tile(tile, slot, xs_hbm, pos_ref, owner_ref, nfree_ref,
               stage, meta_s, cnt_s, base_s, tbl_s, sem):
    tq = ATTN_TILE
    mrow = slot * META_ROWS

    def alloc(b, carry):
        c0 = cnt_s[b]
        c1 = c0 + meta_s[mrow + 8, b]

        def take(k, c):
            nf = nfree_ref[0]
            tbl_s[b * MAX_CHUNKS + k] = nf
            owner_ref[nf] = b
            nfree_ref[0] = nf + 1
            return c

        lax.fori_loop(_chunk_no(c0 + MOE_TILE - 1), _chunk_no(c1 + MOE_TILE - 1), take, 0)
        base_s[b] = c0
        cnt_s[b] = c1
        return carry

    lax.fori_loop(0, N_BUCKETS, alloc, 0)

    def send(r2, carry):
        for k in range(DMA_QUEUES):
            r = r2 * DMA_QUEUES + k
            b = meta_s[mrow, r]
            rank = base_s[b] + meta_s[mrow + 1, r] * 256 + meta_s[mrow + 2, r]
            p = tbl_s[b * MAX_CHUNKS + _chunk_no(rank)] * MOE_TILE + _chunk_row(rank)
            pos_ref[tile * tq + r] = p
            _item_copy(xs_hbm, p, stage, slot * tq + r, ROW_TILES, sem.at[slot],
                       True).start(priority=k)
        return carry

    lax.fori_loop(0, tq // DMA_QUEUES, send, 0, unroll=4)


def _attn_kernel(x_ref, kv_ref, gx_ref, wq_ref, wo_ref, gf_ref, wr_ref, br_ref, tri_ref, eye_ref,
                 xs_hbm, pos_ref, owner_ref, nfree_ref,
                 o_scr, stage, meta_v, meta_s, cnt_s, base_s, tbl_s, sem_rows, sem_meta):
    g = pl.program_id(0) * pl.num_programs(1) + pl.program_id(1)
    n_steps = pl.num_programs(0) * pl.num_programs(1)
    tq = ATTN_TILE
    slot = g % 2
    n_chunks = owner_ref.shape[0]

    def meta_copy(s):
        return pltpu.make_async_copy(
            meta_v, meta_s.at[pl.ds(pl.multiple_of(s * META_ROWS, META_ROWS), META_ROWS)],
            sem_meta.at[s])

    send = functools.partial(_send_tile, xs_hbm=xs_hbm, pos_ref=pos_ref, owner_ref=owner_ref,
                             nfree_ref=nfree_ref, stage=stage, meta_s=meta_s, cnt_s=cnt_s,
                             base_s=base_s, tbl_s=tbl_s, sem=sem_rows)
    rows_wait = lambda s: _items_wait(xs_hbm, stage, s * tq, tq, ROW_TILES, sem_rows.at[s], True)

    @pl.when(g == 0)
    def _():
        nfree_ref[0] = 0
        for b in range(N_BUCKETS):
            cnt_s[b] = 0

        def clear(c, carry):
            owner_ref[c] = N_BUCKETS - 1
            return carry

        lax.fori_loop(0, n_chunks, clear, 0)

        def clear_tbl(c, carry):
            tbl_s[c] = 0
            return carry

        lax.fori_loop(0, N_BUCKETS * MAX_CHUNKS, clear_tbl, 0)
        meta_v[...] = jnp.zeros_like(meta_v)

    @pl.when(g >= 1)
    def _():
        meta_copy(1 - slot).wait()
        send(g - 1, 1 - slot)

    x = x_ref[0]
    h = _rms(x, gx_ref[...]).astype(BF16)
    q = _dot(h, wq_ref[...])
    for hd in range(XATTN_HEADS):
        lo = hd * XATTN_HEAD_DIM
        qh = q[:, lo:lo + XATTN_HEAD_DIM].astype(BF16)
        kh = kv_ref[0, :, lo:lo + XATTN_HEAD_DIM]
        vh = kv_ref[0, :, D_MODEL + lo:D_MODEL + lo + XATTN_HEAD_DIM]
        s = _dot_nt(qh, kh)
        p = jnp.exp(s - jnp.max(s, axis=-1, keepdims=True))
        inv = 1.0 / jnp.sum(p, axis=-1, keepdims=True)
        o_scr[:, lo:lo + XATTN_HEAD_DIM] = (_dot(p.astype(BF16), vh) * inv).astype(BF16)
    x2 = x + _dot(o_scr[...], wo_ref[...])
    logits = _router_logits(_rms(x2, gf_ref[...]), wr_ref, br_ref)
    info, counts = _route(logits, tri_ref[...])

    @pl.when(g >= 2)
    def _():
        rows_wait(slot)

    row0 = pl.multiple_of(slot * tq * ROW_TILES, tq * ROW_TILES)
    for s in range(ROW_TILES):
        stage[pl.ds(row0 + s, tq, stride=ROW_TILES), :] = x2[:, s * LANES:(s + 1) * LANES]
    meta_v[0:8, :] = _dot_tn(info.astype(BF16), eye_ref[...])[0:8, :].astype(jnp.int32)
    meta_v[8:16, 0:LANES] = jnp.broadcast_to(counts, (8, LANES)).astype(jnp.int32)
    meta_copy(slot).start()

    @pl.when(g == n_steps - 1)
    def _():
        meta_copy(slot).wait()
        send(g, slot)
        rows_wait(1 - slot)
        rows_wait(slot)
        stage[...] = jnp.zeros_like(stage)
        zsem = sem_rows.at[2]

        def pad_bucket(b, carry):
            c = cnt_s[b]
            used = _chunk_row(c)
            chunk = tbl_s[b * MAX_CHUNKS + _chunk_no(jnp.maximum(c - 1, 0))]
            first = jnp.where(used == 0, MOE_TILE, used)

            def fill(r, cc):
                _item_copy(xs_hbm, chunk * MOE_TILE + r, stage, 0, ROW_TILES, zsem, True).start()
                return cc

            def drain(r, cc):
                _item_copy(xs_hbm, chunk * MOE_TILE + r, stage, 0, ROW_TILES, zsem, True).wait()
                return cc

            lax.fori_loop(first, MOE_TILE, fill, 0)
            lax.fori_loop(first, MOE_TILE, drain, 0)
            return carry

        lax.fori_loop(0, N_BUCKETS, pad_bucket, 0)

        def fill_chunk(c, carry):
            rows = MOE_TILE * ROW_TILES
            cp = pltpu.make_async_copy(
                stage.at[pl.ds(0, rows)],
                xs_hbm.at[pl.ds(pl.multiple_of(c * rows, rows), rows)], zsem)
            cp.start()
            cp.wait()
            return carry

        lax.fori_loop(nfree_ref[0], n_chunks, fill_chunk, 0)


def _attention(x1, kv, xattn_norm, w_q, w_o, ffn_norm, w_router, b_router):
    bsz, seq, d = x1.shape
    mlen = kv.shape[1]
    tq = ATTN_TILE
    nj = seq // tq
    const = lambda b, j: (0, 0)
    tile = lambda b, j: (b, j, 0)
    tri = jnp.tri(tq, tq, -1, dtype=BF16)
    eye = jnp.eye(tq, dtype=BF16)
    t = bsz * seq
    n_chunks = t // MOE_TILE + N_BUCKETS
    assert n_chunks <= MAX_CHUNKS and bsz * nj >= 2 and tq * ROW_TILES >= MOE_TILE * ROW_TILES
    smem = pl.BlockSpec(memory_space=pltpu.SMEM)
    return pl.pallas_call(
        _attn_kernel,
        grid=(bsz, nj),
        in_specs=[
            pl.BlockSpec((1, tq, d), tile),
            pl.BlockSpec((1, mlen, 2 * d), lambda b, j: (b, 0, 0)),
            pl.BlockSpec((1, d), const),
            pl.BlockSpec((d, d), const),
            pl.BlockSpec((d, d), const),
            pl.BlockSpec((1, d), const),
            pl.BlockSpec((d, 2 * LANES), const),
            pl.BlockSpec((1, LANES), const),
            pl.BlockSpec((tq, tq), const),
            pl.BlockSpec((tq, tq), const),
        ],
        out_specs=[pl.BlockSpec(memory_space=pl.ANY), smem, smem, smem],
        out_shape=[
            jax.ShapeDtypeStruct((n_chunks * MOE_TILE * ROW_TILES, LANES), F32),
            jax.ShapeDtypeStruct((t,), jnp.int32),
            jax.ShapeDtypeStruct((n_chunks,), jnp.int32),
            jax.ShapeDtypeStruct((1,), jnp.int32),
        ],
        scratch_shapes=[
            pltpu.VMEM((tq, d), BF16),
            pltpu.VMEM((2 * tq * ROW_TILES, LANES), F32),
            pltpu.VMEM((META_ROWS, tq), jnp.int32),
            pltpu.SMEM((2 * META_ROWS, tq), jnp.int32),
            pltpu.SMEM((N_BUCKETS,), jnp.int32),
            pltpu.SMEM((N_BUCKETS,), jnp.int32),
            pltpu.SMEM((N_BUCKETS * MAX_CHUNKS,), jnp.int32),
            pltpu.SemaphoreType.DMA((3,)),
            pltpu.SemaphoreType.DMA((2,)),
        ],
        compiler_params=pltpu.CompilerParams(
            dimension_semantics=("arbitrary", "arbitrary"), vmem_limit_bytes=VMEM_LIMIT),
        name="xattn_router",
    )(x1, kv, xattn_norm, w_q, w_o, ffn_norm, w_router, b_router, tri, eye)


def _item_copy(hbm, hbm_item, buf, buf_item, rows_per_item, sem, to_hbm):
    h = hbm.at[pl.ds(pl.multiple_of(hbm_item * rows_per_item, rows_per_item), rows_per_item)]
    b = buf.at[pl.ds(pl.multiple_of(buf_item * rows_per_item, rows_per_item), rows_per_item)]
    return pltpu.make_async_copy(b, h, sem) if to_hbm else pltpu.make_async_copy(h, b, sem)


def _items_wait(hbm, buf, buf_item, n_items, rows_per_item, sem, to_hbm):
    n = n_items * rows_per_item
    h = hbm.at[pl.ds(0, n)]
    b = buf.at[pl.ds(pl.multiple_of(buf_item * rows_per_item, rows_per_item), n)]
    (pltpu.make_async_copy(b, h, sem) if to_hbm else pltpu.make_async_copy(h, b, sem)).wait()


def _silu(x):
    return x * _sigmoid(x)


def _moe_kernel(elo_ref, ehi_ref, used_ref, order_ref,
                xs_ref, gf_ref, wr_ref, br_ref, gfin_ref,
                wg_lo, wu_lo, wd_lo, wg_hi, wu_hi, wd_hi, y_ref):
    del order_ref
    i = pl.program_id(0)
    tm = MOE_TILE

    @pl.when(used_ref[i] == 1)
    def _():
        x2 = jnp.concatenate(
            [xs_ref[pl.ds(s, tm, stride=ROW_TILES), :] for s in range(ROW_TILES)], axis=1)
        h3 = _rms(x2, gf_ref[...])
        g_lo, g_hi = _gates(_router_logits(h3, wr_ref, br_ref), elo_ref[i], ehi_ref[i])
        x = h3.astype(BF16)

        def expert(wg, wu, wd):
            hid = (_silu(_dot(x, wg[0])) * _dot(x, wu[0])).astype(BF16)
            return _dot(hid, wd[0])

        moe = g_lo * expert(wg_lo, wu_lo, wd_lo) + g_hi * expert(wg_hi, wu_hi, wd_hi)
        out = _rms(x2 + moe, gfin_ref[...])
        for s in range(ROW_TILES):
            y_ref[pl.ds(s, tm, stride=ROW_TILES), :] = out[:, s * LANES:(s + 1) * LANES]

    @pl.when(used_ref[i] == 0)
    def _():
        y_ref[...] = jnp.zeros_like(y_ref)


def _moe_sparse(e_lo, e_hi, used, order, xs, ffn_norm, w_router, b_router, final_norm,
                w_gate, w_up, w_down):
    tm = MOE_TILE
    nt = used.shape[0]
    d = D_MODEL
    lo = lambda i, elo, ehi, used, order: (elo[i], 0, 0)
    hi = lambda i, elo, ehi, used, order: (ehi[i], 0, 0)
    chunk = lambda i, elo, ehi, used, order: (order[i], 0)
    const = lambda i, *_: (0, 0)
    grid_spec = pltpu.PrefetchScalarGridSpec(
        num_scalar_prefetch=4,
        grid=(nt,),
        in_specs=[
            pl.BlockSpec((tm * ROW_TILES, LANES), chunk),
            pl.BlockSpec((1, d), const),
            pl.BlockSpec((d, 2 * LANES), const),
            pl.BlockSpec((1, LANES), const),
            pl.BlockSpec((1, d), const),
            pl.BlockSpec((1, d, D_EXPERT), lo),
            pl.BlockSpec((1, d, D_EXPERT), lo),
            pl.BlockSpec((1, D_EXPERT, d), lo),
            pl.BlockSpec((1, d, D_EXPERT), hi),
            pl.BlockSpec((1, d, D_EXPERT), hi),
            pl.BlockSpec((1, D_EXPERT, d), hi),
        ],
        out_specs=pl.BlockSpec((tm * ROW_TILES, LANES), chunk),
    )
    return pl.pallas_call(
        _moe_kernel,
        grid_spec=grid_spec,
        out_shape=jax.ShapeDtypeStruct((nt * tm * ROW_TILES, LANES), F32),
        compiler_params=pltpu.CompilerParams(
            dimension_semantics=("arbitrary",), vmem_limit_bytes=VMEM_LIMIT),
        name="moe_sparse",
    )(e_lo, e_hi, used, order, xs, ffn_norm, w_router, b_router, final_norm,
      w_gate, w_up, w_down, w_gate, w_up, w_down)


def _final_kernel(pos_ref, y_hbm, o_ref, ybuf, sem):
    i = pl.program_id(0)
    nt = pl.num_programs(0)
    tf = FINAL_TILE

    def start(tile, slot):
        def fetch(r2, carry):
            for k in range(DMA_QUEUES):
                r = r2 * DMA_QUEUES + k
                _item_copy(y_hbm, pos_ref[tile * tf + r], ybuf, slot * tf + r, ROW_TILES,
                           sem.at[slot], False).start(priority=k)
            return carry

        lax.fori_loop(0, tf // DMA_QUEUES, fetch, 0, unroll=4)

    @pl.when(i == 0)
    def _():
        start(0, 0)

    @pl.when(i + 1 < nt)
    def _():
        start(i + 1, (i + 1) % 2)

    slot = i % 2
    _items_wait(y_hbm, ybuf, slot * tf, tf, ROW_TILES, sem.at[slot], False)
    row0 = pl.multiple_of(slot * tf * ROW_TILES, ROW_TILES)
    for s in range(ROW_TILES):
        o_ref[:, s * LANES:(s + 1) * LANES] = ybuf[pl.ds(row0 + s, tf, stride=ROW_TILES), :]


def _final(pos, y):
    t = pos.shape[0]
    d = D_MODEL
    tf = FINAL_TILE
    grid_spec = pltpu.PrefetchScalarGridSpec(
        num_scalar_prefetch=1,
        grid=(t // tf,),
        in_specs=[pl.BlockSpec(memory_space=pl.ANY)],
        out_specs=pl.BlockSpec((tf, d), lambda i, *_: (i, 0)),
        scratch_shapes=[
            pltpu.VMEM((2 * tf * ROW_TILES, LANES), F32),
            pltpu.SemaphoreType.DMA((2,)),
        ],
    )
    return pl.pallas_call(
        _final_kernel,
        grid_spec=grid_spec,
        out_shape=jax.ShapeDtypeStruct((t, d), F32),
        compiler_params=pltpu.CompilerParams(
            dimension_semantics=("arbitrary",), vmem_limit_bytes=VMEM_LIMIT),
        name="moe_unpermute",
    )(pos, y)


def _chunk_tables(owner, nfree):
    n = owner.shape[0]
    ids = jnp.arange(n, dtype=jnp.int32)
    key = jnp.where(ids < nfree[0], owner, N_BUCKETS) * n + ids
    slot_of_chunk = jnp.sum((key[None, :] < key[:, None]).astype(jnp.int32), axis=1)
    order = jnp.sum(jnp.where(slot_of_chunk[None, :] == ids[:, None], ids[None, :], 0), axis=1)
    used = (order < nfree[0]).astype(jnp.int32)
    tile_bucket = owner[order]
    pair = tile_bucket % N_PAIRS
    group0 = (tile_bucket // N_PAIRS) * EXPERTS_PER_GROUP
    e_lo = group0 + jnp.array([0, 0, 0, 1, 1, 2], jnp.int32)[pair]
    e_hi = group0 + jnp.array([1, 2, 3, 2, 3, 3], jnp.int32)[pair]
    return e_lo, e_hi, used, order


def kernel(x, mem, mix_norm, w_in, conv_w, hgrn_lb, hgrn_norm, w_out, xattn_norm, mem_norm,
           w_q, w_kv, w_o, ffn_norm, w_group, b_group, w_expert, b_expert, w_gate, w_up,
           w_down, final_norm):
    bsz, seq, d = x.shape
    assert d == D_MODEL and seq % SEQ_TILE == 0 and seq % ATTN_TILE == 0
    assert (bsz * seq) % MOE_TILE == 0 and (bsz * seq) % FINAL_TILE == 0
    assert mix_norm.shape[0] == 1
    bf = lambda w: w.astype(BF16)

    x1 = _mixer(x, mix_norm, bf(w_in[0]), conv_w[0], hgrn_lb, hgrn_norm, bf(w_out[0]))
    kv = _kv_proj(mem, mem_norm, bf(w_kv[0]))

    pad = LANES - N_GROUPS - N_EXPERTS
    w_router = jnp.concatenate(
        [w_group[0], w_expert[0], jnp.zeros((d, pad), F32)], axis=1)
    b_router = jnp.concatenate(
        [b_group[0], b_expert[0], jnp.zeros((pad,), F32)])[None, :]
    w_router_hi = bf(w_router)
    w_router_lo = bf(w_router - w_router_hi.astype(F32))
    w_router2 = jnp.concatenate([w_router_hi, w_router_lo], axis=1)
    w_q_scaled = bf(w_q[0] * (XATTN_HEAD_DIM ** -0.5))
    xs, pos, owner, nfree = _attention(x1, kv, xattn_norm, w_q_scaled, bf(w_o[0]), ffn_norm,
                                       w_router2, b_router)
    e_lo, e_hi, used, order = _chunk_tables(owner, nfree)
    y = _moe_sparse(e_lo, e_hi, used, order, xs, ffn_norm, w_router2, b_router,
                    final_norm[None, :], bf(w_gate[0]), bf(w_up[0]), bf(w_down[0]))
    return _final(pos, y).reshape(bsz, seq, d)
```

```python
import jax
import jax.numpy as jnp
from jax import lax
from jax.experimental import pallas as pl
from jax.experimental.pallas import tpu as pltpu

F32 = jnp.float32
BF16 = jnp.bfloat16

D_MODEL = 1024
CONV_WIDTH = 512
HGRN_WIDTH = 512
HGRN_HEADS = 4
HEAD_DIM = 128
N_PROJ_SLOTS = 7
PROJ_WIDTH = N_PROJ_SLOTS * 512
XATTN_HEADS = 4
XATTN_HEAD_DIM = 256
N_GROUPS = 4
EXPERTS_PER_GROUP = 4
N_EXPERTS = 16
D_EXPERT = 512
EPS = 1e-6

LANES = 128
SUBLANES = 8
CHUNK = 64
CHUNK_LEVELS = 6
SEQ_TILE = 256
ATTN_TILE = 512
DISPATCH_TILE = 512
MOE_TILE = 256
FINAL_TILE = 512
N_PAIRS = 6
N_BUCKETS = N_GROUPS * N_PAIRS
ROW_TILES = D_MODEL // LANES
DMA_QUEUES = 2
VMEM_LIMIT = 56 * 1024 * 1024


def _rms(x, g):
    return x * lax.rsqrt(jnp.mean(x * x, axis=-1, keepdims=True) + EPS) * g


def _dot(a, b):
    return jnp.dot(a, b, preferred_element_type=F32)


def _dot_nt(a, b):
    return lax.dot_general(a, b, (((1,), (1,)), ((), ())), preferred_element_type=F32)


def _dot_tn(a, b):
    return lax.dot_general(a, b, (((0,), (0,)), ((), ())), preferred_element_type=F32)


def _roll_rows(x, shift):
    return pltpu.roll(x, shift % x.shape[0], axis=0)


def _level_exponents(logf2, use_level):
    n = CHUNK // SUBLANES
    sub = lax.broadcasted_iota(jnp.int32, (SUBLANES, logf2.shape[1]), 0)
    roll = lambda x, s: pltpu.roll(x, s % SUBLANES, axis=0)
    r = [logf2[SUBLANES * j:SUBLANES * (j + 1), :] for j in range(n)]
    for lvl in range(1, CHUNK_LEVELS + 1):
        half = 1 << (lvl - 1)
        g = [None] * n
        if half < SUBLANES:
            second = (sub & half) != 0
            for j in range(n):
                last = r[j]
                w = 1
                while w < half:
                    last = jnp.where((sub & w) != 0, last, roll(last, -w))
                    w *= 2
                tot = jnp.where(second, roll(last, half), last)
                g[j] = jnp.where(second, r[j], tot - r[j])
                r[j] = jnp.where(second, r[j] + tot, r[j])
        else:
            hv = half // SUBLANES
            for j0 in range(0, n, 2 * hv):
                mid = r[j0 + hv - 1]
                tot = jnp.broadcast_to(mid[SUBLANES - 1:SUBLANES, :], mid.shape)
                for j in range(j0, j0 + hv):
                    g[j] = tot - r[j]
                for j in range(j0 + hv, j0 + 2 * hv):
                    g[j] = r[j]
                    r[j] = r[j] + tot
        use_level(lvl, jnp.concatenate(g, axis=0))
    return jnp.concatenate(r, axis=0)


def _sigmoid(x):
    return 0.5 * jnp.tanh(0.5 * x) + 0.5


def _split_levels():
    import numpy as np
    t = np.arange(CHUNK)[:, None]
    s = np.arange(CHUNK)[None, :]
    msb = np.floor(np.log2(np.maximum(t ^ s, 1))).astype(np.int32) + 1
    return np.where(s < t, msb, np.where(s == t, 0, -1)).astype(np.int32)


def _hgrn_chunk(q, z, v, lb, levels):
    half_th = 0.5 * jnp.tanh(0.5 * z)
    one_m_lb = 1.0 - lb
    logf2 = jnp.log2(lb + one_m_lb * (0.5 + half_th))
    k = one_m_lb * (0.5 - half_th)

    qb = q.astype(BF16)
    kb = k.astype(BF16)
    scores = [jnp.where(levels == 0, jnp.sum(q * k, axis=-1, keepdims=True), 0.0)]

    def use_level(lvl, g):
        decay = jnp.exp2(g).astype(BF16)
        scores[0] = jnp.where(levels == lvl, _dot_nt(qb * decay, kb * decay), scores[0])

    b2 = _level_exponents(logf2, use_level)
    a = scores[0]
    b2_last = b2[CHUNK - 1:CHUNK, :]

    qe = (q * jnp.exp2(b2)).astype(BF16)
    kd = (k * jnp.exp2(b2_last - b2)).astype(BF16)
    return a, qe, kd, v.astype(BF16), b2_last


def _hgrn_tile(chunks, st):
    (a0, qe0, kd0, v0, bl0), (a1, qe1, kd1, v1, bl1), (a2, qe2, kd2, v2, bl2), \
        (a3, qe3, kd3, v3, bl3) = chunks
    scale = lambda x, log2_decay: x * jnp.exp2(log2_decay).astype(BF16)
    bf = lambda x: x.astype(BF16)
    cum1 = bl0
    cum2 = cum1 + bl1
    cum3 = cum2 + bl2
    cum4 = cum3 + bl3
    s10 = bf(_dot_nt(qe1, kd0))
    s32 = bf(_dot_nt(qe3, kd2))
    s8 = bf(_dot_nt(jnp.concatenate([qe2, scale(qe3, bl2)], axis=0),
                    jnp.concatenate([scale(kd0, bl1), kd1], axis=0)))
    v01 = jnp.concatenate([v0, v1], axis=0)
    stb = bf(st)
    o0 = _dot(bf(a0), v0) + _dot_nt(qe0, stb)
    o1 = _dot(bf(a1), v1) + _dot(s10, v0) + _dot_nt(scale(qe1, cum1), stb)
    o2 = _dot(bf(a2), v2) + _dot(s8[:CHUNK], v01) + _dot_nt(scale(qe2, cum2), stb)
    o3 = (_dot(bf(a3), v3) + _dot(s32, v2) + _dot(s8[CHUNK:], v01)
          + _dot_nt(scale(qe3, cum3), stb))
    kd_all = jnp.concatenate(
        [scale(kd0, cum4 - cum1), scale(kd1, cum4 - cum2), scale(kd2, bl3), kd3], axis=0)
    v_all = jnp.concatenate([v0, v1, v2, v3], axis=0)
    st_new = st * jnp.exp2(cum4) + _dot_tn(v_all, kd_all)
    return [o0, o1, o2, o3], st_new


def _mixer_kernel(x_ref, gmix_ref, win_ref, convw_ref, lbraw_ref, hnorm_ref, wout_ref, lvl_ref,
                  o_ref, p_scr, y_scr, st_scr, tail_scr):
    j = pl.program_id(1)

    @pl.when(j == 0)
    def _():
        st_scr[...] = jnp.zeros_like(st_scr)
        tail_scr[...] = jnp.zeros_like(tail_scr)

    x = x_ref[0]
    h = _rms(x, gmix_ref[...]).astype(BF16)
    p_scr[...] = _dot(h, win_ref[...])

    ts = x.shape[0]
    cb = p_scr[:, 0:CONV_WIDTH]
    u = p_scr[:, CONV_WIDTH:2 * CONV_WIDTH] * p_scr[:, 2 * CONV_WIDTH:3 * CONV_WIDTH]
    row = lax.broadcasted_iota(jnp.int32, u.shape, 0)
    prev1 = tail_scr[7:8, :]
    prev2 = tail_scr[6:7, :]
    u1 = jnp.where(row == 0, prev1, _roll_rows(u, 1))
    u2 = jnp.where(row == 0, prev2, jnp.where(row == 1, prev1, _roll_rows(u, 2)))
    cw = convw_ref[...]
    conv = u2 * cw[0:1, :] + u1 * cw[1:2, :] + u * cw[2:3, :]
    y_scr[:, 0:CONV_WIDTH] = (cb * conv).astype(BF16)
    tail_scr[...] = u[ts - 8:ts, :]

    raw = lbraw_ref[...]
    mx = jnp.max(raw, axis=0, keepdims=True)
    ex = jnp.exp(raw - mx)
    lb_all = ex[0:1, :] / jnp.sum(ex, axis=0, keepdims=True)
    hn = hnorm_ref[...]

    for hd in range(HGRN_HEADS):
        lo = hd * HEAD_DIM
        sl = slice(lo, lo + HEAD_DIM)
        col = lambda slot: slice(slot * 512 + lo, slot * 512 + lo + HEAD_DIM)
        chunks = []
        for c in range(ts // CHUNK):
            rows = slice(c * CHUNK, (c + 1) * CHUNK)
            chunks.append(_hgrn_chunk(p_scr[rows, col(3)], p_scr[rows, col(4)],
                                      p_scr[rows, col(5)], lb_all[:, sl], lvl_ref[...]))
        outs, st_new = _hgrn_tile(chunks, st_scr[hd])
        st_scr[hd] = st_new
        for c, o in enumerate(outs):
            rows = slice(c * CHUNK, (c + 1) * CHUNK)
            g = p_scr[rows, col(6)]
            o = o * lax.rsqrt(jnp.mean(o * o, axis=-1, keepdims=True) + EPS) * hn[:, sl]
            y_scr[rows, CONV_WIDTH + lo:CONV_WIDTH + lo + HEAD_DIM] = \
                (o * (g * _sigmoid(g))).astype(BF16)

    o_ref[0] = x + _dot(y_scr[...], wout_ref[...])


def _mixer(x, mix_norm, w_in, conv_w, hgrn_lb, hgrn_norm, w_out):
    bsz, seq, d = x.shape
    ts = SEQ_TILE
    const = lambda b, j: (0, 0)
    return pl.pallas_call(
        _mixer_kernel,
        grid=(bsz, seq // ts),
        in_specs=[
            pl.BlockSpec((1, ts, d), lambda b, j: (b, j, 0)),
            pl.BlockSpec((1, d), const),
            pl.BlockSpec((d, PROJ_WIDTH), const),
            pl.BlockSpec((3, CONV_WIDTH), const),
            pl.BlockSpec((2, HGRN_WIDTH), const),
            pl.BlockSpec((1, HGRN_WIDTH), const),
            pl.BlockSpec((d, d), const),
            pl.BlockSpec((CHUNK, CHUNK), const),
        ],
        out_specs=pl.BlockSpec((1, ts, d), lambda b, j: (b, j, 0)),
        out_shape=jax.ShapeDtypeStruct((bsz, seq, d), F32),
        scratch_shapes=[
            pltpu.VMEM((ts, PROJ_WIDTH), F32),
            pltpu.VMEM((ts, d), BF16),
            pltpu.VMEM((HGRN_HEADS, HEAD_DIM, HEAD_DIM), F32),
            pltpu.VMEM((8, CONV_WIDTH), F32),
        ],
        compiler_params=pltpu.CompilerParams(
            dimension_semantics=("arbitrary", "arbitrary"), vmem_limit_bytes=VMEM_LIMIT),
        name="mixer",
    )(x, mix_norm, w_in, conv_w, hgrn_lb, hgrn_norm, w_out, jnp.asarray(_split_levels()))


def _kv_kernel(m_ref, g_ref, w_ref, o_ref):
    h = _rms(m_ref[0], g_ref[...]).astype(BF16)
    o_ref[0] = _dot(h, w_ref[...]).astype(BF16)


def _kv_proj(mem, mem_norm, w_kv):
    bsz, mlen, d = mem.shape
    const = lambda b: (0, 0)
    return pl.pallas_call(
        _kv_kernel,
        grid=(bsz,),
        in_specs=[
            pl.BlockSpec((1, mlen, d), lambda b: (b, 0, 0)),
            pl.BlockSpec((1, d), const),
            pl.BlockSpec((d, 2 * d), const),
        ],
        out_specs=pl.BlockSpec((1, mlen, 2 * d), lambda b: (b, 0, 0)),
        out_shape=jax.ShapeDtypeStruct((bsz, mlen, 2 * d), BF16),
        compiler_params=pltpu.CompilerParams(
            dimension_semantics=("arbitrary",), vmem_limit_bytes=VMEM_LIMIT),
        name="kv_proj",
    )(mem, mem_norm, w_kv)


def _first_argmax(vals, lane):
    mx = jnp.max(vals, axis=-1, keepdims=True)
    idx = jnp.min(jnp.where(vals == mx, lane, float(LANES)), axis=-1, keepdims=True)
    return mx, idx


def _router_logits(h3, wr_ref, br_ref):
    h3_hi = h3.astype(BF16)
    h3_lo = (h3 - h3_hi.astype(F32)).astype(BF16)
    two = _dot(h3_hi, wr_ref[...])
    return (two[:, :LANES] + two[:, LANES:]) + _dot(h3_lo, wr_ref[:, :LANES]) + br_ref[...]


def _route(logits, running, tri):
    lane = lax.broadcasted_iota(jnp.int32, logits.shape, 1).astype(F32)
    neg = jnp.float32(-jnp.inf)
    gl = jnp.where(lane < N_GROUPS, logits, neg)
    _, gidx = _first_argmax(gl, lane)
    base = N_GROUPS + EXPERTS_PER_GROUP * gidx
    el = jnp.where((lane >= base) & (lane < base + EXPERTS_PER_GROUP), logits, neg)
    _, i1 = _first_argmax(el, lane)
    _, i2 = _first_argmax(jnp.where(lane == i1, neg, el), lane)
    lo = jnp.minimum(i1, i2) - base
    hi = jnp.maximum(i1, i2) - base
    bucket = gidx * N_PAIRS + (lo * (7.0 - lo)) * 0.5 + hi - lo - 1.0

    onehot = lane == bucket
    before = _dot(tri, onehot.astype(BF16))
    rank = jnp.sum(jnp.where(onehot, before + running, 0.0), axis=-1, keepdims=True)
    running = running + jnp.sum(onehot.astype(F32), axis=0, keepdims=True)
    rank_hi = jnp.floor(rank * (1.0 / 256.0))
    rank_lo = rank - 256.0 * rank_hi
    info = jnp.where(lane == 0, bucket, jnp.where(lane == 1, rank_hi,
                                                  jnp.where(lane == 2, rank_lo, 0.0)))
    return info, running


def _gates(logits, e_lo, e_hi):
    lane = lax.broadcasted_iota(jnp.int32, logits.shape, 1)
    neg = jnp.float32(-jnp.inf)
    gl = jnp.where(lane < N_GROUPS, logits, neg)
    g_p = 1.0 / jnp.sum(jnp.exp(gl - jnp.max(gl, axis=-1, keepdims=True)), axis=-1, keepdims=True)
    l_lo = jnp.sum(jnp.where(lane == N_GROUPS + e_lo, logits, 0.0), axis=-1, keepdims=True)
    l_hi = jnp.sum(jnp.where(lane == N_GROUPS + e_hi, logits, 0.0), axis=-1, keepdims=True)
    m = jnp.maximum(l_lo, l_hi)
    p_lo = jnp.exp(l_lo - m)
    p_hi = jnp.exp(l_hi - m)
    inv = g_p / (p_lo + p_hi)
    return p_lo * inv, p_hi * inv


def _attn_kernel(x_ref, kv_ref, gx_ref, wq_ref, wo_ref, gf_ref, wr_ref, br_ref, tri_ref, eye_ref,
                 x2r_ref, brk_ref, counts_ref, o_scr, cnt_scr):
    @pl.when((pl.program_id(0) == 0) & (pl.program_id(1) == 0))
    def _():
        cnt_scr[...] = jnp.zeros_like(cnt_scr)

    x = x_ref[0]
    tq = x.shape[0]
    h = _rms(x, gx_ref[...]).astype(BF16)
    q = _dot(h, wq_ref[...])
    for hd in range(XATTN_HEADS):
        lo = hd * XATTN_HEAD_DIM
        qh = q[:, lo:lo + XATTN_HEAD_DIM].astype(BF16)
        kh = kv_ref[0, :, lo:lo + XATTN_HEAD_DIM]
        vh = kv_ref[0, :, D_MODEL + lo:D_MODEL + lo + XATTN_HEAD_DIM]
        s = _dot_nt(qh, kh)
        p = jnp.exp(s - jnp.max(s, axis=-1, keepdims=True))
        inv = 1.0 / jnp.sum(p, axis=-1, keepdims=True)
        o_scr[:, lo:lo + XATTN_HEAD_DIM] = (_dot(p.astype(BF16), vh) * inv).astype(BF16)
    x2 = x + _dot(o_scr[...], wo_ref[...])
    for s in range(ROW_TILES):
        x2r_ref[pl.ds(s, tq, stride=ROW_TILES), :] = x2[:, s * LANES:(s + 1) * LANES]
    logits = _router_logits(_rms(x2, gf_ref[...]), wr_ref, br_ref)
    info, running = _route(logits, cnt_scr[...], tri_ref[...])
    cnt_scr[...] = running
    counts_ref[...] = running
    brk_ref[...] = _dot_tn(info.astype(BF16), eye_ref[...])[0:8, :].astype(jnp.int32)


def _attention(x1, kv, xattn_norm, w_q, w_o, ffn_norm, w_router, b_router):
    bsz, seq, d = x1.shape
    mlen = kv.shape[1]
    tq = ATTN_TILE
    nj = seq // tq
    const = lambda b, j: (0, 0)
    tile = lambda b, j: (b, j, 0)
    tri = jnp.tri(tq, tq, -1, dtype=BF16)
    eye = jnp.eye(tq, dtype=BF16)
    return pl.pallas_call(
        _attn_kernel,
        grid=(bsz, nj),
        in_specs=[
            pl.BlockSpec((1, tq, d), tile),
            pl.BlockSpec((1, mlen, 2 * d), lambda b, j: (b, 0, 0)),
            pl.BlockSpec((1, d), const),
            pl.BlockSpec((d, d), const),
            pl.BlockSpec((d, d), const),
            pl.BlockSpec((1, d), const),
            pl.BlockSpec((d, 2 * LANES), const),
            pl.BlockSpec((1, LANES), const),
            pl.BlockSpec((tq, tq), const),
            pl.BlockSpec((tq, tq), const),
        ],
        out_specs=[
            pl.BlockSpec((tq * ROW_TILES, LANES), lambda b, j: (b * nj + j, 0)),
            pl.BlockSpec((8, tq), lambda b, j: (0, b * nj + j)),
            pl.BlockSpec((1, LANES), const),
        ],
        out_shape=[
            jax.ShapeDtypeStruct((bsz * seq * ROW_TILES, LANES), F32),
            jax.ShapeDtypeStruct((8, bsz * seq), jnp.int32),
            jax.ShapeDtypeStruct((1, LANES), F32),
        ],
        scratch_shapes=[pltpu.VMEM((tq, d), BF16), pltpu.VMEM((1, LANES), F32)],
        compiler_params=pltpu.CompilerParams(
            dimension_semantics=("arbitrary", "arbitrary"), vmem_limit_bytes=VMEM_LIMIT),
        name="xattn_router",
    )(x1, kv, xattn_norm, w_q, w_o, ffn_norm, w_router, b_router, tri, eye)


def _item_copy(hbm, hbm_item, buf, buf_item, rows_per_item, sem, to_hbm):
    h = hbm.at[pl.ds(pl.multiple_of(hbm_item * rows_per_item, rows_per_item), rows_per_item)]
    b = buf.at[pl.ds(pl.multiple_of(buf_item * rows_per_item, rows_per_item), rows_per_item)]
    return pltpu.make_async_copy(b, h, sem) if to_hbm else pltpu.make_async_copy(h, b, sem)


def _items_wait(hbm, buf, buf_item, n_items, rows_per_item, sem, to_hbm):
    n = n_items * rows_per_item
    h = hbm.at[pl.ds(0, n)]
    b = buf.at[pl.ds(pl.multiple_of(buf_item * rows_per_item, rows_per_item), n)]
    (pltpu.make_async_copy(b, h, sem) if to_hbm else pltpu.make_async_copy(h, b, sem)).wait()


def _dispatch_kernel(bucket_ref, rank_ref, starts_ref, fill_ref, ntiles_ref,
                     x2r_ref, xs_hbm, pos_ref, buf, sem):
    i = pl.program_id(0)
    nt = pl.num_programs(0)
    td = DISPATCH_TILE
    rpi = ROW_TILES
    slot = i % 2
    base = slot * td

    @pl.when(i >= 2)
    def _():
        _items_wait(xs_hbm, buf, base, td, rpi, sem.at[slot], True)

    buf[pl.ds(pl.multiple_of(base * rpi, td * rpi), td * rpi), :] = x2r_ref[...]

    def send(r2, carry):
        for k in range(DMA_QUEUES):
            r = r2 * DMA_QUEUES + k
            t = i * td + r
            p = starts_ref[bucket_ref[t]] + rank_ref[t]
            pos_ref[t] = p
            _item_copy(xs_hbm, p, buf, base + r, rpi, sem.at[slot], True).start(priority=k)
        return carry

    lax.fori_loop(0, td // DMA_QUEUES, send, 0, unroll=4)

    @pl.when(i == nt - 1)
    def _():
        _items_wait(xs_hbm, buf, base, td, rpi, sem.at[slot], True)
        _items_wait(xs_hbm, buf, (1 - slot) * td, td, rpi, sem.at[1 - slot], True)
        buf[...] = jnp.zeros_like(buf)
        zsem = sem.at[2]
        for b in range(N_BUCKETS):
            def fill(p, carry):
                _item_copy(xs_hbm, p, buf, 0, rpi, zsem, True).start()
                return carry

            def drain(p, carry):
                _item_copy(xs_hbm, p, buf, 0, rpi, zsem, True).wait()
                return carry

            lax.fori_loop(fill_ref[b], fill_ref[N_BUCKETS + b], fill, 0)
            lax.fori_loop(fill_ref[b], fill_ref[N_BUCKETS + b], drain, 0)

        n_tiles_total = xs_hbm.shape[0] // (MOE_TILE * rpi)

        def fill_tile(tile, carry):
            cp = pltpu.make_async_copy(
                buf.at[pl.ds(0, MOE_TILE * rpi)],
                xs_hbm.at[pl.ds(pl.multiple_of(tile * MOE_TILE * rpi, MOE_TILE * rpi),
                                MOE_TILE * rpi)], zsem)
            cp.start()
            cp.wait()
            return carry

        lax.fori_loop(ntiles_ref[0], n_tiles_total, fill_tile, 0)


def _dispatch(bucket, rank, starts, fill, ntiles, x2r, n_sorted):
    t = x2r.shape[0] // ROW_TILES
    td = DISPATCH_TILE
    grid_spec = pltpu.PrefetchScalarGridSpec(
        num_scalar_prefetch=5,
        grid=(t // td,),
        in_specs=[pl.BlockSpec((td * ROW_TILES, LANES), lambda i, *_: (i, 0))],
        out_specs=[pl.BlockSpec(memory_space=pl.ANY), pl.BlockSpec(memory_space=pltpu.SMEM)],
        scratch_shapes=[
            pltpu.VMEM((2 * td * ROW_TILES, LANES), F32),
            pltpu.SemaphoreType.DMA((3,)),
        ],
    )
    return pl.pallas_call(
        _dispatch_kernel,
        grid_spec=grid_spec,
        out_shape=[jax.ShapeDtypeStruct((n_sorted * ROW_TILES, LANES), F32),
                   jax.ShapeDtypeStruct((t,), jnp.int32)],
        compiler_params=pltpu.CompilerParams(
            dimension_semantics=("arbitrary",), vmem_limit_bytes=VMEM_LIMIT),
        name="moe_dispatch",
    )(bucket, rank, starts, fill, ntiles, x2r)


def _silu(x):
    return x * _sigmoid(x)


def _moe_kernel(elo_ref, ehi_ref, used_ref,
                xs_ref, gf_ref, wr_ref, br_ref, gfin_ref,
                wg_lo, wu_lo, wd_lo, wg_hi, wu_hi, wd_hi, y_ref):
    i = pl.program_id(0)
    tm = MOE_TILE

    @pl.when(used_ref[i] == 1)
    def _():
        x2 = jnp.concatenate(
            [xs_ref[pl.ds(s, tm, stride=ROW_TILES), :] for s in range(ROW_TILES)], axis=1)
        h3 = _rms(x2, gf_ref[...])
        g_lo, g_hi = _gates(_router_logits(h3, wr_ref, br_ref), elo_ref[i], ehi_ref[i])
        x = h3.astype(BF16)

        def expert(wg, wu, wd):
            hid = (_silu(_dot(x, wg[0])) * _dot(x, wu[0])).astype(BF16)
            return _dot(hid, wd[0])

        moe = g_lo * expert(wg_lo, wu_lo, wd_lo) + g_hi * expert(wg_hi, wu_hi, wd_hi)
        out = _rms(x2 + moe, gfin_ref[...])
        for s in range(ROW_TILES):
            y_ref[pl.ds(s, tm, stride=ROW_TILES), :] = out[:, s * LANES:(s + 1) * LANES]

    @pl.when(used_ref[i] == 0)
    def _():
        y_ref[...] = jnp.zeros_like(y_ref)


def _moe_sparse(e_lo, e_hi, used, xs, ffn_norm, w_router, b_router, final_norm,
                w_gate, w_up, w_down):
    tm = MOE_TILE
    nt = used.shape[0]
    d = D_MODEL
    lo = lambda i, elo, ehi, used: (elo[i], 0, 0)
    hi = lambda i, elo, ehi, used: (ehi[i], 0, 0)
    const = lambda i, *_: (0, 0)
    grid_spec = pltpu.PrefetchScalarGridSpec(
        num_scalar_prefetch=3,
        grid=(nt,),
        in_specs=[
            pl.BlockSpec((tm * ROW_TILES, LANES), lambda i, *_: (i, 0)),
            pl.BlockSpec((1, d), const),
            pl.BlockSpec((d, 2 * LANES), const),
            pl.BlockSpec((1, LANES), const),
            pl.BlockSpec((1, d), const),
            pl.BlockSpec((1, d, D_EXPERT), lo),
            pl.BlockSpec((1, d, D_EXPERT), lo),
            pl.BlockSpec((1, D_EXPERT, d), lo),
            pl.BlockSpec((1, d, D_EXPERT), hi),
            pl.BlockSpec((1, d, D_EXPERT), hi),
            pl.BlockSpec((1, D_EXPERT, d), hi),
        ],
        out_specs=pl.BlockSpec((tm * ROW_TILES, LANES), lambda i, *_: (i, 0)),
    )
    return pl.pallas_call(
        _moe_kernel,
        grid_spec=grid_spec,
        out_shape=jax.ShapeDtypeStruct((nt * tm * ROW_TILES, LANES), F32),
        compiler_params=pltpu.CompilerParams(
            dimension_semantics=("arbitrary",), vmem_limit_bytes=VMEM_LIMIT),
        name="moe_sparse",
    )(e_lo, e_hi, used, xs, ffn_norm, w_router, b_router, final_norm,
      w_gate, w_up, w_down, w_gate, w_up, w_down)


def _final_kernel(pos_ref, y_hbm, o_ref, ybuf, sem):
    i = pl.program_id(0)
    nt = pl.num_programs(0)
    tf = FINAL_TILE

    def start(tile, slot):
        def fetch(r2, carry):
            for k in range(DMA_QUEUES):
                r = r2 * DMA_QUEUES + k
                _item_copy(y_hbm, pos_ref[tile * tf + r], ybuf, slot * tf + r, ROW_TILES,
                           sem.at[slot], False).start(priority=k)
            return carry

        lax.fori_loop(0, tf // DMA_QUEUES, fetch, 0, unroll=4)

    @pl.when(i == 0)
    def _():
        start(0, 0)

    @pl.when(i + 1 < nt)
    def _():
        start(i + 1, (i + 1) % 2)

    slot = i % 2
    _items_wait(y_hbm, ybuf, slot * tf, tf, ROW_TILES, sem.at[slot], False)
    row0 = pl.multiple_of(slot * tf * ROW_TILES, ROW_TILES)
    for s in range(ROW_TILES):
        o_ref[:, s * LANES:(s + 1) * LANES] = ybuf[pl.ds(row0 + s, tf, stride=ROW_TILES), :]


def _final(pos, y):
    t = pos.shape[0]
    d = D_MODEL
    tf = FINAL_TILE
    grid_spec = pltpu.PrefetchScalarGridSpec(
        num_scalar_prefetch=1,
        grid=(t // tf,),
        in_specs=[pl.BlockSpec(memory_space=pl.ANY)],
        out_specs=pl.BlockSpec((tf, d), lambda i, *_: (i, 0)),
        scratch_shapes=[
            pltpu.VMEM((2 * tf * ROW_TILES, LANES), F32),
            pltpu.SemaphoreType.DMA((2,)),
        ],
    )
    return pl.pallas_call(
        _final_kernel,
        grid_spec=grid_spec,
        out_shape=jax.ShapeDtypeStruct((t, d), F32),
        compiler_params=pltpu.CompilerParams(
            dimension_semantics=("arbitrary",), vmem_limit_bytes=VMEM_LIMIT),
        name="moe_unpermute",
    )(pos, y)


def _routing_tables(counts, t):
    tm = MOE_TILE
    nt = t // tm + N_BUCKETS
    cnt = counts[0, :N_BUCKETS].astype(jnp.int32)
    padded = ((cnt + tm - 1) // tm) * tm
    ends = jnp.cumsum(padded)
    starts = ends - padded
    fill = jnp.concatenate([starts + cnt, ends])
    ntiles = ends[-1:] // tm
    tile_start = jnp.arange(nt, dtype=jnp.int32) * tm
    tile_bucket = jnp.sum((ends[None, :] <= tile_start[:, None]).astype(jnp.int32), axis=1)
    used = (tile_bucket < N_BUCKETS).astype(jnp.int32)
    tile_bucket = jnp.minimum(tile_bucket, N_BUCKETS - 1)
    pair = tile_bucket % N_PAIRS
    group0 = (tile_bucket // N_PAIRS) * EXPERTS_PER_GROUP
    e_lo = group0 + jnp.array([0, 0, 0, 1, 1, 2], jnp.int32)[pair]
    e_hi = group0 + jnp.array([1, 2, 3, 2, 3, 3], jnp.int32)[pair]
    return starts, fill, ntiles, e_lo, e_hi, used, nt


def kernel(x, mem, mix_norm, w_in, conv_w, hgrn_lb, hgrn_norm, w_out, xattn_norm, mem_norm,
           w_q, w_kv, w_o, ffn_norm, w_group, b_group, w_expert, b_expert, w_gate, w_up,
           w_down, final_norm):
    bsz, seq, d = x.shape
    assert d == D_MODEL and seq % SEQ_TILE == 0 and seq % ATTN_TILE == 0
    assert (bsz * seq) % MOE_TILE == 0 and (bsz * seq) % FINAL_TILE == 0
    assert (bsz * seq) % DISPATCH_TILE == 0 and bsz * seq >= 2 * DISPATCH_TILE
    assert 2 * DISPATCH_TILE >= MOE_TILE and mix_norm.shape[0] == 1
    bf = lambda w: w.astype(BF16)

    x1 = _mixer(x, mix_norm, bf(w_in[0]), conv_w[0], hgrn_lb, hgrn_norm, bf(w_out[0]))
    kv = _kv_proj(mem, mem_norm, bf(w_kv[0]))

    pad = LANES - N_GROUPS - N_EXPERTS
    w_router = jnp.concatenate(
        [w_group[0], w_expert[0], jnp.zeros((d, pad), F32)], axis=1)
    b_router = jnp.concatenate(
        [b_group[0], b_expert[0], jnp.zeros((pad,), F32)])[None, :]
    w_router_hi = bf(w_router)
    w_router_lo = bf(w_router - w_router_hi.astype(F32))
    w_router2 = jnp.concatenate([w_router_hi, w_router_lo], axis=1)
    w_q_scaled = bf(w_q[0] * (XATTN_HEAD_DIM ** -0.5))
    x2r, brk, counts = _attention(x1, kv, xattn_norm, w_q_scaled, bf(w_o[0]), ffn_norm,
                                  w_router2, b_router)

    t = bsz * seq
    bucket = brk[0]
    rank = brk[1] * 256 + brk[2]
    starts, fill, ntiles, e_lo, e_hi, used, nt = _routing_tables(counts, t)
    xs, pos = _dispatch(bucket, rank, starts, fill, ntiles, x2r, nt * MOE_TILE)
    y = _moe_sparse(e_lo, e_hi, used, xs, ffn_norm, w_router2, b_router, final_norm[None, :],
                    bf(w_gate[0]), bf(w_up[0]), bf(w_down[0]))
    return _final(pos, y).reshape(bsz, seq, d)
```

```python
import jax
import jax.numpy as jnp
from jax import lax
from jax.experimental import pallas as pl
from jax.experimental.pallas import tpu as pltpu

F32 = jnp.float32
BF16 = jnp.bfloat16

D_MODEL = 1024
CONV_WIDTH = 512
HGRN_WIDTH = 512
HGRN_HEADS = 4
HEAD_DIM = 128
N_PROJ_SLOTS = 7
PROJ_WIDTH = N_PROJ_SLOTS * 512
XATTN_HEADS = 4
XATTN_HEAD_DIM = 256
N_GROUPS = 4
EXPERTS_PER_GROUP = 4
N_EXPERTS = 16
D_EXPERT = 512
EPS = 1e-6

LANES = 128
SUBLANES = 8
CHUNK = 64
CHUNK_LEVELS = 6
HGRN_TILE = 4 * CHUNK
SEQ_TILE = 512
ATTN_TILE = 1024
RANK_BLOCK = 256
DISPATCH_TILE = 512
MOE_TILE = 256
FINAL_TILE = 512
N_PAIRS = 6
N_BUCKETS = N_GROUPS * N_PAIRS
ROW_TILES = D_MODEL // LANES
DMA_QUEUES = 2
VMEM_LIMIT = 56 * 1024 * 1024


def _rms(x, g):
    return x * lax.rsqrt(jnp.mean(x * x, axis=-1, keepdims=True) + EPS) * g


def _dot(a, b):
    return jnp.dot(a, b, preferred_element_type=F32)


def _dot_nt(a, b):
    return lax.dot_general(a, b, (((1,), (1,)), ((), ())), preferred_element_type=F32)


def _dot_tn(a, b):
    return lax.dot_general(a, b, (((0,), (0,)), ((), ())), preferred_element_type=F32)


def _roll_rows(x, shift):
    return pltpu.roll(x, shift % x.shape[0], axis=0)


def _level_exponents(logf2, use_level):
    n = CHUNK // SUBLANES
    sub = lax.broadcasted_iota(jnp.int32, (SUBLANES, logf2.shape[1]), 0)
    roll = lambda x, s: pltpu.roll(x, s % SUBLANES, axis=0)
    r = [logf2[SUBLANES * j:SUBLANES * (j + 1), :] for j in range(n)]
    for lvl in range(1, CHUNK_LEVELS + 1):
        half = 1 << (lvl - 1)
        g = [None] * n
        if half < SUBLANES:
            second = (sub & half) != 0
            for j in range(n):
                last = r[j]
                w = 1
                while w < half:
                    last = jnp.where((sub & w) != 0, last, roll(last, -w))
                    w *= 2
                tot = jnp.where(second, roll(last, half), last)
                g[j] = jnp.where(second, r[j], tot - r[j])
                r[j] = jnp.where(second, r[j] + tot, r[j])
        else:
            hv = half // SUBLANES
            for j0 in range(0, n, 2 * hv):
                mid = r[j0 + hv - 1]
                tot = jnp.broadcast_to(mid[SUBLANES - 1:SUBLANES, :], mid.shape)
                for j in range(j0, j0 + hv):
                    g[j] = tot - r[j]
                for j in range(j0 + hv, j0 + 2 * hv):
                    g[j] = r[j]
                    r[j] = r[j] + tot
        use_level(lvl, jnp.concatenate(g, axis=0))
    return jnp.concatenate(r, axis=0)


def _sigmoid(x):
    return 0.5 * jnp.tanh(0.5 * x) + 0.5


def _split_levels():
    import numpy as np
    t = np.arange(CHUNK)[:, None]
    s = np.arange(CHUNK)[None, :]
    msb = np.floor(np.log2(np.maximum(t ^ s, 1))).astype(np.int32) + 1
    return np.where(s < t, msb, np.where(s == t, 0, -1)).astype(np.int32)


def _hgrn_chunk(q, z, v, lb, levels):
    half_th = 0.5 * jnp.tanh(0.5 * z)
    one_m_lb = 1.0 - lb
    logf2 = jnp.log2(lb + one_m_lb * (0.5 + half_th))
    k = one_m_lb * (0.5 - half_th)

    qb = q.astype(BF16)
    kb = k.astype(BF16)
    scores = [jnp.where(levels == 0, jnp.sum(q * k, axis=-1, keepdims=True), 0.0)]

    def use_level(lvl, g):
        decay = jnp.exp2(g).astype(BF16)
        scores[0] = jnp.where(levels == lvl, _dot_nt(qb * decay, kb * decay), scores[0])

    b2 = _level_exponents(logf2, use_level)
    a = scores[0]
    b2_last = b2[CHUNK - 1:CHUNK, :]

    qe = (q * jnp.exp2(b2)).astype(BF16)
    kd = (k * jnp.exp2(b2_last - b2)).astype(BF16)
    return a, qe, kd, v.astype(BF16), b2_last


def _hgrn_tile(chunks, st):
    (a0, qe0, kd0, v0, bl0), (a1, qe1, kd1, v1, bl1), (a2, qe2, kd2, v2, bl2), \
        (a3, qe3, kd3, v3, bl3) = chunks
    scale = lambda x, log2_decay: x * jnp.exp2(log2_decay).astype(BF16)
    bf = lambda x: x.astype(BF16)
    cum1 = bl0
    cum2 = cum1 + bl1
    cum3 = cum2 + bl2
    cum4 = cum3 + bl3
    s10 = bf(_dot_nt(qe1, kd0))
    s32 = bf(_dot_nt(qe3, kd2))
    s8 = bf(_dot_nt(jnp.concatenate([qe2, scale(qe3, bl2)], axis=0),
                    jnp.concatenate([scale(kd0, bl1), kd1], axis=0)))
    v01 = jnp.concatenate([v0, v1], axis=0)
    stb = bf(st)
    o0 = _dot(bf(a0), v0) + _dot_nt(qe0, stb)
    o1 = _dot(bf(a1), v1) + _dot(s10, v0) + _dot_nt(scale(qe1, cum1), stb)
    o2 = _dot(bf(a2), v2) + _dot(s8[:CHUNK], v01) + _dot_nt(scale(qe2, cum2), stb)
    o3 = (_dot(bf(a3), v3) + _dot(s32, v2) + _dot(s8[CHUNK:], v01)
          + _dot_nt(scale(qe3, cum3), stb))
    kd_all = jnp.concatenate(
        [scale(kd0, cum4 - cum1), scale(kd1, cum4 - cum2), scale(kd2, bl3), kd3], axis=0)
    v_all = jnp.concatenate([v0, v1, v2, v3], axis=0)
    st_new = st * jnp.exp2(cum4) + _dot_tn(v_all, kd_all)
    return [o0, o1, o2, o3], st_new


def _mixer_kernel(x_ref, gmix_ref, win_ref, convw_ref, lbraw_ref, hnorm_ref, wout_ref, lvl_ref,
                  o_ref, p_scr, y_scr, st_scr, tail_scr):
    j = pl.program_id(1)

    @pl.when(j == 0)
    def _():
        st_scr[...] = jnp.zeros_like(st_scr)
        tail_scr[...] = jnp.zeros_like(tail_scr)

    x = x_ref[0]
    h = _rms(x, gmix_ref[...]).astype(BF16)
    p_scr[...] = _dot(h, win_ref[...])

    ts = x.shape[0]
    cb = p_scr[:, 0:CONV_WIDTH]
    u = p_scr[:, CONV_WIDTH:2 * CONV_WIDTH] * p_scr[:, 2 * CONV_WIDTH:3 * CONV_WIDTH]
    row = lax.broadcasted_iota(jnp.int32, u.shape, 0)
    prev1 = tail_scr[7:8, :]
    prev2 = tail_scr[6:7, :]
    u1 = jnp.where(row == 0, prev1, _roll_rows(u, 1))
    u2 = jnp.where(row == 0, prev2, jnp.where(row == 1, prev1, _roll_rows(u, 2)))
    cw = convw_ref[...]
    conv = u2 * cw[0:1, :] + u1 * cw[1:2, :] + u * cw[2:3, :]
    y_scr[:, 0:CONV_WIDTH] = (cb * conv).astype(BF16)
    tail_scr[...] = u[ts - 8:ts, :]

    raw = lbraw_ref[...]
    mx = jnp.max(raw, axis=0, keepdims=True)
    ex = jnp.exp(raw - mx)
    lb_all = ex[0:1, :] / jnp.sum(ex, axis=0, keepdims=True)
    hn = hnorm_ref[...]

    for hd in range(HGRN_HEADS):
        lo = hd * HEAD_DIM
        sl = slice(lo, lo + HEAD_DIM)
        col = lambda slot: slice(slot * 512 + lo, slot * 512 + lo + HEAD_DIM)
        for t0 in range(0, ts, HGRN_TILE):
            chunks = []
            for r0 in range(t0, t0 + HGRN_TILE, CHUNK):
                rows = slice(r0, r0 + CHUNK)
                chunks.append(_hgrn_chunk(p_scr[rows, col(3)], p_scr[rows, col(4)],
                                          p_scr[rows, col(5)], lb_all[:, sl], lvl_ref[...]))
            outs, st_new = _hgrn_tile(chunks, st_scr[hd])
            st_scr[hd] = st_new
            for c, o in enumerate(outs):
                rows = slice(t0 + c * CHUNK, t0 + (c + 1) * CHUNK)
                g = p_scr[rows, col(6)]
                o = o * lax.rsqrt(jnp.mean(o * o, axis=-1, keepdims=True) + EPS) * hn[:, sl]
                y_scr[rows, CONV_WIDTH + lo:CONV_WIDTH + lo + HEAD_DIM] = \
                    (o * (g * _sigmoid(g))).astype(BF16)

    o_ref[0] = x + _dot(y_scr[...], wout_ref[...])


def _mixer(x, mix_norm, w_in, conv_w, hgrn_lb, hgrn_norm, w_out):
    bsz, seq, d = x.shape
    ts = SEQ_TILE
    const = lambda b, j: (0, 0)
    return pl.pallas_call(
        _mixer_kernel,
        grid=(bsz, seq // ts),
        in_specs=[
            pl.BlockSpec((1, ts, d), lambda b, j: (b, j, 0)),
            pl.BlockSpec((1, d), const),
            pl.BlockSpec((d, PROJ_WIDTH), const),
            pl.BlockSpec((3, CONV_WIDTH), const),
            pl.BlockSpec((2, HGRN_WIDTH), const),
            pl.BlockSpec((1, HGRN_WIDTH), const),
            pl.BlockSpec((d, d), const),
            pl.BlockSpec((CHUNK, CHUNK), const),
        ],
        out_specs=pl.BlockSpec((1, ts, d), lambda b, j: (b, j, 0)),
        out_shape=jax.ShapeDtypeStruct((bsz, seq, d), F32),
        scratch_shapes=[
            pltpu.VMEM((ts, PROJ_WIDTH), F32),
            pltpu.VMEM((ts, d), BF16),
            pltpu.VMEM((HGRN_HEADS, HEAD_DIM, HEAD_DIM), F32),
            pltpu.VMEM((8, CONV_WIDTH), F32),
        ],
        compiler_params=pltpu.CompilerParams(
            dimension_semantics=("arbitrary", "arbitrary"), vmem_limit_bytes=VMEM_LIMIT),
        name="mixer",
    )(x, mix_norm, w_in, conv_w, hgrn_lb, hgrn_norm, w_out, jnp.asarray(_split_levels()))


def _kv_kernel(m_ref, g_ref, w_ref, o_ref):
    h = _rms(m_ref[0], g_ref[...]).astype(BF16)
    o_ref[0] = _dot(h, w_ref[...]).astype(BF16)


def _kv_proj(mem, mem_norm, w_kv):
    bsz, mlen, d = mem.shape
    const = lambda b: (0, 0)
    return pl.pallas_call(
        _kv_kernel,
        grid=(bsz,),
        in_specs=[
            pl.BlockSpec((1, mlen, d), lambda b: (b, 0, 0)),
            pl.BlockSpec((1, d), const),
            pl.BlockSpec((d, 2 * d), const),
        ],
        out_specs=pl.BlockSpec((1, mlen, 2 * d), lambda b: (b, 0, 0)),
        out_shape=jax.ShapeDtypeStruct((bsz, mlen, 2 * d), BF16),
        compiler_params=pltpu.CompilerParams(
            dimension_semantics=("arbitrary",), vmem_limit_bytes=VMEM_LIMIT),
        name="kv_proj",
    )(mem, mem_norm, w_kv)


def _first_argmax(vals, lane):
    mx = jnp.max(vals, axis=-1, keepdims=True)
    idx = jnp.min(jnp.where(vals == mx, lane, float(LANES)), axis=-1, keepdims=True)
    return mx, idx


def _router_logits(h3, wr_ref, br_ref):
    h3_hi = h3.astype(BF16)
    h3_lo = (h3 - h3_hi.astype(F32)).astype(BF16)
    two = _dot(h3_hi, wr_ref[...])
    return (two[:, :LANES] + two[:, LANES:]) + _dot(h3_lo, wr_ref[:, :LANES]) + br_ref[...]


def _route(logits, running, tri):
    lane = lax.broadcasted_iota(jnp.int32, logits.shape, 1).astype(F32)
    neg = jnp.float32(-jnp.inf)
    gl = jnp.where(lane < N_GROUPS, logits, neg)
    _, gidx = _first_argmax(gl, lane)
    base = N_GROUPS + EXPERTS_PER_GROUP * gidx
    el = jnp.where((lane >= base) & (lane < base + EXPERTS_PER_GROUP), logits, neg)
    _, i1 = _first_argmax(el, lane)
    _, i2 = _first_argmax(jnp.where(lane == i1, neg, el), lane)
    lo = jnp.minimum(i1, i2) - base
    hi = jnp.maximum(i1, i2) - base
    bucket = gidx * N_PAIRS + (lo * (7.0 - lo)) * 0.5 + hi - lo - 1.0

    onehot = lane == bucket
    ranks = []
    for r0 in range(0, logits.shape[0], RANK_BLOCK):
        oh = onehot[r0:r0 + RANK_BLOCK]
        before = _dot(tri, oh.astype(BF16))
        ranks.append(jnp.sum(jnp.where(oh, before + running, 0.0), axis=-1, keepdims=True))
        running = running + jnp.sum(oh.astype(F32), axis=0, keepdims=True)
    rank = jnp.concatenate(ranks, axis=0)
    info = jnp.where(lane == 0, bucket, jnp.where(lane == 1, rank, 0.0))
    return info, running


def _gates(logits, e_lo, e_hi):
    lane = lax.broadcasted_iota(jnp.int32, logits.shape, 1)
    neg = jnp.float32(-jnp.inf)
    gl = jnp.where(lane < N_GROUPS, logits, neg)
    g_p = 1.0 / jnp.sum(jnp.exp(gl - jnp.max(gl, axis=-1, keepdims=True)), axis=-1, keepdims=True)
    l_lo = jnp.sum(jnp.where(lane == N_GROUPS + e_lo, logits, 0.0), axis=-1, keepdims=True)
    l_hi = jnp.sum(jnp.where(lane == N_GROUPS + e_hi, logits, 0.0), axis=-1, keepdims=True)
    m = jnp.maximum(l_lo, l_hi)
    p_lo = jnp.exp(l_lo - m)
    p_hi = jnp.exp(l_hi - m)
    inv = g_p / (p_lo + p_hi)
    return p_lo * inv, p_hi * inv


def _attn_kernel(x_ref, kv_ref, gx_ref, wq_ref, wo_ref, gf_ref, wr_ref, br_ref, tri_ref,
                 x2r_ref, brk_ref, counts_ref, o_scr, cnt_scr):
    @pl.when((pl.program_id(0) == 0) & (pl.program_id(1) == 0))
    def _():
        cnt_scr[...] = jnp.zeros_like(cnt_scr)

    x = x_ref[0]
    tq = x.shape[0]
    h = _rms(x, gx_ref[...]).astype(BF16)
    q = _dot(h, wq_ref[...])
    for hd in range(XATTN_HEADS):
        lo = hd * XATTN_HEAD_DIM
        qh = q[:, lo:lo + XATTN_HEAD_DIM].astype(BF16)
        kh = kv_ref[0, :, lo:lo + XATTN_HEAD_DIM]
        vh = kv_ref[0, :, D_MODEL + lo:D_MODEL + lo + XATTN_HEAD_DIM]
        s = _dot_nt(qh, kh)
        p = jnp.exp(s - jnp.max(s, axis=-1, keepdims=True))
        inv = 1.0 / jnp.sum(p, axis=-1, keepdims=True)
        o_scr[:, lo:lo + XATTN_HEAD_DIM] = (_dot(p.astype(BF16), vh) * inv).astype(BF16)
    x2 = x + _dot(o_scr[...], wo_ref[...])
    for s in range(ROW_TILES):
        x2r_ref[pl.ds(s, tq, stride=ROW_TILES), :] = x2[:, s * LANES:(s + 1) * LANES]
    logits = _router_logits(_rms(x2, gf_ref[...]), wr_ref, br_ref)
    info, running = _route(logits, cnt_scr[...], tri_ref[...])
    cnt_scr[...] = running
    counts_ref[...] = running
    brk_ref[...] = info.T[0:8, :].astype(jnp.int32)


def _attention(x1, kv, xattn_norm, w_q, w_o, ffn_norm, w_router, b_router):
    bsz, seq, d = x1.shape
    mlen = kv.shape[1]
    tq = ATTN_TILE
    nj = seq // tq
    const = lambda b, j: (0, 0)
    tile = lambda b, j: (b, j, 0)
    tri = jnp.tri(RANK_BLOCK, RANK_BLOCK, -1, dtype=BF16)
    return pl.pallas_call(
        _attn_kernel,
        grid=(bsz, nj),
        in_specs=[
            pl.BlockSpec((1, tq, d), tile),
            pl.BlockSpec((1, mlen, 2 * d), lambda b, j: (b, 0, 0)),
            pl.BlockSpec((1, d), const),
            pl.BlockSpec((d, d), const),
            pl.BlockSpec((d, d), const),
            pl.BlockSpec((1, d), const),
            pl.BlockSpec((d, 2 * LANES), const),
            pl.BlockSpec((1, LANES), const),
            pl.BlockSpec((RANK_BLOCK, RANK_BLOCK), const),
        ],
        out_specs=[
            pl.BlockSpec((tq * ROW_TILES, LANES), lambda b, j: (b * nj + j, 0)),
            pl.BlockSpec((8, tq), lambda b, j: (0, b * nj + j)),
            pl.BlockSpec((1, LANES), const),
        ],
        out_shape=[
            jax.ShapeDtypeStruct((bsz * seq * ROW_TILES, LANES), F32),
            jax.ShapeDtypeStruct((8, bsz * seq), jnp.int32),
            jax.ShapeDtypeStruct((1, LANES), F32),
        ],
        scratch_shapes=[pltpu.VMEM((tq, d), BF16), pltpu.VMEM((1, LANES), F32)],
        compiler_params=pltpu.CompilerParams(
            dimension_semantics=("arbitrary", "arbitrary"), vmem_limit_bytes=VMEM_LIMIT),
        name="xattn_router",
    )(x1, kv, xattn_norm, w_q, w_o, ffn_norm, w_router, b_router, tri)


def _item_copy(hbm, hbm_item, buf, buf_item, rows_per_item, sem, to_hbm):
    h = hbm.at[pl.ds(pl.multiple_of(hbm_item * rows_per_item, rows_per_item), rows_per_item)]
    b = buf.at[pl.ds(pl.multiple_of(buf_item * rows_per_item, rows_per_item), rows_per_item)]
    return pltpu.make_async_copy(b, h, sem) if to_hbm else pltpu.make_async_copy(h, b, sem)


def _items_wait(hbm, buf, buf_item, n_items, rows_per_item, sem, to_hbm):
    n = n_items * rows_per_item
    h = hbm.at[pl.ds(0, n)]
    b = buf.at[pl.ds(pl.multiple_of(buf_item * rows_per_item, rows_per_item), n)]
    (pltpu.make_async_copy(b, h, sem) if to_hbm else pltpu.make_async_copy(h, b, sem)).wait()


def _dispatch_kernel(bucket_ref, rank_ref, starts_ref, fill_ref, ntiles_ref,
                     x2r_ref, xs_hbm, pos_ref, buf, sem):
    i = pl.program_id(0)
    nt = pl.num_programs(0)
    td = DISPATCH_TILE
    rpi = ROW_TILES
    slot = i % 2
    base = slot * td

    @pl.when(i >= 2)
    def _():
        _items_wait(xs_hbm, buf, base, td, rpi, sem.at[slot], True)

    buf[pl.ds(pl.multiple_of(base * rpi, td * rpi), td * rpi), :] = x2r_ref[...]

    def send(r2, carry):
        for k in range(DMA_QUEUES):
            r = r2 * DMA_QUEUES + k
            t = i * td + r
            p = starts_ref[bucket_ref[t]] + rank_ref[t]
            pos_ref[t] = p
            _item_copy(xs_hbm, p, buf, base + r, rpi, sem.at[slot], True).start(priority=k)
        return carry

    lax.fori_loop(0, td // DMA_QUEUES, send, 0, unroll=4)

    @pl.when(i == nt - 1)
    def _():
        _items_wait(xs_hbm, buf, base, td, rpi, sem.at[slot], True)
        _items_wait(xs_hbm, buf, (1 - slot) * td, td, rpi, sem.at[1 - slot], True)
        buf[...] = jnp.zeros_like(buf)
        zsem = sem.at[2]
        for b in range(N_BUCKETS):
            def fill(p, carry):
                _item_copy(xs_hbm, p, buf, 0, rpi, zsem, True).start()
                return carry

            def drain(p, carry):
                _item_copy(xs_hbm, p, buf, 0, rpi, zsem, True).wait()
                return carry

            lax.fori_loop(fill_ref[b], fill_ref[N_BUCKETS + b], fill, 0)
            lax.fori_loop(fill_ref[b], fill_ref[N_BUCKETS + b], drain, 0)

        n_tiles_total = xs_hbm.shape[0] // (MOE_TILE * rpi)

        def fill_tile(tile, carry):
            cp = pltpu.make_async_copy(
                buf.at[pl.ds(0, MOE_TILE * rpi)],
                xs_hbm.at[pl.ds(pl.multiple_of(tile * MOE_TILE * rpi, MOE_TILE * rpi),
                                MOE_TILE * rpi)], zsem)
            cp.start()
            cp.wait()
            return carry

        lax.fori_loop(ntiles_ref[0], n_tiles_total, fill_tile, 0)


def _dispatch(bucket, rank, starts, fill, ntiles, x2r, n_sorted):
    t = x2r.shape[0] // ROW_TILES
    td = DISPATCH_TILE
    grid_spec = pltpu.PrefetchScalarGridSpec(
        num_scalar_prefetch=5,
        grid=(t // td,),
        in_specs=[pl.BlockSpec((td * ROW_TILES, LANES), lambda i, *_: (i, 0))],
        out_specs=[pl.BlockSpec(memory_space=pl.ANY), pl.BlockSpec(memory_space=pltpu.SMEM)],
        scratch_shapes=[
            pltpu.VMEM((2 * td * ROW_TILES, LANES), F32),
            pltpu.SemaphoreType.DMA((3,)),
        ],
    )
    return pl.pallas_call(
        _dispatch_kernel,
        grid_spec=grid_spec,
        out_shape=[jax.ShapeDtypeStruct((n_sorted * ROW_TILES, LANES), F32),
                   jax.ShapeDtypeStruct((t,), jnp.int32)],
        compiler_params=pltpu.CompilerParams(
            dimension_semantics=("arbitrary",), vmem_limit_bytes=VMEM_LIMIT),
        name="moe_dispatch",
    )(bucket, rank, starts, fill, ntiles, x2r)


def _silu(x):
    return x * _sigmoid(x)


def _moe_kernel(elo_ref, ehi_ref, used_ref,
                xs_ref, gf_ref, wr_ref, br_ref, gfin_ref,
                wg_lo, wu_lo, wd_lo, wg_hi, wu_hi, wd_hi, y_ref):
    i = pl.program_id(0)
    tm = MOE_TILE

    @pl.when(used_ref[i] == 1)
    def _():
        x2 = jnp.concatenate(
            [xs_ref[pl.ds(s, tm, stride=ROW_TILES), :] for s in range(ROW_TILES)], axis=1)
        h3 = _rms(x2, gf_ref[...])
        g_lo, g_hi = _gates(_router_logits(h3, wr_ref, br_ref), elo_ref[i], ehi_ref[i])
        x = h3.astype(BF16)

        def expert(wg, wu, wd):
            hid = (_silu(_dot(x, wg[0])) * _dot(x, wu[0])).astype(BF16)
            return _dot(hid, wd[0])

        moe = g_lo * expert(wg_lo, wu_lo, wd_lo) + g_hi * expert(wg_hi, wu_hi, wd_hi)
        out = _rms(x2 + moe, gfin_ref[...])
        for s in range(ROW_TILES):
            y_ref[pl.ds(s, tm, stride=ROW_TILES), :] = out[:, s * LANES:(s + 1) * LANES]

    @pl.when(used_ref[i] == 0)
    def _():
        y_ref[...] = jnp.zeros_like(y_ref)


def _moe_sparse(e_lo, e_hi, used, xs, ffn_norm, w_router, b_router, final_norm,
                w_gate, w_up, w_down):
    tm = MOE_TILE
    nt = used.shape[0]
    d = D_MODEL
    lo = lambda i, elo, ehi, used: (elo[i], 0, 0)
    hi = lambda i, elo, ehi, used: (ehi[i], 0, 0)
    const = lambda i, *_: (0, 0)
    grid_spec = pltpu.PrefetchScalarGridSpec(
        num_scalar_prefetch=3,
        grid=(nt,),
        in_specs=[
            pl.BlockSpec((tm * ROW_TILES, LANES), lambda i, *_: (i, 0)),
            pl.BlockSpec((1, d), const),
            pl.BlockSpec((d, 2 * LANES), const),
            pl.BlockSpec((1, LANES), const),
            pl.BlockSpec((1, d), const),
            pl.BlockSpec((1, d, D_EXPERT), lo),
            pl.BlockSpec((1, d, D_EXPERT), lo),
            pl.BlockSpec((1, D_EXPERT, d), lo),
            pl.BlockSpec((1, d, D_EXPERT), hi),
            pl.BlockSpec((1, d, D_EXPERT), hi),
            pl.BlockSpec((1, D_EXPERT, d), hi),
        ],
        out_specs=pl.BlockSpec((tm * ROW_TILES, LANES), lambda i, *_: (i, 0)),
    )
    return pl.pallas_call(
        _moe_kernel,
        grid_spec=grid_spec,
        out_shape=jax.ShapeDtypeStruct((nt * tm * ROW_TILES, LANES), F32),
        compiler_params=pltpu.CompilerParams(
            dimension_semantics=("arbitrary",), vmem_limit_bytes=VMEM_LIMIT),
        name="moe_sparse",
    )(e_lo, e_hi, used, xs, ffn_norm, w_router, b_router, final_norm,
      w_gate, w_up, w_down, w_gate, w_up, w_down)


def _final_kernel(pos_ref, y_hbm, o_ref, ybuf, sem):
    i = pl.program_id(0)
    nt = pl.num_programs(0)
    tf = FINAL_TILE

    def start(tile, slot):
        def fetch(r2, carry):
            for k in range(DMA_QUEUES):
                r = r2 * DMA_QUEUES + k
                _item_copy(y_hbm, pos_ref[tile * tf + r], ybuf, slot * tf + r, ROW_TILES,
                           sem.at[slot], False).start(priority=k)
            return carry

        lax.fori_loop(0, tf // DMA_QUEUES, fetch, 0, unroll=4)

    @pl.when(i == 0)
    def _():
        start(0, 0)

    @pl.when(i + 1 < nt)
    def _():
        start(i + 1, (i + 1) % 2)

    slot = i % 2
    _items_wait(y_hbm, ybuf, slot * tf, tf, ROW_TILES, sem.at[slot], False)
    row0 = pl.multiple_of(slot * tf * ROW_TILES, ROW_TILES)
    for s in range(ROW_TILES):
        o_ref[:, s * LANES:(s + 1) * LANES] = ybuf[pl.ds(row0 + s, tf, stride=ROW_TILES), :]


def _final(pos, y):
    t = pos.shape[0]
    d = D_MODEL
    tf = FINAL_TILE
    grid_spec = pltpu.PrefetchScalarGridSpec(
        num_scalar_prefetch=1,
        grid=(t // tf,),
        in_specs=[pl.BlockSpec(memory_space=pl.ANY)],
        out_specs=pl.BlockSpec((tf, d), lambda i, *_: (i, 0)),
        scratch_shapes=[
            pltpu.VMEM((2 * tf * ROW_TILES, LANES), F32),
            pltpu.SemaphoreType.DMA((2,)),
        ],
    )
    return pl.pallas_call(
        _final_kernel,
        grid_spec=grid_spec,
        out_shape=jax.ShapeDtypeStruct((t, d), F32),
        compiler_params=pltpu.CompilerParams(
            dimension_semantics=("arbitrary",), vmem_limit_bytes=VMEM_LIMIT),
        name="moe_unpermute",
    )(pos, y)


def _routing_tables(counts, t):
    tm = MOE_TILE
    nt = t // tm + N_BUCKETS
    cnt = counts[0, :N_BUCKETS].astype(jnp.int32)
    padded = ((cnt + tm - 1) // tm) * tm
    ends = jnp.cumsum(padded)
    starts = ends - padded
    fill = jnp.concatenate([starts + cnt, ends])
    ntiles = ends[-1:] // tm
    tile_start = jnp.arange(nt, dtype=jnp.int32) * tm
    tile_bucket = jnp.sum((ends[None, :] <= tile_start[:, None]).astype(jnp.int32), axis=1)
    used = (tile_bucket < N_BUCKETS).astype(jnp.int32)
    tile_bucket = jnp.minimum(tile_bucket, N_BUCKETS - 1)
    pair = tile_bucket % N_PAIRS
    group0 = (tile_bucket // N_PAIRS) * EXPERTS_PER_GROUP
    e_lo = group0 + jnp.array([0, 0, 0, 1, 1, 2], jnp.int32)[pair]
    e_hi = group0 + jnp.array([1, 2, 3, 2, 3, 3], jnp.int32)[pair]
    return starts, fill, ntiles, e_lo, e_hi, used, nt


def kernel(x, mem, mix_norm, w_in, conv_w, hgrn_lb, hgrn_norm, w_out, xattn_norm, mem_norm,
           w_q, w_kv, w_o, ffn_norm, w_group, b_group, w_expert, b_expert, w_gate, w_up,
           w_down, final_norm):
    bsz, seq, d = x.shape
    assert d == D_MODEL and seq % SEQ_TILE == 0 and seq % ATTN_TILE == 0
    assert (bsz * seq) % MOE_TILE == 0 and (bsz * seq) % FINAL_TILE == 0
    assert (bsz * seq) % DISPATCH_TILE == 0 and bsz * seq >= 2 * DISPATCH_TILE
    assert 2 * DISPATCH_TILE >= MOE_TILE and mix_norm.shape[0] == 1
    bf = lambda w: w.astype(BF16)

    x1 = _mixer(x, mix_norm, bf(w_in[0]), conv_w[0], hgrn_lb, hgrn_norm, bf(w_out[0]))
    kv = _kv_proj(mem, mem_norm, bf(w_kv[0]))

    pad = LANES - N_GROUPS - N_EXPERTS
    w_router = jnp.concatenate(
        [w_group[0], w_expert[0], jnp.zeros((d, pad), F32)], axis=1)
    b_router = jnp.concatenate(
        [b_group[0], b_expert[0], jnp.zeros((pad,), F32)])[None, :]
    w_router_hi = bf(w_router)
    w_router_lo = bf(w_router - w_router_hi.astype(F32))
    w_router2 = jnp.concatenate([w_router_hi, w_router_lo], axis=1)
    w_q_scaled = bf(w_q[0] * (XATTN_HEAD_DIM ** -0.5))
    x2r, brk, counts = _attention(x1, kv, xattn_norm, w_q_scaled, bf(w_o[0]), ffn_norm,
                                  w_router2, b_router)

    t = bsz * seq
    bucket = brk[0]
    rank = brk[1]
    starts, fill, ntiles, e_lo, e_hi, used, nt = _routing_tables(counts, t)
    xs, pos = _dispatch(bucket, rank, starts, fill, ntiles, x2r, nt * MOE_TILE)
    y = _moe_sparse(e_lo, e_hi, used, xs, ffn_norm, w_router2, b_router, final_norm[None, :],
                    bf(w_gate[0]), bf(w_up[0]), bf(w_down[0]))
    return _final(pos, y).reshape(bsz, seq, d)
```

```python
import jax
import jax.numpy as jnp
from jax import lax
from jax.experimental import pallas as pl
from jax.experimental.pallas import tpu as pltpu

F32 = jnp.float32
BF16 = jnp.bfloat16

D_MODEL = 1024
CONV_WIDTH = 512
HGRN_WIDTH = 512
HGRN_HEADS = 4
HEAD_DIM = 128
N_PROJ_SLOTS = 7
PROJ_WIDTH = N_PROJ_SLOTS * 512
XATTN_HEADS = 4
XATTN_HEAD_DIM = 256
N_GROUPS = 4
EXPERTS_PER_GROUP = 4
N_EXPERTS = 16
D_EXPERT = 512
EPS = 1e-6

LANES = 128
SUBLANES = 8
CHUNK = 64
CHUNK_LEVELS = 6
HGRN_TILE = 4 * CHUNK
SEQ_TILE = 512
ATTN_TILE = 1024
RANK_BLOCK = 256
DISPATCH_TILE = 512
MOE_TILE = 256
STAGGER_ROWS = 11
FINAL_TILE = 512
N_PAIRS = 6
N_BUCKETS = N_GROUPS * N_PAIRS
ROW_TILES = D_MODEL // LANES
DMA_QUEUES = 2
VMEM_LIMIT = 56 * 1024 * 1024


def _rms(x, g):
    return x * lax.rsqrt(jnp.mean(x * x, axis=-1, keepdims=True) + EPS) * g


def _dot(a, b):
    return jnp.dot(a, b, preferred_element_type=F32)


def _dot_nt(a, b):
    return lax.dot_general(a, b, (((1,), (1,)), ((), ())), preferred_element_type=F32)


def _dot_tn(a, b):
    return lax.dot_general(a, b, (((0,), (0,)), ((), ())), preferred_element_type=F32)


def _roll_rows(x, shift):
    return pltpu.roll(x, shift % x.shape[0], axis=0)


def _level_exponents(logf2, use_level):
    n = CHUNK // SUBLANES
    sub = lax.broadcasted_iota(jnp.int32, (SUBLANES, logf2.shape[1]), 0)
    roll = lambda x, s: pltpu.roll(x, s % SUBLANES, axis=0)
    r = [logf2[SUBLANES * j:SUBLANES * (j + 1), :] for j in range(n)]
    for lvl in range(1, CHUNK_LEVELS + 1):
        half = 1 << (lvl - 1)
        g = [None] * n
        if half < SUBLANES:
            second = (sub & half) != 0
            for j in range(n):
                last = r[j]
                w = 1
                while w < half:
                    last = jnp.where((sub & w) != 0, last, roll(last, -w))
                    w *= 2
                tot = jnp.where(second, roll(last, half), last)
                g[j] = jnp.where(second, r[j], tot - r[j])
                r[j] = jnp.where(second, r[j] + tot, r[j])
        else:
            hv = half // SUBLANES
            for j0 in range(0, n, 2 * hv):
                mid = r[j0 + hv - 1]
                tot = jnp.broadcast_to(mid[SUBLANES - 1:SUBLANES, :], mid.shape)
                for j in range(j0, j0 + hv):
                    g[j] = tot - r[j]
                for j in range(j0 + hv, j0 + 2 * hv):
                    g[j] = r[j]
                    r[j] = r[j] + tot
        use_level(lvl, jnp.concatenate(g, axis=0))
    return jnp.concatenate(r, axis=0)


def _sigmoid(x):
    return 0.5 * jnp.tanh(0.5 * x) + 0.5


def _split_levels():
    import numpy as np
    t = np.arange(CHUNK)[:, None]
    s = np.arange(CHUNK)[None, :]
    msb = np.floor(np.log2(np.maximum(t ^ s, 1))).astype(np.int32) + 1
    return np.where(s < t, msb, np.where(s == t, 0, -1)).astype(np.int32)


def _hgrn_chunk(q, z, v, lb, levels):
    half_th = 0.5 * jnp.tanh(0.5 * z)
    one_m_lb = 1.0 - lb
    logf2 = jnp.log2(lb + one_m_lb * (0.5 + half_th))
    k = one_m_lb * (0.5 - half_th)

    qb = q.astype(BF16)
    kb = k.astype(BF16)
    scores = [jnp.where(levels == 0, jnp.sum(q * k, axis=-1, keepdims=True), 0.0)]

    def use_level(lvl, g):
        decay = jnp.exp2(g).astype(BF16)
        scores[0] = jnp.where(levels == lvl, _dot_nt(qb * decay, kb * decay), scores[0])

    b2 = _level_exponents(logf2, use_level)
    a = scores[0]
    b2_last = b2[CHUNK - 1:CHUNK, :]

    qe = (q * jnp.exp2(b2)).astype(BF16)
    kd = (k * jnp.exp2(b2_last - b2)).astype(BF16)
    return a, qe, kd, v.astype(BF16), b2_last


def _hgrn_tile(chunks, st):
    (a0, qe0, kd0, v0, bl0), (a1, qe1, kd1, v1, bl1), (a2, qe2, kd2, v2, bl2), \
        (a3, qe3, kd3, v3, bl3) = chunks
    scale = lambda x, log2_decay: x * jnp.exp2(log2_decay).astype(BF16)
    bf = lambda x: x.astype(BF16)
    cum1 = bl0
    cum2 = cum1 + bl1
    cum3 = cum2 + bl2
    cum4 = cum3 + bl3
    s10 = bf(_dot_nt(qe1, kd0))
    s32 = bf(_dot_nt(qe3, kd2))
    s8 = bf(_dot_nt(jnp.concatenate([qe2, scale(qe3, bl2)], axis=0),
                    jnp.concatenate([scale(kd0, bl1), kd1], axis=0)))
    v01 = jnp.concatenate([v0, v1], axis=0)
    stb = bf(st)
    o0 = _dot(bf(a0), v0) + _dot_nt(qe0, stb)
    o1 = _dot(bf(a1), v1) + _dot(s10, v0) + _dot_nt(scale(qe1, cum1), stb)
    o2 = _dot(bf(a2), v2) + _dot(s8[:CHUNK], v01) + _dot_nt(scale(qe2, cum2), stb)
    o3 = (_dot(bf(a3), v3) + _dot(s32, v2) + _dot(s8[CHUNK:], v01)
          + _dot_nt(scale(qe3, cum3), stb))
    kd_all = jnp.concatenate(
        [scale(kd0, cum4 - cum1), scale(kd1, cum4 - cum2), scale(kd2, bl3), kd3], axis=0)
    v_all = jnp.concatenate([v0, v1, v2, v3], axis=0)
    st_new = st * jnp.exp2(cum4) + _dot_tn(v_all, kd_all)
    return [o0, o1, o2, o3], st_new


def _mixer_kernel(x_ref, gmix_ref, win_ref, convw_ref, lbraw_ref, hnorm_ref, wout_ref, lvl_ref,
                  o_ref, p_scr, y_scr, st_scr, tail_scr):
    j = pl.program_id(1)

    @pl.when(j == 0)
    def _():
        st_scr[...] = jnp.zeros_like(st_scr)
        tail_scr[...] = jnp.zeros_like(tail_scr)

    x = x_ref[0]
    h = _rms(x, gmix_ref[...]).astype(BF16)
    p_scr[...] = _dot(h, win_ref[...])

    ts = x.shape[0]
    cb = p_scr[:, 0:CONV_WIDTH]
    u = p_scr[:, CONV_WIDTH:2 * CONV_WIDTH] * p_scr[:, 2 * CONV_WIDTH:3 * CONV_WIDTH]
    row = lax.broadcasted_iota(jnp.int32, u.shape, 0)
    prev1 = tail_scr[7:8, :]
    prev2 = tail_scr[6:7, :]
    u1 = jnp.where(row == 0, prev1, _roll_rows(u, 1))
    u2 = jnp.where(row == 0, prev2, jnp.where(row == 1, prev1, _roll_rows(u, 2)))
    cw = convw_ref[...]
    conv = u2 * cw[0:1, :] + u1 * cw[1:2, :] + u * cw[2:3, :]
    y_scr[:, 0:CONV_WIDTH] = (cb * conv).astype(BF16)
    tail_scr[...] = u[ts - 8:ts, :]

    raw = lbraw_ref[...]
    mx = jnp.max(raw, axis=0, keepdims=True)
    ex = jnp.exp(raw - mx)
    lb_all = ex[0:1, :] / jnp.sum(ex, axis=0, keepdims=True)
    hn = hnorm_ref[...]

    for hd in range(HGRN_HEADS):
        lo = hd * HEAD_DIM
        sl = slice(lo, lo + HEAD_DIM)
        col = lambda slot: slice(slot * 512 + lo, slot * 512 + lo + HEAD_DIM)
        for t0 in range(0, ts, HGRN_TILE):
            chunks = []
            for r0 in range(t0, t0 + HGRN_TILE, CHUNK):
                rows = slice(r0, r0 + CHUNK)
                chunks.append(_hgrn_chunk(p_scr[rows, col(3)], p_scr[rows, col(4)],
                                          p_scr[rows, col(5)], lb_all[:, sl], lvl_ref[...]))
            outs, st_new = _hgrn_tile(chunks, st_scr[hd])
            st_scr[hd] = st_new
            for c, o in enumerate(outs):
                rows = slice(t0 + c * CHUNK, t0 + (c + 1) * CHUNK)
                g = p_scr[rows, col(6)]
                o = o * lax.rsqrt(jnp.mean(o * o, axis=-1, keepdims=True) + EPS) * hn[:, sl]
                y_scr[rows, CONV_WIDTH + lo:CONV_WIDTH + lo + HEAD_DIM] = \
                    (o * (g * _sigmoid(g))).astype(BF16)

    o_ref[0] = x + _dot(y_scr[...], wout_ref[...])


def _mixer(x, mix_norm, w_in, conv_w, hgrn_lb, hgrn_norm, w_out):
    bsz, seq, d = x.shape
    ts = SEQ_TILE
    const = lambda b, j: (0, 0)
    return pl.pallas_call(
        _mixer_kernel,
        grid=(bsz, seq // ts),
        in_specs=[
            pl.BlockSpec((1, ts, d), lambda b, j: (b, j, 0)),
            pl.BlockSpec((1, d), const),
            pl.BlockSpec((d, PROJ_WIDTH), const),
            pl.BlockSpec((3, CONV_WIDTH), const),
            pl.BlockSpec((2, HGRN_WIDTH), const),
            pl.BlockSpec((1, HGRN_WIDTH), const),
            pl.BlockSpec((d, d), const),
            pl.BlockSpec((CHUNK, CHUNK), const),
        ],
        out_specs=pl.BlockSpec((1, ts, d), lambda b, j: (b, j, 0)),
        out_shape=jax.ShapeDtypeStruct((bsz, seq, d), F32),
        scratch_shapes=[
            pltpu.VMEM((ts, PROJ_WIDTH), F32),
            pltpu.VMEM((ts, d), BF16),
            pltpu.VMEM((HGRN_HEADS, HEAD_DIM, HEAD_DIM), F32),
            pltpu.VMEM((8, CONV_WIDTH), F32),
        ],
        compiler_params=pltpu.CompilerParams(
            dimension_semantics=("arbitrary", "arbitrary"), vmem_limit_bytes=VMEM_LIMIT),
        name="mixer",
    )(x, mix_norm, w_in, conv_w, hgrn_lb, hgrn_norm, w_out, jnp.asarray(_split_levels()))


def _kv_kernel(m_ref, g_ref, w_ref, o_ref):
    h = _rms(m_ref[0], g_ref[...]).astype(BF16)
    o_ref[0] = _dot(h, w_ref[...]).astype(BF16)


def _kv_proj(mem, mem_norm, w_kv):
    bsz, mlen, d = mem.shape
    const = lambda b: (0, 0)
    return pl.pallas_call(
        _kv_kernel,
        grid=(bsz,),
        in_specs=[
            pl.BlockSpec((1, mlen, d), lambda b: (b, 0, 0)),
            pl.BlockSpec((1, d), const),
            pl.BlockSpec((d, 2 * d), const),
        ],
        out_specs=pl.BlockSpec((1, mlen, 2 * d), lambda b: (b, 0, 0)),
        out_shape=jax.ShapeDtypeStruct((bsz, mlen, 2 * d), BF16),
        compiler_params=pltpu.CompilerParams(
            dimension_semantics=("arbitrary",), vmem_limit_bytes=VMEM_LIMIT),
        name="kv_proj",
    )(mem, mem_norm, w_kv)


def _first_argmax(vals, lane):
    mx = jnp.max(vals, axis=-1, keepdims=True)
    idx = jnp.min(jnp.where(vals == mx, lane, float(LANES)), axis=-1, keepdims=True)
    return mx, idx


def _router_logits(h3, wr_ref, br_ref):
    h3_hi = h3.astype(BF16)
    h3_lo = (h3 - h3_hi.astype(F32)).astype(BF16)
    two = _dot(h3_hi, wr_ref[...])
    return (two[:, :LANES] + two[:, LANES:]) + _dot(h3_lo, wr_ref[:, :LANES]) + br_ref[...]


def _route(logits, running, tri):
    lane = lax.broadcasted_iota(jnp.int32, logits.shape, 1).astype(F32)
    neg = jnp.float32(-jnp.inf)
    gl = jnp.where(lane < N_GROUPS, logits, neg)
    _, gidx = _first_argmax(gl, lane)
    base = N_GROUPS + EXPERTS_PER_GROUP * gidx
    el = jnp.where((lane >= base) & (lane < base + EXPERTS_PER_GROUP), logits, neg)
    _, i1 = _first_argmax(el, lane)
    _, i2 = _first_argmax(jnp.where(lane == i1, neg, el), lane)
    lo = jnp.minimum(i1, i2) - base
    hi = jnp.maximum(i1, i2) - base
    bucket = gidx * N_PAIRS + (lo * (7.0 - lo)) * 0.5 + hi - lo - 1.0

    onehot = lane == bucket
    ranks = []
    for r0 in range(0, logits.shape[0], RANK_BLOCK):
        oh = onehot[r0:r0 + RANK_BLOCK]
        before = _dot(tri, oh.astype(BF16))
        ranks.append(jnp.sum(jnp.where(oh, before + running, 0.0), axis=-1, keepdims=True))
        running = running + jnp.sum(oh.astype(F32), axis=0, keepdims=True)
    rank = jnp.concatenate(ranks, axis=0)
    info = jnp.where(lane == 0, bucket, jnp.where(lane == 1, rank, 0.0))
    return info, running


def _gates(logits, e_lo, e_hi):
    lane = lax.broadcasted_iota(jnp.int32, logits.shape, 1)
    neg = jnp.float32(-jnp.inf)
    gl = jnp.where(lane < N_GROUPS, logits, neg)
    g_p = 1.0 / jnp.sum(jnp.exp(gl - jnp.max(gl, axis=-1, keepdims=True)), axis=-1, keepdims=True)
    l_lo = jnp.sum(jnp.where(lane == N_GROUPS + e_lo, logits, 0.0), axis=-1, keepdims=True)
    l_hi = jnp.sum(jnp.where(lane == N_GROUPS + e_hi, logits, 0.0), axis=-1, keepdims=True)
    m = jnp.maximum(l_lo, l_hi)
    p_lo = jnp.exp(l_lo - m)
    p_hi = jnp.exp(l_hi - m)
    inv = g_p / (p_lo + p_hi)
    return p_lo * inv, p_hi * inv


def _attn_kernel(x_ref, kv_ref, gx_ref, wq_ref, wo_ref, gf_ref, wr_ref, br_ref, tri_ref,
                 x2r_ref, brk_ref, counts_ref, o_scr, cnt_scr):
    @pl.when((pl.program_id(0) == 0) & (pl.program_id(1) == 0))
    def _():
        cnt_scr[...] = jnp.zeros_like(cnt_scr)

    x = x_ref[0]
    tq = x.shape[0]
    h = _rms(x, gx_ref[...]).astype(BF16)
    q = _dot(h, wq_ref[...])
    for hd in range(XATTN_HEADS):
        lo = hd * XATTN_HEAD_DIM
        qh = q[:, lo:lo + XATTN_HEAD_DIM].astype(BF16)
        kh = kv_ref[0, :, lo:lo + XATTN_HEAD_DIM]
        vh = kv_ref[0, :, D_MODEL + lo:D_MODEL + lo + XATTN_HEAD_DIM]
        s = _dot_nt(qh, kh)
        p = jnp.exp(s - jnp.max(s, axis=-1, keepdims=True))
        inv = 1.0 / jnp.sum(p, axis=-1, keepdims=True)
        o_scr[:, lo:lo + XATTN_HEAD_DIM] = (_dot(p.astype(BF16), vh) * inv).astype(BF16)
    x2 = x + _dot(o_scr[...], wo_ref[...])
    for s in range(ROW_TILES):
        x2r_ref[pl.ds(s, tq, stride=ROW_TILES), :] = x2[:, s * LANES:(s + 1) * LANES]
    logits = _router_logits(_rms(x2, gf_ref[...]), wr_ref, br_ref)
    info, running = _route(logits, cnt_scr[...], tri_ref[...])
    cnt_scr[...] = running
    counts_ref[...] = running
    brk_ref[...] = info.T[0:8, :].astype(jnp.int32)


def _attention(x1, kv, xattn_norm, w_q, w_o, ffn_norm, w_router, b_router):
    bsz, seq, d = x1.shape
    mlen = kv.shape[1]
    tq = ATTN_TILE
    nj = seq // tq
    const = lambda b, j: (0, 0)
    tile = lambda b, j: (b, j, 0)
    tri = jnp.tri(RANK_BLOCK, RANK_BLOCK, -1, dtype=BF16)
    return pl.pallas_call(
        _attn_kernel,
        grid=(bsz, nj),
        in_specs=[
            pl.BlockSpec((1, tq, d), tile),
            pl.BlockSpec((1, mlen, 2 * d), lambda b, j: (b, 0, 0)),
            pl.BlockSpec((1, d), const),
            pl.BlockSpec((d, d), const),
            pl.BlockSpec((d, d), const),
            pl.BlockSpec((1, d), const),
            pl.BlockSpec((d, 2 * LANES), const),
            pl.BlockSpec((1, LANES), const),
            pl.BlockSpec((RANK_BLOCK, RANK_BLOCK), const),
        ],
        out_specs=[
            pl.BlockSpec((tq * ROW_TILES, LANES), lambda b, j: (b * nj + j, 0)),
            pl.BlockSpec((8, tq), lambda b, j: (0, b * nj + j)),
            pl.BlockSpec((1, LANES), const),
        ],
        out_shape=[
            jax.ShapeDtypeStruct((bsz * seq * ROW_TILES, LANES), F32),
            jax.ShapeDtypeStruct((8, bsz * seq), jnp.int32),
            jax.ShapeDtypeStruct((1, LANES), F32),
        ],
        scratch_shapes=[pltpu.VMEM((tq, d), BF16), pltpu.VMEM((1, LANES), F32)],
        compiler_params=pltpu.CompilerParams(
            dimension_semantics=("arbitrary", "arbitrary"), vmem_limit_bytes=VMEM_LIMIT),
        name="xattn_router",
    )(x1, kv, xattn_norm, w_q, w_o, ffn_norm, w_router, b_router, tri)


def _item_copy(hbm, hbm_item, buf, buf_item, rows_per_item, sem, to_hbm):
    h = hbm.at[pl.ds(pl.multiple_of(hbm_item * rows_per_item, rows_per_item), rows_per_item)]
    b = buf.at[pl.ds(pl.multiple_of(buf_item * rows_per_item, rows_per_item), rows_per_item)]
    return pltpu.make_async_copy(b, h, sem) if to_hbm else pltpu.make_async_copy(h, b, sem)


def _items_wait(hbm, buf, buf_item, n_items, rows_per_item, sem, to_hbm):
    n = n_items * rows_per_item
    h = hbm.at[pl.ds(0, n)]
    b = buf.at[pl.ds(pl.multiple_of(buf_item * rows_per_item, rows_per_item), n)]
    (pltpu.make_async_copy(b, h, sem) if to_hbm else pltpu.make_async_copy(h, b, sem)).wait()


def _stagger(p, bucket):
    row = (p + bucket * STAGGER_ROWS) & (MOE_TILE - 1)
    return (p & ~(MOE_TILE - 1)) | row


def _dispatch_kernel(bucket_ref, rank_ref, starts_ref, fill_ref, ntiles_ref,
                     x2r_ref, xs_hbm, pos_ref, buf, sem):
    i = pl.program_id(0)
    nt = pl.num_programs(0)
    td = DISPATCH_TILE
    rpi = ROW_TILES
    slot = i % 2
    base = slot * td

    @pl.when(i >= 2)
    def _():
        _items_wait(xs_hbm, buf, base, td, rpi, sem.at[slot], True)

    buf[pl.ds(pl.multiple_of(base * rpi, td * rpi), td * rpi), :] = x2r_ref[...]

    def send(r2, carry):
        for k in range(DMA_QUEUES):
            r = r2 * DMA_QUEUES + k
            t = i * td + r
            b = bucket_ref[t]
            p = _stagger(starts_ref[b] + rank_ref[t], b)
            pos_ref[t] = p
            _item_copy(xs_hbm, p, buf, base + r, rpi, sem.at[slot], True).start(priority=k)
        return carry

    lax.fori_loop(0, td // DMA_QUEUES, send, 0, unroll=4)

    @pl.when(i == nt - 1)
    def _():
        _items_wait(xs_hbm, buf, base, td, rpi, sem.at[slot], True)
        _items_wait(xs_hbm, buf, (1 - slot) * td, td, rpi, sem.at[1 - slot], True)
        buf[...] = jnp.zeros_like(buf)
        zsem = sem.at[2]
        for b in range(N_BUCKETS):
            def fill(p, carry):
                _item_copy(xs_hbm, _stagger(p, b), buf, 0, rpi, zsem, True).start()
                return carry

            def drain(p, carry):
                _item_copy(xs_hbm, _stagger(p, b), buf, 0, rpi, zsem, True).wait()
                return carry

            lax.fori_loop(fill_ref[b], fill_ref[N_BUCKETS + b], fill, 0)
            lax.fori_loop(fill_ref[b], fill_ref[N_BUCKETS + b], drain, 0)

        n_tiles_total = xs_hbm.shape[0] // (MOE_TILE * rpi)

        def fill_tile(tile, carry):
            cp = pltpu.make_async_copy(
                buf.at[pl.ds(0, MOE_TILE * rpi)],
                xs_hbm.at[pl.ds(pl.multiple_of(tile * MOE_TILE * rpi, MOE_TILE * rpi),
                                MOE_TILE * rpi)], zsem)
            cp.start()
            cp.wait()
            return carry

        lax.fori_loop(ntiles_ref[0], n_tiles_total, fill_tile, 0)


def _dispatch(bucket, rank, starts, fill, ntiles, x2r, n_sorted):
    t = x2r.shape[0] // ROW_TILES
    td = DISPATCH_TILE
    grid_spec = pltpu.PrefetchScalarGridSpec(
        num_scalar_prefetch=5,
        grid=(t // td,),
        in_specs=[pl.BlockSpec((td * ROW_TILES, LANES), lambda i, *_: (i, 0))],
        out_specs=[pl.BlockSpec(memory_space=pl.ANY), pl.BlockSpec(memory_space=pltpu.SMEM)],
        scratch_shapes=[
            pltpu.VMEM((2 * td * ROW_TILES, LANES), F32),
            pltpu.SemaphoreType.DMA((3,)),
        ],
    )
    return pl.pallas_call(
        _dispatch_kernel,
        grid_spec=grid_spec,
        out_shape=[jax.ShapeDtypeStruct((n_sorted * ROW_TILES, LANES), F32),
                   jax.ShapeDtypeStruct((t,), jnp.int32)],
        compiler_params=pltpu.CompilerParams(
            dimension_semantics=("arbitrary",), vmem_limit_bytes=VMEM_LIMIT),
        name="moe_dispatch",
    )(bucket, rank, starts, fill, ntiles, x2r)


def _silu(x):
    return x * _sigmoid(x)


def _moe_kernel(elo_ref, ehi_ref, used_ref,
                xs_ref, gf_ref, wr_ref, br_ref, gfin_ref,
                wg_lo, wu_lo, wd_lo, wg_hi, wu_hi, wd_hi, y_ref):
    i = pl.program_id(0)
    tm = MOE_TILE

    @pl.when(used_ref[i] == 1)
    def _():
        x2 = jnp.concatenate(
            [xs_ref[pl.ds(s, tm, stride=ROW_TILES), :] for s in range(ROW_TILES)], axis=1)
        h3 = _rms(x2, gf_ref[...])
        g_lo, g_hi = _gates(_router_logits(h3, wr_ref, br_ref), elo_ref[i], ehi_ref[i])
        x = h3.astype(BF16)

        def expert(wg, wu, wd):
            hid = (_silu(_dot(x, wg[0])) * _dot(x, wu[0])).astype(BF16)
            return _dot(hid, wd[0])

        moe = g_lo * expert(wg_lo, wu_lo, wd_lo) + g_hi * expert(wg_hi, wu_hi, wd_hi)
        out = _rms(x2 + moe, gfin_ref[...])
        for s in range(ROW_TILES):
            y_ref[pl.ds(s, tm, stride=ROW_TILES), :] = out[:, s * LANES:(s + 1) * LANES]

    @pl.when(used_ref[i] == 0)
    def _():
        y_ref[...] = jnp.zeros_like(y_ref)


def _moe_sparse(e_lo, e_hi, used, xs, ffn_norm, w_router, b_router, final_norm,
                w_gate, w_up, w_down):
    tm = MOE_TILE
    nt = used.shape[0]
    d = D_MODEL
    lo = lambda i, elo, ehi, used: (elo[i], 0, 0)
    hi = lambda i, elo, ehi, used: (ehi[i], 0, 0)
    const = lambda i, *_: (0, 0)
    grid_spec = pltpu.PrefetchScalarGridSpec(
        num_scalar_prefetch=3,
        grid=(nt,),
        in_specs=[
            pl.BlockSpec((tm * ROW_TILES, LANES), lambda i, *_: (i, 0)),
            pl.BlockSpec((1, d), const),
            pl.BlockSpec((d, 2 * LANES), const),
            pl.BlockSpec((1, LANES), const),
            pl.BlockSpec((1, d), const),
            pl.BlockSpec((1, d, D_EXPERT), lo),
            pl.BlockSpec((1, d, D_EXPERT), lo),
            pl.BlockSpec((1, D_EXPERT, d), lo),
            pl.BlockSpec((1, d, D_EXPERT), hi),
            pl.BlockSpec((1, d, D_EXPERT), hi),
            pl.BlockSpec((1, D_EXPERT, d), hi),
        ],
        out_specs=pl.BlockSpec((tm * ROW_TILES, LANES), lambda i, *_: (i, 0)),
    )
    return pl.pallas_call(
        _moe_kernel,
        grid_spec=grid_spec,
        out_shape=jax.ShapeDtypeStruct((nt * tm * ROW_TILES, LANES), F32),
        compiler_params=pltpu.CompilerParams(
            dimension_semantics=("arbitrary",), vmem_limit_bytes=VMEM_LIMIT),
        name="moe_sparse",
    )(e_lo, e_hi, used, xs, ffn_norm, w_router, b_router, final_norm,
      w_gate, w_up, w_down, w_gate, w_up, w_down)


def _final_kernel(pos_ref, y_hbm, o_ref, ybuf, sem):
    i = pl.program_id(0)
    nt = pl.num_programs(0)
    tf = FINAL_TILE

    def start(tile, slot):
        def fetch(r2, carry):
            for k in range(DMA_QUEUES):
                r = r2 * DMA_QUEUES + k
                _item_copy(y_hbm, pos_ref[tile * tf + r], ybuf, slot * tf + r, ROW_TILES,
                           sem.at[slot], False).start(priority=k)
            return carry

        lax.fori_loop(0, tf // DMA_QUEUES, fetch, 0, unroll=4)

    @pl.when(i == 0)
    def _():
        start(0, 0)

    @pl.when(i + 1 < nt)
    def _():
        start(i + 1, (i + 1) % 2)

    slot = i % 2
    _items_wait(y_hbm, ybuf, slot * tf, tf, ROW_TILES, sem.at[slot], False)
    row0 = pl.multiple_of(slot * tf * ROW_TILES, ROW_TILES)
    for s in range(ROW_TILES):
        o_ref[:, s * LANES:(s + 1) * LANES] = ybuf[pl.ds(row0 + s, tf, stride=ROW_TILES), :]


def _final(pos, y):
    t = pos.shape[0]
    d = D_MODEL
    tf = FINAL_TILE
    grid_spec = pltpu.PrefetchScalarGridSpec(
        num_scalar_prefetch=1,
        grid=(t // tf,),
        in_specs=[pl.BlockSpec(memory_space=pl.ANY)],
        out_specs=pl.BlockSpec((tf, d), lambda i, *_: (i, 0)),
        scratch_shapes=[
            pltpu.VMEM((2 * tf * ROW_TILES, LANES), F32),
            pltpu.SemaphoreType.DMA((2,)),
        ],
    )
    return pl.pallas_call(
        _final_kernel,
        grid_spec=grid_spec,
        out_shape=jax.ShapeDtypeStruct((t, d), F32),
        compiler_params=pltpu.CompilerParams(
            dimension_semantics=("arbitrary",), vmem_limit_bytes=VMEM_LIMIT),
        name="moe_unpermute",
    )(pos, y)


def _routing_tables(counts, t):
    tm = MOE_TILE
    nt = t // tm + N_BUCKETS
    cnt = counts[0, :N_BUCKETS].astype(jnp.int32)
    padded = ((cnt + tm - 1) // tm) * tm
    ends = jnp.cumsum(padded)
    starts = ends - padded
    fill = jnp.concatenate([starts + cnt, ends])
    ntiles = ends[-1:] // tm
    tile_start = jnp.arange(nt, dtype=jnp.int32) * tm
    tile_bucket = jnp.sum((ends[None, :] <= tile_start[:, None]).astype(jnp.int32), axis=1)
    used = (tile_bucket < N_BUCKETS).astype(jnp.int32)
    tile_bucket = jnp.minimum(tile_bucket, N_BUCKETS - 1)
    pair = tile_bucket % N_PAIRS
    group0 = (tile_bucket // N_PAIRS) * EXPERTS_PER_GROUP
    e_lo = group0 + jnp.array([0, 0, 0, 1, 1, 2], jnp.int32)[pair]
    e_hi = group0 + jnp.array([1, 2, 3, 2, 3, 3], jnp.int32)[pair]
    return starts, fill, ntiles, e_lo, e_hi, used, nt


def kernel(x, mem, mix_norm, w_in, conv_w, hgrn_lb, hgrn_norm, w_out, xattn_norm, mem_norm,
           w_q, w_kv, w_o, ffn_norm, w_group, b_group, w_expert, b_expert, w_gate, w_up,
           w_down, final_norm):
    bsz, seq, d = x.shape
    assert d == D_MODEL and seq % SEQ_TILE == 0 and seq % ATTN_TILE == 0
    assert (bsz * seq) % MOE_TILE == 0 and (bsz * seq) % FINAL_TILE == 0
    assert (bsz * seq) % DISPATCH_TILE == 0 and bsz * seq >= 2 * DISPATCH_TILE
    assert 2 * DISPATCH_TILE >= MOE_TILE and mix_norm.shape[0] == 1
    bf = lambda w: w.astype(BF16)

    x1 = _mixer(x, mix_norm, bf(w_in[0]), conv_w[0], hgrn_lb, hgrn_norm, bf(w_out[0]))
    kv = _kv_proj(mem, mem_norm, bf(w_kv[0]))

    pad = LANES - N_GROUPS - N_EXPERTS
    w_router = jnp.concatenate(
        [w_group[0], w_expert[0], jnp.zeros((d, pad), F32)], axis=1)
    b_router = jnp.concatenate(
        [b_group[0], b_expert[0], jnp.zeros((pad,), F32)])[None, :]
    w_router_hi = bf(w_router)
    w_router_lo = bf(w_router - w_router_hi.astype(F32))
    w_router2 = jnp.concatenate([w_router_hi, w_router_lo], axis=1)
    w_q_scaled = bf(w_q[0] * (XATTN_HEAD_DIM ** -0.5))
    x2r, brk, counts = _attention(x1, kv, xattn_norm, w_q_scaled, bf(w_o[0]), ffn_norm,
                                  w_router2, b_router)

    t = bsz * seq
    bucket = brk[0]
    rank = brk[1]
    starts, fill, ntiles, e_lo, e_hi, used, nt = _routing_tables(counts, t)
    xs, pos = _dispatch(bucket, rank, starts, fill, ntiles, x2r, nt * MOE_TILE)
    y = _moe_sparse(e_lo, e_hi, used, xs, ffn_norm, w_router2, b_router, final_norm[None, :],
                    bf(w_gate[0]), bf(w_up[0]), bf(w_down[0]))
    return _final(pos, y).reshape(bsz, seq, d)
```

```python
import jax
import jax.numpy as jnp
from jax import lax
from jax.experimental import pallas as pl
from jax.experimental.pallas import tpu as pltpu

F32 = jnp.float32
BF16 = jnp.bfloat16

D_MODEL = 1024
CONV_WIDTH = 512
HGRN_WIDTH = 512
HGRN_HEADS = 4
HEAD_DIM = 128
N_PROJ_SLOTS = 7
PROJ_WIDTH = N_PROJ_SLOTS * 512
XATTN_HEADS = 4
XATTN_HEAD_DIM = 256
N_GROUPS = 4
EXPERTS_PER_GROUP = 4
N_EXPERTS = 16
D_EXPERT = 512
EPS = 1e-6

LANES = 128
SUBLANES = 8
CHUNK = 64
CHUNK_LEVELS = 6
HGRN_TILE = 4 * CHUNK
SEQ_TILE = 512
ATTN_TILE = 1024
RANK_BLOCK = 256
DISPATCH_TILE = 512
MOE_TILE = 256
FINAL_TILE = 512
N_PAIRS = 6
N_BUCKETS = N_GROUPS * N_PAIRS
ROW_TILES = D_MODEL // LANES
DMA_QUEUES = 2
VMEM_LIMIT = 56 * 1024 * 1024


def _rms(x, g):
    return x * lax.rsqrt(jnp.mean(x * x, axis=-1, keepdims=True) + EPS) * g


def _dot(a, b):
    return jnp.dot(a, b, preferred_element_type=F32)


def _dot_nt(a, b):
    return lax.dot_general(a, b, (((1,), (1,)), ((), ())), preferred_element_type=F32)


def _dot_tn(a, b):
    return lax.dot_general(a, b, (((0,), (0,)), ((), ())), preferred_element_type=F32)


def _roll_rows(x, shift):
    return pltpu.roll(x, shift % x.shape[0], axis=0)


def _level_exponents(logf2, use_level):
    n = CHUNK // SUBLANES
    sub = lax.broadcasted_iota(jnp.int32, (SUBLANES, logf2.shape[1]), 0)
    roll = lambda x, s: pltpu.roll(x, s % SUBLANES, axis=0)
    r = [logf2[SUBLANES * j:SUBLANES * (j + 1), :] for j in range(n)]
    for lvl in range(1, CHUNK_LEVELS + 1):
        half = 1 << (lvl - 1)
        g = [None] * n
        if half < SUBLANES:
            second = (sub & half) != 0
            for j in range(n):
                last = r[j]
                w = 1
                while w < half:
                    last = jnp.where((sub & w) != 0, last, roll(last, -w))
                    w *= 2
                tot = jnp.where(second, roll(last, half), last)
                g[j] = jnp.where(second, r[j], tot - r[j])
                r[j] = jnp.where(second, r[j] + tot, r[j])
        else:
            hv = half // SUBLANES
            for j0 in range(0, n, 2 * hv):
                mid = r[j0 + hv - 1]
                tot = jnp.broadcast_to(mid[SUBLANES - 1:SUBLANES, :], mid.shape)
                for j in range(j0, j0 + hv):
                    g[j] = tot - r[j]
                for j in range(j0 + hv, j0 + 2 * hv):
                    g[j] = r[j]
                    r[j] = r[j] + tot
        use_level(lvl, jnp.concatenate(g, axis=0))
    return jnp.concatenate(r, axis=0)


def _sigmoid(x):
    return 0.5 * jnp.tanh(0.5 * x) + 0.5


def _split_levels():
    import numpy as np
    t = np.arange(CHUNK)[:, None]
    s = np.arange(CHUNK)[None, :]
    msb = np.floor(np.log2(np.maximum(t ^ s, 1))).astype(np.int32) + 1
    return np.where(s < t, msb, np.where(s == t, 0, -1)).astype(np.int32)


def _hgrn_chunk(q, z, v, lb, levels):
    half_th = 0.5 * jnp.tanh(0.5 * z)
    one_m_lb = 1.0 - lb
    logf2 = jnp.log2(lb + one_m_lb * (0.5 + half_th))
    k = one_m_lb * (0.5 - half_th)

    qb = q.astype(BF16)
    kb = k.astype(BF16)
    scores = [jnp.where(levels == 0, jnp.sum(q * k, axis=-1, keepdims=True), 0.0)]

    def use_level(lvl, g):
        decay = jnp.exp2(g).astype(BF16)
        scores[0] = jnp.where(levels == lvl, _dot_nt(qb * decay, kb * decay), scores[0])

    b2 = _level_exponents(logf2, use_level)
    a = scores[0]
    b2_last = b2[CHUNK - 1:CHUNK, :]

    qe = (q * jnp.exp2(b2)).astype(BF16)
    kd = (k * jnp.exp2(b2_last - b2)).astype(BF16)
    return a, qe, kd, v.astype(BF16), b2_last


def _hgrn_tile(chunks, st):
    (a0, qe0, kd0, v0, bl0), (a1, qe1, kd1, v1, bl1), (a2, qe2, kd2, v2, bl2), \
        (a3, qe3, kd3, v3, bl3) = chunks
    scale = lambda x, log2_decay: x * jnp.exp2(log2_decay).astype(BF16)
    bf = lambda x: x.astype(BF16)
    cum1 = bl0
    cum2 = cum1 + bl1
    cum3 = cum2 + bl2
    cum4 = cum3 + bl3
    s10 = bf(_dot_nt(qe1, kd0))
    s32 = bf(_dot_nt(qe3, kd2))
    s8 = bf(_dot_nt(jnp.concatenate([qe2, scale(qe3, bl2)], axis=0),
                    jnp.concatenate([scale(kd0, bl1), kd1], axis=0)))
    v01 = jnp.concatenate([v0, v1], axis=0)
    stb = bf(st)
    o0 = _dot(bf(a0), v0) + _dot_nt(qe0, stb)
    o1 = _dot(bf(a1), v1) + _dot(s10, v0) + _dot_nt(scale(qe1, cum1), stb)
    o2 = _dot(bf(a2), v2) + _dot(s8[:CHUNK], v01) + _dot_nt(scale(qe2, cum2), stb)
    o3 = (_dot(bf(a3), v3) + _dot(s32, v2) + _dot(s8[CHUNK:], v01)
          + _dot_nt(scale(qe3, cum3), stb))
    kd_all = jnp.concatenate(
        [scale(kd0, cum4 - cum1), scale(kd1, cum4 - cum2), scale(kd2, bl3), kd3], axis=0)
    v_all = jnp.concatenate([v0, v1, v2, v3], axis=0)
    st_new = st * jnp.exp2(cum4) + _dot_tn(v_all, kd_all)
    return [o0, o1, o2, o3], st_new


def _mixer_kernel(x_ref, gmix_ref, win_ref, convw_ref, lbraw_ref, hnorm_ref, wout_ref, lvl_ref,
                  o_ref, p_scr, y_scr, st_scr, tail_scr):
    j = pl.program_id(1)

    @pl.when(j == 0)
    def _():
        st_scr[...] = jnp.zeros_like(st_scr)
        tail_scr[...] = jnp.zeros_like(tail_scr)

    x = x_ref[0]
    h = _rms(x, gmix_ref[...]).astype(BF16)
    p_scr[...] = _dot(h, win_ref[...])

    ts = x.shape[0]
    cb = p_scr[:, 0:CONV_WIDTH]
    u = p_scr[:, CONV_WIDTH:2 * CONV_WIDTH] * p_scr[:, 2 * CONV_WIDTH:3 * CONV_WIDTH]
    row = lax.broadcasted_iota(jnp.int32, u.shape, 0)
    prev1 = tail_scr[7:8, :]
    prev2 = tail_scr[6:7, :]
    u1 = jnp.where(row == 0, prev1, _roll_rows(u, 1))
    u2 = jnp.where(row == 0, prev2, jnp.where(row == 1, prev1, _roll_rows(u, 2)))
    cw = convw_ref[...]
    conv = u2 * cw[0:1, :] + u1 * cw[1:2, :] + u * cw[2:3, :]
    y_scr[:, 0:CONV_WIDTH] = (cb * conv).astype(BF16)
    tail_scr[...] = u[ts - 8:ts, :]

    raw = lbraw_ref[...]
    mx = jnp.max(raw, axis=0, keepdims=True)
    ex = jnp.exp(raw - mx)
    lb_all = ex[0:1, :] / jnp.sum(ex, axis=0, keepdims=True)
    hn = hnorm_ref[...]

    for hd in range(HGRN_HEADS):
        lo = hd * HEAD_DIM
        sl = slice(lo, lo + HEAD_DIM)
        col = lambda slot: slice(slot * 512 + lo, slot * 512 + lo + HEAD_DIM)
        for t0 in range(0, ts, HGRN_TILE):
            chunks = []
            for r0 in range(t0, t0 + HGRN_TILE, CHUNK):
                rows = slice(r0, r0 + CHUNK)
                chunks.append(_hgrn_chunk(p_scr[rows, col(3)], p_scr[rows, col(4)],
                                          p_scr[rows, col(5)], lb_all[:, sl], lvl_ref[...]))
            outs, st_new = _hgrn_tile(chunks, st_scr[hd])
            st_scr[hd] = st_new
            for c, o in enumerate(outs):
                rows = slice(t0 + c * CHUNK, t0 + (c + 1) * CHUNK)
                g = p_scr[rows, col(6)]
                o = o * lax.rsqrt(jnp.mean(o * o, axis=-1, keepdims=True) + EPS) * hn[:, sl]
                y_scr[rows, CONV_WIDTH + lo:CONV_WIDTH + lo + HEAD_DIM] = \
                    (o * (g * _sigmoid(g))).astype(BF16)

    o_ref[0] = x + _dot(y_scr[...], wout_ref[...])


def _mixer(x, mix_norm, w_in, conv_w, hgrn_lb, hgrn_norm, w_out):
    bsz, seq, d = x.shape
    ts = SEQ_TILE
    const = lambda b, j: (0, 0)
    return pl.pallas_call(
        _mixer_kernel,
        grid=(bsz, seq // ts),
        in_specs=[
            pl.BlockSpec((1, ts, d), lambda b, j: (b, j, 0)),
            pl.BlockSpec((1, d), const),
            pl.BlockSpec((d, PROJ_WIDTH), const),
            pl.BlockSpec((3, CONV_WIDTH), const),
            pl.BlockSpec((2, HGRN_WIDTH), const),
            pl.BlockSpec((1, HGRN_WIDTH), const),
            pl.BlockSpec((d, d), const),
            pl.BlockSpec((CHUNK, CHUNK), const),
        ],
        out_specs=pl.BlockSpec((1, ts, d), lambda b, j: (b, j, 0)),
        out_shape=jax.ShapeDtypeStruct((bsz, seq, d), F32),
        scratch_shapes=[
            pltpu.VMEM((ts, PROJ_WIDTH), F32),
            pltpu.VMEM((ts, d), BF16),
            pltpu.VMEM((HGRN_HEADS, HEAD_DIM, HEAD_DIM), F32),
            pltpu.VMEM((8, CONV_WIDTH), F32),
        ],
        compiler_params=pltpu.CompilerParams(
            dimension_semantics=("arbitrary", "arbitrary"), vmem_limit_bytes=VMEM_LIMIT),
        name="mixer",
    )(x, mix_norm, w_in, conv_w, hgrn_lb, hgrn_norm, w_out, jnp.asarray(_split_levels()))


def _kv_kernel(m_ref, g_ref, w_ref, o_ref):
    h = _rms(m_ref[0], g_ref[...]).astype(BF16)
    o_ref[0] = _dot(h, w_ref[...]).astype(BF16)


def _kv_proj(mem, mem_norm, w_kv):
    bsz, mlen, d = mem.shape
    const = lambda b: (0, 0)
    return pl.pallas_call(
        _kv_kernel,
        grid=(bsz,),
        in_specs=[
            pl.BlockSpec((1, mlen, d), lambda b: (b, 0, 0)),
            pl.BlockSpec((1, d), const),
            pl.BlockSpec((d, 2 * d), const),
        ],
        out_specs=pl.BlockSpec((1, mlen, 2 * d), lambda b: (b, 0, 0)),
        out_shape=jax.ShapeDtypeStruct((bsz, mlen, 2 * d), BF16),
        compiler_params=pltpu.CompilerParams(
            dimension_semantics=("arbitrary",), vmem_limit_bytes=VMEM_LIMIT),
        name="kv_proj",
    )(mem, mem_norm, w_kv)


def _first_argmax(vals, lane):
    mx = jnp.max(vals, axis=-1, keepdims=True)
    idx = jnp.min(jnp.where(vals == mx, lane, float(LANES)), axis=-1, keepdims=True)
    return mx, idx


def _router_logits(h3, wr_ref, br_ref):
    h3_hi = h3.astype(BF16)
    h3_lo = (h3 - h3_hi.astype(F32)).astype(BF16)
    two = _dot(h3_hi, wr_ref[...])
    return (two[:, :LANES] + two[:, LANES:]) + _dot(h3_lo, wr_ref[:, :LANES]) + br_ref[...]


def _route(logits, running, tri):
    lane = lax.broadcasted_iota(jnp.int32, logits.shape, 1).astype(F32)
    neg = jnp.float32(-jnp.inf)
    gl = jnp.where(lane < N_GROUPS, logits, neg)
    _, gidx = _first_argmax(gl, lane)
    base = N_GROUPS + EXPERTS_PER_GROUP * gidx
    el = jnp.where((lane >= base) & (lane < base + EXPERTS_PER_GROUP), logits, neg)
    _, i1 = _first_argmax(el, lane)
    _, i2 = _first_argmax(jnp.where(lane == i1, neg, el), lane)
    lo = jnp.minimum(i1, i2) - base
    hi = jnp.maximum(i1, i2) - base
    bucket = gidx * N_PAIRS + (lo * (7.0 - lo)) * 0.5 + hi - lo - 1.0

    onehot = lane == bucket
    ranks = []
    for r0 in range(0, logits.shape[0], RANK_BLOCK):
        oh = onehot[r0:r0 + RANK_BLOCK]
        before = _dot(tri, oh.astype(BF16))
        ranks.append(jnp.sum(jnp.where(oh, before + running, 0.0), axis=-1, keepdims=True))
        running = running + jnp.sum(oh.astype(F32), axis=0, keepdims=True)
    rank = jnp.concatenate(ranks, axis=0)
    info = jnp.where(lane == 0, bucket, jnp.where(lane == 1, rank, 0.0))
    return info, running


def _gates(logits, e_lo, e_hi):
    lane = lax.broadcasted_iota(jnp.int32, logits.shape, 1)
    neg = jnp.float32(-jnp.inf)
    gl = jnp.where(lane < N_GROUPS, logits, neg)
    g_p = 1.0 / jnp.sum(jnp.exp(gl - jnp.max(gl, axis=-1, keepdims=True)), axis=-1, keepdims=True)
    l_lo = jnp.sum(jnp.where(lane == N_GROUPS + e_lo, logits, 0.0), axis=-1, keepdims=True)
    l_hi = jnp.sum(jnp.where(lane == N_GROUPS + e_hi, logits, 0.0), axis=-1, keepdims=True)
    m = jnp.maximum(l_lo, l_hi)
    p_lo = jnp.exp(l_lo - m)
    p_hi = jnp.exp(l_hi - m)
    inv = g_p / (p_lo + p_hi)
    return p_lo * inv, p_hi * inv


def _attn_kernel(x_ref, kv_ref, gx_ref, wq_ref, wo_ref, gf_ref, wr_ref, br_ref, tri_ref,
                 x2r_ref, brk_ref, counts_ref, o_scr, cnt_scr):
    @pl.when((pl.program_id(0) == 0) & (pl.program_id(1) == 0))
    def _():
        cnt_scr[...] = jnp.zeros_like(cnt_scr)

    x = x_ref[0]
    tq = x.shape[0]
    h = _rms(x, gx_ref[...]).astype(BF16)
    q = _dot(h, wq_ref[...])
    for hd in range(XATTN_HEADS):
        lo = hd * XATTN_HEAD_DIM
        qh = q[:, lo:lo + XATTN_HEAD_DIM].astype(BF16)
        kh = kv_ref[0, :, lo:lo + XATTN_HEAD_DIM]
        vh = kv_ref[0, :, D_MODEL + lo:D_MODEL + lo + XATTN_HEAD_DIM]
        s = _dot_nt(qh, kh)
        p = jnp.exp(s - jnp.max(s, axis=-1, keepdims=True))
        inv = 1.0 / jnp.sum(p, axis=-1, keepdims=True)
        o_scr[:, lo:lo + XATTN_HEAD_DIM] = (_dot(p.astype(BF16), vh) * inv).astype(BF16)
    x2 = x + _dot(o_scr[...], wo_ref[...])
    for s in range(ROW_TILES):
        x2r_ref[pl.ds(s, tq, stride=ROW_TILES), :] = x2[:, s * LANES:(s + 1) * LANES]
    logits = _router_logits(_rms(x2, gf_ref[...]), wr_ref, br_ref)
    info, running = _route(logits, cnt_scr[...], tri_ref[...])
    cnt_scr[...] = running
    counts_ref[...] = running
    brk_ref[...] = info.T[0:8, :].astype(jnp.int32)


def _attention(x1, kv, xattn_norm, w_q, w_o, ffn_norm, w_router, b_router):
    bsz, seq, d = x1.shape
    mlen = kv.shape[1]
    tq = ATTN_TILE
    nj = seq // tq
    const = lambda b, j: (0, 0)
    tile = lambda b, j: (b, j, 0)
    tri = jnp.tri(RANK_BLOCK, RANK_BLOCK, -1, dtype=BF16)
    return pl.pallas_call(
        _attn_kernel,
        grid=(bsz, nj),
        in_specs=[
            pl.BlockSpec((1, tq, d), tile),
            pl.BlockSpec((1, mlen, 2 * d), lambda b, j: (b, 0, 0)),
            pl.BlockSpec((1, d), const),
            pl.BlockSpec((d, d), const),
            pl.BlockSpec((d, d), const),
            pl.BlockSpec((1, d), const),
            pl.BlockSpec((d, 2 * LANES), const),
            pl.BlockSpec((1, LANES), const),
            pl.BlockSpec((RANK_BLOCK, RANK_BLOCK), const),
        ],
        out_specs=[
            pl.BlockSpec((tq * ROW_TILES, LANES), lambda b, j: (b * nj + j, 0)),
            pl.BlockSpec((8, tq), lambda b, j: (0, b * nj + j)),
            pl.BlockSpec((1, LANES), const),
        ],
        out_shape=[
            jax.ShapeDtypeStruct((bsz * seq * ROW_TILES, LANES), F32),
            jax.ShapeDtypeStruct((8, bsz * seq), jnp.int32),
            jax.ShapeDtypeStruct((1, LANES), F32),
        ],
        scratch_shapes=[pltpu.VMEM((tq, d), BF16), pltpu.VMEM((1, LANES), F32)],
        compiler_params=pltpu.CompilerParams(
            dimension_semantics=("arbitrary", "arbitrary"), vmem_limit_bytes=VMEM_LIMIT),
        name="xattn_router",
    )(x1, kv, xattn_norm, w_q, w_o, ffn_norm, w_router, b_router, tri)


def _item_copy(hbm, hbm_item, buf, buf_item, rows_per_item, sem, to_hbm):
    h = hbm.at[pl.ds(pl.multiple_of(hbm_item * rows_per_item, rows_per_item), rows_per_item)]
    b = buf.at[pl.ds(pl.multiple_of(buf_item * rows_per_item, rows_per_item), rows_per_item)]
    return pltpu.make_async_copy(b, h, sem) if to_hbm else pltpu.make_async_copy(h, b, sem)


def _items_wait(hbm, buf, buf_item, n_items, rows_per_item, sem, to_hbm):
    n = n_items * rows_per_item
    h = hbm.at[pl.ds(0, n)]
    b = buf.at[pl.ds(pl.multiple_of(buf_item * rows_per_item, rows_per_item), n)]
    (pltpu.make_async_copy(b, h, sem) if to_hbm else pltpu.make_async_copy(h, b, sem)).wait()


def _dispatch_kernel(pos_ref, fill_ref, ntiles_ref, x2r_ref, xs_hbm, buf, sem):
    i = pl.program_id(0)
    nt = pl.num_programs(0)
    td = DISPATCH_TILE
    rpi = ROW_TILES
    slot = i % 2
    base = slot * td

    @pl.when(i >= 2)
    def _():
        _items_wait(xs_hbm, buf, base, td, rpi, sem.at[slot], True)

    buf[pl.ds(pl.multiple_of(base * rpi, td * rpi), td * rpi), :] = x2r_ref[...]

    def send(r2, carry):
        for k in range(DMA_QUEUES):
            r = r2 * DMA_QUEUES + k
            _item_copy(xs_hbm, pos_ref[i * td + r], buf, base + r, rpi, sem.at[slot],
                       True).start(priority=k)
        return carry

    lax.fori_loop(0, td // DMA_QUEUES, send, 0, unroll=4)

    @pl.when(i == nt - 1)
    def _():
        _items_wait(xs_hbm, buf, base, td, rpi, sem.at[slot], True)
        _items_wait(xs_hbm, buf, (1 - slot) * td, td, rpi, sem.at[1 - slot], True)
        buf[...] = jnp.zeros_like(buf)
        zsem = sem.at[2]
        for b in range(N_BUCKETS):
            def fill(p, carry):
                _item_copy(xs_hbm, p, buf, 0, rpi, zsem, True).start()
                return carry

            def drain(p, carry):
                _item_copy(xs_hbm, p, buf, 0, rpi, zsem, True).wait()
                return carry

            lax.fori_loop(fill_ref[b], fill_ref[N_BUCKETS + b], fill, 0)
            lax.fori_loop(fill_ref[b], fill_ref[N_BUCKETS + b], drain, 0)

        n_tiles_total = xs_hbm.shape[0] // (MOE_TILE * rpi)

        def fill_tile(tile, carry):
            cp = pltpu.make_async_copy(
                buf.at[pl.ds(0, MOE_TILE * rpi)],
                xs_hbm.at[pl.ds(pl.multiple_of(tile * MOE_TILE * rpi, MOE_TILE * rpi),
                                MOE_TILE * rpi)], zsem)
            cp.start()
            cp.wait()
            return carry

        lax.fori_loop(ntiles_ref[0], n_tiles_total, fill_tile, 0)


def _dispatch(pos, fill, ntiles, x2r, n_sorted):
    t = x2r.shape[0] // ROW_TILES
    td = DISPATCH_TILE
    grid_spec = pltpu.PrefetchScalarGridSpec(
        num_scalar_prefetch=3,
        grid=(t // td,),
        in_specs=[pl.BlockSpec((td * ROW_TILES, LANES), lambda i, *_: (i, 0))],
        out_specs=pl.BlockSpec(memory_space=pl.ANY),
        scratch_shapes=[
            pltpu.VMEM((2 * td * ROW_TILES, LANES), F32),
            pltpu.SemaphoreType.DMA((3,)),
        ],
    )
    return pl.pallas_call(
        _dispatch_kernel,
        grid_spec=grid_spec,
        out_shape=jax.ShapeDtypeStruct((n_sorted * ROW_TILES, LANES), F32),
        compiler_params=pltpu.CompilerParams(
            dimension_semantics=("arbitrary",), vmem_limit_bytes=VMEM_LIMIT),
        name="moe_dispatch",
    )(pos, fill, ntiles, x2r)


def _silu(x):
    return x * _sigmoid(x)


def _moe_kernel(elo_ref, ehi_ref, used_ref,
                xs_ref, gf_ref, wr_ref, br_ref, gfin_ref,
                wg_lo, wu_lo, wd_lo, wg_hi, wu_hi, wd_hi, y_ref):
    i = pl.program_id(0)
    tm = MOE_TILE

    @pl.when(used_ref[i] == 1)
    def _():
        x2 = jnp.concatenate(
            [xs_ref[pl.ds(s, tm, stride=ROW_TILES), :] for s in range(ROW_TILES)], axis=1)
        h3 = _rms(x2, gf_ref[...])
        g_lo, g_hi = _gates(_router_logits(h3, wr_ref, br_ref), elo_ref[i], ehi_ref[i])
        x = h3.astype(BF16)

        def expert(wg, wu, wd):
            hid = (_silu(_dot(x, wg[0])) * _dot(x, wu[0])).astype(BF16)
            return _dot(hid, wd[0])

        moe = g_lo * expert(wg_lo, wu_lo, wd_lo) + g_hi * expert(wg_hi, wu_hi, wd_hi)
        out = _rms(x2 + moe, gfin_ref[...])
        for s in range(ROW_TILES):
            y_ref[pl.ds(s, tm, stride=ROW_TILES), :] = out[:, s * LANES:(s + 1) * LANES]

    @pl.when(used_ref[i] == 0)
    def _():
        y_ref[...] = jnp.zeros_like(y_ref)


def _moe_sparse(e_lo, e_hi, used, xs, ffn_norm, w_router, b_router, final_norm,
                w_gate, w_up, w_down):
    tm = MOE_TILE
    nt = used.shape[0]
    d = D_MODEL
    lo = lambda i, elo, ehi, used: (elo[i], 0, 0)
    hi = lambda i, elo, ehi, used: (ehi[i], 0, 0)
    const = lambda i, *_: (0, 0)
    grid_spec = pltpu.PrefetchScalarGridSpec(
        num_scalar_prefetch=3,
        grid=(nt,),
        in_specs=[
            pl.BlockSpec((tm * ROW_TILES, LANES), lambda i, *_: (i, 0)),
            pl.BlockSpec((1, d), const),
            pl.BlockSpec((d, 2 * LANES), const),
            pl.BlockSpec((1, LANES), const),
            pl.BlockSpec((1, d), const),
            pl.BlockSpec((1, d, D_EXPERT), lo),
            pl.BlockSpec((1, d, D_EXPERT), lo),
            pl.BlockSpec((1, D_EXPERT, d), lo),
            pl.BlockSpec((1, d, D_EXPERT), hi),
            pl.BlockSpec((1, d, D_EXPERT), hi),
            pl.BlockSpec((1, D_EXPERT, d), hi),
        ],
        out_specs=pl.BlockSpec((tm * ROW_TILES, LANES), lambda i, *_: (i, 0)),
    )
    return pl.pallas_call(
        _moe_kernel,
        grid_spec=grid_spec,
        out_shape=jax.ShapeDtypeStruct((nt * tm * ROW_TILES, LANES), F32),
        compiler_params=pltpu.CompilerParams(
            dimension_semantics=("arbitrary",), vmem_limit_bytes=VMEM_LIMIT),
        name="moe_sparse",
    )(e_lo, e_hi, used, xs, ffn_norm, w_router, b_router, final_norm,
      w_gate, w_up, w_down, w_gate, w_up, w_down)


def _final_kernel(pos_ref, y_hbm, o_ref, ybuf, sem):
    i = pl.program_id(0)
    nt = pl.num_programs(0)
    tf = FINAL_TILE

    def start(tile, slot):
        def fetch(r2, carry):
            for k in range(DMA_QUEUES):
                r = r2 * DMA_QUEUES + k
                _item_copy(y_hbm, pos_ref[tile * tf + r], ybuf, slot * tf + r, ROW_TILES,
                           sem.at[slot], False).start(priority=k)
            return carry

        lax.fori_loop(0, tf // DMA_QUEUES, fetch, 0, unroll=4)

    @pl.when(i == 0)
    def _():
        start(0, 0)

    @pl.when(i + 1 < nt)
    def _():
        start(i + 1, (i + 1) % 2)

    slot = i % 2
    _items_wait(y_hbm, ybuf, slot * tf, tf, ROW_TILES, sem.at[slot], False)
    row0 = pl.multiple_of(slot * tf * ROW_TILES, ROW_TILES)
    for s in range(ROW_TILES):
        o_ref[:, s * LANES:(s + 1) * LANES] = ybuf[pl.ds(row0 + s, tf, stride=ROW_TILES), :]


def _final(pos, y):
    t = pos.shape[0]
    d = D_MODEL
    tf = FINAL_TILE
    grid_spec = pltpu.PrefetchScalarGridSpec(
        num_scalar_prefetch=1,
        grid=(t // tf,),
        in_specs=[pl.BlockSpec(memory_space=pl.ANY)],
        out_specs=pl.BlockSpec((tf, d), lambda i, *_: (i, 0)),
        scratch_shapes=[
            pltpu.VMEM((2 * tf * ROW_TILES, LANES), F32),
            pltpu.SemaphoreType.DMA((2,)),
        ],
    )
    return pl.pallas_call(
        _final_kernel,
        grid_spec=grid_spec,
        out_shape=jax.ShapeDtypeStruct((t, d), F32),
        compiler_params=pltpu.CompilerParams(
            dimension_semantics=("arbitrary",), vmem_limit_bytes=VMEM_LIMIT),
        name="moe_unpermute",
    )(pos, y)


def _routing_tables(counts, bucket, rank):
    tm = MOE_TILE
    t = bucket.shape[0]
    nt = t // tm + N_BUCKETS
    cnt = counts[0, :N_BUCKETS].astype(jnp.int32)
    padded = ((cnt + tm - 1) // tm) * tm
    ends = jnp.cumsum(padded)
    starts = ends - padded
    pos = rank
    for b in range(N_BUCKETS):
        pos = pos + jnp.where(bucket == b, starts[b], 0)
    fill = jnp.concatenate([starts + cnt, ends])
    ntiles = ends[-1:] // tm
    tile_start = jnp.arange(nt, dtype=jnp.int32) * tm
    tile_bucket = jnp.sum((ends[None, :] <= tile_start[:, None]).astype(jnp.int32), axis=1)
    used = (tile_bucket < N_BUCKETS).astype(jnp.int32)
    tile_bucket = jnp.minimum(tile_bucket, N_BUCKETS - 1)
    pair = tile_bucket % N_PAIRS
    group0 = (tile_bucket // N_PAIRS) * EXPERTS_PER_GROUP
    e_lo = group0 + jnp.array([0, 0, 0, 1, 1, 2], jnp.int32)[pair]
    e_hi = group0 + jnp.array([1, 2, 3, 2, 3, 3], jnp.int32)[pair]
    return pos, fill, ntiles, e_lo, e_hi, used, nt


def kernel(x, mem, mix_norm, w_in, conv_w, hgrn_lb, hgrn_norm, w_out, xattn_norm, mem_norm,
           w_q, w_kv, w_o, ffn_norm, w_group, b_group, w_expert, b_expert, w_gate, w_up,
           w_down, final_norm):
    bsz, seq, d = x.shape
    assert d == D_MODEL and seq % SEQ_TILE == 0 and seq % ATTN_TILE == 0
    assert (bsz * seq) % MOE_TILE == 0 and (bsz * seq) % FINAL_TILE == 0
    assert (bsz * seq) % DISPATCH_TILE == 0 and bsz * seq >= 2 * DISPATCH_TILE
    assert 2 * DISPATCH_TILE >= MOE_TILE and mix_norm.shape[0] == 1
    bf = lambda w: w.astype(BF16)

    x1 = _mixer(x, mix_norm, bf(w_in[0]), conv_w[0], hgrn_lb, hgrn_norm, bf(w_out[0]))
    kv = _kv_proj(mem, mem_norm, bf(w_kv[0]))

    pad = LANES - N_GROUPS - N_EXPERTS
    w_router = jnp.concatenate(
        [w_group[0], w_expert[0], jnp.zeros((d, pad), F32)], axis=1)
    b_router = jnp.concatenate(
        [b_group[0], b_expert[0], jnp.zeros((pad,), F32)])[None, :]
    w_router_hi = bf(w_router)
    w_router_lo = bf(w_router - w_router_hi.astype(F32))
    w_router2 = jnp.concatenate([w_router_hi, w_router_lo], axis=1)
    w_q_scaled = bf(w_q[0] * (XATTN_HEAD_DIM ** -0.5))
    x2r, brk, counts = _attention(x1, kv, xattn_norm, w_q_scaled, bf(w_o[0]), ffn_norm,
                                  w_router2, b_router)

    pos, fill, ntiles, e_lo, e_hi, used, nt = _routing_tables(counts, brk[0], brk[1])
    xs = _dispatch(pos, fill, ntiles, x2r, nt * MOE_TILE)
    y = _moe_sparse(e_lo, e_hi, used, xs, ffn_norm, w_router2, b_router, final_norm[None, :],
                    bf(w_gate[0]), bf(w_up[0]), bf(w_down[0]))
    return _final(pos, y).reshape(bsz, seq, d)
```

```python
import jax
import jax.numpy as jnp
from jax import lax
from jax.experimental import pallas as pl
from jax.experimental.pallas import tpu as pltpu

F32 = jnp.float32
BF16 = jnp.bfloat16

D_MODEL = 1024
CONV_WIDTH = 512
HGRN_WIDTH = 512
HGRN_HEADS = 4
HEAD_DIM = 128
N_PROJ_SLOTS = 7
PROJ_WIDTH = N_PROJ_SLOTS * 512
XATTN_HEADS = 4
XATTN_HEAD_DIM = 256
N_GROUPS = 4
EXPERTS_PER_GROUP = 4
N_EXPERTS = 16
D_EXPERT = 512
EPS = 1e-6

LANES = 128
SUBLANES = 8
CHUNK = 64
CHUNK_LEVELS = 6
HGRN_TILE = 4 * CHUNK
SEQ_TILE = 512
ATTN_TILE = 1024
RANK_BLOCK = 256
DISPATCH_TILE = 1024
MOE_TILE = 256
FINAL_TILE = 1024
N_PAIRS = 6
N_BUCKETS = N_GROUPS * N_PAIRS
ROW_TILES = D_MODEL // LANES
DMA_QUEUES = 2
VMEM_LIMIT = 56 * 1024 * 1024


def _rms(x, g):
    return x * lax.rsqrt(jnp.mean(x * x, axis=-1, keepdims=True) + EPS) * g


def _dot(a, b):
    return jnp.dot(a, b, preferred_element_type=F32)


def _dot_nt(a, b):
    return lax.dot_general(a, b, (((1,), (1,)), ((), ())), preferred_element_type=F32)


def _dot_tn(a, b):
    return lax.dot_general(a, b, (((0,), (0,)), ((), ())), preferred_element_type=F32)


def _roll_rows(x, shift):
    return pltpu.roll(x, shift % x.shape[0], axis=0)


def _level_exponents(logf2, use_level):
    n = CHUNK // SUBLANES
    sub = lax.broadcasted_iota(jnp.int32, (SUBLANES, logf2.shape[1]), 0)
    roll = lambda x, s: pltpu.roll(x, s % SUBLANES, axis=0)
    r = [logf2[SUBLANES * j:SUBLANES * (j + 1), :] for j in range(n)]
    for lvl in range(1, CHUNK_LEVELS + 1):
        half = 1 << (lvl - 1)
        g = [None] * n
        if half < SUBLANES:
            second = (sub & half) != 0
            for j in range(n):
                last = r[j]
                w = 1
                while w < half:
                    last = jnp.where((sub & w) != 0, last, roll(last, -w))
                    w *= 2
                tot = jnp.where(second, roll(last, half), last)
                g[j] = jnp.where(second, r[j], tot - r[j])
                r[j] = jnp.where(second, r[j] + tot, r[j])
        else:
            hv = half // SUBLANES
            for j0 in range(0, n, 2 * hv):
                mid = r[j0 + hv - 1]
                tot = jnp.broadcast_to(mid[SUBLANES - 1:SUBLANES, :], mid.shape)
                for j in range(j0, j0 + hv):
                    g[j] = tot - r[j]
                for j in range(j0 + hv, j0 + 2 * hv):
                    g[j] = r[j]
                    r[j] = r[j] + tot
        use_level(lvl, jnp.concatenate(g, axis=0))
    return jnp.concatenate(r, axis=0)


def _sigmoid(x):
    return 0.5 * jnp.tanh(0.5 * x) + 0.5


def _split_levels():
    import numpy as np
    t = np.arange(CHUNK)[:, None]
    s = np.arange(CHUNK)[None, :]
    msb = np.floor(np.log2(np.maximum(t ^ s, 1))).astype(np.int32) + 1
    return np.where(s < t, msb, np.where(s == t, 0, -1)).astype(np.int32)


def _hgrn_chunk(q, z, v, lb, levels):
    half_th = 0.5 * jnp.tanh(0.5 * z)
    one_m_lb = 1.0 - lb
    logf2 = jnp.log2(lb + one_m_lb * (0.5 + half_th))
    k = one_m_lb * (0.5 - half_th)

    qb = q.astype(BF16)
    kb = k.astype(BF16)
    scores = [jnp.where(levels == 0, jnp.sum(q * k, axis=-1, keepdims=True), 0.0)]

    def use_level(lvl, g):
        decay = jnp.exp2(g).astype(BF16)
        scores[0] = jnp.where(levels == lvl, _dot_nt(qb * decay, kb * decay), scores[0])

    b2 = _level_exponents(logf2, use_level)
    a = scores[0]
    b2_last = b2[CHUNK - 1:CHUNK, :]

    qe = (q * jnp.exp2(b2)).astype(BF16)
    kd = (k * jnp.exp2(b2_last - b2)).astype(BF16)
    return a, qe, kd, v.astype(BF16), b2_last


def _hgrn_tile(chunks, st):
    (a0, qe0, kd0, v0, bl0), (a1, qe1, kd1, v1, bl1), (a2, qe2, kd2, v2, bl2), \
        (a3, qe3, kd3, v3, bl3) = chunks
    scale = lambda x, log2_decay: x * jnp.exp2(log2_decay).astype(BF16)
    bf = lambda x: x.astype(BF16)
    cum1 = bl0
    cum2 = cum1 + bl1
    cum3 = cum2 + bl2
    cum4 = cum3 + bl3
    s10 = bf(_dot_nt(qe1, kd0))
    s32 = bf(_dot_nt(qe3, kd2))
    s8 = bf(_dot_nt(jnp.concatenate([qe2, scale(qe3, bl2)], axis=0),
                    jnp.concatenate([scale(kd0, bl1), kd1], axis=0)))
    v01 = jnp.concatenate([v0, v1], axis=0)
    stb = bf(st)
    o0 = _dot(bf(a0), v0) + _dot_nt(qe0, stb)
    o1 = _dot(bf(a1), v1) + _dot(s10, v0) + _dot_nt(scale(qe1, cum1), stb)
    o2 = _dot(bf(a2), v2) + _dot(s8[:CHUNK], v01) + _dot_nt(scale(qe2, cum2), stb)
    o3 = (_dot(bf(a3), v3) + _dot(s32, v2) + _dot(s8[CHUNK:], v01)
          + _dot_nt(scale(qe3, cum3), stb))
    kd_all = jnp.concatenate(
        [scale(kd0, cum4 - cum1), scale(kd1, cum4 - cum2), scale(kd2, bl3), kd3], axis=0)
    v_all = jnp.concatenate([v0, v1, v2, v3], axis=0)
    st_new = st * jnp.exp2(cum4) + _dot_tn(v_all, kd_all)
    return [o0, o1, o2, o3], st_new


def _mixer_kernel(x_ref, gmix_ref, win_ref, convw_ref, lbraw_ref, hnorm_ref, wout_ref, lvl_ref,
                  o_ref, p_scr, y_scr, st_scr, tail_scr):
    j = pl.program_id(1)

    @pl.when(j == 0)
    def _():
        st_scr[...] = jnp.zeros_like(st_scr)
        tail_scr[...] = jnp.zeros_like(tail_scr)

    x = x_ref[0]
    h = _rms(x, gmix_ref[...]).astype(BF16)
    p_scr[...] = _dot(h, win_ref[...])

    ts = x.shape[0]
    cb = p_scr[:, 0:CONV_WIDTH]
    u = p_scr[:, CONV_WIDTH:2 * CONV_WIDTH] * p_scr[:, 2 * CONV_WIDTH:3 * CONV_WIDTH]
    row = lax.broadcasted_iota(jnp.int32, u.shape, 0)
    prev1 = tail_scr[7:8, :]
    prev2 = tail_scr[6:7, :]
    u1 = jnp.where(row == 0, prev1, _roll_rows(u, 1))
    u2 = jnp.where(row == 0, prev2, jnp.where(row == 1, prev1, _roll_rows(u, 2)))
    cw = convw_ref[...]
    conv = u2 * cw[0:1, :] + u1 * cw[1:2, :] + u * cw[2:3, :]
    y_scr[:, 0:CONV_WIDTH] = (cb * conv).astype(BF16)
    tail_scr[...] = u[ts - 8:ts, :]

    raw = lbraw_ref[...]
    mx = jnp.max(raw, axis=0, keepdims=True)
    ex = jnp.exp(raw - mx)
    lb_all = ex[0:1, :] / jnp.sum(ex, axis=0, keepdims=True)
    hn = hnorm_ref[...]

    for hd in range(HGRN_HEADS):
        lo = hd * HEAD_DIM
        sl = slice(lo, lo + HEAD_DIM)
        col = lambda slot: slice(slot * 512 + lo, slot * 512 + lo + HEAD_DIM)
        for t0 in range(0, ts, HGRN_TILE):
            chunks = []
            for r0 in range(t0, t0 + HGRN_TILE, CHUNK):
                rows = slice(r0, r0 + CHUNK)
                chunks.append(_hgrn_chunk(p_scr[rows, col(3)], p_scr[rows, col(4)],
                                          p_scr[rows, col(5)], lb_all[:, sl], lvl_ref[...]))
            outs, st_new = _hgrn_tile(chunks, st_scr[hd])
            st_scr[hd] = st_new
            for c, o in enumerate(outs):
                rows = slice(t0 + c * CHUNK, t0 + (c + 1) * CHUNK)
                g = p_scr[rows, col(6)]
                o = o * lax.rsqrt(jnp.mean(o * o, axis=-1, keepdims=True) + EPS) * hn[:, sl]
                y_scr[rows, CONV_WIDTH + lo:CONV_WIDTH + lo + HEAD_DIM] = \
                    (o * (g * _sigmoid(g))).astype(BF16)

    o_ref[0] = x + _dot(y_scr[...], wout_ref[...])


def _mixer(x, mix_norm, w_in, conv_w, hgrn_lb, hgrn_norm, w_out):
    bsz, seq, d = x.shape
    ts = SEQ_TILE
    const = lambda b, j: (0, 0)
    return pl.pallas_call(
        _mixer_kernel,
        grid=(bsz, seq // ts),
        in_specs=[
            pl.BlockSpec((1, ts, d), lambda b, j: (b, j, 0)),
            pl.BlockSpec((1, d), const),
            pl.BlockSpec((d, PROJ_WIDTH), const),
            pl.BlockSpec((3, CONV_WIDTH), const),
            pl.BlockSpec((2, HGRN_WIDTH), const),
            pl.BlockSpec((1, HGRN_WIDTH), const),
            pl.BlockSpec((d, d), const),
            pl.BlockSpec((CHUNK, CHUNK), const),
        ],
        out_specs=pl.BlockSpec((1, ts, d), lambda b, j: (b, j, 0)),
        out_shape=jax.ShapeDtypeStruct((bsz, seq, d), F32),
        scratch_shapes=[
            pltpu.VMEM((ts, PROJ_WIDTH), F32),
            pltpu.VMEM((ts, d), BF16),
            pltpu.VMEM((HGRN_HEADS, HEAD_DIM, HEAD_DIM), F32),
            pltpu.VMEM((8, CONV_WIDTH), F32),
        ],
        compiler_params=pltpu.CompilerParams(
            dimension_semantics=("arbitrary", "arbitrary"), vmem_limit_bytes=VMEM_LIMIT),
        name="mixer",
    )(x, mix_norm, w_in, conv_w, hgrn_lb, hgrn_norm, w_out, jnp.asarray(_split_levels()))


def _kv_kernel(m_ref, g_ref, w_ref, o_ref):
    h = _rms(m_ref[0], g_ref[...]).astype(BF16)
    o_ref[0] = _dot(h, w_ref[...]).astype(BF16)


def _kv_proj(mem, mem_norm, w_kv):
    bsz, mlen, d = mem.shape
    const = lambda b: (0, 0)
    return pl.pallas_call(
        _kv_kernel,
        grid=(bsz,),
        in_specs=[
            pl.BlockSpec((1, mlen, d), lambda b: (b, 0, 0)),
            pl.BlockSpec((1, d), const),
            pl.BlockSpec((d, 2 * d), const),
        ],
        out_specs=pl.BlockSpec((1, mlen, 2 * d), lambda b: (b, 0, 0)),
        out_shape=jax.ShapeDtypeStruct((bsz, mlen, 2 * d), BF16),
        compiler_params=pltpu.CompilerParams(
            dimension_semantics=("arbitrary",), vmem_limit_bytes=VMEM_LIMIT),
        name="kv_proj",
    )(mem, mem_norm, w_kv)


def _first_argmax(vals, lane):
    mx = jnp.max(vals, axis=-1, keepdims=True)
    idx = jnp.min(jnp.where(vals == mx, lane, float(LANES)), axis=-1, keepdims=True)
    return mx, idx


def _router_logits(h3, wr_ref, br_ref):
    h3_hi = h3.astype(BF16)
    h3_lo = (h3 - h3_hi.astype(F32)).astype(BF16)
    two = _dot(h3_hi, wr_ref[...])
    return (two[:, :LANES] + two[:, LANES:]) + _dot(h3_lo, wr_ref[:, :LANES]) + br_ref[...]


def _route(logits, running, tri):
    lane = lax.broadcasted_iota(jnp.int32, logits.shape, 1).astype(F32)
    neg = jnp.float32(-jnp.inf)
    gl = jnp.where(lane < N_GROUPS, logits, neg)
    _, gidx = _first_argmax(gl, lane)
    base = N_GROUPS + EXPERTS_PER_GROUP * gidx
    el = jnp.where((lane >= base) & (lane < base + EXPERTS_PER_GROUP), logits, neg)
    _, i1 = _first_argmax(el, lane)
    _, i2 = _first_argmax(jnp.where(lane == i1, neg, el), lane)
    lo = jnp.minimum(i1, i2) - base
    hi = jnp.maximum(i1, i2) - base
    bucket = gidx * N_PAIRS + (lo * (7.0 - lo)) * 0.5 + hi - lo - 1.0

    onehot = lane == bucket
    ranks = []
    for r0 in range(0, logits.shape[0], RANK_BLOCK):
        oh = onehot[r0:r0 + RANK_BLOCK]
        before = _dot(tri, oh.astype(BF16))
        ranks.append(jnp.sum(jnp.where(oh, before + running, 0.0), axis=-1, keepdims=True))
        running = running + jnp.sum(oh.astype(F32), axis=0, keepdims=True)
    rank = jnp.concatenate(ranks, axis=0)
    info = jnp.where(lane == 0, bucket, jnp.where(lane == 1, rank, 0.0))
    return info, running


def _gates(logits, e_lo, e_hi):
    lane = lax.broadcasted_iota(jnp.int32, logits.shape, 1)
    neg = jnp.float32(-jnp.inf)
    gl = jnp.where(lane < N_GROUPS, logits, neg)
    g_p = 1.0 / jnp.sum(jnp.exp(gl - jnp.max(gl, axis=-1, keepdims=True)), axis=-1, keepdims=True)
    l_lo = jnp.sum(jnp.where(lane == N_GROUPS + e_lo, logits, 0.0), axis=-1, keepdims=True)
    l_hi = jnp.sum(jnp.where(lane == N_GROUPS + e_hi, logits, 0.0), axis=-1, keepdims=True)
    m = jnp.maximum(l_lo, l_hi)
    p_lo = jnp.exp(l_lo - m)
    p_hi = jnp.exp(l_hi - m)
    inv = g_p / (p_lo + p_hi)
    return p_lo * inv, p_hi * inv


def _attn_kernel(x_ref, kv_ref, gx_ref, wq_ref, wo_ref, gf_ref, wr_ref, br_ref, tri_ref,
                 x2r_ref, brk_ref, counts_ref, o_scr, cnt_scr):
    @pl.when((pl.program_id(0) == 0) & (pl.program_id(1) == 0))
    def _():
        cnt_scr[...] = jnp.zeros_like(cnt_scr)

    x = x_ref[0]
    tq = x.shape[0]
    h = _rms(x, gx_ref[...]).astype(BF16)
    q = _dot(h, wq_ref[...])
    for hd in range(XATTN_HEADS):
        lo = hd * XATTN_HEAD_DIM
        qh = q[:, lo:lo + XATTN_HEAD_DIM].astype(BF16)
        kh = kv_ref[0, :, lo:lo + XATTN_HEAD_DIM]
        vh = kv_ref[0, :, D_MODEL + lo:D_MODEL + lo + XATTN_HEAD_DIM]
        s = _dot_nt(qh, kh)
        p = jnp.exp(s - jnp.max(s, axis=-1, keepdims=True))
        inv = 1.0 / jnp.sum(p, axis=-1, keepdims=True)
        o_scr[:, lo:lo + XATTN_HEAD_DIM] = (_dot(p.astype(BF16), vh) * inv).astype(BF16)
    x2 = x + _dot(o_scr[...], wo_ref[...])
    for s in range(ROW_TILES):
        x2r_ref[pl.ds(s, tq, stride=ROW_TILES), :] = x2[:, s * LANES:(s + 1) * LANES]
    logits = _router_logits(_rms(x2, gf_ref[...]), wr_ref, br_ref)
    info, running = _route(logits, cnt_scr[...], tri_ref[...])
    cnt_scr[...] = running
    counts_ref[...] = running
    brk_ref[...] = info.T[0:8, :].astype(jnp.int32)


def _attention(x1, kv, xattn_norm, w_q, w_o, ffn_norm, w_router, b_router):
    bsz, seq, d = x1.shape
    mlen = kv.shape[1]
    tq = ATTN_TILE
    nj = seq // tq
    const = lambda b, j: (0, 0)
    tile = lambda b, j: (b, j, 0)
    tri = jnp.tri(RANK_BLOCK, RANK_BLOCK, -1, dtype=BF16)
    return pl.pallas_call(
        _attn_kernel,
        grid=(bsz, nj),
        in_specs=[
            pl.BlockSpec((1, tq, d), tile),
            pl.BlockSpec((1, mlen, 2 * d), lambda b, j: (b, 0, 0)),
            pl.BlockSpec((1, d), const),
            pl.BlockSpec((d, d), const),
            pl.BlockSpec((d, d), const),
            pl.BlockSpec((1, d), const),
            pl.BlockSpec((d, 2 * LANES), const),
            pl.BlockSpec((1, LANES), const),
            pl.BlockSpec((RANK_BLOCK, RANK_BLOCK), const),
        ],
        out_specs=[
            pl.BlockSpec((tq * ROW_TILES, LANES), lambda b, j: (b * nj + j, 0)),
            pl.BlockSpec((8, tq), lambda b, j: (0, b * nj + j)),
            pl.BlockSpec((1, LANES), const),
        ],
        out_shape=[
            jax.ShapeDtypeStruct((bsz * seq * ROW_TILES, LANES), F32),
            jax.ShapeDtypeStruct((8, bsz * seq), jnp.int32),
            jax.ShapeDtypeStruct((1, LANES), F32),
        ],
        scratch_shapes=[pltpu.VMEM((tq, d), BF16), pltpu.VMEM((1, LANES), F32)],
        compiler_params=pltpu.CompilerParams(
            dimension_semantics=("arbitrary", "arbitrary"), vmem_limit_bytes=VMEM_LIMIT),
        name="xattn_router",
    )(x1, kv, xattn_norm, w_q, w_o, ffn_norm, w_router, b_router, tri)


def _item_copy(hbm, hbm_item, buf, buf_item, rows_per_item, sem, to_hbm):
    h = hbm.at[pl.ds(pl.multiple_of(hbm_item * rows_per_item, rows_per_item), rows_per_item)]
    b = buf.at[pl.ds(pl.multiple_of(buf_item * rows_per_item, rows_per_item), rows_per_item)]
    return pltpu.make_async_copy(b, h, sem) if to_hbm else pltpu.make_async_copy(h, b, sem)


def _items_wait(hbm, buf, buf_item, n_items, rows_per_item, sem, to_hbm):
    n = n_items * rows_per_item
    h = hbm.at[pl.ds(0, n)]
    b = buf.at[pl.ds(pl.multiple_of(buf_item * rows_per_item, rows_per_item), n)]
    (pltpu.make_async_copy(b, h, sem) if to_hbm else pltpu.make_async_copy(h, b, sem)).wait()


def _dispatch_kernel(pos_ref, fill_ref, ntiles_ref, x2r_ref, xs_hbm, buf, sem):
    i = pl.program_id(0)
    nt = pl.num_programs(0)
    td = DISPATCH_TILE
    rpi = ROW_TILES
    slot = i % 2
    base = slot * td

    @pl.when(i >= 2)
    def _():
        _items_wait(xs_hbm, buf, base, td, rpi, sem.at[slot], True)

    buf[pl.ds(pl.multiple_of(base * rpi, td * rpi), td * rpi), :] = x2r_ref[...]

    def send(r2, carry):
        for k in range(DMA_QUEUES):
            r = r2 * DMA_QUEUES + k
            _item_copy(xs_hbm, pos_ref[i * td + r], buf, base + r, rpi, sem.at[slot],
                       True).start(priority=k)
        return carry

    lax.fori_loop(0, td // DMA_QUEUES, send, 0, unroll=4)

    @pl.when(i == nt - 1)
    def _():
        _items_wait(xs_hbm, buf, base, td, rpi, sem.at[slot], True)
        _items_wait(xs_hbm, buf, (1 - slot) * td, td, rpi, sem.at[1 - slot], True)
        buf[...] = jnp.zeros_like(buf)
        zsem = sem.at[2]
        for b in range(N_BUCKETS):
            def fill(p, carry):
                _item_copy(xs_hbm, p, buf, 0, rpi, zsem, True).start()
                return carry

            def drain(p, carry):
                _item_copy(xs_hbm, p, buf, 0, rpi, zsem, True).wait()
                return carry

            lax.fori_loop(fill_ref[b], fill_ref[N_BUCKETS + b], fill, 0)
            lax.fori_loop(fill_ref[b], fill_ref[N_BUCKETS + b], drain, 0)

        n_tiles_total = xs_hbm.shape[0] // (MOE_TILE * rpi)

        def fill_tile(tile, carry):
            cp = pltpu.make_async_copy(
                buf.at[pl.ds(0, MOE_TILE * rpi)],
                xs_hbm.at[pl.ds(pl.multiple_of(tile * MOE_TILE * rpi, MOE_TILE * rpi),
                                MOE_TILE * rpi)], zsem)
            cp.start()
            cp.wait()
            return carry

        lax.fori_loop(ntiles_ref[0], n_tiles_total, fill_tile, 0)


def _dispatch(pos, fill, ntiles, x2r, n_sorted):
    t = x2r.shape[0] // ROW_TILES
    td = DISPATCH_TILE
    grid_spec = pltpu.PrefetchScalarGridSpec(
        num_scalar_prefetch=3,
        grid=(t // td,),
        in_specs=[pl.BlockSpec((td * ROW_TILES, LANES), lambda i, *_: (i, 0))],
        out_specs=pl.BlockSpec(memory_space=pl.ANY),
        scratch_shapes=[
            pltpu.VMEM((2 * td * ROW_TILES, LANES), F32),
            pltpu.SemaphoreType.DMA((3,)),
        ],
    )
    return pl.pallas_call(
        _dispatch_kernel,
        grid_spec=grid_spec,
        out_shape=jax.ShapeDtypeStruct((n_sorted * ROW_TILES, LANES), F32),
        compiler_params=pltpu.CompilerParams(
            dimension_semantics=("arbitrary",), vmem_limit_bytes=VMEM_LIMIT),
        name="moe_dispatch",
    )(pos, fill, ntiles, x2r)


def _silu(x):
    return x * _sigmoid(x)


def _moe_kernel(elo_ref, ehi_ref, used_ref,
                xs_ref, gf_ref, wr_ref, br_ref, gfin_ref,
                wg_lo, wu_lo, wd_lo, wg_hi, wu_hi, wd_hi, y_ref):
    i = pl.program_id(0)
    tm = MOE_TILE

    @pl.when(used_ref[i] == 1)
    def _():
        x2 = jnp.concatenate(
            [xs_ref[pl.ds(s, tm, stride=ROW_TILES), :] for s in range(ROW_TILES)], axis=1)
        h3 = _rms(x2, gf_ref[...])
        g_lo, g_hi = _gates(_router_logits(h3, wr_ref, br_ref), elo_ref[i], ehi_ref[i])
        x = h3.astype(BF16)

        def expert(wg, wu, wd):
            hid = (_silu(_dot(x, wg[0])) * _dot(x, wu[0])).astype(BF16)
            return _dot(hid, wd[0])

        moe = g_lo * expert(wg_lo, wu_lo, wd_lo) + g_hi * expert(wg_hi, wu_hi, wd_hi)
        out = _rms(x2 + moe, gfin_ref[...])
        for s in range(ROW_TILES):
            y_ref[pl.ds(s, tm, stride=ROW_TILES), :] = out[:, s * LANES:(s + 1) * LANES]

    @pl.when(used_ref[i] == 0)
    def _():
        y_ref[...] = jnp.zeros_like(y_ref)


def _moe_sparse(e_lo, e_hi, used, xs, ffn_norm, w_router, b_router, final_norm,
                w_gate, w_up, w_down):
    tm = MOE_TILE
    nt = used.shape[0]
    d = D_MODEL
    lo = lambda i, elo, ehi, used: (elo[i], 0, 0)
    hi = lambda i, elo, ehi, used: (ehi[i], 0, 0)
    const = lambda i, *_: (0, 0)
    grid_spec = pltpu.PrefetchScalarGridSpec(
        num_scalar_prefetch=3,
        grid=(nt,),
        in_specs=[
            pl.BlockSpec((tm * ROW_TILES, LANES), lambda i, *_: (i, 0)),
            pl.BlockSpec((1, d), const),
            pl.BlockSpec((d, 2 * LANES), const),
            pl.BlockSpec((1, LANES), const),
            pl.BlockSpec((1, d), const),
            pl.BlockSpec((1, d, D_EXPERT), lo),
            pl.BlockSpec((1, d, D_EXPERT), lo),
            pl.BlockSpec((1, D_EXPERT, d), lo),
            pl.BlockSpec((1, d, D_EXPERT), hi),
            pl.BlockSpec((1, d, D_EXPERT), hi),
            pl.BlockSpec((1, D_EXPERT, d), hi),
        ],
        out_specs=pl.BlockSpec((tm * ROW_TILES, LANES), lambda i, *_: (i, 0)),
    )
    return pl.pallas_call(
        _moe_kernel,
        grid_spec=grid_spec,
        out_shape=jax.ShapeDtypeStruct((nt * tm * ROW_TILES, LANES), F32),
        compiler_params=pltpu.CompilerParams(
            dimension_semantics=("arbitrary",), vmem_limit_bytes=VMEM_LIMIT),
        name="moe_sparse",
    )(e_lo, e_hi, used, xs, ffn_norm, w_router, b_router, final_norm,
      w_gate, w_up, w_down, w_gate, w_up, w_down)


def _final_kernel(pos_ref, y_hbm, o_ref, ybuf, sem):
    i = pl.program_id(0)
    nt = pl.num_programs(0)
    tf = FINAL_TILE

    def start(tile, slot):
        def fetch(r2, carry):
            for k in range(DMA_QUEUES):
                r = r2 * DMA_QUEUES + k
                _item_copy(y_hbm, pos_ref[tile * tf + r], ybuf, slot * tf + r, ROW_TILES,
                           sem.at[slot], False).start(priority=k)
            return carry

        lax.fori_loop(0, tf // DMA_QUEUES, fetch, 0, unroll=4)

    @pl.when(i == 0)
    def _():
        start(0, 0)

    @pl.when(i + 1 < nt)
    def _():
        start(i + 1, (i + 1) % 2)

    slot = i % 2
    _items_wait(y_hbm, ybuf, slot * tf, tf, ROW_TILES, sem.at[slot], False)
    row0 = pl.multiple_of(slot * tf * ROW_TILES, ROW_TILES)
    for s in range(ROW_TILES):
        o_ref[:, s * LANES:(s + 1) * LANES] = ybuf[pl.ds(row0 + s, tf, stride=ROW_TILES), :]


def _final(pos, y):
    t = pos.shape[0]
    d = D_MODEL
    tf = FINAL_TILE
    grid_spec = pltpu.PrefetchScalarGridSpec(
        num_scalar_prefetch=1,
        grid=(t // tf,),
        in_specs=[pl.BlockSpec(memory_space=pl.ANY)],
        out_specs=pl.BlockSpec((tf, d), lambda i, *_: (i, 0)),
        scratch_shapes=[
            pltpu.VMEM((2 * tf * ROW_TILES, LANES), F32),
            pltpu.SemaphoreType.DMA((2,)),
        ],
    )
    return pl.pallas_call(
        _final_kernel,
        grid_spec=grid_spec,
        out_shape=jax.ShapeDtypeStruct((t, d), F32),
        compiler_params=pltpu.CompilerParams(
            dimension_semantics=("arbitrary",), vmem_limit_bytes=VMEM_LIMIT),
        name="moe_unpermute",
    )(pos, y)


def _positions_kernel(starts_ref, brk_ref, pos_ref):
    bucket = brk_ref[0:1, :]
    pos = brk_ref[1:2, :]
    for b in range(N_BUCKETS):
        pos = pos + jnp.where(bucket == b, starts_ref[b], 0)
    pos_ref[...] = pos


def _positions(starts, brk):
    t = brk.shape[1]
    grid_spec = pltpu.PrefetchScalarGridSpec(
        num_scalar_prefetch=1,
        grid=(1,),
        in_specs=[pl.BlockSpec(brk.shape, lambda i, starts: (0, 0))],
        out_specs=pl.BlockSpec((1, t), lambda i, starts: (0, 0)),
    )
    return pl.pallas_call(
        _positions_kernel,
        grid_spec=grid_spec,
        out_shape=jax.ShapeDtypeStruct((1, t), jnp.int32),
        name="moe_positions",
    )(starts, brk).reshape(t)


def _routing_tables(counts, brk):
    tm = MOE_TILE
    t = brk.shape[1]
    nt = t // tm + N_BUCKETS
    cnt = counts[0, :N_BUCKETS].astype(jnp.int32)
    padded = ((cnt + tm - 1) // tm) * tm
    ends = jnp.cumsum(padded)
    starts = ends - padded
    pos = _positions(starts, brk)
    fill = jnp.concatenate([starts + cnt, ends])
    ntiles = ends[-1:] // tm
    tile_start = jnp.arange(nt, dtype=jnp.int32) * tm
    tile_bucket = jnp.sum((ends[None, :] <= tile_start[:, None]).astype(jnp.int32), axis=1)
    used = (tile_bucket < N_BUCKETS).astype(jnp.int32)
    tile_bucket = jnp.minimum(tile_bucket, N_BUCKETS - 1)
    pair = tile_bucket % N_PAIRS
    group0 = (tile_bucket // N_PAIRS) * EXPERTS_PER_GROUP
    e_lo = group0 + jnp.array([0, 0, 0, 1, 1, 2], jnp.int32)[pair]
    e_hi = group0 + jnp.array([1, 2, 3, 2, 3, 3], jnp.int32)[pair]
    return pos, fill, ntiles, e_lo, e_hi, used, nt


def kernel(x, mem, mix_norm, w_in, conv_w, hgrn_lb, hgrn_norm, w_out, xattn_norm, mem_norm,
           w_q, w_kv, w_o, ffn_norm, w_group, b_group, w_expert, b_expert, w_gate, w_up,
           w_down, final_norm):
    bsz, seq, d = x.shape
    assert d == D_MODEL and seq % SEQ_TILE == 0 and seq % ATTN_TILE == 0
    assert (bsz * seq) % MOE_TILE == 0 and (bsz * seq) % FINAL_TILE == 0
    assert (bsz * seq) % DISPATCH_TILE == 0 and bsz * seq >= 2 * DISPATCH_TILE
    assert 2 * DISPATCH_TILE >= MOE_TILE and mix_norm.shape[0] == 1
    bf = lambda w: w.astype(BF16)

    x1 = _mixer(x, mix_norm, bf(w_in[0]), conv_w[0], hgrn_lb, hgrn_norm, bf(w_out[0]))
    kv = _kv_proj(mem, mem_norm, bf(w_kv[0]))

    pad = LANES - N_GROUPS - N_EXPERTS
    w_router = jnp.concatenate(
        [w_group[0], w_expert[0], jnp.zeros((d, pad), F32)], axis=1)
    b_router = jnp.concatenate(
        [b_group[0], b_expert[0], jnp.zeros((pad,), F32)])[None, :]
    w_router_hi = bf(w_router)
    w_router_lo = bf(w_router - w_router_hi.astype(F32))
    w_router2 = jnp.concatenate([w_router_hi, w_router_lo], axis=1)
    w_q_scaled = bf(w_q[0] * (XATTN_HEAD_DIM ** -0.5))
    x2r, brk, counts = _attention(x1, kv, xattn_norm, w_q_scaled, bf(w_o[0]), ffn_norm,
                                  w_router2, b_router)

    pos, fill, ntiles, e_lo, e_hi, used, nt = _routing_tables(counts, brk)
    xs = _dispatch(pos, fill, ntiles, x2r, nt * MOE_TILE)
    y = _moe_sparse(e_lo, e_hi, used, xs, ffn_norm, w_router2, b_router, final_norm[None, :],
                    bf(w_gate[0]), bf(w_up[0]), bf(w_down[0]))
    return _final(pos, y).reshape(bsz, seq, d)
```

```python
import jax
import jax.numpy as jnp
from jax import lax
from jax.experimental import pallas as pl
from jax.experimental.pallas import tpu as pltpu

F32 = jnp.float32
BF16 = jnp.bfloat16

D_MODEL = 1024
CONV_WIDTH = 512
HGRN_WIDTH = 512
HGRN_HEADS = 4
HEAD_DIM = 128
N_PROJ_SLOTS = 7
PROJ_WIDTH = N_PROJ_SLOTS * 512
XATTN_HEADS = 4
XATTN_HEAD_DIM = 256
N_GROUPS = 4
EXPERTS_PER_GROUP = 4
N_EXPERTS = 16
D_EXPERT = 512
EPS = 1e-6

LANES = 128
SUBLANES = 8
CHUNK = 64
CHUNK_LEVELS = 6
HGRN_TILE = 4 * CHUNK
SEQ_TILE = 512
ATTN_TILE = 1024
RANK_BLOCK = 256
DISPATCH_TILE = 1024
MOE_TILE = 512
FINAL_TILE = 1024
N_PAIRS = 6
N_BUCKETS = N_GROUPS * N_PAIRS
ROW_TILES = D_MODEL // LANES
DMA_QUEUES = 2
VMEM_LIMIT = 56 * 1024 * 1024


def _rms(x, g):
    return x * lax.rsqrt(jnp.mean(x * x, axis=-1, keepdims=True) + EPS) * g


def _dot(a, b):
    return jnp.dot(a, b, preferred_element_type=F32)


def _dot_nt(a, b):
    return lax.dot_general(a, b, (((1,), (1,)), ((), ())), preferred_element_type=F32)


def _dot_tn(a, b):
    return lax.dot_general(a, b, (((0,), (0,)), ((), ())), preferred_element_type=F32)


def _roll_rows(x, shift):
    return pltpu.roll(x, shift % x.shape[0], axis=0)


def _level_exponents(logf2, use_level):
    n = CHUNK // SUBLANES
    sub = lax.broadcasted_iota(jnp.int32, (SUBLANES, logf2.shape[1]), 0)
    roll = lambda x, s: pltpu.roll(x, s % SUBLANES, axis=0)
    r = [logf2[SUBLANES * j:SUBLANES * (j + 1), :] for j in range(n)]
    for lvl in range(1, CHUNK_LEVELS + 1):
        half = 1 << (lvl - 1)
        g = [None] * n
        if half < SUBLANES:
            second = (sub & half) != 0
            for j in range(n):
                last = r[j]
                w = 1
                while w < half:
                    last = jnp.where((sub & w) != 0, last, roll(last, -w))
                    w *= 2
                tot = jnp.where(second, roll(last, half), last)
                g[j] = jnp.where(second, r[j], tot - r[j])
                r[j] = jnp.where(second, r[j] + tot, r[j])
        else:
            hv = half // SUBLANES
            for j0 in range(0, n, 2 * hv):
                mid = r[j0 + hv - 1]
                tot = jnp.broadcast_to(mid[SUBLANES - 1:SUBLANES, :], mid.shape)
                for j in range(j0, j0 + hv):
                    g[j] = tot - r[j]
                for j in range(j0 + hv, j0 + 2 * hv):
                    g[j] = r[j]
                    r[j] = r[j] + tot
        use_level(lvl, jnp.concatenate(g, axis=0))
    return jnp.concatenate(r, axis=0)


def _sigmoid(x):
    return 0.5 * jnp.tanh(0.5 * x) + 0.5


def _split_levels():
    import numpy as np
    t = np.arange(CHUNK)[:, None]
    s = np.arange(CHUNK)[None, :]
    msb = np.floor(np.log2(np.maximum(t ^ s, 1))).astype(np.int32) + 1
    return np.where(s < t, msb, np.where(s == t, 0, -1)).astype(np.int32)


def _hgrn_chunk(q, z, v, lb, levels):
    half_th = 0.5 * jnp.tanh(0.5 * z)
    one_m_lb = 1.0 - lb
    logf2 = jnp.log2(lb + one_m_lb * (0.5 + half_th))
    k = one_m_lb * (0.5 - half_th)

    qb = q.astype(BF16)
    kb = k.astype(BF16)
    scores = [jnp.where(levels == 0, jnp.sum(q * k, axis=-1, keepdims=True), 0.0)]

    def use_level(lvl, g):
        decay = jnp.exp2(g).astype(BF16)
        scores[0] = jnp.where(levels == lvl, _dot_nt(qb * decay, kb * decay), scores[0])

    b2 = _level_exponents(logf2, use_level)
    a = scores[0]
    b2_last = b2[CHUNK - 1:CHUNK, :]

    qe = (q * jnp.exp2(b2)).astype(BF16)
    kd = (k * jnp.exp2(b2_last - b2)).astype(BF16)
    return a, qe, kd, v.astype(BF16), b2_last


def _hgrn_tile(chunks, st):
    (a0, qe0, kd0, v0, bl0), (a1, qe1, kd1, v1, bl1), (a2, qe2, kd2, v2, bl2), \
        (a3, qe3, kd3, v3, bl3) = chunks
    scale = lambda x, log2_decay: x * jnp.exp2(log2_decay).astype(BF16)
    bf = lambda x: x.astype(BF16)
    cum1 = bl0
    cum2 = cum1 + bl1
    cum3 = cum2 + bl2
    cum4 = cum3 + bl3
    s10 = bf(_dot_nt(qe1, kd0))
    s32 = bf(_dot_nt(qe3, kd2))
    s8 = bf(_dot_nt(jnp.concatenate([qe2, scale(qe3, bl2)], axis=0),
                    jnp.concatenate([scale(kd0, bl1), kd1], axis=0)))
    v01 = jnp.concatenate([v0, v1], axis=0)
    stb = bf(st)
    o0 = _dot(bf(a0), v0) + _dot_nt(qe0, stb)
    o1 = _dot(bf(a1), v1) + _dot(s10, v0) + _dot_nt(scale(qe1, cum1), stb)
    o2 = _dot(bf(a2), v2) + _dot(s8[:CHUNK], v01) + _dot_nt(scale(qe2, cum2), stb)
    o3 = (_dot(bf(a3), v3) + _dot(s32, v2) + _dot(s8[CHUNK:], v01)
          + _dot_nt(scale(qe3, cum3), stb))
    kd_all = jnp.concatenate(
        [scale(kd0, cum4 - cum1), scale(kd1, cum4 - cum2), scale(kd2, bl3), kd3], axis=0)
    v_all = jnp.concatenate([v0, v1, v2, v3], axis=0)
    st_new = st * jnp.exp2(cum4) + _dot_tn(v_all, kd_all)
    return [o0, o1, o2, o3], st_new


def _mixer_kernel(x_ref, gmix_ref, win_ref, convw_ref, lbraw_ref, hnorm_ref, wout_ref, lvl_ref,
                  o_ref, p_scr, y_scr, st_scr, tail_scr):
    j = pl.program_id(1)

    @pl.when(j == 0)
    def _():
        st_scr[...] = jnp.zeros_like(st_scr)
        tail_scr[...] = jnp.zeros_like(tail_scr)

    x = x_ref[0]
    h = _rms(x, gmix_ref[...]).astype(BF16)
    p_scr[...] = _dot(h, win_ref[...])

    ts = x.shape[0]
    cb = p_scr[:, 0:CONV_WIDTH]
    u = p_scr[:, CONV_WIDTH:2 * CONV_WIDTH] * p_scr[:, 2 * CONV_WIDTH:3 * CONV_WIDTH]
    row = lax.broadcasted_iota(jnp.int32, u.shape, 0)
    prev1 = tail_scr[7:8, :]
    prev2 = tail_scr[6:7, :]
    u1 = jnp.where(row == 0, prev1, _roll_rows(u, 1))
    u2 = jnp.where(row == 0, prev2, jnp.where(row == 1, prev1, _roll_rows(u, 2)))
    cw = convw_ref[...]
    conv = u2 * cw[0:1, :] + u1 * cw[1:2, :] + u * cw[2:3, :]
    y_scr[:, 0:CONV_WIDTH] = (cb * conv).astype(BF16)
    tail_scr[...] = u[ts - 8:ts, :]

    raw = lbraw_ref[...]
    mx = jnp.max(raw, axis=0, keepdims=True)
    ex = jnp.exp(raw - mx)
    lb_all = ex[0:1, :] / jnp.sum(ex, axis=0, keepdims=True)
    hn = hnorm_ref[...]

    for hd in range(HGRN_HEADS):
        lo = hd * HEAD_DIM
        sl = slice(lo, lo + HEAD_DIM)
        col = lambda slot: slice(slot * 512 + lo, slot * 512 + lo + HEAD_DIM)
        for t0 in range(0, ts, HGRN_TILE):
            chunks = []
            for r0 in range(t0, t0 + HGRN_TILE, CHUNK):
                rows = slice(r0, r0 + CHUNK)
                chunks.append(_hgrn_chunk(p_scr[rows, col(3)], p_scr[rows, col(4)],
                                          p_scr[rows, col(5)], lb_all[:, sl], lvl_ref[...]))
            outs, st_new = _hgrn_tile(chunks, st_scr[hd])
            st_scr[hd] = st_new
            for c, o in enumerate(outs):
                rows = slice(t0 + c * CHUNK, t0 + (c + 1) * CHUNK)
                g = p_scr[rows, col(6)]
                o = o * lax.rsqrt(jnp.mean(o * o, axis=-1, keepdims=True) + EPS) * hn[:, sl]
                y_scr[rows, CONV_WIDTH + lo:CONV_WIDTH + lo + HEAD_DIM] = \
                    (o * (g * _sigmoid(g))).astype(BF16)

    o_ref[0] = x + _dot(y_scr[...], wout_ref[...])


def _mixer(x, mix_norm, w_in, conv_w, hgrn_lb, hgrn_norm, w_out):
    bsz, seq, d = x.shape
    ts = SEQ_TILE
    const = lambda b, j: (0, 0)
    return pl.pallas_call(
        _mixer_kernel,
        grid=(bsz, seq // ts),
        in_specs=[
            pl.BlockSpec((1, ts, d), lambda b, j: (b, j, 0)),
            pl.BlockSpec((1, d), const),
            pl.BlockSpec((d, PROJ_WIDTH), const),
            pl.BlockSpec((3, CONV_WIDTH), const),
            pl.BlockSpec((2, HGRN_WIDTH), const),
            pl.BlockSpec((1, HGRN_WIDTH), const),
            pl.BlockSpec((d, d), const),
            pl.BlockSpec((CHUNK, CHUNK), const),
        ],
        out_specs=pl.BlockSpec((1, ts, d), lambda b, j: (b, j, 0)),
        out_shape=jax.ShapeDtypeStruct((bsz, seq, d), F32),
        scratch_shapes=[
            pltpu.VMEM((ts, PROJ_WIDTH), F32),
            pltpu.VMEM((ts, d), BF16),
            pltpu.VMEM((HGRN_HEADS, HEAD_DIM, HEAD_DIM), F32),
            pltpu.VMEM((8, CONV_WIDTH), F32),
        ],
        compiler_params=pltpu.CompilerParams(
            dimension_semantics=("arbitrary", "arbitrary"), vmem_limit_bytes=VMEM_LIMIT),
        name="mixer",
    )(x, mix_norm, w_in, conv_w, hgrn_lb, hgrn_norm, w_out, jnp.asarray(_split_levels()))


def _kv_kernel(m_ref, g_ref, w_ref, o_ref):
    h = _rms(m_ref[0], g_ref[...]).astype(BF16)
    o_ref[0] = _dot(h, w_ref[...]).astype(BF16)


def _kv_proj(mem, mem_norm, w_kv):
    bsz, mlen, d = mem.shape
    const = lambda b: (0, 0)
    return pl.pallas_call(
        _kv_kernel,
        grid=(bsz,),
        in_specs=[
            pl.BlockSpec((1, mlen, d), lambda b: (b, 0, 0)),
            pl.BlockSpec((1, d), const),
            pl.BlockSpec((d, 2 * d), const),
        ],
        out_specs=pl.BlockSpec((1, mlen, 2 * d), lambda b: (b, 0, 0)),
        out_shape=jax.ShapeDtypeStruct((bsz, mlen, 2 * d), BF16),
        compiler_params=pltpu.CompilerParams(
            dimension_semantics=("arbitrary",), vmem_limit_bytes=VMEM_LIMIT),
        name="kv_proj",
    )(mem, mem_norm, w_kv)


def _first_argmax(vals, lane):
    mx = jnp.max(vals, axis=-1, keepdims=True)
    idx = jnp.min(jnp.where(vals == mx, lane, float(LANES)), axis=-1, keepdims=True)
    return mx, idx


def _router_logits(h3, wr_ref, br_ref):
    h3_hi = h3.astype(BF16)
    h3_lo = (h3 - h3_hi.astype(F32)).astype(BF16)
    two = _dot(h3_hi, wr_ref[...])
    return (two[:, :LANES] + two[:, LANES:]) + _dot(h3_lo, wr_ref[:, :LANES]) + br_ref[...]


def _route(logits, running, tri):
    lane = lax.broadcasted_iota(jnp.int32, logits.shape, 1).astype(F32)
    neg = jnp.float32(-jnp.inf)
    gl = jnp.where(lane < N_GROUPS, logits, neg)
    _, gidx = _first_argmax(gl, lane)
    base = N_GROUPS + EXPERTS_PER_GROUP * gidx
    el = jnp.where((lane >= base) & (lane < base + EXPERTS_PER_GROUP), logits, neg)
    _, i1 = _first_argmax(el, lane)
    _, i2 = _first_argmax(jnp.where(lane == i1, neg, el), lane)
    lo = jnp.minimum(i1, i2) - base
    hi = jnp.maximum(i1, i2) - base
    bucket = gidx * N_PAIRS + (lo * (7.0 - lo)) * 0.5 + hi - lo - 1.0

    onehot = lane == bucket
    ranks = []
    for r0 in range(0, logits.shape[0], RANK_BLOCK):
        oh = onehot[r0:r0 + RANK_BLOCK]
        before = _dot(tri, oh.astype(BF16))
        ranks.append(jnp.sum(jnp.where(oh, before + running, 0.0), axis=-1, keepdims=True))
        running = running + jnp.sum(oh.astype(F32), axis=0, keepdims=True)
    rank = jnp.concatenate(ranks, axis=0)
    info = jnp.where(lane == 0, bucket, jnp.where(lane == 1, rank, 0.0))
    return info, running


def _gates(logits, e_lo, e_hi):
    lane = lax.broadcasted_iota(jnp.int32, logits.shape, 1)
    neg = jnp.float32(-jnp.inf)
    gl = jnp.where(lane < N_GROUPS, logits, neg)
    g_p = 1.0 / jnp.sum(jnp.exp(gl - jnp.max(gl, axis=-1, keepdims=True)), axis=-1, keepdims=True)
    l_lo = jnp.sum(jnp.where(lane == N_GROUPS + e_lo, logits, 0.0), axis=-1, keepdims=True)
    l_hi = jnp.sum(jnp.where(lane == N_GROUPS + e_hi, logits, 0.0), axis=-1, keepdims=True)
    m = jnp.maximum(l_lo, l_hi)
    p_lo = jnp.exp(l_lo - m)
    p_hi = jnp.exp(l_hi - m)
    inv = g_p / (p_lo + p_hi)
    return p_lo * inv, p_hi * inv


def _attn_kernel(x_ref, kv_ref, gx_ref, wq_ref, wo_ref, gf_ref, wr_ref, br_ref, tri_ref,
                 x2r_ref, brk_ref, counts_ref, o_scr, cnt_scr):
    @pl.when((pl.program_id(0) == 0) & (pl.program_id(1) == 0))
    def _():
        cnt_scr[...] = jnp.zeros_like(cnt_scr)

    x = x_ref[0]
    tq = x.shape[0]
    h = _rms(x, gx_ref[...]).astype(BF16)
    q = _dot(h, wq_ref[...])
    for hd in range(XATTN_HEADS):
        lo = hd * XATTN_HEAD_DIM
        qh = q[:, lo:lo + XATTN_HEAD_DIM].astype(BF16)
        kh = kv_ref[0, :, lo:lo + XATTN_HEAD_DIM]
        vh = kv_ref[0, :, D_MODEL + lo:D_MODEL + lo + XATTN_HEAD_DIM]
        s = _dot_nt(qh, kh)
        p = jnp.exp(s - jnp.max(s, axis=-1, keepdims=True))
        inv = 1.0 / jnp.sum(p, axis=-1, keepdims=True)
        o_scr[:, lo:lo + XATTN_HEAD_DIM] = (_dot(p.astype(BF16), vh) * inv).astype(BF16)
    x2 = x + _dot(o_scr[...], wo_ref[...])
    for s in range(ROW_TILES):
        x2r_ref[pl.ds(s, tq, stride=ROW_TILES), :] = x2[:, s * LANES:(s + 1) * LANES]
    logits = _router_logits(_rms(x2, gf_ref[...]), wr_ref, br_ref)
    info, running = _route(logits, cnt_scr[...], tri_ref[...])
    cnt_scr[...] = running
    counts_ref[...] = running
    brk_ref[...] = info.T[0:8, :].astype(jnp.int32)


def _attention(x1, kv, xattn_norm, w_q, w_o, ffn_norm, w_router, b_router):
    bsz, seq, d = x1.shape
    mlen = kv.shape[1]
    tq = ATTN_TILE
    nj = seq // tq
    const = lambda b, j: (0, 0)
    tile = lambda b, j: (b, j, 0)
    tri = jnp.tri(RANK_BLOCK, RANK_BLOCK, -1, dtype=BF16)
    return pl.pallas_call(
        _attn_kernel,
        grid=(bsz, nj),
        in_specs=[
            pl.BlockSpec((1, tq, d), tile),
            pl.BlockSpec((1, mlen, 2 * d), lambda b, j: (b, 0, 0)),
            pl.BlockSpec((1, d), const),
            pl.BlockSpec((d, d), const),
            pl.BlockSpec((d, d), const),
            pl.BlockSpec((1, d), const),
            pl.BlockSpec((d, 2 * LANES), const),
            pl.BlockSpec((1, LANES), const),
            pl.BlockSpec((RANK_BLOCK, RANK_BLOCK), const),
        ],
        out_specs=[
            pl.BlockSpec((tq * ROW_TILES, LANES), lambda b, j: (b * nj + j, 0)),
            pl.BlockSpec((8, tq), lambda b, j: (0, b * nj + j)),
            pl.BlockSpec((1, LANES), const),
        ],
        out_shape=[
            jax.ShapeDtypeStruct((bsz * seq * ROW_TILES, LANES), F32),
            jax.ShapeDtypeStruct((8, bsz * seq), jnp.int32),
            jax.ShapeDtypeStruct((1, LANES), F32),
        ],
        scratch_shapes=[pltpu.VMEM((tq, d), BF16), pltpu.VMEM((1, LANES), F32)],
        compiler_params=pltpu.CompilerParams(
            dimension_semantics=("arbitrary", "arbitrary"), vmem_limit_bytes=VMEM_LIMIT),
        name="xattn_router",
    )(x1, kv, xattn_norm, w_q, w_o, ffn_norm, w_router, b_router, tri)


def _item_copy(hbm, hbm_item, buf, buf_item, rows_per_item, sem, to_hbm):
    h = hbm.at[pl.ds(pl.multiple_of(hbm_item * rows_per_item, rows_per_item), rows_per_item)]
    b = buf.at[pl.ds(pl.multiple_of(buf_item * rows_per_item, rows_per_item), rows_per_item)]
    return pltpu.make_async_copy(b, h, sem) if to_hbm else pltpu.make_async_copy(h, b, sem)


def _items_wait(hbm, buf, buf_item, n_items, rows_per_item, sem, to_hbm):
    n = n_items * rows_per_item
    h = hbm.at[pl.ds(0, n)]
    b = buf.at[pl.ds(pl.multiple_of(buf_item * rows_per_item, rows_per_item), n)]
    (pltpu.make_async_copy(b, h, sem) if to_hbm else pltpu.make_async_copy(h, b, sem)).wait()


def _dispatch_kernel(pos_ref, fill_ref, ntiles_ref, x2r_ref, xs_hbm, buf, sem):
    i = pl.program_id(0)
    nt = pl.num_programs(0)
    td = DISPATCH_TILE
    rpi = ROW_TILES
    slot = i % 2
    base = slot * td

    @pl.when(i >= 2)
    def _():
        _items_wait(xs_hbm, buf, base, td, rpi, sem.at[slot], True)

    buf[pl.ds(pl.multiple_of(base * rpi, td * rpi), td * rpi), :] = x2r_ref[...]

    def send(r2, carry):
        for k in range(DMA_QUEUES):
            r = r2 * DMA_QUEUES + k
            _item_copy(xs_hbm, pos_ref[i * td + r], buf, base + r, rpi, sem.at[slot],
                       True).start(priority=k)
        return carry

    lax.fori_loop(0, td // DMA_QUEUES, send, 0, unroll=4)

    @pl.when(i == nt - 1)
    def _():
        _items_wait(xs_hbm, buf, base, td, rpi, sem.at[slot], True)
        _items_wait(xs_hbm, buf, (1 - slot) * td, td, rpi, sem.at[1 - slot], True)
        buf[...] = jnp.zeros_like(buf)
        zsem = sem.at[2]
        for b in range(N_BUCKETS):
            def fill(p, carry):
                _item_copy(xs_hbm, p, buf, 0, rpi, zsem, True).start()
                return carry

            def drain(p, carry):
                _item_copy(xs_hbm, p, buf, 0, rpi, zsem, True).wait()
                return carry

            lax.fori_loop(fill_ref[b], fill_ref[N_BUCKETS + b], fill, 0)
            lax.fori_loop(fill_ref[b], fill_ref[N_BUCKETS + b], drain, 0)

        n_tiles_total = xs_hbm.shape[0] // (MOE_TILE * rpi)

        def fill_tile(tile, carry):
            cp = pltpu.make_async_copy(
                buf.at[pl.ds(0, MOE_TILE * rpi)],
                xs_hbm.at[pl.ds(pl.multiple_of(tile * MOE_TILE * rpi, MOE_TILE * rpi),
                                MOE_TILE * rpi)], zsem)
            cp.start()
            cp.wait()
            return carry

        lax.fori_loop(ntiles_ref[0], n_tiles_total, fill_tile, 0)


def _dispatch(pos, fill, ntiles, x2r, n_sorted):
    t = x2r.shape[0] // ROW_TILES
    td = DISPATCH_TILE
    grid_spec = pltpu.PrefetchScalarGridSpec(
        num_scalar_prefetch=3,
        grid=(t // td,),
        in_specs=[pl.BlockSpec((td * ROW_TILES, LANES), lambda i, *_: (i, 0))],
        out_specs=pl.BlockSpec(memory_space=pl.ANY),
        scratch_shapes=[
            pltpu.VMEM((2 * td * ROW_TILES, LANES), F32),
            pltpu.SemaphoreType.DMA((3,)),
        ],
    )
    return pl.pallas_call(
        _dispatch_kernel,
        grid_spec=grid_spec,
        out_shape=jax.ShapeDtypeStruct((n_sorted * ROW_TILES, LANES), F32),
        compiler_params=pltpu.CompilerParams(
            dimension_semantics=("arbitrary",), vmem_limit_bytes=VMEM_LIMIT),
        name="moe_dispatch",
    )(pos, fill, ntiles, x2r)


def _silu(x):
    return x * _sigmoid(x)


def _moe_kernel(elo_ref, ehi_ref, used_ref,
                xs_ref, gf_ref, wr_ref, br_ref, gfin_ref,
                wg_lo, wu_lo, wd_lo, wg_hi, wu_hi, wd_hi, y_ref):
    i = pl.program_id(0)
    tm = MOE_TILE

    @pl.when(used_ref[i] == 1)
    def _():
        x2 = jnp.concatenate(
            [xs_ref[pl.ds(s, tm, stride=ROW_TILES), :] for s in range(ROW_TILES)], axis=1)
        h3 = _rms(x2, gf_ref[...])
        g_lo, g_hi = _gates(_router_logits(h3, wr_ref, br_ref), elo_ref[i], ehi_ref[i])
        x = h3.astype(BF16)

        def expert(wg, wu, wd):
            hid = (_silu(_dot(x, wg[0])) * _dot(x, wu[0])).astype(BF16)
            return _dot(hid, wd[0])

        moe = g_lo * expert(wg_lo, wu_lo, wd_lo) + g_hi * expert(wg_hi, wu_hi, wd_hi)
        out = _rms(x2 + moe, gfin_ref[...])
        for s in range(ROW_TILES):
            y_ref[pl.ds(s, tm, stride=ROW_TILES), :] = out[:, s * LANES:(s + 1) * LANES]

    @pl.when(used_ref[i] == 0)
    def _():
        y_ref[...] = jnp.zeros_like(y_ref)


def _moe_sparse(e_lo, e_hi, used, xs, ffn_norm, w_router, b_router, final_norm,
                w_gate, w_up, w_down):
    tm = MOE_TILE
    nt = used.shape[0]
    d = D_MODEL
    lo = lambda i, elo, ehi, used: (elo[i], 0, 0)
    hi = lambda i, elo, ehi, used: (ehi[i], 0, 0)
    const = lambda i, *_: (0, 0)
    grid_spec = pltpu.PrefetchScalarGridSpec(
        num_scalar_prefetch=3,
        grid=(nt,),
        in_specs=[
            pl.BlockSpec((tm * ROW_TILES, LANES), lambda i, *_: (i, 0)),
            pl.BlockSpec((1, d), const),
            pl.BlockSpec((d, 2 * LANES), const),
            pl.BlockSpec((1, LANES), const),
            pl.BlockSpec((1, d), const),
            pl.BlockSpec((1, d, D_EXPERT), lo),
            pl.BlockSpec((1, d, D_EXPERT), lo),
            pl.BlockSpec((1, D_EXPERT, d), lo),
            pl.BlockSpec((1, d, D_EXPERT), hi),
            pl.BlockSpec((1, d, D_EXPERT), hi),
            pl.BlockSpec((1, D_EXPERT, d), hi),
        ],
        out_specs=pl.BlockSpec((tm * ROW_TILES, LANES), lambda i, *_: (i, 0)),
    )
    return pl.pallas_call(
        _moe_kernel,
        grid_spec=grid_spec,
        out_shape=jax.ShapeDtypeStruct((nt * tm * ROW_TILES, LANES), F32),
        compiler_params=pltpu.CompilerParams(
            dimension_semantics=("arbitrary",), vmem_limit_bytes=VMEM_LIMIT),
        name="moe_sparse",
    )(e_lo, e_hi, used, xs, ffn_norm, w_router, b_router, final_norm,
      w_gate, w_up, w_down, w_gate, w_up, w_down)


def _final_kernel(pos_ref, y_hbm, o_ref, ybuf, sem):
    i = pl.program_id(0)
    nt = pl.num_programs(0)
    tf = FINAL_TILE

    def start(tile, slot):
        def fetch(r2, carry):
            for k in range(DMA_QUEUES):
                r = r2 * DMA_QUEUES + k
                _item_copy(y_hbm, pos_ref[tile * tf + r], ybuf, slot * tf + r, ROW_TILES,
                           sem.at[slot], False).start(priority=k)
            return carry

        lax.fori_loop(0, tf // DMA_QUEUES, fetch, 0, unroll=4)

    @pl.when(i == 0)
    def _():
        start(0, 0)

    @pl.when(i + 1 < nt)
    def _():
        start(i + 1, (i + 1) % 2)

    slot = i % 2
    _items_wait(y_hbm, ybuf, slot * tf, tf, ROW_TILES, sem.at[slot], False)
    row0 = pl.multiple_of(slot * tf * ROW_TILES, ROW_TILES)
    for s in range(ROW_TILES):
        o_ref[:, s * LANES:(s + 1) * LANES] = ybuf[pl.ds(row0 + s, tf, stride=ROW_TILES), :]


def _final(pos, y):
    t = pos.shape[0]
    d = D_MODEL
    tf = FINAL_TILE
    grid_spec = pltpu.PrefetchScalarGridSpec(
        num_scalar_prefetch=1,
        grid=(t // tf,),
        in_specs=[pl.BlockSpec(memory_space=pl.ANY)],
        out_specs=pl.BlockSpec((tf, d), lambda i, *_: (i, 0)),
        scratch_shapes=[
            pltpu.VMEM((2 * tf * ROW_TILES, LANES), F32),
            pltpu.SemaphoreType.DMA((2,)),
        ],
    )
    return pl.pallas_call(
        _final_kernel,
        grid_spec=grid_spec,
        out_shape=jax.ShapeDtypeStruct((t, d), F32),
        compiler_params=pltpu.CompilerParams(
            dimension_semantics=("arbitrary",), vmem_limit_bytes=VMEM_LIMIT),
        name="moe_unpermute",
    )(pos, y)


def _positions_kernel(starts_ref, brk_ref, pos_ref):
    bucket = brk_ref[0:1, :]
    pos = brk_ref[1:2, :]
    for b in range(N_BUCKETS):
        pos = pos + jnp.where(bucket == b, starts_ref[b], 0)
    pos_ref[...] = pos


def _positions(starts, brk):
    t = brk.shape[1]
    grid_spec = pltpu.PrefetchScalarGridSpec(
        num_scalar_prefetch=1,
        grid=(1,),
        in_specs=[pl.BlockSpec(brk.shape, lambda i, starts: (0, 0))],
        out_specs=pl.BlockSpec((1, t), lambda i, starts: (0, 0)),
    )
    return pl.pallas_call(
        _positions_kernel,
        grid_spec=grid_spec,
        out_shape=jax.ShapeDtypeStruct((1, t), jnp.int32),
        name="moe_positions",
    )(starts, brk).reshape(t)


def _routing_tables(counts, brk):
    tm = MOE_TILE
    t = brk.shape[1]
    nt = t // tm + N_BUCKETS
    cnt = counts[0, :N_BUCKETS].astype(jnp.int32)
    padded = ((cnt + tm - 1) // tm) * tm
    ends = jnp.cumsum(padded)
    starts = ends - padded
    pos = _positions(starts, brk)
    fill = jnp.concatenate([starts + cnt, ends])
    ntiles = ends[-1:] // tm
    tile_start = jnp.arange(nt, dtype=jnp.int32) * tm
    tile_bucket = jnp.sum((ends[None, :] <= tile_start[:, None]).astype(jnp.int32), axis=1)
    used = (tile_bucket < N_BUCKETS).astype(jnp.int32)
    tile_bucket = jnp.minimum(tile_bucket, N_BUCKETS - 1)
    pair = tile_bucket % N_PAIRS
    group0 = (tile_bucket // N_PAIRS) * EXPERTS_PER_GROUP
    e_lo = group0 + jnp.array([0, 0, 0, 1, 1, 2], jnp.int32)[pair]
    e_hi = group0 + jnp.array([1, 2, 3, 2, 3, 3], jnp.int32)[pair]
    return pos, fill, ntiles, e_lo, e_hi, used, nt


def kernel(x, mem, mix_norm, w_in, conv_w, hgrn_lb, hgrn_norm, w_out, xattn_norm, mem_norm,
           w_q, w_kv, w_o, ffn_norm, w_group, b_group, w_expert, b_expert, w_gate, w_up,
           w_down, final_norm):
    bsz, seq, d = x.shape
    assert d == D_MODEL and seq % SEQ_TILE == 0 and seq % ATTN_TILE == 0
    assert (bsz * seq) % MOE_TILE == 0 and (bsz * seq) % FINAL_TILE == 0
    assert (bsz * seq) % DISPATCH_TILE == 0 and bsz * seq >= 2 * DISPATCH_TILE
    assert 2 * DISPATCH_TILE >= MOE_TILE and mix_norm.shape[0] == 1
    bf = lambda w: w.astype(BF16)

    x1 = _mixer(x, mix_norm, bf(w_in[0]), conv_w[0], hgrn_lb, hgrn_norm, bf(w_out[0]))
    kv = _kv_proj(mem, mem_norm, bf(w_kv[0]))

    pad = LANES - N_GROUPS - N_EXPERTS
    w_router = jnp.concatenate(
        [w_group[0], w_expert[0], jnp.zeros((d, pad), F32)], axis=1)
    b_router = jnp.concatenate(
        [b_group[0], b_expert[0], jnp.zeros((pad,), F32)])[None, :]
    w_router_hi = bf(w_router)
    w_router_lo = bf(w_router - w_router_hi.astype(F32))
    w_router2 = jnp.concatenate([w_router_hi, w_router_lo], axis=1)
    w_q_scaled = bf(w_q[0] * (XATTN_HEAD_DIM ** -0.5))
    x2r, brk, counts = _attention(x1, kv, xattn_norm, w_q_scaled, bf(w_o[0]), ffn_norm,
                                  w_router2, b_router)

    pos, fill, ntiles, e_lo, e_hi, used, nt = _routing_tables(counts, brk)
    xs = _dispatch(pos, fill, ntiles, x2r, nt * MOE_TILE)
    y = _moe_sparse(e_lo, e_hi, used, xs, ffn_norm, w_router2, b_router, final_norm[None, :],
                    bf(w_gate[0]), bf(w_up[0]), bf(w_down[0]))
    return _final(pos, y).reshape(bsz, seq, d)
```

```python
import jax
import jax.numpy as jnp
from jax import lax
from jax.experimental import pallas as pl
from jax.experimental.pallas import tpu as pltpu

F32 = jnp.float32
BF16 = jnp.bfloat16

D_MODEL = 1024
CONV_WIDTH = 512
HGRN_WIDTH = 512
HGRN_HEADS = 4
HEAD_DIM = 128
N_PROJ_SLOTS = 7
PROJ_WIDTH = N_PROJ_SLOTS * 512
XATTN_HEADS = 4
XATTN_HEAD_DIM = 256
N_GROUPS = 4
EXPERTS_PER_GROUP = 4
N_EXPERTS = 16
D_EXPERT = 512
EPS = 1e-6

LANES = 128
SUBLANES = 8
CHUNK = 64
CHUNK_LEVELS = 6
HGRN_TILE = 4 * CHUNK
SEQ_TILE = 512
ATTN_TILE = 1024
RANK_BLOCK = 256
DISPATCH_TILE = 1024
MOE_TILE = 512
FINAL_TILE = 1024
N_PAIRS = 6
N_BUCKETS = N_GROUPS * N_PAIRS
ROW_TILES = D_MODEL // LANES
DMA_QUEUES = 2
VMEM_LIMIT = 56 * 1024 * 1024


def _rms(x, g):
    return x * lax.rsqrt(jnp.mean(x * x, axis=-1, keepdims=True) + EPS) * g


def _dot(a, b):
    return jnp.dot(a, b, preferred_element_type=F32)


def _dot_nt(a, b):
    return lax.dot_general(a, b, (((1,), (1,)), ((), ())), preferred_element_type=F32)


def _dot_tn(a, b):
    return lax.dot_general(a, b, (((0,), (0,)), ((), ())), preferred_element_type=F32)


def _roll_rows(x, shift):
    return pltpu.roll(x, shift % x.shape[0], axis=0)


def _level_exponents(logf2, use_level):
    n = CHUNK // SUBLANES
    sub = lax.broadcasted_iota(jnp.int32, (SUBLANES, logf2.shape[1]), 0)
    roll = lambda x, s: pltpu.roll(x, s % SUBLANES, axis=0)
    r = [logf2[SUBLANES * j:SUBLANES * (j + 1), :] for j in range(n)]
    for lvl in range(1, CHUNK_LEVELS + 1):
        half = 1 << (lvl - 1)
        g = [None] * n
        if half < SUBLANES:
            second = (sub & half) != 0
            for j in range(n):
                last = r[j]
                w = 1
                while w < half:
                    last = jnp.where((sub & w) != 0, last, roll(last, -w))
                    w *= 2
                tot = jnp.where(second, roll(last, half), last)
                g[j] = jnp.where(second, r[j], tot - r[j])
                r[j] = jnp.where(second, r[j] + tot, r[j])
        else:
            hv = half // SUBLANES
            for j0 in range(0, n, 2 * hv):
                mid = r[j0 + hv - 1]
                tot = jnp.broadcast_to(mid[SUBLANES - 1:SUBLANES, :], mid.shape)
                for j in range(j0, j0 + hv):
                    g[j] = tot - r[j]
                for j in range(j0 + hv, j0 + 2 * hv):
                    g[j] = r[j]
                    r[j] = r[j] + tot
        use_level(lvl, jnp.concatenate(g, axis=0))
    return jnp.concatenate(r, axis=0)


def _sigmoid(x):
    return 0.5 * jnp.tanh(0.5 * x) + 0.5


def _split_levels():
    import numpy as np
    t = np.arange(CHUNK)[:, None]
    s = np.arange(CHUNK)[None, :]
    msb = np.floor(np.log2(np.maximum(t ^ s, 1))).astype(np.int32) + 1
    return np.where(s < t, msb, np.where(s == t, 0, -1)).astype(np.int32)


def _hgrn_chunk(q, z, v, lb, levels):
    half_th = 0.5 * jnp.tanh(0.5 * z)
    one_m_lb = 1.0 - lb
    logf2 = jnp.log2(lb + one_m_lb * (0.5 + half_th))
    k = one_m_lb * (0.5 - half_th)

    qb = q.astype(BF16)
    kb = k.astype(BF16)
    scores = [jnp.where(levels == 0, jnp.sum(q * k, axis=-1, keepdims=True), 0.0)]

    def use_level(lvl, g):
        decay = jnp.exp2(g).astype(BF16)
        scores[0] = jnp.where(levels == lvl, _dot_nt(qb * decay, kb * decay), scores[0])

    b2 = _level_exponents(logf2, use_level)
    a = scores[0]
    b2_last = b2[CHUNK - 1:CHUNK, :]

    qe = (q * jnp.exp2(b2)).astype(BF16)
    kd = (k * jnp.exp2(b2_last - b2)).astype(BF16)
    return a, qe, kd, v.astype(BF16), b2_last


def _hgrn_tile(chunks, st):
    (a0, qe0, kd0, v0, bl0), (a1, qe1, kd1, v1, bl1), (a2, qe2, kd2, v2, bl2), \
        (a3, qe3, kd3, v3, bl3) = chunks
    scale = lambda x, log2_decay: x * jnp.exp2(log2_decay).astype(BF16)
    bf = lambda x: x.astype(BF16)
    cum1 = bl0
    cum2 = cum1 + bl1
    cum3 = cum2 + bl2
    cum4 = cum3 + bl3
    s10 = bf(_dot_nt(qe1, kd0))
    s32 = bf(_dot_nt(qe3, kd2))
    s8 = bf(_dot_nt(jnp.concatenate([qe2, scale(qe3, bl2)], axis=0),
                    jnp.concatenate([scale(kd0, bl1), kd1], axis=0)))
    v01 = jnp.concatenate([v0, v1], axis=0)
    stb = bf(st)
    o0 = _dot(bf(a0), v0) + _dot_nt(qe0, stb)
    o1 = _dot(bf(a1), v1) + _dot(s10, v0) + _dot_nt(scale(qe1, cum1), stb)
    o2 = _dot(bf(a2), v2) + _dot(s8[:CHUNK], v01) + _dot_nt(scale(qe2, cum2), stb)
    o3 = (_dot(bf(a3), v3) + _dot(s32, v2) + _dot(s8[CHUNK:], v01)
          + _dot_nt(scale(qe3, cum3), stb))
    kd_all = jnp.concatenate(
        [scale(kd0, cum4 - cum1), scale(kd1, cum4 - cum2), scale(kd2, bl3), kd3], axis=0)
    v_all = jnp.concatenate([v0, v1, v2, v3], axis=0)
    st_new = st * jnp.exp2(cum4) + _dot_tn(v_all, kd_all)
    return [o0, o1, o2, o3], st_new


def _mixer_kernel(x_ref, gmix_ref, win_ref, convw_ref, lbraw_ref, hnorm_ref, wout_ref, lvl_ref,
                  o_ref, p_scr, y_scr, st_scr, tail_scr):
    j = pl.program_id(1)

    @pl.when(j == 0)
    def _():
        st_scr[...] = jnp.zeros_like(st_scr)
        tail_scr[...] = jnp.zeros_like(tail_scr)

    x = x_ref[0]
    h = _rms(x, gmix_ref[...]).astype(BF16)
    p_scr[...] = _dot(h, win_ref[...])

    ts = x.shape[0]
    cb = p_scr[:, 0:CONV_WIDTH]
    u = p_scr[:, CONV_WIDTH:2 * CONV_WIDTH] * p_scr[:, 2 * CONV_WIDTH:3 * CONV_WIDTH]
    row = lax.broadcasted_iota(jnp.int32, u.shape, 0)
    prev1 = tail_scr[7:8, :]
    prev2 = tail_scr[6:7, :]
    u1 = jnp.where(row == 0, prev1, _roll_rows(u, 1))
    u2 = jnp.where(row == 0, prev2, jnp.where(row == 1, prev1, _roll_rows(u, 2)))
    cw = convw_ref[...]
    conv = u2 * cw[0:1, :] + u1 * cw[1:2, :] + u * cw[2:3, :]
    y_scr[:, 0:CONV_WIDTH] = (cb * conv).astype(BF16)
    tail_scr[...] = u[ts - 8:ts, :]

    raw = lbraw_ref[...]
    mx = jnp.max(raw, axis=0, keepdims=True)
    ex = jnp.exp(raw - mx)
    lb_all = ex[0:1, :] / jnp.sum(ex, axis=0, keepdims=True)
    hn = hnorm_ref[...]

    for hd in range(HGRN_HEADS):
        lo = hd * HEAD_DIM
        sl = slice(lo, lo + HEAD_DIM)
        col = lambda slot: slice(slot * 512 + lo, slot * 512 + lo + HEAD_DIM)
        for t0 in range(0, ts, HGRN_TILE):
            chunks = []
            for r0 in range(t0, t0 + HGRN_TILE, CHUNK):
                rows = slice(r0, r0 + CHUNK)
                chunks.append(_hgrn_chunk(p_scr[rows, col(3)], p_scr[rows, col(4)],
                                          p_scr[rows, col(5)], lb_all[:, sl], lvl_ref[...]))
            outs, st_new = _hgrn_tile(chunks, st_scr[hd])
            st_scr[hd] = st_new
            for c, o in enumerate(outs):
                rows = slice(t0 + c * CHUNK, t0 + (c + 1) * CHUNK)
                g = p_scr[rows, col(6)]
                o = o * lax.rsqrt(jnp.mean(o * o, axis=-1, keepdims=True) + EPS) * hn[:, sl]
                y_scr[rows, CONV_WIDTH + lo:CONV_WIDTH + lo + HEAD_DIM] = \
                    (o * (g * _sigmoid(g))).astype(BF16)

    o_ref[0] = x + _dot(y_scr[...], wout_ref[...])


def _mixer(x, mix_norm, w_in, conv_w, hgrn_lb, hgrn_norm, w_out):
    bsz, seq, d = x.shape
    ts = SEQ_TILE
    const = lambda b, j: (0, 0)
    return pl.pallas_call(
        _mixer_kernel,
        grid=(bsz, seq // ts),
        in_specs=[
            pl.BlockSpec((1, ts, d), lambda b, j: (b, j, 0)),
            pl.BlockSpec((1, d), const),
            pl.BlockSpec((d, PROJ_WIDTH), const),
            pl.BlockSpec((3, CONV_WIDTH), const),
            pl.BlockSpec((2, HGRN_WIDTH), const),
            pl.BlockSpec((1, HGRN_WIDTH), const),
            pl.BlockSpec((d, d), const),
            pl.BlockSpec((CHUNK, CHUNK), const),
        ],
        out_specs=pl.BlockSpec((1, ts, d), lambda b, j: (b, j, 0)),
        out_shape=jax.ShapeDtypeStruct((bsz, seq, d), F32),
        scratch_shapes=[
            pltpu.VMEM((ts, PROJ_WIDTH), F32),
            pltpu.VMEM((ts, d), BF16),
            pltpu.VMEM((HGRN_HEADS, HEAD_DIM, HEAD_DIM), F32),
            pltpu.VMEM((8, CONV_WIDTH), F32),
        ],
        compiler_params=pltpu.CompilerParams(
            dimension_semantics=("arbitrary", "arbitrary"), vmem_limit_bytes=VMEM_LIMIT),
        name="mixer",
    )(x, mix_norm, w_in, conv_w, hgrn_lb, hgrn_norm, w_out, jnp.asarray(_split_levels()))


def _kv_kernel(m_ref, g_ref, w_ref, o_ref):
    h = _rms(m_ref[0], g_ref[...]).astype(BF16)
    o_ref[0] = _dot(h, w_ref[...]).astype(BF16)


def _kv_proj(mem, mem_norm, w_kv):
    bsz, mlen, d = mem.shape
    const = lambda b: (0, 0)
    return pl.pallas_call(
        _kv_kernel,
        grid=(bsz,),
        in_specs=[
            pl.BlockSpec((1, mlen, d), lambda b: (b, 0, 0)),
            pl.BlockSpec((1, d), const),
            pl.BlockSpec((d, 2 * d), const),
        ],
        out_specs=pl.BlockSpec((1, mlen, 2 * d), lambda b: (b, 0, 0)),
        out_shape=jax.ShapeDtypeStruct((bsz, mlen, 2 * d), BF16),
        compiler_params=pltpu.CompilerParams(
            dimension_semantics=("arbitrary",), vmem_limit_bytes=VMEM_LIMIT),
        name="kv_proj",
    )(mem, mem_norm, w_kv)


def _first_argmax(vals, lane):
    mx = jnp.max(vals, axis=-1, keepdims=True)
    idx = jnp.min(jnp.where(vals == mx, lane, float(LANES)), axis=-1, keepdims=True)
    return mx, idx


def _router_logits(h3, wr_ref, br_ref):
    h3_hi = h3.astype(BF16)
    h3_lo = (h3 - h3_hi.astype(F32)).astype(BF16)
    two = _dot(h3_hi, wr_ref[...])
    return (two[:, :LANES] + two[:, LANES:]) + _dot(h3_lo, wr_ref[:, :LANES]) + br_ref[...]


def _route(logits, running, tri):
    lane = lax.broadcasted_iota(jnp.int32, logits.shape, 1).astype(F32)
    neg = jnp.float32(-jnp.inf)
    gl = jnp.where(lane < N_GROUPS, logits, neg)
    _, gidx = _first_argmax(gl, lane)
    base = N_GROUPS + EXPERTS_PER_GROUP * gidx
    el = jnp.where((lane >= base) & (lane < base + EXPERTS_PER_GROUP), logits, neg)
    _, i1 = _first_argmax(el, lane)
    _, i2 = _first_argmax(jnp.where(lane == i1, neg, el), lane)
    lo = jnp.minimum(i1, i2) - base
    hi = jnp.maximum(i1, i2) - base
    bucket = gidx * N_PAIRS + (lo * (7.0 - lo)) * 0.5 + hi - lo - 1.0

    onehot = lane == bucket
    ranks = []
    for r0 in range(0, logits.shape[0], RANK_BLOCK):
        oh = onehot[r0:r0 + RANK_BLOCK]
        before = _dot(tri, oh.astype(BF16))
        ranks.append(jnp.sum(jnp.where(oh, before + running, 0.0), axis=-1, keepdims=True))
        running = running + jnp.sum(oh.astype(F32), axis=0, keepdims=True)
    rank = jnp.concatenate(ranks, axis=0)
    info = jnp.where(lane == 0, bucket, jnp.where(lane == 1, rank, 0.0))
    return info, running


def _gates(logits, e_lo, e_hi):
    lane = lax.broadcasted_iota(jnp.int32, logits.shape, 1)
    neg = jnp.float32(-jnp.inf)
    gl = jnp.where(lane < N_GROUPS, logits, neg)
    g_p = 1.0 / jnp.sum(jnp.exp(gl - jnp.max(gl, axis=-1, keepdims=True)), axis=-1, keepdims=True)
    l_lo = jnp.sum(jnp.where(lane == N_GROUPS + e_lo, logits, 0.0), axis=-1, keepdims=True)
    l_hi = jnp.sum(jnp.where(lane == N_GROUPS + e_hi, logits, 0.0), axis=-1, keepdims=True)
    m = jnp.maximum(l_lo, l_hi)
    p_lo = jnp.exp(l_lo - m)
    p_hi = jnp.exp(l_hi - m)
    inv = g_p / (p_lo + p_hi)
    return p_lo * inv, p_hi * inv


def _attn_kernel(x_ref, kv_ref, gx_ref, wq_ref, wo_ref, gf_ref, wr_ref, br_ref, tri_ref,
                 x2r_ref, brk_ref, counts_ref, o_scr, cnt_scr):
    @pl.when((pl.program_id(0) == 0) & (pl.program_id(1) == 0))
    def _():
        cnt_scr[...] = jnp.zeros_like(cnt_scr)

    x = x_ref[0]
    tq = x.shape[0]
    h = _rms(x, gx_ref[...]).astype(BF16)
    q = _dot(h, wq_ref[...])
    for hd in range(XATTN_HEADS):
        lo = hd * XATTN_HEAD_DIM
        qh = q[:, lo:lo + XATTN_HEAD_DIM].astype(BF16)
        kh = kv_ref[0, :, lo:lo + XATTN_HEAD_DIM]
        vh = kv_ref[0, :, D_MODEL + lo:D_MODEL + lo + XATTN_HEAD_DIM]
        s = _dot_nt(qh, kh)
        p = jnp.exp(s - jnp.max(s, axis=-1, keepdims=True))
        inv = 1.0 / jnp.sum(p, axis=-1, keepdims=True)
        o_scr[:, lo:lo + XATTN_HEAD_DIM] = (_dot(p.astype(BF16), vh) * inv).astype(BF16)
    x2 = x + _dot(o_scr[...], wo_ref[...])
    for s in range(ROW_TILES):
        x2r_ref[pl.ds(s, tq, stride=ROW_TILES), :] = x2[:, s * LANES:(s + 1) * LANES]
    logits = _router_logits(_rms(x2, gf_ref[...]), wr_ref, br_ref)
    info, running = _route(logits, cnt_scr[...], tri_ref[...])
    cnt_scr[...] = running
    counts_ref[...] = running
    brk_ref[...] = info.T[0:8, :].astype(jnp.int32)


def _attention(x1, kv, xattn_norm, w_q, w_o, ffn_norm, w_router, b_router):
    bsz, seq, d = x1.shape
    mlen = kv.shape[1]
    tq = ATTN_TILE
    nj = seq // tq
    const = lambda b, j: (0, 0)
    tile = lambda b, j: (b, j, 0)
    tri = jnp.tri(RANK_BLOCK, RANK_BLOCK, -1, dtype=BF16)
    return pl.pallas_call(
        _attn_kernel,
        grid=(bsz, nj),
        in_specs=[
            pl.BlockSpec((1, tq, d), tile),
            pl.BlockSpec((1, mlen, 2 * d), lambda b, j: (b, 0, 0)),
            pl.BlockSpec((1, d), const),
            pl.BlockSpec((d, d), const),
            pl.BlockSpec((d, d), const),
            pl.BlockSpec((1, d), const),
            pl.BlockSpec((d, 2 * LANES), const),
            pl.BlockSpec((1, LANES), const),
            pl.BlockSpec((RANK_BLOCK, RANK_BLOCK), const),
        ],
        out_specs=[
            pl.BlockSpec((tq * ROW_TILES, LANES), lambda b, j: (b * nj + j, 0)),
            pl.BlockSpec((8, tq), lambda b, j: (0, b * nj + j)),
            pl.BlockSpec((1, LANES), const),
        ],
        out_shape=[
            jax.ShapeDtypeStruct((bsz * seq * ROW_TILES, LANES), F32),
            jax.ShapeDtypeStruct((8, bsz * seq), jnp.int32),
            jax.ShapeDtypeStruct((1, LANES), F32),
        ],
        scratch_shapes=[pltpu.VMEM((tq, d), BF16), pltpu.VMEM((1, LANES), F32)],
        compiler_params=pltpu.CompilerParams(
            dimension_semantics=("arbitrary", "arbitrary"), vmem_limit_bytes=VMEM_LIMIT),
        name="xattn_router",
    )(x1, kv, xattn_norm, w_q, w_o, ffn_norm, w_router, b_router, tri)


def _item_copy(hbm, hbm_item, buf, buf_item, rows_per_item, sem, to_hbm):
    h = hbm.at[pl.ds(pl.multiple_of(hbm_item * rows_per_item, rows_per_item), rows_per_item)]
    b = buf.at[pl.ds(pl.multiple_of(buf_item * rows_per_item, rows_per_item), rows_per_item)]
    return pltpu.make_async_copy(b, h, sem) if to_hbm else pltpu.make_async_copy(h, b, sem)


def _items_wait(hbm, buf, buf_item, n_items, rows_per_item, sem, to_hbm):
    n = n_items * rows_per_item
    h = hbm.at[pl.ds(0, n)]
    b = buf.at[pl.ds(pl.multiple_of(buf_item * rows_per_item, rows_per_item), n)]
    (pltpu.make_async_copy(b, h, sem) if to_hbm else pltpu.make_async_copy(h, b, sem)).wait()


def _dispatch_kernel(pos_ref, fill_ref, ntiles_ref, x2r_ref, xs_hbm, buf, sem):
    i = pl.program_id(0)
    nt = pl.num_programs(0)
    td = DISPATCH_TILE
    rpi = ROW_TILES
    slot = i % 2
    base = slot * td

    @pl.when(i >= 2)
    def _():
        _items_wait(xs_hbm, buf, base, td, rpi, sem.at[slot], True)

    buf[pl.ds(pl.multiple_of(base * rpi, td * rpi), td * rpi), :] = x2r_ref[...]

    def send(r2, carry):
        for k in range(DMA_QUEUES):
            r = r2 * DMA_QUEUES + k
            _item_copy(xs_hbm, pos_ref[i * td + r], buf, base + r, rpi, sem.at[slot],
                       True).start(priority=k)
        return carry

    lax.fori_loop(0, td // DMA_QUEUES, send, 0, unroll=4)

    @pl.when(i == nt - 1)
    def _():
        _items_wait(xs_hbm, buf, base, td, rpi, sem.at[slot], True)
        _items_wait(xs_hbm, buf, (1 - slot) * td, td, rpi, sem.at[1 - slot], True)
        buf[...] = jnp.zeros_like(buf)
        zsem = sem.at[2]
        for b in range(N_BUCKETS):
            def fill(p, carry):
                _item_copy(xs_hbm, p, buf, 0, rpi, zsem, True).start()
                return carry

            def drain(p, carry):
                _item_copy(xs_hbm, p, buf, 0, rpi, zsem, True).wait()
                return carry

            lax.fori_loop(fill_ref[b], fill_ref[N_BUCKETS + b], fill, 0)
            lax.fori_loop(fill_ref[b], fill_ref[N_BUCKETS + b], drain, 0)

        n_tiles_total = xs_hbm.shape[0] // (MOE_TILE * rpi)

        def tile_copy(tile):
            return pltpu.make_async_copy(
                buf.at[pl.ds(0, MOE_TILE * rpi)],
                xs_hbm.at[pl.ds(pl.multiple_of(tile * MOE_TILE * rpi, MOE_TILE * rpi),
                                MOE_TILE * rpi)], zsem)

        def fill_tile(tile, carry):
            tile_copy(tile).start()
            return carry

        def drain_tile(tile, carry):
            tile_copy(tile).wait()
            return carry

        lax.fori_loop(ntiles_ref[0], n_tiles_total, fill_tile, 0)
        lax.fori_loop(ntiles_ref[0], n_tiles_total, drain_tile, 0)


def _dispatch(pos, fill, ntiles, x2r, n_sorted):
    t = x2r.shape[0] // ROW_TILES
    td = DISPATCH_TILE
    grid_spec = pltpu.PrefetchScalarGridSpec(
        num_scalar_prefetch=3,
        grid=(t // td,),
        in_specs=[pl.BlockSpec((td * ROW_TILES, LANES), lambda i, *_: (i, 0))],
        out_specs=pl.BlockSpec(memory_space=pl.ANY),
        scratch_shapes=[
            pltpu.VMEM((2 * td * ROW_TILES, LANES), F32),
            pltpu.SemaphoreType.DMA((3,)),
        ],
    )
    return pl.pallas_call(
        _dispatch_kernel,
        grid_spec=grid_spec,
        out_shape=jax.ShapeDtypeStruct((n_sorted * ROW_TILES, LANES), F32),
        compiler_params=pltpu.CompilerParams(
            dimension_semantics=("arbitrary",), vmem_limit_bytes=VMEM_LIMIT),
        name="moe_dispatch",
    )(pos, fill, ntiles, x2r)


def _silu(x):
    return x * _sigmoid(x)


def _moe_kernel(elo_ref, ehi_ref, used_ref,
                xs_ref, gf_ref, wr_ref, br_ref, gfin_ref,
                wg_lo, wu_lo, wd_lo, wg_hi, wu_hi, wd_hi, y_ref):
    i = pl.program_id(0)
    tm = MOE_TILE

    @pl.when(used_ref[i] == 1)
    def _():
        x2 = jnp.concatenate(
            [xs_ref[pl.ds(s, tm, stride=ROW_TILES), :] for s in range(ROW_TILES)], axis=1)
        h3 = _rms(x2, gf_ref[...])
        g_lo, g_hi = _gates(_router_logits(h3, wr_ref, br_ref), elo_ref[i], ehi_ref[i])
        x = h3.astype(BF16)

        def expert(wg, wu, wd):
            hid = (_silu(_dot(x, wg[0])) * _dot(x, wu[0])).astype(BF16)
            return _dot(hid, wd[0])

        moe = g_lo * expert(wg_lo, wu_lo, wd_lo) + g_hi * expert(wg_hi, wu_hi, wd_hi)
        out = _rms(x2 + moe, gfin_ref[...])
        for s in range(ROW_TILES):
            y_ref[pl.ds(s, tm, stride=ROW_TILES), :] = out[:, s * LANES:(s + 1) * LANES]

    @pl.when(used_ref[i] == 0)
    def _():
        y_ref[...] = jnp.zeros_like(y_ref)


def _moe_sparse(e_lo, e_hi, used, xs, ffn_norm, w_router, b_router, final_norm,
                w_gate, w_up, w_down):
    tm = MOE_TILE
    nt = used.shape[0]
    d = D_MODEL
    lo = lambda i, elo, ehi, used: (elo[i], 0, 0)
    hi = lambda i, elo, ehi, used: (ehi[i], 0, 0)
    const = lambda i, *_: (0, 0)
    grid_spec = pltpu.PrefetchScalarGridSpec(
        num_scalar_prefetch=3,
        grid=(nt,),
        in_specs=[
            pl.BlockSpec((tm * ROW_TILES, LANES), lambda i, *_: (i, 0)),
            pl.BlockSpec((1, d), const),
            pl.BlockSpec((d, 2 * LANES), const),
            pl.BlockSpec((1, LANES), const),
            pl.BlockSpec((1, d), const),
            pl.BlockSpec((1, d, D_EXPERT), lo),
            pl.BlockSpec((1, d, D_EXPERT), lo),
            pl.BlockSpec((1, D_EXPERT, d), lo),
            pl.BlockSpec((1, d, D_EXPERT), hi),
            pl.BlockSpec((1, d, D_EXPERT), hi),
            pl.BlockSpec((1, D_EXPERT, d), hi),
        ],
        out_specs=pl.BlockSpec((tm * ROW_TILES, LANES), lambda i, *_: (i, 0)),
    )
    return pl.pallas_call(
        _moe_kernel,
        grid_spec=grid_spec,
        out_shape=jax.ShapeDtypeStruct((nt * tm * ROW_TILES, LANES), F32),
        compiler_params=pltpu.CompilerParams(
            dimension_semantics=("arbitrary",), vmem_limit_bytes=VMEM_LIMIT),
        name="moe_sparse",
    )(e_lo, e_hi, used, xs, ffn_norm, w_router, b_router, final_norm,
      w_gate, w_up, w_down, w_gate, w_up, w_down)


def _final_kernel(pos_ref, y_hbm, o_ref, ybuf, sem):
    i = pl.program_id(0)
    nt = pl.num_programs(0)
    tf = FINAL_TILE

    def start(tile, slot):
        def fetch(r2, carry):
            for k in range(DMA_QUEUES):
                r = r2 * DMA_QUEUES + k
                _item_copy(y_hbm, pos_ref[tile * tf + r], ybuf, slot * tf + r, ROW_TILES,
                           sem.at[slot], False).start(priority=k)
            return carry

        lax.fori_loop(0, tf // DMA_QUEUES, fetch, 0, unroll=4)

    @pl.when(i == 0)
    def _():
        start(0, 0)

    @pl.when(i + 1 < nt)
    def _():
        start(i + 1, (i + 1) % 2)

    slot = i % 2
    _items_wait(y_hbm, ybuf, slot * tf, tf, ROW_TILES, sem.at[slot], False)
    row0 = pl.multiple_of(slot * tf * ROW_TILES, ROW_TILES)
    for s in range(ROW_TILES):
        o_ref[:, s * LANES:(s + 1) * LANES] = ybuf[pl.ds(row0 + s, tf, stride=ROW_TILES), :]


def _final(pos, y):
    t = pos.shape[0]
    d = D_MODEL
    tf = FINAL_TILE
    grid_spec = pltpu.PrefetchScalarGridSpec(
        num_scalar_prefetch=1,
        grid=(t // tf,),
        in_specs=[pl.BlockSpec(memory_space=pl.ANY)],
        out_specs=pl.BlockSpec((tf, d), lambda i, *_: (i, 0)),
        scratch_shapes=[
            pltpu.VMEM((2 * tf * ROW_TILES, LANES), F32),
            pltpu.SemaphoreType.DMA((2,)),
        ],
    )
    return pl.pallas_call(
        _final_kernel,
        grid_spec=grid_spec,
        out_shape=jax.ShapeDtypeStruct((t, d), F32),
        compiler_params=pltpu.CompilerParams(
            dimension_semantics=("arbitrary",), vmem_limit_bytes=VMEM_LIMIT),
        name="moe_unpermute",
    )(pos, y)


def _positions_kernel(starts_ref, brk_ref, pos_ref):
    bucket = brk_ref[0:1, :]
    pos = brk_ref[1:2, :]
    for b in range(N_BUCKETS):
        pos = pos + jnp.where(bucket == b, starts_ref[b], 0)
    pos_ref[...] = pos


def _positions(starts, brk):
    t = brk.shape[1]
    grid_spec = pltpu.PrefetchScalarGridSpec(
        num_scalar_prefetch=1,
        grid=(1,),
        in_specs=[pl.BlockSpec(brk.shape, lambda i, starts: (0, 0))],
        out_specs=pl.BlockSpec((1, t), lambda i, starts: (0, 0)),
    )
    return pl.pallas_call(
        _positions_kernel,
        grid_spec=grid_spec,
        out_shape=jax.ShapeDtypeStruct((1, t), jnp.int32),
        name="moe_positions",
    )(starts, brk).reshape(t)


def _routing_tables(counts, brk):
    tm = MOE_TILE
    t = brk.shape[1]
    nt = t // tm + N_BUCKETS
    cnt = counts[0, :N_BUCKETS].astype(jnp.int32)
    padded = ((cnt + tm - 1) // tm) * tm
    ends = jnp.cumsum(padded)
    starts = ends - padded
    pos = _positions(starts, brk)
    fill = jnp.concatenate([starts + cnt, ends])
    ntiles = ends[-1:] // tm
    tile_start = jnp.arange(nt, dtype=jnp.int32) * tm
    tile_bucket = jnp.sum((ends[None, :] <= tile_start[:, None]).astype(jnp.int32), axis=1)
    used = (tile_bucket < N_BUCKETS).astype(jnp.int32)
    tile_bucket = jnp.minimum(tile_bucket, N_BUCKETS - 1)
    pair = tile_bucket % N_PAIRS
    group0 = (tile_bucket // N_PAIRS) * EXPERTS_PER_GROUP
    e_lo = group0 + jnp.array([0, 0, 0, 1, 1, 2], jnp.int32)[pair]
    e_hi = group0 + jnp.array([1, 2, 3, 2, 3, 3], jnp.int32)[pair]
    return pos, fill, ntiles, e_lo, e_hi, used, nt


def kernel(x, mem, mix_norm, w_in, conv_w, hgrn_lb, hgrn_norm, w_out, xattn_norm, mem_norm,
           w_q, w_kv, w_o, ffn_norm, w_group, b_group, w_expert, b_expert, w_gate, w_up,
           w_down, final_norm):
    bsz, seq, d = x.shape
    assert d == D_MODEL and seq % SEQ_TILE == 0 and seq % ATTN_TILE == 0
    assert (bsz * seq) % MOE_TILE == 0 and (bsz * seq) % FINAL_TILE == 0
    assert (bsz * seq) % DISPATCH_TILE == 0 and bsz * seq >= 2 * DISPATCH_TILE
    assert 2 * DISPATCH_TILE >= MOE_TILE and mix_norm.shape[0] == 1
    bf = lambda w: w.astype(BF16)

    x1 = _mixer(x, mix_norm, bf(w_in[0]), conv_w[0], hgrn_lb, hgrn_norm, bf(w_out[0]))
    kv = _kv_proj(mem, mem_norm, bf(w_kv[0]))

    pad = LANES - N_GROUPS - N_EXPERTS
    w_router = jnp.concatenate(
        [w_group[0], w_expert[0], jnp.zeros((d, pad), F32)], axis=1)
    b_router = jnp.concatenate(
        [b_group[0], b_expert[0], jnp.zeros((pad,), F32)])[None, :]
    w_router_hi = bf(w_router)
    w_router_lo = bf(w_router - w_router_hi.astype(F32))
    w_router2 = jnp.concatenate([w_router_hi, w_router_lo], axis=1)
    w_q_scaled = bf(w_q[0] * (XATTN_HEAD_DIM ** -0.5))
    x2r, brk, counts = _attention(x1, kv, xattn_norm, w_q_scaled, bf(w_o[0]), ffn_norm,
                                  w_router2, b_router)

    pos, fill, ntiles, e_lo, e_hi, used, nt = _routing_tables(counts, brk)
    xs = _dispatch(pos, fill, ntiles, x2r, nt * MOE_TILE)
    y = _moe_sparse(e_lo, e_hi, used, xs, ffn_norm, w_router2, b_router, final_norm[None, :],
                    bf(w_gate[0]), bf(w_up[0]), bf(w_down[0]))
    return _final(pos, y).reshape(bsz, seq, d)
```

```python
import jax
import jax.numpy as jnp
from jax import lax
from jax.experimental import pallas as pl
from jax.experimental.pallas import tpu as pltpu

F32 = jnp.float32
BF16 = jnp.bfloat16

D_MODEL = 1024
CONV_WIDTH = 512
HGRN_WIDTH = 512
HGRN_HEADS = 4
HEAD_DIM = 128
N_PROJ_SLOTS = 7
PROJ_WIDTH = N_PROJ_SLOTS * 512
XATTN_HEADS = 4
XATTN_HEAD_DIM = 256
N_GROUPS = 4
EXPERTS_PER_GROUP = 4
N_EXPERTS = 16
D_EXPERT = 512
EPS = 1e-6

LANES = 128
SUBLANES = 8
CHUNK = 64
CHUNK_LEVELS = 6
HGRN_TILE = 4 * CHUNK
SEQ_TILE = 512
ATTN_TILE = 1024
RANK_BLOCK = 256
DISPATCH_TILE = 1024
MOE_TILE = 512
FINAL_TILE = 1024
N_PAIRS = 6
N_BUCKETS = N_GROUPS * N_PAIRS
ROW_TILES = D_MODEL // LANES
DMA_QUEUES = 2
VMEM_LIMIT = 56 * 1024 * 1024


def _rms(x, g):
    return x * lax.rsqrt(jnp.mean(x * x, axis=-1, keepdims=True) + EPS) * g


def _dot(a, b):
    return jnp.dot(a, b, preferred_element_type=F32)


def _dot_nt(a, b):
    return lax.dot_general(a, b, (((1,), (1,)), ((), ())), preferred_element_type=F32)


def _dot_tn(a, b):
    return lax.dot_general(a, b, (((0,), (0,)), ((), ())), preferred_element_type=F32)


def _roll_rows(x, shift):
    return pltpu.roll(x, shift % x.shape[0], axis=0)


def _level_exponents(logf2, use_level):
    n = CHUNK // SUBLANES
    sub = lax.broadcasted_iota(jnp.int32, (SUBLANES, logf2.shape[1]), 0)
    roll = lambda x, s: pltpu.roll(x, s % SUBLANES, axis=0)
    r = [logf2[SUBLANES * j:SUBLANES * (j + 1), :] for j in range(n)]
    for lvl in range(1, CHUNK_LEVELS + 1):
        half = 1 << (lvl - 1)
        g = [None] * n
        if half < SUBLANES:
            second = (sub & half) != 0
            for j in range(n):
                last = r[j]
                w = 1
                while w < half:
                    last = jnp.where((sub & w) != 0, last, roll(last, -w))
                    w *= 2
                tot = jnp.where(second, roll(last, half), last)
                g[j] = jnp.where(second, r[j], tot - r[j])
                r[j] = jnp.where(second, r[j] + tot, r[j])
        else:
            hv = half // SUBLANES
            for j0 in range(0, n, 2 * hv):
                mid = r[j0 + hv - 1]
                tot = jnp.broadcast_to(mid[SUBLANES - 1:SUBLANES, :], mid.shape)
                for j in range(j0, j0 + hv):
                    g[j] = tot - r[j]
                for j in range(j0 + hv, j0 + 2 * hv):
                    g[j] = r[j]
                    r[j] = r[j] + tot
        use_level(lvl, jnp.concatenate(g, axis=0))
    return jnp.concatenate(r, axis=0)


def _sigmoid(x):
    return 0.5 * jnp.tanh(0.5 * x) + 0.5


def _split_levels():
    import numpy as np
    t = np.arange(CHUNK)[:, None]
    s = np.arange(CHUNK)[None, :]
    msb = np.floor(np.log2(np.maximum(t ^ s, 1))).astype(np.int32) + 1
    return np.where(s < t, msb, np.where(s == t, 0, -1)).astype(np.int32)


def _hgrn_chunk(q, z, v, lb, levels):
    half_th = 0.5 * jnp.tanh(0.5 * z)
    one_m_lb = 1.0 - lb
    logf2 = jnp.log2(lb + one_m_lb * (0.5 + half_th))
    k = one_m_lb * (0.5 - half_th)

    qb = q.astype(BF16)
    kb = k.astype(BF16)
    scores = [jnp.where(levels == 0, jnp.sum(q * k, axis=-1, keepdims=True), 0.0)]

    def use_level(lvl, g):
        decay = jnp.exp2(g).astype(BF16)
        scores[0] = jnp.where(levels == lvl, _dot_nt(qb * decay, kb * decay), scores[0])

    b2 = _level_exponents(logf2, use_level)
    a = scores[0]
    b2_last = b2[CHUNK - 1:CHUNK, :]

    qe = (q * jnp.exp2(b2)).astype(BF16)
    kd = (k * jnp.exp2(b2_last - b2)).astype(BF16)
    return a, qe, kd, v.astype(BF16), b2_last


def _hgrn_tile(chunks, st):
    (a0, qe0, kd0, v0, bl0), (a1, qe1, kd1, v1, bl1), (a2, qe2, kd2, v2, bl2), \
        (a3, qe3, kd3, v3, bl3) = chunks
    scale = lambda x, log2_decay: x * jnp.exp2(log2_decay).astype(BF16)
    bf = lambda x: x.astype(BF16)
    cum1 = bl0
    cum2 = cum1 + bl1
    cum3 = cum2 + bl2
    cum4 = cum3 + bl3
    s10 = bf(_dot_nt(qe1, kd0))
    s32 = bf(_dot_nt(qe3, kd2))
    s8 = bf(_dot_nt(jnp.concatenate([qe2, scale(qe3, bl2)], axis=0),
                    jnp.concatenate([scale(kd0, bl1), kd1], axis=0)))
    v01 = jnp.concatenate([v0, v1], axis=0)
    stb = bf(st)
    o0 = _dot(bf(a0), v0) + _dot_nt(qe0, stb)
    o1 = _dot(bf(a1), v1) + _dot(s10, v0) + _dot_nt(scale(qe1, cum1), stb)
    o2 = _dot(bf(a2), v2) + _dot(s8[:CHUNK], v01) + _dot_nt(scale(qe2, cum2), stb)
    o3 = (_dot(bf(a3), v3) + _dot(s32, v2) + _dot(s8[CHUNK:], v01)
          + _dot_nt(scale(qe3, cum3), stb))
    kd_all = jnp.concatenate(
        [scale(kd0, cum4 - cum1), scale(kd1, cum4 - cum2), scale(kd2, bl3), kd3], axis=0)
    v_all = jnp.concatenate([v0, v1, v2, v3], axis=0)
    st_new = st * jnp.exp2(cum4) + _dot_tn(v_all, kd_all)
    return [o0, o1, o2, o3], st_new


def _mixer_kernel(x_ref, gmix_ref, win_ref, convw_ref, lbraw_ref, hnorm_ref, wout_ref, lvl_ref,
                  o_ref, p_scr, y_scr, st_scr, tail_scr):
    j = pl.program_id(1)

    @pl.when(j == 0)
    def _():
        st_scr[...] = jnp.zeros_like(st_scr)
        tail_scr[...] = jnp.zeros_like(tail_scr)

    x = x_ref[0]
    h = _rms(x, gmix_ref[...]).astype(BF16)
    p_scr[...] = _dot(h, win_ref[...])

    ts = x.shape[0]
    cb = p_scr[:, 0:CONV_WIDTH]
    u = p_scr[:, CONV_WIDTH:2 * CONV_WIDTH] * p_scr[:, 2 * CONV_WIDTH:3 * CONV_WIDTH]
    row = lax.broadcasted_iota(jnp.int32, u.shape, 0)
    prev1 = tail_scr[7:8, :]
    prev2 = tail_scr[6:7, :]
    u1 = jnp.where(row == 0, prev1, _roll_rows(u, 1))
    u2 = jnp.where(row == 0, prev2, jnp.where(row == 1, prev1, _roll_rows(u, 2)))
    cw = convw_ref[...]
    conv = u2 * cw[0:1, :] + u1 * cw[1:2, :] + u * cw[2:3, :]
    y_scr[:, 0:CONV_WIDTH] = (cb * conv).astype(BF16)
    tail_scr[...] = u[ts - 8:ts, :]

    raw = lbraw_ref[...]
    mx = jnp.max(raw, axis=0, keepdims=True)
    ex = jnp.exp(raw - mx)
    lb_all = ex[0:1, :] / jnp.sum(ex, axis=0, keepdims=True)
    hn = hnorm_ref[...]

    for hd in range(HGRN_HEADS):
        lo = hd * HEAD_DIM
        sl = slice(lo, lo + HEAD_DIM)
        col = lambda slot: slice(slot * 512 + lo, slot * 512 + lo + HEAD_DIM)
        for t0 in range(0, ts, HGRN_TILE):
            chunks = []
            for r0 in range(t0, t0 + HGRN_TILE, CHUNK):
                rows = slice(r0, r0 + CHUNK)
                chunks.append(_hgrn_chunk(p_scr[rows, col(3)], p_scr[rows, col(4)],
                                          p_scr[rows, col(5)], lb_all[:, sl], lvl_ref[...]))
            outs, st_new = _hgrn_tile(chunks, st_scr[hd])
            st_scr[hd] = st_new
            for c, o in enumerate(outs):
                rows = slice(t0 + c * CHUNK, t0 + (c + 1) * CHUNK)
                g = p_scr[rows, col(6)]
                o = o * lax.rsqrt(jnp.mean(o * o, axis=-1, keepdims=True) + EPS) * hn[:, sl]
                y_scr[rows, CONV_WIDTH + lo:CONV_WIDTH + lo + HEAD_DIM] = \
                    (o * (g * _sigmoid(g))).astype(BF16)

    o_ref[0] = x + _dot(y_scr[...], wout_ref[...])


def _mixer(x, mix_norm, w_in, conv_w, hgrn_lb, hgrn_norm, w_out):
    bsz, seq, d = x.shape
    ts = SEQ_TILE
    const = lambda b, j: (0, 0)
    return pl.pallas_call(
        _mixer_kernel,
        grid=(bsz, seq // ts),
        in_specs=[
            pl.BlockSpec((1, ts, d), lambda b, j: (b, j, 0)),
            pl.BlockSpec((1, d), const),
            pl.BlockSpec((d, PROJ_WIDTH), const),
            pl.BlockSpec((3, CONV_WIDTH), const),
            pl.BlockSpec((2, HGRN_WIDTH), const),
            pl.BlockSpec((1, HGRN_WIDTH), const),
            pl.BlockSpec((d, d), const),
            pl.BlockSpec((CHUNK, CHUNK), const),
        ],
        out_specs=pl.BlockSpec((1, ts, d), lambda b, j: (b, j, 0)),
        out_shape=jax.ShapeDtypeStruct((bsz, seq, d), F32),
        scratch_shapes=[
            pltpu.VMEM((ts, PROJ_WIDTH), F32),
            pltpu.VMEM((ts, d), BF16),
            pltpu.VMEM((HGRN_HEADS, HEAD_DIM, HEAD_DIM), F32),
            pltpu.VMEM((8, CONV_WIDTH), F32),
        ],
        compiler_params=pltpu.CompilerParams(
            dimension_semantics=("arbitrary", "arbitrary"), vmem_limit_bytes=VMEM_LIMIT),
        name="mixer",
    )(x, mix_norm, w_in, conv_w, hgrn_lb, hgrn_norm, w_out, jnp.asarray(_split_levels()))


def _kv_kernel(m_ref, g_ref, w_ref, o_ref):
    h = _rms(m_ref[0], g_ref[...]).astype(BF16)
    o_ref[0] = _dot(h, w_ref[...]).astype(BF16)


def _kv_proj(mem, mem_norm, w_kv):
    bsz, mlen, d = mem.shape
    const = lambda b: (0, 0)
    return pl.pallas_call(
        _kv_kernel,
        grid=(bsz,),
        in_specs=[
            pl.BlockSpec((1, mlen, d), lambda b: (b, 0, 0)),
            pl.BlockSpec((1, d), const),
            pl.BlockSpec((d, 2 * d), const),
        ],
        out_specs=pl.BlockSpec((1, mlen, 2 * d), lambda b: (b, 0, 0)),
        out_shape=jax.ShapeDtypeStruct((bsz, mlen, 2 * d), BF16),
        compiler_params=pltpu.CompilerParams(
            dimension_semantics=("arbitrary",), vmem_limit_bytes=VMEM_LIMIT),
        name="kv_proj",
    )(mem, mem_norm, w_kv)


def _first_argmax(vals, lane):
    mx = jnp.max(vals, axis=-1, keepdims=True)
    idx = jnp.min(jnp.where(vals == mx, lane, float(LANES)), axis=-1, keepdims=True)
    return mx, idx


def _router_logits(h3, wr_ref, br_ref):
    h3_hi = h3.astype(BF16)
    h3_lo = (h3 - h3_hi.astype(F32)).astype(BF16)
    two = _dot(h3_hi, wr_ref[...])
    return (two[:, :LANES] + two[:, LANES:]) + _dot(h3_lo, wr_ref[:, :LANES]) + br_ref[...]


def _route(logits, running, tri):
    lane = lax.broadcasted_iota(jnp.int32, logits.shape, 1).astype(F32)
    neg = jnp.float32(-jnp.inf)
    gl = jnp.where(lane < N_GROUPS, logits, neg)
    _, gidx = _first_argmax(gl, lane)
    base = N_GROUPS + EXPERTS_PER_GROUP * gidx
    el = jnp.where((lane >= base) & (lane < base + EXPERTS_PER_GROUP), logits, neg)
    _, i1 = _first_argmax(el, lane)
    _, i2 = _first_argmax(jnp.where(lane == i1, neg, el), lane)
    lo = jnp.minimum(i1, i2) - base
    hi = jnp.maximum(i1, i2) - base
    bucket = gidx * N_PAIRS + (lo * (7.0 - lo)) * 0.5 + hi - lo - 1.0

    onehot = lane == bucket
    ranks = []
    for r0 in range(0, logits.shape[0], RANK_BLOCK):
        oh = onehot[r0:r0 + RANK_BLOCK]
        before = _dot(tri, oh.astype(BF16))
        ranks.append(jnp.sum(jnp.where(oh, before + running, 0.0), axis=-1, keepdims=True))
        running = running + jnp.sum(oh.astype(F32), axis=0, keepdims=True)
    rank = jnp.concatenate(ranks, axis=0)
    info = jnp.where(lane == 0, bucket, jnp.where(lane == 1, rank, 0.0))
    return info, running


def _gates(logits, e_lo, e_hi):
    lane = lax.broadcasted_iota(jnp.int32, logits.shape, 1)
    neg = jnp.float32(-jnp.inf)
    gl = jnp.where(lane < N_GROUPS, logits, neg)
    g_p = 1.0 / jnp.sum(jnp.exp(gl - jnp.max(gl, axis=-1, keepdims=True)), axis=-1, keepdims=True)
    l_lo = jnp.sum(jnp.where(lane == N_GROUPS + e_lo, logits, 0.0), axis=-1, keepdims=True)
    l_hi = jnp.sum(jnp.where(lane == N_GROUPS + e_hi, logits, 0.0), axis=-1, keepdims=True)
    m = jnp.maximum(l_lo, l_hi)
    p_lo = jnp.exp(l_lo - m)
    p_hi = jnp.exp(l_hi - m)
    inv = g_p / (p_lo + p_hi)
    return p_lo * inv, p_hi * inv


def _attn_kernel(x_ref, kv_ref, gx_ref, wq_ref, wo_ref, gf_ref, wr_ref, br_ref, tri_ref,
                 x2r_ref, brk_ref, counts_ref, o_scr, cnt_scr):
    @pl.when((pl.program_id(0) == 0) & (pl.program_id(1) == 0))
    def _():
        cnt_scr[...] = jnp.zeros_like(cnt_scr)

    x = x_ref[0]
    tq = x.shape[0]
    h = _rms(x, gx_ref[...]).astype(BF16)
    q = _dot(h, wq_ref[...])
    for hd in range(XATTN_HEADS):
        lo = hd * XATTN_HEAD_DIM
        qh = q[:, lo:lo + XATTN_HEAD_DIM].astype(BF16)
        kh = kv_ref[0, :, lo:lo + XATTN_HEAD_DIM]
        vh = kv_ref[0, :, D_MODEL + lo:D_MODEL + lo + XATTN_HEAD_DIM]
        s = _dot_nt(qh, kh)
        p = jnp.exp(s - jnp.max(s, axis=-1, keepdims=True))
        inv = 1.0 / jnp.sum(p, axis=-1, keepdims=True)
        o_scr[:, lo:lo + XATTN_HEAD_DIM] = (_dot(p.astype(BF16), vh) * inv).astype(BF16)
    x2 = x + _dot(o_scr[...], wo_ref[...])
    for s in range(ROW_TILES):
        x2r_ref[pl.ds(s, tq, stride=ROW_TILES), :] = x2[:, s * LANES:(s + 1) * LANES]
    logits = _router_logits(_rms(x2, gf_ref[...]), wr_ref, br_ref)
    info, running = _route(logits, cnt_scr[...], tri_ref[...])
    cnt_scr[...] = running
    counts_ref[...] = running
    brk_ref[...] = info.T[0:8, :].astype(jnp.int32)


def _attention(x1, kv, xattn_norm, w_q, w_o, ffn_norm, w_router, b_router):
    bsz, seq, d = x1.shape
    mlen = kv.shape[1]
    tq = ATTN_TILE
    nj = seq // tq
    const = lambda b, j: (0, 0)
    tile = lambda b, j: (b, j, 0)
    tri = jnp.tri(RANK_BLOCK, RANK_BLOCK, -1, dtype=BF16)
    return pl.pallas_call(
        _attn_kernel,
        grid=(bsz, nj),
        in_specs=[
            pl.BlockSpec((1, tq, d), tile),
            pl.BlockSpec((1, mlen, 2 * d), lambda b, j: (b, 0, 0)),
            pl.BlockSpec((1, d), const),
            pl.BlockSpec((d, d), const),
            pl.BlockSpec((d, d), const),
            pl.BlockSpec((1, d), const),
            pl.BlockSpec((d, 2 * LANES), const),
            pl.BlockSpec((1, LANES), const),
            pl.BlockSpec((RANK_BLOCK, RANK_BLOCK), const),
        ],
        out_specs=[
            pl.BlockSpec((tq * ROW_TILES, LANES), lambda b, j: (b * nj + j, 0)),
            pl.BlockSpec((8, tq), lambda b, j: (0, b * nj + j)),
            pl.BlockSpec((1, LANES), const),
        ],
        out_shape=[
            jax.ShapeDtypeStruct((bsz * seq * ROW_TILES, LANES), F32),
            jax.ShapeDtypeStruct((8, bsz * seq), jnp.int32),
            jax.ShapeDtypeStruct((1, LANES), F32),
        ],
        scratch_shapes=[pltpu.VMEM((tq, d), BF16), pltpu.VMEM((1, LANES), F32)],
        compiler_params=pltpu.CompilerParams(
            dimension_semantics=("arbitrary", "arbitrary"), vmem_limit_bytes=VMEM_LIMIT),
        name="xattn_router",
    )(x1, kv, xattn_norm, w_q, w_o, ffn_norm, w_router, b_router, tri)


def _item_copy(hbm, hbm_item, buf, buf_item, rows_per_item, sem, to_hbm):
    h = hbm.at[pl.ds(pl.multiple_of(hbm_item * rows_per_item, rows_per_item), rows_per_item)]
    b = buf.at[pl.ds(pl.multiple_of(buf_item * rows_per_item, rows_per_item), rows_per_item)]
    return pltpu.make_async_copy(b, h, sem) if to_hbm else pltpu.make_async_copy(h, b, sem)


def _items_wait(hbm, buf, buf_item, n_items, rows_per_item, sem, to_hbm):
    n = n_items * rows_per_item
    h = hbm.at[pl.ds(0, n)]
    b = buf.at[pl.ds(pl.multiple_of(buf_item * rows_per_item, rows_per_item), n)]
    (pltpu.make_async_copy(b, h, sem) if to_hbm else pltpu.make_async_copy(h, b, sem)).wait()


def _dispatch_kernel(pos_ref, fill_ref, ntiles_ref, x2r_ref, xs_hbm, buf, sem):
    i = pl.program_id(0)
    nt = pl.num_programs(0)
    td = DISPATCH_TILE
    rpi = ROW_TILES
    slot = i % 2
    base = slot * td

    @pl.when(i >= 2)
    def _():
        _items_wait(xs_hbm, buf, base, td, rpi, sem.at[slot], True)

    buf[pl.ds(pl.multiple_of(base * rpi, td * rpi), td * rpi), :] = x2r_ref[...]

    def send(r2, carry):
        for k in range(DMA_QUEUES):
            r = r2 * DMA_QUEUES + k
            _item_copy(xs_hbm, pos_ref[i * td + r], buf, base + r, rpi, sem.at[slot],
                       True).start(priority=k)
        return carry

    lax.fori_loop(0, td // DMA_QUEUES, send, 0, unroll=4)

    @pl.when(i == nt - 1)
    def _():
        _items_wait(xs_hbm, buf, base, td, rpi, sem.at[slot], True)
        _items_wait(xs_hbm, buf, (1 - slot) * td, td, rpi, sem.at[1 - slot], True)
        buf[...] = jnp.zeros_like(buf)
        zsem = sem.at[2]
        for b in range(N_BUCKETS):
            def fill(p, carry):
                _item_copy(xs_hbm, p, buf, 0, rpi, zsem, True).start()
                return carry

            def drain(p, carry):
                _item_copy(xs_hbm, p, buf, 0, rpi, zsem, True).wait()
                return carry

            lax.fori_loop(fill_ref[b], fill_ref[N_BUCKETS + b], fill, 0)
            lax.fori_loop(fill_ref[b], fill_ref[N_BUCKETS + b], drain, 0)

        n_tiles_total = xs_hbm.shape[0] // (MOE_TILE * rpi)

        def tile_copy(tile):
            return pltpu.make_async_copy(
                buf.at[pl.ds(0, MOE_TILE * rpi)],
                xs_hbm.at[pl.ds(pl.multiple_of(tile * MOE_TILE * rpi, MOE_TILE * rpi),
                                MOE_TILE * rpi)], zsem)

        def fill_tile(tile, carry):
            tile_copy(tile).start()
            return carry

        def drain_tile(tile, carry):
            tile_copy(tile).wait()
            return carry

        lax.fori_loop(ntiles_ref[0], n_tiles_total, fill_tile, 0)
        lax.fori_loop(ntiles_ref[0], n_tiles_total, drain_tile, 0)


def _dispatch(pos, fill, ntiles, x2r, n_sorted):
    t = x2r.shape[0] // ROW_TILES
    td = DISPATCH_TILE
    grid_spec = pltpu.PrefetchScalarGridSpec(
        num_scalar_prefetch=3,
        grid=(t // td,),
        in_specs=[pl.BlockSpec((td * ROW_TILES, LANES), lambda i, *_: (i, 0))],
        out_specs=pl.BlockSpec(memory_space=pl.ANY),
        scratch_shapes=[
            pltpu.VMEM((2 * td * ROW_TILES, LANES), F32),
            pltpu.SemaphoreType.DMA((3,)),
        ],
    )
    return pl.pallas_call(
        _dispatch_kernel,
        grid_spec=grid_spec,
        out_shape=jax.ShapeDtypeStruct((n_sorted * ROW_TILES, LANES), F32),
        compiler_params=pltpu.CompilerParams(
            dimension_semantics=("arbitrary",), vmem_limit_bytes=VMEM_LIMIT),
        name="moe_dispatch",
    )(pos, fill, ntiles, x2r)


def _silu(x):
    return x * _sigmoid(x)


def _moe_kernel(elo_ref, ehi_ref, used_ref,
                xs_ref, gf_ref, wr_ref, br_ref, gfin_ref,
                wgu_lo, wd_lo, wgu_hi, wd_hi, y_ref):
    i = pl.program_id(0)
    tm = MOE_TILE

    @pl.when(used_ref[i] == 1)
    def _():
        x2 = jnp.concatenate(
            [xs_ref[pl.ds(s, tm, stride=ROW_TILES), :] for s in range(ROW_TILES)], axis=1)
        h3 = _rms(x2, gf_ref[...])
        g_lo, g_hi = _gates(_router_logits(h3, wr_ref, br_ref), elo_ref[i], ehi_ref[i])
        x = h3.astype(BF16)

        def expert(wgu, wd):
            gu = _dot(x, wgu[0])
            hid = (_silu(gu[:, :D_EXPERT]) * gu[:, D_EXPERT:]).astype(BF16)
            return _dot(hid, wd[0])

        moe = g_lo * expert(wgu_lo, wd_lo) + g_hi * expert(wgu_hi, wd_hi)
        out = _rms(x2 + moe, gfin_ref[...])
        for s in range(ROW_TILES):
            y_ref[pl.ds(s, tm, stride=ROW_TILES), :] = out[:, s * LANES:(s + 1) * LANES]

    @pl.when(used_ref[i] == 0)
    def _():
        y_ref[...] = jnp.zeros_like(y_ref)


def _moe_sparse(e_lo, e_hi, used, xs, ffn_norm, w_router, b_router, final_norm,
                w_gate_up, w_down):
    tm = MOE_TILE
    nt = used.shape[0]
    d = D_MODEL
    lo = lambda i, elo, ehi, used: (elo[i], 0, 0)
    hi = lambda i, elo, ehi, used: (ehi[i], 0, 0)
    const = lambda i, *_: (0, 0)
    grid_spec = pltpu.PrefetchScalarGridSpec(
        num_scalar_prefetch=3,
        grid=(nt,),
        in_specs=[
            pl.BlockSpec((tm * ROW_TILES, LANES), lambda i, *_: (i, 0)),
            pl.BlockSpec((1, d), const),
            pl.BlockSpec((d, 2 * LANES), const),
            pl.BlockSpec((1, LANES), const),
            pl.BlockSpec((1, d), const),
            pl.BlockSpec((1, d, 2 * D_EXPERT), lo),
            pl.BlockSpec((1, D_EXPERT, d), lo),
            pl.BlockSpec((1, d, 2 * D_EXPERT), hi),
            pl.BlockSpec((1, D_EXPERT, d), hi),
        ],
        out_specs=pl.BlockSpec((tm * ROW_TILES, LANES), lambda i, *_: (i, 0)),
    )
    return pl.pallas_call(
        _moe_kernel,
        grid_spec=grid_spec,
        out_shape=jax.ShapeDtypeStruct((nt * tm * ROW_TILES, LANES), F32),
        compiler_params=pltpu.CompilerParams(
            dimension_semantics=("arbitrary",), vmem_limit_bytes=VMEM_LIMIT),
        name="moe_sparse",
    )(e_lo, e_hi, used, xs, ffn_norm, w_router, b_router, final_norm,
      w_gate_up, w_down, w_gate_up, w_down)


def _final_kernel(pos_ref, y_hbm, o_ref, ybuf, sem):
    i = pl.program_id(0)
    nt = pl.num_programs(0)
    tf = FINAL_TILE

    def start(tile, slot):
        def fetch(r2, carry):
            for k in range(DMA_QUEUES):
                r = r2 * DMA_QUEUES + k
                _item_copy(y_hbm, pos_ref[tile * tf + r], ybuf, slot * tf + r, ROW_TILES,
                           sem.at[slot], False).start(priority=k)
            return carry

        lax.fori_loop(0, tf // DMA_QUEUES, fetch, 0, unroll=4)

    @pl.when(i == 0)
    def _():
        start(0, 0)

    @pl.when(i + 1 < nt)
    def _():
        start(i + 1, (i + 1) % 2)

    slot = i % 2
    _items_wait(y_hbm, ybuf, slot * tf, tf, ROW_TILES, sem.at[slot], False)
    row0 = pl.multiple_of(slot * tf * ROW_TILES, ROW_TILES)
    for s in range(ROW_TILES):
        o_ref[:, s * LANES:(s + 1) * LANES] = ybuf[pl.ds(row0 + s, tf, stride=ROW_TILES), :]


def _final(pos, y):
    t = pos.shape[0]
    d = D_MODEL
    tf = FINAL_TILE
    grid_spec = pltpu.PrefetchScalarGridSpec(
        num_scalar_prefetch=1,
        grid=(t // tf,),
        in_specs=[pl.BlockSpec(memory_space=pl.ANY)],
        out_specs=pl.BlockSpec((tf, d), lambda i, *_: (i, 0)),
        scratch_shapes=[
            pltpu.VMEM((2 * tf * ROW_TILES, LANES), F32),
            pltpu.SemaphoreType.DMA((2,)),
        ],
    )
    return pl.pallas_call(
        _final_kernel,
        grid_spec=grid_spec,
        out_shape=jax.ShapeDtypeStruct((t, d), F32),
        compiler_params=pltpu.CompilerParams(
            dimension_semantics=("arbitrary",), vmem_limit_bytes=VMEM_LIMIT),
        name="moe_unpermute",
    )(pos, y)


def _positions_kernel(starts_ref, brk_ref, pos_ref):
    bucket = brk_ref[0:1, :]
    pos = brk_ref[1:2, :]
    for b in range(N_BUCKETS):
        pos = pos + jnp.where(bucket == b, starts_ref[b], 0)
    pos_ref[...] = pos


def _positions(starts, brk):
    t = brk.shape[1]
    grid_spec = pltpu.PrefetchScalarGridSpec(
        num_scalar_prefetch=1,
        grid=(1,),
        in_specs=[pl.BlockSpec(brk.shape, lambda i, starts: (0, 0))],
        out_specs=pl.BlockSpec((1, t), lambda i, starts: (0, 0)),
    )
    return pl.pallas_call(
        _positions_kernel,
        grid_spec=grid_spec,
        out_shape=jax.ShapeDtypeStruct((1, t), jnp.int32),
        name="moe_positions",
    )(starts, brk).reshape(t)


def _routing_tables(counts, brk):
    tm = MOE_TILE
    t = brk.shape[1]
    nt = t // tm + N_BUCKETS
    cnt = counts[0, :N_BUCKETS].astype(jnp.int32)
    padded = ((cnt + tm - 1) // tm) * tm
    ends = jnp.cumsum(padded)
    starts = ends - padded
    pos = _positions(starts, brk)
    fill = jnp.concatenate([starts + cnt, ends])
    ntiles = ends[-1:] // tm
    tile_start = jnp.arange(nt, dtype=jnp.int32) * tm
    tile_bucket = jnp.sum((ends[None, :] <= tile_start[:, None]).astype(jnp.int32), axis=1)
    used = (tile_bucket < N_BUCKETS).astype(jnp.int32)
    tile_bucket = jnp.minimum(tile_bucket, N_BUCKETS - 1)
    pair = tile_bucket % N_PAIRS
    group0 = (tile_bucket // N_PAIRS) * EXPERTS_PER_GROUP
    e_lo = group0 + jnp.array([0, 0, 0, 1, 1, 2], jnp.int32)[pair]
    e_hi = group0 + jnp.array([1, 2, 3, 2, 3, 3], jnp.int32)[pair]
    return pos, fill, ntiles, e_lo, e_hi, used, nt


def kernel(x, mem, mix_norm, w_in, conv_w, hgrn_lb, hgrn_norm, w_out, xattn_norm, mem_norm,
           w_q, w_kv, w_o, ffn_norm, w_group, b_group, w_expert, b_expert, w_gate, w_up,
           w_down, final_norm):
    bsz, seq, d = x.shape
    assert d == D_MODEL and seq % SEQ_TILE == 0 and seq % ATTN_TILE == 0
    assert (bsz * seq) % MOE_TILE == 0 and (bsz * seq) % FINAL_TILE == 0
    assert (bsz * seq) % DISPATCH_TILE == 0 and bsz * seq >= 2 * DISPATCH_TILE
    assert 2 * DISPATCH_TILE >= MOE_TILE and mix_norm.shape[0] == 1
    bf = lambda w: w.astype(BF16)

    x1 = _mixer(x, mix_norm, bf(w_in[0]), conv_w[0], hgrn_lb, hgrn_norm, bf(w_out[0]))
    kv = _kv_proj(mem, mem_norm, bf(w_kv[0]))

    pad = LANES - N_GROUPS - N_EXPERTS
    w_router = jnp.concatenate(
        [w_group[0], w_expert[0], jnp.zeros((d, pad), F32)], axis=1)
    b_router = jnp.concatenate(
        [b_group[0], b_expert[0], jnp.zeros((pad,), F32)])[None, :]
    w_router_hi = bf(w_router)
    w_router_lo = bf(w_router - w_router_hi.astype(F32))
    w_router2 = jnp.concatenate([w_router_hi, w_router_lo], axis=1)
    w_q_scaled = bf(w_q[0] * (XATTN_HEAD_DIM ** -0.5))
    x2r, brk, counts = _attention(x1, kv, xattn_norm, w_q_scaled, bf(w_o[0]), ffn_norm,
                                  w_router2, b_router)

    pos, fill, ntiles, e_lo, e_hi, used, nt = _routing_tables(counts, brk)
    xs = _dispatch(pos, fill, ntiles, x2r, nt * MOE_TILE)
    w_gate_up = jnp.concatenate([bf(w_gate[0]), bf(w_up[0])], axis=-1)
    y = _moe_sparse(e_lo, e_hi, used, xs, ffn_norm, w_router2, b_router, final_norm[None, :],
                    w_gate_up, bf(w_down[0]))
    return _final(pos, y).reshape(bsz, seq, d)
```

```python
import jax
import jax.numpy as jnp
from jax import lax
from jax.experimental import pallas as pl
from jax.experimental.pallas import tpu as pltpu

F32 = jnp.float32
BF16 = jnp.bfloat16

D_MODEL = 1024
CONV_WIDTH = 512
HGRN_WIDTH = 512
HGRN_HEADS = 4
HEAD_DIM = 128
N_PROJ_SLOTS = 7
PROJ_WIDTH = N_PROJ_SLOTS * 512
XATTN_HEADS = 4
XATTN_HEAD_DIM = 256
N_GROUPS = 4
EXPERTS_PER_GROUP = 4
N_EXPERTS = 16
D_EXPERT = 512
EPS = 1e-6

LANES = 128
SUBLANES = 8
CHUNK = 64
CHUNK_LEVELS = 6
HGRN_TILE = 4 * CHUNK
SEQ_TILE = 512
ATTN_TILE = 1024
RANK_BLOCK = 256
DISPATCH_TILE = 1024
MOE_TILE = 512
FINAL_TILE = 1024
N_PAIRS = 6
N_BUCKETS = N_GROUPS * N_PAIRS
ROW_TILES = D_MODEL // LANES
DMA_QUEUES = 2
VMEM_LIMIT = 56 * 1024 * 1024


def _rms(x, g):
    return x * lax.rsqrt(jnp.mean(x * x, axis=-1, keepdims=True) + EPS) * g


def _dot(a, b):
    return jnp.dot(a, b, preferred_element_type=F32)


def _dot_nt(a, b):
    return lax.dot_general(a, b, (((1,), (1,)), ((), ())), preferred_element_type=F32)


def _dot_tn(a, b):
    return lax.dot_general(a, b, (((0,), (0,)), ((), ())), preferred_element_type=F32)


def _roll_rows(x, shift):
    return pltpu.roll(x, shift % x.shape[0], axis=0)


def _level_exponents(logf2, use_level):
    n = CHUNK // SUBLANES
    sub = lax.broadcasted_iota(jnp.int32, (SUBLANES, logf2.shape[1]), 0)
    roll = lambda x, s: pltpu.roll(x, s % SUBLANES, axis=0)
    r = [logf2[SUBLANES * j:SUBLANES * (j + 1), :] for j in range(n)]
    for lvl in range(1, CHUNK_LEVELS + 1):
        half = 1 << (lvl - 1)
        g = [None] * n
        if half < SUBLANES:
            second = (sub & half) != 0
            for j in range(n):
                last = r[j]
                w = 1
                while w < half:
                    last = jnp.where((sub & w) != 0, last, roll(last, -w))
                    w *= 2
                tot = jnp.where(second, roll(last, half), last)
                g[j] = jnp.where(second, r[j], tot - r[j])
                r[j] = jnp.where(second, r[j] + tot, r[j])
        else:
            hv = half // SUBLANES
            for j0 in range(0, n, 2 * hv):
                mid = r[j0 + hv - 1]
                tot = jnp.broadcast_to(mid[SUBLANES - 1:SUBLANES, :], mid.shape)
                for j in range(j0, j0 + hv):
                    g[j] = tot - r[j]
                for j in range(j0 + hv, j0 + 2 * hv):
                    g[j] = r[j]
                    r[j] = r[j] + tot
        use_level(lvl, jnp.concatenate(g, axis=0))
    return jnp.concatenate(r, axis=0)


def _sigmoid(x):
    return 0.5 * jnp.tanh(0.5 * x) + 0.5


def _split_levels():
    import numpy as np
    t = np.arange(CHUNK)[:, None]
    s = np.arange(CHUNK)[None, :]
    msb = np.floor(np.log2(np.maximum(t ^ s, 1))).astype(np.int32) + 1
    return np.where(s < t, msb, np.where(s == t, 0, -1)).astype(np.int32)


def _hgrn_chunk(q, z, v, lb, levels):
    half_th = 0.5 * jnp.tanh(0.5 * z)
    one_m_lb = 1.0 - lb
    logf2 = jnp.log2(lb + one_m_lb * (0.5 + half_th))
    k = one_m_lb * (0.5 - half_th)

    qb = q.astype(BF16)
    kb = k.astype(BF16)
    scores = [jnp.where(levels == 0, jnp.sum(q * k, axis=-1, keepdims=True), 0.0)]

    def use_level(lvl, g):
        decay = jnp.exp2(g).astype(BF16)
        scores[0] = jnp.where(levels == lvl, _dot_nt(qb * decay, kb * decay), scores[0])

    b2 = _level_exponents(logf2, use_level)
    a = scores[0]
    b2_last = b2[CHUNK - 1:CHUNK, :]

    qe = (q * jnp.exp2(b2)).astype(BF16)
    kd = (k * jnp.exp2(b2_last - b2)).astype(BF16)
    return a, qe, kd, v.astype(BF16), b2_last


def _hgrn_tile(chunks, st):
    (a0, qe0, kd0, v0, bl0), (a1, qe1, kd1, v1, bl1), (a2, qe2, kd2, v2, bl2), \
        (a3, qe3, kd3, v3, bl3) = chunks
    scale = lambda x, log2_decay: x * jnp.exp2(log2_decay).astype(BF16)
    bf = lambda x: x.astype(BF16)
    cum1 = bl0
    cum2 = cum1 + bl1
    cum3 = cum2 + bl2
    cum4 = cum3 + bl3
    s10 = bf(_dot_nt(qe1, kd0))
    s32 = bf(_dot_nt(qe3, kd2))
    s8 = bf(_dot_nt(jnp.concatenate([qe2, scale(qe3, bl2)], axis=0),
                    jnp.concatenate([scale(kd0, bl1), kd1], axis=0)))
    v01 = jnp.concatenate([v0, v1], axis=0)
    stb = bf(st)
    o0 = _dot(bf(a0), v0) + _dot_nt(qe0, stb)
    o1 = _dot(bf(a1), v1) + _dot(s10, v0) + _dot_nt(scale(qe1, cum1), stb)
    o2 = _dot(bf(a2), v2) + _dot(s8[:CHUNK], v01) + _dot_nt(scale(qe2, cum2), stb)
    o3 = (_dot(bf(a3), v3) + _dot(s32, v2) + _dot(s8[CHUNK:], v01)
          + _dot_nt(scale(qe3, cum3), stb))
    kd_all = jnp.concatenate(
        [scale(kd0, cum4 - cum1), scale(kd1, cum4 - cum2), scale(kd2, bl3), kd3], axis=0)
    v_all = jnp.concatenate([v0, v1, v2, v3], axis=0)
    st_new = st * jnp.exp2(cum4) + _dot_tn(v_all, kd_all)
    return [o0, o1, o2, o3], st_new


def _mixer_kernel(x_ref, gmix_ref, win_ref, convw_ref, lbraw_ref, hnorm_ref, wout_ref, lvl_ref,
                  o_ref, p_scr, y_scr, st_scr, tail_scr):
    j = pl.program_id(1)

    @pl.when(j == 0)
    def _():
        st_scr[...] = jnp.zeros_like(st_scr)
        tail_scr[...] = jnp.zeros_like(tail_scr)

    x = x_ref[0]
    h = _rms(x, gmix_ref[...]).astype(BF16)
    p_scr[...] = _dot(h, win_ref[...])

    ts = x.shape[0]
    cb = p_scr[:, 0:CONV_WIDTH]
    u = p_scr[:, CONV_WIDTH:2 * CONV_WIDTH] * p_scr[:, 2 * CONV_WIDTH:3 * CONV_WIDTH]
    row = lax.broadcasted_iota(jnp.int32, u.shape, 0)
    prev1 = tail_scr[7:8, :]
    prev2 = tail_scr[6:7, :]
    u1 = jnp.where(row == 0, prev1, _roll_rows(u, 1))
    u2 = jnp.where(row == 0, prev2, jnp.where(row == 1, prev1, _roll_rows(u, 2)))
    cw = convw_ref[...]
    conv = u2 * cw[0:1, :] + u1 * cw[1:2, :] + u * cw[2:3, :]
    y_scr[:, 0:CONV_WIDTH] = (cb * conv).astype(BF16)
    tail_scr[...] = u[ts - 8:ts, :]

    raw = lbraw_ref[...]
    mx = jnp.max(raw, axis=0, keepdims=True)
    ex = jnp.exp(raw - mx)
    lb_all = ex[0:1, :] / jnp.sum(ex, axis=0, keepdims=True)
    hn = hnorm_ref[...]

    for hd in range(HGRN_HEADS):
        lo = hd * HEAD_DIM
        sl = slice(lo, lo + HEAD_DIM)
        col = lambda slot: slice(slot * 512 + lo, slot * 512 + lo + HEAD_DIM)
        for t0 in range(0, ts, HGRN_TILE):
            chunks = []
            for r0 in range(t0, t0 + HGRN_TILE, CHUNK):
                rows = slice(r0, r0 + CHUNK)
                chunks.append(_hgrn_chunk(p_scr[rows, col(3)], p_scr[rows, col(4)],
                                          p_scr[rows, col(5)], lb_all[:, sl], lvl_ref[...]))
            outs, st_new = _hgrn_tile(chunks, st_scr[hd])
            st_scr[hd] = st_new
            for c, o in enumerate(outs):
                rows = slice(t0 + c * CHUNK, t0 + (c + 1) * CHUNK)
                g = p_scr[rows, col(6)]
                o = o * lax.rsqrt(jnp.mean(o * o, axis=-1, keepdims=True) + EPS) * hn[:, sl]
                y_scr[rows, CONV_WIDTH + lo:CONV_WIDTH + lo + HEAD_DIM] = \
                    (o * (g * _sigmoid(g))).astype(BF16)

    o_ref[0] = x + _dot(y_scr[...], wout_ref[...])


def _mixer(x, mix_norm, w_in, conv_w, hgrn_lb, hgrn_norm, w_out):
    bsz, seq, d = x.shape
    ts = SEQ_TILE
    const = lambda b, j: (0, 0)
    return pl.pallas_call(
        _mixer_kernel,
        grid=(bsz, seq // ts),
        in_specs=[
            pl.BlockSpec((1, ts, d), lambda b, j: (b, j, 0)),
            pl.BlockSpec((1, d), const),
            pl.BlockSpec((d, PROJ_WIDTH), const),
            pl.BlockSpec((3, CONV_WIDTH), const),
            pl.BlockSpec((2, HGRN_WIDTH), const),
            pl.BlockSpec((1, HGRN_WIDTH), const),
            pl.BlockSpec((d, d), const),
            pl.BlockSpec((CHUNK, CHUNK), const),
        ],
        out_specs=pl.BlockSpec((1, ts, d), lambda b, j: (b, j, 0)),
        out_shape=jax.ShapeDtypeStruct((bsz, seq, d), F32),
        scratch_shapes=[
            pltpu.VMEM((ts, PROJ_WIDTH), F32),
            pltpu.VMEM((ts, d), BF16),
            pltpu.VMEM((HGRN_HEADS, HEAD_DIM, HEAD_DIM), F32),
            pltpu.VMEM((8, CONV_WIDTH), F32),
        ],
        compiler_params=pltpu.CompilerParams(
            dimension_semantics=("arbitrary", "arbitrary"), vmem_limit_bytes=VMEM_LIMIT),
        name="mixer",
    )(x, mix_norm, w_in, conv_w, hgrn_lb, hgrn_norm, w_out, jnp.asarray(_split_levels()))


def _kv_kernel(m_ref, g_ref, w_ref, o_ref):
    h = _rms(m_ref[0], g_ref[...]).astype(BF16)
    o_ref[0] = _dot(h, w_ref[...]).astype(BF16)


def _kv_proj(mem, mem_norm, w_kv):
    bsz, mlen, d = mem.shape
    const = lambda b: (0, 0)
    return pl.pallas_call(
        _kv_kernel,
        grid=(bsz,),
        in_specs=[
            pl.BlockSpec((1, mlen, d), lambda b: (b, 0, 0)),
            pl.BlockSpec((1, d), const),
            pl.BlockSpec((d, 2 * d), const),
        ],
        out_specs=pl.BlockSpec((1, mlen, 2 * d), lambda b: (b, 0, 0)),
        out_shape=jax.ShapeDtypeStruct((bsz, mlen, 2 * d), BF16),
        compiler_params=pltpu.CompilerParams(
            dimension_semantics=("arbitrary",), vmem_limit_bytes=VMEM_LIMIT),
        name="kv_proj",
    )(mem, mem_norm, w_kv)


def _first_argmax(vals, lane):
    mx = jnp.max(vals, axis=-1, keepdims=True)
    idx = jnp.min(jnp.where(vals == mx, lane, float(LANES)), axis=-1, keepdims=True)
    return mx, idx


def _router_logits(h3, wr_ref, br_ref):
    h3_hi = h3.astype(BF16)
    h3_lo = (h3 - h3_hi.astype(F32)).astype(BF16)
    two = _dot(h3_hi, wr_ref[...])
    return (two[:, :LANES] + two[:, LANES:]) + _dot(h3_lo, wr_ref[:, :LANES]) + br_ref[...]


def _route(logits, running, tri):
    lane = lax.broadcasted_iota(jnp.int32, logits.shape, 1).astype(F32)
    neg = jnp.float32(-jnp.inf)
    gl = jnp.where(lane < N_GROUPS, logits, neg)
    _, gidx = _first_argmax(gl, lane)
    base = N_GROUPS + EXPERTS_PER_GROUP * gidx
    el = jnp.where((lane >= base) & (lane < base + EXPERTS_PER_GROUP), logits, neg)
    _, i1 = _first_argmax(el, lane)
    _, i2 = _first_argmax(jnp.where(lane == i1, neg, el), lane)
    lo = jnp.minimum(i1, i2) - base
    hi = jnp.maximum(i1, i2) - base
    bucket = gidx * N_PAIRS + (lo * (7.0 - lo)) * 0.5 + hi - lo - 1.0

    onehot = lane == bucket
    ranks = []
    for r0 in range(0, logits.shape[0], RANK_BLOCK):
        oh = onehot[r0:r0 + RANK_BLOCK]
        before = _dot(tri, oh.astype(BF16))
        ranks.append(jnp.sum(jnp.where(oh, before + running, 0.0), axis=-1, keepdims=True))
        running = running + jnp.sum(oh.astype(F32), axis=0, keepdims=True)
    rank = jnp.concatenate(ranks, axis=0)
    info = jnp.where(lane == 0, bucket, jnp.where(lane == 1, rank, 0.0))
    return info, running


def _gates(logits, e_lo, e_hi):
    lane = lax.broadcasted_iota(jnp.int32, logits.shape, 1)
    neg = jnp.float32(-jnp.inf)
    gl = jnp.where(lane < N_GROUPS, logits, neg)
    g_p = 1.0 / jnp.sum(jnp.exp(gl - jnp.max(gl, axis=-1, keepdims=True)), axis=-1, keepdims=True)
    l_lo = jnp.sum(jnp.where(lane == N_GROUPS + e_lo, logits, 0.0), axis=-1, keepdims=True)
    l_hi = jnp.sum(jnp.where(lane == N_GROUPS + e_hi, logits, 0.0), axis=-1, keepdims=True)
    m = jnp.maximum(l_lo, l_hi)
    p_lo = jnp.exp(l_lo - m)
    p_hi = jnp.exp(l_hi - m)
    inv = g_p / (p_lo + p_hi)
    return p_lo * inv, p_hi * inv


def _attn_kernel(x_ref, kv_ref, gx_ref, wq_ref, wo_ref, gf_ref, wr_ref, br_ref, tri_ref,
                 x2r_ref, brk_ref, counts_ref, o_scr, cnt_scr):
    @pl.when((pl.program_id(0) == 0) & (pl.program_id(1) == 0))
    def _():
        cnt_scr[...] = jnp.zeros_like(cnt_scr)

    x = x_ref[0]
    tq = x.shape[0]
    h = _rms(x, gx_ref[...]).astype(BF16)
    q = _dot(h, wq_ref[...])
    for hd in range(XATTN_HEADS):
        lo = hd * XATTN_HEAD_DIM
        qh = q[:, lo:lo + XATTN_HEAD_DIM].astype(BF16)
        kh = kv_ref[0, :, lo:lo + XATTN_HEAD_DIM]
        vh = kv_ref[0, :, D_MODEL + lo:D_MODEL + lo + XATTN_HEAD_DIM]
        s = _dot_nt(qh, kh)
        p = jnp.exp(s - jnp.max(s, axis=-1, keepdims=True))
        inv = 1.0 / jnp.sum(p, axis=-1, keepdims=True)
        o_scr[:, lo:lo + XATTN_HEAD_DIM] = (_dot(p.astype(BF16), vh) * inv).astype(BF16)
    x2 = x + _dot(o_scr[...], wo_ref[...])
    for s in range(ROW_TILES):
        x2r_ref[pl.ds(s, tq, stride=ROW_TILES), :] = x2[:, s * LANES:(s + 1) * LANES]
    logits = _router_logits(_rms(x2, gf_ref[...]), wr_ref, br_ref)
    info, running = _route(logits, cnt_scr[...], tri_ref[...])
    cnt_scr[...] = running
    counts_ref[...] = running
    brk_ref[...] = info.T[0:8, :].astype(jnp.int32)


def _attention(x1, kv, xattn_norm, w_q, w_o, ffn_norm, w_router, b_router):
    bsz, seq, d = x1.shape
    mlen = kv.shape[1]
    tq = ATTN_TILE
    nj = seq // tq
    const = lambda b, j: (0, 0)
    tile = lambda b, j: (b, j, 0)
    tri = jnp.tri(RANK_BLOCK, RANK_BLOCK, -1, dtype=BF16)
    return pl.pallas_call(
        _attn_kernel,
        grid=(bsz, nj),
        in_specs=[
            pl.BlockSpec((1, tq, d), tile),
            pl.BlockSpec((1, mlen, 2 * d), lambda b, j: (b, 0, 0)),
            pl.BlockSpec((1, d), const),
            pl.BlockSpec((d, d), const),
            pl.BlockSpec((d, d), const),
            pl.BlockSpec((1, d), const),
            pl.BlockSpec((d, 2 * LANES), const),
            pl.BlockSpec((1, LANES), const),
            pl.BlockSpec((RANK_BLOCK, RANK_BLOCK), const),
        ],
        out_specs=[
            pl.BlockSpec((tq * ROW_TILES, LANES), lambda b, j: (b * nj + j, 0)),
            pl.BlockSpec((8, tq), lambda b, j: (0, b * nj + j)),
            pl.BlockSpec((1, LANES), const),
        ],
        out_shape=[
            jax.ShapeDtypeStruct((bsz * seq * ROW_TILES, LANES), F32),
            jax.ShapeDtypeStruct((8, bsz * seq), jnp.int32),
            jax.ShapeDtypeStruct((1, LANES), F32),
        ],
        scratch_shapes=[pltpu.VMEM((tq, d), BF16), pltpu.VMEM((1, LANES), F32)],
        compiler_params=pltpu.CompilerParams(
            dimension_semantics=("arbitrary", "arbitrary"), vmem_limit_bytes=VMEM_LIMIT),
        name="xattn_router",
    )(x1, kv, xattn_norm, w_q, w_o, ffn_norm, w_router, b_router, tri)


def _item_copy(hbm, hbm_item, buf, buf_item, rows_per_item, sem, to_hbm):
    h = hbm.at[pl.ds(pl.multiple_of(hbm_item * rows_per_item, rows_per_item), rows_per_item)]
    b = buf.at[pl.ds(pl.multiple_of(buf_item * rows_per_item, rows_per_item), rows_per_item)]
    return pltpu.make_async_copy(b, h, sem) if to_hbm else pltpu.make_async_copy(h, b, sem)


def _items_wait(hbm, buf, buf_item, n_items, rows_per_item, sem, to_hbm):
    n = n_items * rows_per_item
    h = hbm.at[pl.ds(0, n)]
    b = buf.at[pl.ds(pl.multiple_of(buf_item * rows_per_item, rows_per_item), n)]
    (pltpu.make_async_copy(b, h, sem) if to_hbm else pltpu.make_async_copy(h, b, sem)).wait()


def _dispatch_kernel(pos_ref, fill_ref, ntiles_ref, x2r_ref, xs_hbm, buf, sem):
    i = pl.program_id(0)
    nt = pl.num_programs(0)
    td = DISPATCH_TILE
    rpi = ROW_TILES
    slot = i % 2
    base = slot * td

    @pl.when(i >= 2)
    def _():
        _items_wait(xs_hbm, buf, base, td, rpi, sem.at[slot], True)

    buf[pl.ds(pl.multiple_of(base * rpi, td * rpi), td * rpi), :] = x2r_ref[...]

    def send(r2, carry):
        for k in range(DMA_QUEUES):
            r = r2 * DMA_QUEUES + k
            _item_copy(xs_hbm, pos_ref[i * td + r], buf, base + r, rpi, sem.at[slot],
                       True).start(priority=k)
        return carry

    lax.fori_loop(0, td // DMA_QUEUES, send, 0, unroll=4)

    @pl.when(i == nt - 1)
    def _():
        _items_wait(xs_hbm, buf, base, td, rpi, sem.at[slot], True)
        _items_wait(xs_hbm, buf, (1 - slot) * td, td, rpi, sem.at[1 - slot], True)
        buf[...] = jnp.zeros_like(buf)
        zsem = sem.at[2]
        bits = range(MOE_TILE.bit_length() - 2, -1, -1)

        def pad_copy(b, k):
            first = fill_ref[b]
            n = fill_ref[N_BUCKETS + b] - first
            done = (n >> (k + 1)) << (k + 1)
            rows = (1 << k) * rpi
            cp = pltpu.make_async_copy(
                buf.at[pl.ds(0, rows)],
                xs_hbm.at[pl.ds(pl.multiple_of((first + done) * rpi, rpi), rows)], zsem)
            return ((n >> k) & 1) == 1, cp

        for wait in (False, True):
            for b in range(N_BUCKETS):
                for k in bits:
                    present, cp = pad_copy(b, k)

                    @pl.when(present)
                    def _():
                        cp.wait() if wait else cp.start()

        n_tiles_total = xs_hbm.shape[0] // (MOE_TILE * rpi)

        def tile_copy(tile):
            return pltpu.make_async_copy(
                buf.at[pl.ds(0, MOE_TILE * rpi)],
                xs_hbm.at[pl.ds(pl.multiple_of(tile * MOE_TILE * rpi, MOE_TILE * rpi),
                                MOE_TILE * rpi)], zsem)

        def fill_tile(tile, carry):
            tile_copy(tile).start()
            return carry

        def drain_tile(tile, carry):
            tile_copy(tile).wait()
            return carry

        lax.fori_loop(ntiles_ref[0], n_tiles_total, fill_tile, 0)
        lax.fori_loop(ntiles_ref[0], n_tiles_total, drain_tile, 0)


def _dispatch(pos, fill, ntiles, x2r, n_sorted):
    t = x2r.shape[0] // ROW_TILES
    td = DISPATCH_TILE
    grid_spec = pltpu.PrefetchScalarGridSpec(
        num_scalar_prefetch=3,
        grid=(t // td,),
        in_specs=[pl.BlockSpec((td * ROW_TILES, LANES), lambda i, *_: (i, 0))],
        out_specs=pl.BlockSpec(memory_space=pl.ANY),
        scratch_shapes=[
            pltpu.VMEM((2 * td * ROW_TILES, LANES), F32),
            pltpu.SemaphoreType.DMA((3,)),
        ],
    )
    return pl.pallas_call(
        _dispatch_kernel,
        grid_spec=grid_spec,
        out_shape=jax.ShapeDtypeStruct((n_sorted * ROW_TILES, LANES), F32),
        compiler_params=pltpu.CompilerParams(
            dimension_semantics=("arbitrary",), vmem_limit_bytes=VMEM_LIMIT),
        name="moe_dispatch",
    )(pos, fill, ntiles, x2r)


def _silu(x):
    return x * _sigmoid(x)


def _moe_kernel(elo_ref, ehi_ref, used_ref,
                xs_ref, gf_ref, wr_ref, br_ref, gfin_ref,
                wg_lo, wu_lo, wd_lo, wg_hi, wu_hi, wd_hi, y_ref):
    i = pl.program_id(0)
    tm = MOE_TILE

    @pl.when(used_ref[i] == 1)
    def _():
        x2 = jnp.concatenate(
            [xs_ref[pl.ds(s, tm, stride=ROW_TILES), :] for s in range(ROW_TILES)], axis=1)
        h3 = _rms(x2, gf_ref[...])
        g_lo, g_hi = _gates(_router_logits(h3, wr_ref, br_ref), elo_ref[i], ehi_ref[i])
        x = h3.astype(BF16)

        def expert(wg, wu, wd):
            hid = (_silu(_dot(x, wg[0])) * _dot(x, wu[0])).astype(BF16)
            return _dot(hid, wd[0])

        moe = g_lo * expert(wg_lo, wu_lo, wd_lo) + g_hi * expert(wg_hi, wu_hi, wd_hi)
        out = _rms(x2 + moe, gfin_ref[...])
        for s in range(ROW_TILES):
            y_ref[pl.ds(s, tm, stride=ROW_TILES), :] = out[:, s * LANES:(s + 1) * LANES]

    @pl.when(used_ref[i] == 0)
    def _():
        y_ref[...] = jnp.zeros_like(y_ref)


def _moe_sparse(e_lo, e_hi, used, xs, ffn_norm, w_router, b_router, final_norm,
                w_gate, w_up, w_down):
    tm = MOE_TILE
    nt = used.shape[0]
    d = D_MODEL
    lo = lambda i, elo, ehi, used: (elo[i], 0, 0)
    hi = lambda i, elo, ehi, used: (ehi[i], 0, 0)
    const = lambda i, *_: (0, 0)
    grid_spec = pltpu.PrefetchScalarGridSpec(
        num_scalar_prefetch=3,
        grid=(nt,),
        in_specs=[
            pl.BlockSpec((tm * ROW_TILES, LANES), lambda i, *_: (i, 0)),
            pl.BlockSpec((1, d), const),
            pl.BlockSpec((d, 2 * LANES), const),
            pl.BlockSpec((1, LANES), const),
            pl.BlockSpec((1, d), const),
            pl.BlockSpec((1, d, D_EXPERT), lo),
            pl.BlockSpec((1, d, D_EXPERT), lo),
            pl.BlockSpec((1, D_EXPERT, d), lo),
            pl.BlockSpec((1, d, D_EXPERT), hi),
            pl.BlockSpec((1, d, D_EXPERT), hi),
            pl.BlockSpec((1, D_EXPERT, d), hi),
        ],
        out_specs=pl.BlockSpec((tm * ROW_TILES, LANES), lambda i, *_: (i, 0)),
    )
    return pl.pallas_call(
        _moe_kernel,
        grid_spec=grid_spec,
        out_shape=jax.ShapeDtypeStruct((nt * tm * ROW_TILES, LANES), F32),
        compiler_params=pltpu.CompilerParams(
            dimension_semantics=("arbitrary",), vmem_limit_bytes=VMEM_LIMIT),
        name="moe_sparse",
    )(e_lo, e_hi, used, xs, ffn_norm, w_router, b_router, final_norm,
      w_gate, w_up, w_down, w_gate, w_up, w_down)


def _final_kernel(pos_ref, y_hbm, o_ref, ybuf, sem):
    i = pl.program_id(0)
    nt = pl.num_programs(0)
    tf = FINAL_TILE

    def start(tile, slot):
        def fetch(r2, carry):
            for k in range(DMA_QUEUES):
                r = r2 * DMA_QUEUES + k
                _item_copy(y_hbm, pos_ref[tile * tf + r], ybuf, slot * tf + r, ROW_TILES,
                           sem.at[slot], False).start(priority=k)
            return carry

        lax.fori_loop(0, tf // DMA_QUEUES, fetch, 0, unroll=4)

    @pl.when(i == 0)
    def _():
        start(0, 0)

    @pl.when(i + 1 < nt)
    def _():
        start(i + 1, (i + 1) % 2)

    slot = i % 2
    _items_wait(y_hbm, ybuf, slot * tf, tf, ROW_TILES, sem.at[slot], False)
    row0 = pl.multiple_of(slot * tf * ROW_TILES, ROW_TILES)
    for s in range(ROW_TILES):
        o_ref[:, s * LANES:(s + 1) * LANES] = ybuf[pl.ds(row0 + s, tf, stride=ROW_TILES), :]


def _final(pos, y):
    t = pos.shape[0]
    d = D_MODEL
    tf = FINAL_TILE
    grid_spec = pltpu.PrefetchScalarGridSpec(
        num_scalar_prefetch=1,
        grid=(t // tf,),
        in_specs=[pl.BlockSpec(memory_space=pl.ANY)],
        out_specs=pl.BlockSpec((tf, d), lambda i, *_: (i, 0)),
        scratch_shapes=[
            pltpu.VMEM((2 * tf * ROW_TILES, LANES), F32),
            pltpu.SemaphoreType.DMA((2,)),
        ],
    )
    return pl.pallas_call(
        _final_kernel,
        grid_spec=grid_spec,
        out_shape=jax.ShapeDtypeStruct((t, d), F32),
        compiler_params=pltpu.CompilerParams(
            dimension_semantics=("arbitrary",), vmem_limit_bytes=VMEM_LIMIT),
        name="moe_unpermute",
    )(pos, y)


def _positions_kernel(starts_ref, brk_ref, pos_ref):
    bucket = brk_ref[0:1, :]
    pos = brk_ref[1:2, :]
    for b in range(N_BUCKETS):
        pos = pos + jnp.where(bucket == b, starts_ref[b], 0)
    pos_ref[...] = pos


def _positions(starts, brk):
    t = brk.shape[1]
    grid_spec = pltpu.PrefetchScalarGridSpec(
        num_scalar_prefetch=1,
        grid=(1,),
        in_specs=[pl.BlockSpec(brk.shape, lambda i, starts: (0, 0))],
        out_specs=pl.BlockSpec((1, t), lambda i, starts: (0, 0)),
    )
    return pl.pallas_call(
        _positions_kernel,
        grid_spec=grid_spec,
        out_shape=jax.ShapeDtypeStruct((1, t), jnp.int32),
        name="moe_positions",
    )(starts, brk).reshape(t)


def _routing_tables(counts, brk):
    tm = MOE_TILE
    t = brk.shape[1]
    nt = t // tm + N_BUCKETS
    cnt = counts[0, :N_BUCKETS].astype(jnp.int32)
    padded = ((cnt + tm - 1) // tm) * tm
    ends = jnp.cumsum(padded)
    starts = ends - padded
    pos = _positions(starts, brk)
    fill = jnp.concatenate([starts + cnt, ends])
    ntiles = ends[-1:] // tm
    tile_start = jnp.arange(nt, dtype=jnp.int32) * tm
    tile_bucket = jnp.sum((ends[None, :] <= tile_start[:, None]).astype(jnp.int32), axis=1)
    used = (tile_bucket < N_BUCKETS).astype(jnp.int32)
    tile_bucket = jnp.minimum(tile_bucket, N_BUCKETS - 1)
    pair = tile_bucket % N_PAIRS
    group0 = (tile_bucket // N_PAIRS) * EXPERTS_PER_GROUP
    e_lo = group0 + jnp.array([0, 0, 0, 1, 1, 2], jnp.int32)[pair]
    e_hi = group0 + jnp.array([1, 2, 3, 2, 3, 3], jnp.int32)[pair]
    return pos, fill, ntiles, e_lo, e_hi, used, nt


def kernel(x, mem, mix_norm, w_in, conv_w, hgrn_lb, hgrn_norm, w_out, xattn_norm, mem_norm,
           w_q, w_kv, w_o, ffn_norm, w_group, b_group, w_expert, b_expert, w_gate, w_up,
           w_down, final_norm):
    bsz, seq, d = x.shape
    assert d == D_MODEL and seq % SEQ_TILE == 0 and seq % ATTN_TILE == 0
    assert (bsz * seq) % MOE_TILE == 0 and (bsz * seq) % FINAL_TILE == 0
    assert (bsz * seq) % DISPATCH_TILE == 0 and bsz * seq >= 2 * DISPATCH_TILE
    assert 2 * DISPATCH_TILE >= MOE_TILE and mix_norm.shape[0] == 1
    bf = lambda w: w.astype(BF16)

    x1 = _mixer(x, mix_norm, bf(w_in[0]), conv_w[0], hgrn_lb, hgrn_norm, bf(w_out[0]))
    kv = _kv_proj(mem, mem_norm, bf(w_kv[0]))

    pad = LANES - N_GROUPS - N_EXPERTS
    w_router = jnp.concatenate(
        [w_group[0], w_expert[0], jnp.zeros((d, pad), F32)], axis=1)
    b_router = jnp.concatenate(
        [b_group[0], b_expert[0], jnp.zeros((pad,), F32)])[None, :]
    w_router_hi = bf(w_router)
    w_router_lo = bf(w_router - w_router_hi.astype(F32))
    w_router2 = jnp.concatenate([w_router_hi, w_router_lo], axis=1)
    w_q_scaled = bf(w_q[0] * (XATTN_HEAD_DIM ** -0.5))
    x2r, brk, counts = _attention(x1, kv, xattn_norm, w_q_scaled, bf(w_o[0]), ffn_norm,
                                  w_router2, b_router)

    pos, fill, ntiles, e_lo, e_hi, used, nt = _routing_tables(counts, brk)
    xs = _dispatch(pos, fill, ntiles, x2r, nt * MOE_TILE)
    y = _moe_sparse(e_lo, e_hi, used, xs, ffn_norm, w_router2, b_router, final_norm[None, :],
                    bf(w_gate[0]), bf(w_up[0]), bf(w_down[0]))
    return _final(pos, y).reshape(bsz, seq, d)
```

```python
import jax
import jax.numpy as jnp
from jax import lax
from jax.experimental import pallas as pl
from jax.experimental.pallas import tpu as pltpu

F32 = jnp.float32
BF16 = jnp.bfloat16

D_MODEL = 1024
CONV_WIDTH = 512
HGRN_WIDTH = 512
HGRN_HEADS = 4
HEAD_DIM = 128
N_PROJ_SLOTS = 7
PROJ_WIDTH = N_PROJ_SLOTS * 512
XATTN_HEADS = 4
XATTN_HEAD_DIM = 256
N_GROUPS = 4
EXPERTS_PER_GROUP = 4
N_EXPERTS = 16
D_EXPERT = 512
EPS = 1e-6

LANES = 128
SUBLANES = 8
CHUNK = 64
CHUNK_LEVELS = 6
HGRN_TILE = 4 * CHUNK
SEQ_TILE = 512
ATTN_TILE = 1024
RANK_BLOCK = 256
DISPATCH_TILE = 1024
MOE_TILE = 512
FINAL_TILE = 1024
N_PAIRS = 6
N_BUCKETS = N_GROUPS * N_PAIRS
ROW_TILES = D_MODEL // LANES
DMA_QUEUES = 2
VMEM_LIMIT = 56 * 1024 * 1024


def _rms(x, g):
    return x * lax.rsqrt(jnp.mean(x * x, axis=-1, keepdims=True) + EPS) * g


def _dot(a, b):
    return jnp.dot(a, b, preferred_element_type=F32)


def _dot_nt(a, b):
    return lax.dot_general(a, b, (((1,), (1,)), ((), ())), preferred_element_type=F32)


def _dot_tn(a, b):
    return lax.dot_general(a, b, (((0,), (0,)), ((), ())), preferred_element_type=F32)


def _roll_rows(x, shift):
    return pltpu.roll(x, shift % x.shape[0], axis=0)


def _level_exponents(logf2, use_level):
    n = CHUNK // SUBLANES
    sub = lax.broadcasted_iota(jnp.int32, (SUBLANES, logf2.shape[1]), 0)
    roll = lambda x, s: pltpu.roll(x, s % SUBLANES, axis=0)
    r = [logf2[SUBLANES * j:SUBLANES * (j + 1), :] for j in range(n)]
    for lvl in range(1, CHUNK_LEVELS + 1):
        half = 1 << (lvl - 1)
        g = [None] * n
        if half < SUBLANES:
            second = (sub & half) != 0
            for j in range(n):
                last = r[j]
                w = 1
                while w < half:
                    last = jnp.where((sub & w) != 0, last, roll(last, -w))
                    w *= 2
                tot = jnp.where(second, roll(last, half), last)
                g[j] = jnp.where(second, r[j], tot - r[j])
                r[j] = jnp.where(second, r[j] + tot, r[j])
        else:
            hv = half // SUBLANES
            for j0 in range(0, n, 2 * hv):
                mid = r[j0 + hv - 1]
                tot = jnp.broadcast_to(mid[SUBLANES - 1:SUBLANES, :], mid.shape)
                for j in range(j0, j0 + hv):
                    g[j] = tot - r[j]
                for j in range(j0 + hv, j0 + 2 * hv):
                    g[j] = r[j]
                    r[j] = r[j] + tot
        use_level(lvl, jnp.concatenate(g, axis=0))
    return jnp.concatenate(r, axis=0)


def _sigmoid(x):
    return 0.5 * jnp.tanh(0.5 * x) + 0.5


def _split_levels():
    import numpy as np
    t = np.arange(CHUNK)[:, None]
    s = np.arange(CHUNK)[None, :]
    msb = np.floor(np.log2(np.maximum(t ^ s, 1))).astype(np.int32) + 1
    return np.where(s < t, msb, np.where(s == t, 0, -1)).astype(np.int32)


def _hgrn_chunk(q, z, v, lb, levels):
    half_th = 0.5 * jnp.tanh(0.5 * z)
    one_m_lb = 1.0 - lb
    logf2 = jnp.log2(lb + one_m_lb * (0.5 + half_th))
    k = one_m_lb * (0.5 - half_th)

    qb = q.astype(BF16)
    kb = k.astype(BF16)
    scores = [jnp.where(levels == 0, jnp.sum(q * k, axis=-1, keepdims=True), 0.0)]

    def use_level(lvl, g):
        decay = jnp.exp2(g).astype(BF16)
        scores[0] = jnp.where(levels == lvl, _dot_nt(qb * decay, kb * decay), scores[0])

    b2 = _level_exponents(logf2, use_level)
    a = scores[0]
    b2_last = b2[CHUNK - 1:CHUNK, :]

    qe = (q * jnp.exp2(b2)).astype(BF16)
    kd = (k * jnp.exp2(b2_last - b2)).astype(BF16)
    return a, qe, kd, v.astype(BF16), b2_last


def _hgrn_tile(chunks, st):
    (a0, qe0, kd0, v0, bl0), (a1, qe1, kd1, v1, bl1), (a2, qe2, kd2, v2, bl2), \
        (a3, qe3, kd3, v3, bl3) = chunks
    scale = lambda x, log2_decay: x * jnp.exp2(log2_decay).astype(BF16)
    bf = lambda x: x.astype(BF16)
    cum1 = bl0
    cum2 = cum1 + bl1
    cum3 = cum2 + bl2
    cum4 = cum3 + bl3
    s10 = bf(_dot_nt(qe1, kd0))
    s32 = bf(_dot_nt(qe3, kd2))
    s8 = bf(_dot_nt(jnp.concatenate([qe2, scale(qe3, bl2)], axis=0),
                    jnp.concatenate([scale(kd0, bl1), kd1], axis=0)))
    v01 = jnp.concatenate([v0, v1], axis=0)
    stb = bf(st)
    o0 = _dot(bf(a0), v0) + _dot_nt(qe0, stb)
    o1 = _dot(bf(a1), v1) + _dot(s10, v0) + _dot_nt(scale(qe1, cum1), stb)
    o2 = _dot(bf(a2), v2) + _dot(s8[:CHUNK], v01) + _dot_nt(scale(qe2, cum2), stb)
    o3 = (_dot(bf(a3), v3) + _dot(s32, v2) + _dot(s8[CHUNK:], v01)
          + _dot_nt(scale(qe3, cum3), stb))
    kd_all = jnp.concatenate(
        [scale(kd0, cum4 - cum1), scale(kd1, cum4 - cum2), scale(kd2, bl3), kd3], axis=0)
    v_all = jnp.concatenate([v0, v1, v2, v3], axis=0)
    st_new = st * jnp.exp2(cum4) + _dot_tn(v_all, kd_all)
    return [o0, o1, o2, o3], st_new


def _mixer_kernel(x_ref, gmix_ref, win_ref, convw_ref, lbraw_ref, hnorm_ref, wout_ref, lvl_ref,
                  o_ref, p_scr, y_scr, st_scr, tail_scr):
    j = pl.program_id(1)

    @pl.when(j == 0)
    def _():
        st_scr[...] = jnp.zeros_like(st_scr)
        tail_scr[...] = jnp.zeros_like(tail_scr)

    x = x_ref[0]
    h = _rms(x, gmix_ref[...]).astype(BF16)
    p_scr[...] = _dot(h, win_ref[...])

    ts = x.shape[0]
    cb = p_scr[:, 0:CONV_WIDTH]
    u = p_scr[:, CONV_WIDTH:2 * CONV_WIDTH] * p_scr[:, 2 * CONV_WIDTH:3 * CONV_WIDTH]
    row = lax.broadcasted_iota(jnp.int32, u.shape, 0)
    prev1 = tail_scr[7:8, :]
    prev2 = tail_scr[6:7, :]
    u1 = jnp.where(row == 0, prev1, _roll_rows(u, 1))
    u2 = jnp.where(row == 0, prev2, jnp.where(row == 1, prev1, _roll_rows(u, 2)))
    cw = convw_ref[...]
    conv = u2 * cw[0:1, :] + u1 * cw[1:2, :] + u * cw[2:3, :]
    y_scr[:, 0:CONV_WIDTH] = (cb * conv).astype(BF16)
    tail_scr[...] = u[ts - 8:ts, :]

    raw = lbraw_ref[...]
    mx = jnp.max(raw, axis=0, keepdims=True)
    ex = jnp.exp(raw - mx)
    lb_all = ex[0:1, :] / jnp.sum(ex, axis=0, keepdims=True)
    hn = hnorm_ref[...]

    for hd in range(HGRN_HEADS):
        lo = hd * HEAD_DIM
        sl = slice(lo, lo + HEAD_DIM)
        col = lambda slot: slice(slot * 512 + lo, slot * 512 + lo + HEAD_DIM)
        for t0 in range(0, ts, HGRN_TILE):
            chunks = []
            for r0 in range(t0, t0 + HGRN_TILE, CHUNK):
                rows = slice(r0, r0 + CHUNK)
                chunks.append(_hgrn_chunk(p_scr[rows, col(3)], p_scr[rows, col(4)],
                                          p_scr[rows, col(5)], lb_all[:, sl], lvl_ref[...]))
            outs, st_new = _hgrn_tile(chunks, st_scr[hd])
            st_scr[hd] = st_new
            for c, o in enumerate(outs):
                rows = slice(t0 + c * CHUNK, t0 + (c + 1) * CHUNK)
                g = p_scr[rows, col(6)]
                o = o * lax.rsqrt(jnp.mean(o * o, axis=-1, keepdims=True) + EPS) * hn[:, sl]
                y_scr[rows, CONV_WIDTH + lo:CONV_WIDTH + lo + HEAD_DIM] = \
                    (o * (g * _sigmoid(g))).astype(BF16)

    o_ref[0] = x + _dot(y_scr[...], wout_ref[...])


def _mixer(x, mix_norm, w_in, conv_w, hgrn_lb, hgrn_norm, w_out):
    bsz, seq, d = x.shape
    ts = SEQ_TILE
    const = lambda b, j: (0, 0)
    return pl.pallas_call(
        _mixer_kernel,
        grid=(bsz, seq // ts),
        in_specs=[
            pl.BlockSpec((1, ts, d), lambda b, j: (b, j, 0)),
            pl.BlockSpec((1, d), const),
            pl.BlockSpec((d, PROJ_WIDTH), const),
            pl.BlockSpec((3, CONV_WIDTH), const),
            pl.BlockSpec((2, HGRN_WIDTH), const),
            pl.BlockSpec((1, HGRN_WIDTH), const),
            pl.BlockSpec((d, d), const),
            pl.BlockSpec((CHUNK, CHUNK), const),
        ],
        out_specs=pl.BlockSpec((1, ts, d), lambda b, j: (b, j, 0)),
        out_shape=jax.ShapeDtypeStruct((bsz, seq, d), F32),
        scratch_shapes=[
            pltpu.VMEM((ts, PROJ_WIDTH), F32),
            pltpu.VMEM((ts, d), BF16),
            pltpu.VMEM((HGRN_HEADS, HEAD_DIM, HEAD_DIM), F32),
            pltpu.VMEM((8, CONV_WIDTH), F32),
        ],
        compiler_params=pltpu.CompilerParams(
            dimension_semantics=("arbitrary", "arbitrary"), vmem_limit_bytes=VMEM_LIMIT),
        name="mixer",
    )(x, mix_norm, w_in, conv_w, hgrn_lb, hgrn_norm, w_out, jnp.asarray(_split_levels()))


def _kv_kernel(m_ref, g_ref, w_ref, o_ref):
    h = _rms(m_ref[0], g_ref[...]).astype(BF16)
    o_ref[0] = _dot(h, w_ref[...]).astype(BF16)


def _kv_proj(mem, mem_norm, w_kv):
    bsz, mlen, d = mem.shape
    const = lambda b: (0, 0)
    return pl.pallas_call(
        _kv_kernel,
        grid=(bsz,),
        in_specs=[
            pl.BlockSpec((1, mlen, d), lambda b: (b, 0, 0)),
            pl.BlockSpec((1, d), const),
            pl.BlockSpec((d, 2 * d), const),
        ],
        out_specs=pl.BlockSpec((1, mlen, 2 * d), lambda b: (b, 0, 0)),
        out_shape=jax.ShapeDtypeStruct((bsz, mlen, 2 * d), BF16),
        compiler_params=pltpu.CompilerParams(
            dimension_semantics=("arbitrary",), vmem_limit_bytes=VMEM_LIMIT),
        name="kv_proj",
    )(mem, mem_norm, w_kv)


def _first_argmax(vals, lane):
    mx = jnp.max(vals, axis=-1, keepdims=True)
    idx = jnp.min(jnp.where(vals == mx, lane, float(LANES)), axis=-1, keepdims=True)
    return mx, idx


def _router_logits(h3, wr_ref, br_ref):
    h3_hi = h3.astype(BF16)
    h3_lo = (h3 - h3_hi.astype(F32)).astype(BF16)
    two = _dot(h3_hi, wr_ref[...])
    return (two[:, :LANES] + two[:, LANES:]) + _dot(h3_lo, wr_ref[:, :LANES]) + br_ref[...]


def _route(logits, running, tri):
    lane = lax.broadcasted_iota(jnp.int32, logits.shape, 1).astype(F32)
    neg = jnp.float32(-jnp.inf)
    gl = jnp.where(lane < N_GROUPS, logits, neg)
    _, gidx = _first_argmax(gl, lane)
    base = N_GROUPS + EXPERTS_PER_GROUP * gidx
    el = jnp.where((lane >= base) & (lane < base + EXPERTS_PER_GROUP), logits, neg)
    _, i1 = _first_argmax(el, lane)
    _, i2 = _first_argmax(jnp.where(lane == i1, neg, el), lane)
    lo = jnp.minimum(i1, i2) - base
    hi = jnp.maximum(i1, i2) - base
    bucket = gidx * N_PAIRS + (lo * (7.0 - lo)) * 0.5 + hi - lo - 1.0

    onehot = lane == bucket
    ranks = []
    for r0 in range(0, logits.shape[0], RANK_BLOCK):
        oh = onehot[r0:r0 + RANK_BLOCK]
        before = _dot(tri, oh.astype(BF16))
        ranks.append(jnp.sum(jnp.where(oh, before + running, 0.0), axis=-1, keepdims=True))
        running = running + jnp.sum(oh.astype(F32), axis=0, keepdims=True)
    rank = jnp.concatenate(ranks, axis=0)
    info = jnp.where(lane == 0, bucket, jnp.where(lane == 1, rank, 0.0))
    return info, running


def _gates(logits, e_lo, e_hi):
    lane = lax.broadcasted_iota(jnp.int32, logits.shape, 1)
    neg = jnp.float32(-jnp.inf)
    gl = jnp.where(lane < N_GROUPS, logits, neg)
    g_p = 1.0 / jnp.sum(jnp.exp(gl - jnp.max(gl, axis=-1, keepdims=True)), axis=-1, keepdims=True)
    l_lo = jnp.sum(jnp.where(lane == N_GROUPS + e_lo, logits, 0.0), axis=-1, keepdims=True)
    l_hi = jnp.sum(jnp.where(lane == N_GROUPS + e_hi, logits, 0.0), axis=-1, keepdims=True)
    m = jnp.maximum(l_lo, l_hi)
    p_lo = jnp.exp(l_lo - m)
    p_hi = jnp.exp(l_hi - m)
    inv = g_p / (p_lo + p_hi)
    return p_lo * inv, p_hi * inv


def _attn_kernel(x_ref, kv_ref, gx_ref, wq_ref, wo_ref, gf_ref, wr_ref, br_ref, tri_ref,
                 x2r_ref, brk_ref, counts_ref, o_scr, cnt_scr):
    @pl.when((pl.program_id(0) == 0) & (pl.program_id(1) == 0))
    def _():
        cnt_scr[...] = jnp.zeros_like(cnt_scr)

    x = x_ref[0]
    tq = x.shape[0]
    h = _rms(x, gx_ref[...]).astype(BF16)
    q = _dot(h, wq_ref[...])
    for hd in range(XATTN_HEADS):
        lo = hd * XATTN_HEAD_DIM
        qh = q[:, lo:lo + XATTN_HEAD_DIM].astype(BF16)
        kh = kv_ref[0, :, lo:lo + XATTN_HEAD_DIM]
        vh = kv_ref[0, :, D_MODEL + lo:D_MODEL + lo + XATTN_HEAD_DIM]
        s = _dot_nt(qh, kh)
        p = jnp.exp(s - jnp.max(s, axis=-1, keepdims=True))
        inv = 1.0 / jnp.sum(p, axis=-1, keepdims=True)
        o_scr[:, lo:lo + XATTN_HEAD_DIM] = (_dot(p.astype(BF16), vh) * inv).astype(BF16)
    x2 = x + _dot(o_scr[...], wo_ref[...])
    for s in range(ROW_TILES):
        x2r_ref[pl.ds(s, tq, stride=ROW_TILES), :] = x2[:, s * LANES:(s + 1) * LANES]
    logits = _router_logits(_rms(x2, gf_ref[...]), wr_ref, br_ref)
    info, running = _route(logits, cnt_scr[...], tri_ref[...])
    cnt_scr[...] = running
    counts_ref[...] = running
    brk_ref[...] = info.T[0:8, :].astype(jnp.int32)


def _attention(x1, kv, xattn_norm, w_q, w_o, ffn_norm, w_router, b_router):
    bsz, seq, d = x1.shape
    mlen = kv.shape[1]
    tq = ATTN_TILE
    nj = seq // tq
    const = lambda b, j: (0, 0)
    tile = lambda b, j: (b, j, 0)
    tri = jnp.tri(RANK_BLOCK, RANK_BLOCK, -1, dtype=BF16)
    return pl.pallas_call(
        _attn_kernel,
        grid=(bsz, nj),
        in_specs=[
            pl.BlockSpec((1, tq, d), tile),
            pl.BlockSpec((1, mlen, 2 * d), lambda b, j: (b, 0, 0)),
            pl.BlockSpec((1, d), const),
            pl.BlockSpec((d, d), const),
            pl.BlockSpec((d, d), const),
            pl.BlockSpec((1, d), const),
            pl.BlockSpec((d, 2 * LANES), const),
            pl.BlockSpec((1, LANES), const),
            pl.BlockSpec((RANK_BLOCK, RANK_BLOCK), const),
        ],
        out_specs=[
            pl.BlockSpec((tq * ROW_TILES, LANES), lambda b, j: (b * nj + j, 0)),
            pl.BlockSpec((8, tq), lambda b, j: (0, b * nj + j)),
            pl.BlockSpec((1, LANES), const),
        ],
        out_shape=[
            jax.ShapeDtypeStruct((bsz * seq * ROW_TILES, LANES), F32),
            jax.ShapeDtypeStruct((8, bsz * seq), jnp.int32),
            jax.ShapeDtypeStruct((1, LANES), F32),
        ],
        scratch_shapes=[pltpu.VMEM((tq, d), BF16), pltpu.VMEM((1, LANES), F32)],
        compiler_params=pltpu.CompilerParams(
            dimension_semantics=("arbitrary", "arbitrary"), vmem_limit_bytes=VMEM_LIMIT),
        name="xattn_router",
    )(x1, kv, xattn_norm, w_q, w_o, ffn_norm, w_router, b_router, tri)


def _item_copy(hbm, hbm_item, buf, buf_item, rows_per_item, sem, to_hbm):
    h = hbm.at[pl.ds(pl.multiple_of(hbm_item * rows_per_item, rows_per_item), rows_per_item)]
    b = buf.at[pl.ds(pl.multiple_of(buf_item * rows_per_item, rows_per_item), rows_per_item)]
    return pltpu.make_async_copy(b, h, sem) if to_hbm else pltpu.make_async_copy(h, b, sem)


def _items_wait(hbm, buf, buf_item, n_items, rows_per_item, sem, to_hbm):
    n = n_items * rows_per_item
    h = hbm.at[pl.ds(0, n)]
    b = buf.at[pl.ds(pl.multiple_of(buf_item * rows_per_item, rows_per_item), n)]
    (pltpu.make_async_copy(b, h, sem) if to_hbm else pltpu.make_async_copy(h, b, sem)).wait()


def _dispatch_kernel(pos_ref, fill_ref, ntiles_ref, x2r_ref, xs_hbm, buf, sem):
    i = pl.program_id(0)
    nt = pl.num_programs(0)
    td = DISPATCH_TILE
    rpi = ROW_TILES
    slot = i % 2
    base = slot * td

    @pl.when(i >= 2)
    def _():
        _items_wait(xs_hbm, buf, base, td, rpi, sem.at[slot], True)

    buf[pl.ds(pl.multiple_of(base * rpi, td * rpi), td * rpi), :] = x2r_ref[...]

    def send(r2, carry):
        for k in range(DMA_QUEUES):
            r = r2 * DMA_QUEUES + k
            _item_copy(xs_hbm, pos_ref[i * td + r], buf, base + r, rpi, sem.at[slot],
                       True).start(priority=k)
        return carry

    lax.fori_loop(0, td // DMA_QUEUES, send, 0, unroll=4)

    @pl.when(i == nt - 1)
    def _():
        _items_wait(xs_hbm, buf, base, td, rpi, sem.at[slot], True)
        _items_wait(xs_hbm, buf, (1 - slot) * td, td, rpi, sem.at[1 - slot], True)
        buf[...] = jnp.zeros_like(buf)
        zsem = sem.at[2]
        bits = range(MOE_TILE.bit_length() - 2, -1, -1)

        def pad_copy(b, k):
            first = fill_ref[b]
            n = fill_ref[N_BUCKETS + b] - first
            done = (n >> (k + 1)) << (k + 1)
            rows = (1 << k) * rpi
            cp = pltpu.make_async_copy(
                buf.at[pl.ds(0, rows)],
                xs_hbm.at[pl.ds(pl.multiple_of((first + done) * rpi, rpi), rows)], zsem)
            return ((n >> k) & 1) == 1, cp

        for wait in (False, True):
            for b in range(N_BUCKETS):
                for k in bits:
                    present, cp = pad_copy(b, k)

                    @pl.when(present)
                    def _():
                        cp.wait() if wait else cp.start()

        n_tiles_total = xs_hbm.shape[0] // (MOE_TILE * rpi)

        def tile_copy(tile):
            return pltpu.make_async_copy(
                buf.at[pl.ds(0, MOE_TILE * rpi)],
                xs_hbm.at[pl.ds(pl.multiple_of(tile * MOE_TILE * rpi, MOE_TILE * rpi),
                                MOE_TILE * rpi)], zsem)

        def fill_tile(tile, carry):
            tile_copy(tile).start()
            return carry

        def drain_tile(tile, carry):
            tile_copy(tile).wait()
            return carry

        lax.fori_loop(ntiles_ref[0], n_tiles_total, fill_tile, 0)
        lax.fori_loop(ntiles_ref[0], n_tiles_total, drain_tile, 0)


def _dispatch(pos, fill, ntiles, x2r, n_sorted):
    t = x2r.shape[0] // ROW_TILES
    td = DISPATCH_TILE
    grid_spec = pltpu.PrefetchScalarGridSpec(
        num_scalar_prefetch=3,
        grid=(t // td,),
        in_specs=[pl.BlockSpec((td * ROW_TILES, LANES), lambda i, *_: (i, 0))],
        out_specs=pl.BlockSpec(memory_space=pl.ANY),
        scratch_shapes=[
            pltpu.VMEM((2 * td * ROW_TILES, LANES), F32),
            pltpu.SemaphoreType.DMA((3,)),
        ],
    )
    return pl.pallas_call(
        _dispatch_kernel,
        grid_spec=grid_spec,
        out_shape=jax.ShapeDtypeStruct((n_sorted * ROW_TILES, LANES), F32),
        compiler_params=pltpu.CompilerParams(
            dimension_semantics=("arbitrary",), vmem_limit_bytes=VMEM_LIMIT),
        name="moe_dispatch",
    )(pos, fill, ntiles, x2r)


def _silu(x):
    return x * _sigmoid(x)


def _moe_kernel(elo_ref, ehi_ref, used_ref, xtile_ref,
                xs_ref, gf_ref, wr_ref, br_ref, gfin_ref,
                wg_lo, wu_lo, wd_lo, wg_hi, wu_hi, wd_hi, y_ref):
    del xtile_ref
    i = pl.program_id(0)
    tm = MOE_TILE

    @pl.when(used_ref[i] == 1)
    def _():
        x2 = jnp.concatenate(
            [xs_ref[pl.ds(s, tm, stride=ROW_TILES), :] for s in range(ROW_TILES)], axis=1)
        h3 = _rms(x2, gf_ref[...])
        g_lo, g_hi = _gates(_router_logits(h3, wr_ref, br_ref), elo_ref[i], ehi_ref[i])
        x = h3.astype(BF16)

        def expert(wg, wu, wd):
            hid = (_silu(_dot(x, wg[0])) * _dot(x, wu[0])).astype(BF16)
            return _dot(hid, wd[0])

        moe = g_lo * expert(wg_lo, wu_lo, wd_lo) + g_hi * expert(wg_hi, wu_hi, wd_hi)
        out = _rms(x2 + moe, gfin_ref[...])
        for s in range(ROW_TILES):
            y_ref[pl.ds(s, tm, stride=ROW_TILES), :] = out[:, s * LANES:(s + 1) * LANES]

    @pl.when(used_ref[i] == 0)
    def _():
        y_ref[...] = jnp.zeros_like(y_ref)


def _moe_sparse(e_lo, e_hi, used, xs, ffn_norm, w_router, b_router, final_norm,
                w_gate, w_up, w_down):
    tm = MOE_TILE
    nt = used.shape[0]
    d = D_MODEL
    lo = lambda i, elo, ehi, used, xt: (elo[i], 0, 0)
    hi = lambda i, elo, ehi, used, xt: (ehi[i], 0, 0)
    const = lambda i, *_: (0, 0)
    steps = jnp.arange(nt, dtype=jnp.int32)
    xs_tile = jnp.where(used == 1, steps, jnp.maximum(jnp.sum(used) - 1, 0))
    grid_spec = pltpu.PrefetchScalarGridSpec(
        num_scalar_prefetch=4,
        grid=(nt,),
        in_specs=[
            pl.BlockSpec((tm * ROW_TILES, LANES), lambda i, elo, ehi, used, xt: (xt[i], 0)),
            pl.BlockSpec((1, d), const),
            pl.BlockSpec((d, 2 * LANES), const),
            pl.BlockSpec((1, LANES), const),
            pl.BlockSpec((1, d), const),
            pl.BlockSpec((1, d, D_EXPERT), lo),
            pl.BlockSpec((1, d, D_EXPERT), lo),
            pl.BlockSpec((1, D_EXPERT, d), lo),
            pl.BlockSpec((1, d, D_EXPERT), hi),
            pl.BlockSpec((1, d, D_EXPERT), hi),
            pl.BlockSpec((1, D_EXPERT, d), hi),
        ],
        out_specs=pl.BlockSpec((tm * ROW_TILES, LANES), lambda i, *_: (i, 0)),
    )
    return pl.pallas_call(
        _moe_kernel,
        grid_spec=grid_spec,
        out_shape=jax.ShapeDtypeStruct((nt * tm * ROW_TILES, LANES), F32),
        compiler_params=pltpu.CompilerParams(
            dimension_semantics=("arbitrary",), vmem_limit_bytes=VMEM_LIMIT),
        name="moe_sparse",
    )(e_lo, e_hi, used, xs_tile, xs, ffn_norm, w_router, b_router, final_norm,
      w_gate, w_up, w_down, w_gate, w_up, w_down)


def _final_kernel(pos_ref, y_hbm, o_ref, ybuf, sem):
    i = pl.program_id(0)
    nt = pl.num_programs(0)
    tf = FINAL_TILE

    def start(tile, slot):
        def fetch(r2, carry):
            for k in range(DMA_QUEUES):
                r = r2 * DMA_QUEUES + k
                _item_copy(y_hbm, pos_ref[tile * tf + r], ybuf, slot * tf + r, ROW_TILES,
                           sem.at[slot], False).start(priority=k)
            return carry

        lax.fori_loop(0, tf // DMA_QUEUES, fetch, 0, unroll=4)

    @pl.when(i == 0)
    def _():
        start(0, 0)

    @pl.when(i + 1 < nt)
    def _():
        start(i + 1, (i + 1) % 2)

    slot = i % 2
    _items_wait(y_hbm, ybuf, slot * tf, tf, ROW_TILES, sem.at[slot], False)
    row0 = pl.multiple_of(slot * tf * ROW_TILES, ROW_TILES)
    for s in range(ROW_TILES):
        o_ref[:, s * LANES:(s + 1) * LANES] = ybuf[pl.ds(row0 + s, tf, stride=ROW_TILES), :]


def _final(pos, y):
    t = pos.shape[0]
    d = D_MODEL
    tf = FINAL_TILE
    grid_spec = pltpu.PrefetchScalarGridSpec(
        num_scalar_prefetch=1,
        grid=(t // tf,),
        in_specs=[pl.BlockSpec(memory_space=pl.ANY)],
        out_specs=pl.BlockSpec((tf, d), lambda i, *_: (i, 0)),
        scratch_shapes=[
            pltpu.VMEM((2 * tf * ROW_TILES, LANES), F32),
            pltpu.SemaphoreType.DMA((2,)),
        ],
    )
    return pl.pallas_call(
        _final_kernel,
        grid_spec=grid_spec,
        out_shape=jax.ShapeDtypeStruct((t, d), F32),
        compiler_params=pltpu.CompilerParams(
            dimension_semantics=("arbitrary",), vmem_limit_bytes=VMEM_LIMIT),
        name="moe_unpermute",
    )(pos, y)


def _positions_kernel(starts_ref, brk_ref, pos_ref):
    bucket = brk_ref[0:1, :]
    pos = brk_ref[1:2, :]
    for b in range(N_BUCKETS):
        pos = pos + jnp.where(bucket == b, starts_ref[b], 0)
    pos_ref[...] = pos


def _positions(starts, brk):
    t = brk.shape[1]
    grid_spec = pltpu.PrefetchScalarGridSpec(
        num_scalar_prefetch=1,
        grid=(1,),
        in_specs=[pl.BlockSpec(brk.shape, lambda i, starts: (0, 0))],
        out_specs=pl.BlockSpec((1, t), lambda i, starts: (0, 0)),
    )
    return pl.pallas_call(
        _positions_kernel,
        grid_spec=grid_spec,
        out_shape=jax.ShapeDtypeStruct((1, t), jnp.int32),
        name="moe_positions",
    )(starts, brk).reshape(t)


def _routing_tables(counts, brk):
    tm = MOE_TILE
    t = brk.shape[1]
    nt = t // tm + N_BUCKETS
    cnt = counts[0, :N_BUCKETS].astype(jnp.int32)
    padded = ((cnt + tm - 1) // tm) * tm
    ends = jnp.cumsum(padded)
    starts = ends - padded
    pos = _positions(starts, brk)
    fill = jnp.concatenate([starts + cnt, ends])
    ntiles = ends[-1:] // tm
    tile_start = jnp.arange(nt, dtype=jnp.int32) * tm
    tile_bucket = jnp.sum((ends[None, :] <= tile_start[:, None]).astype(jnp.int32), axis=1)
    used = (tile_bucket < N_BUCKETS).astype(jnp.int32)
    tile_bucket = jnp.minimum(tile_bucket, N_BUCKETS - 1)
    pair = tile_bucket % N_PAIRS
    group0 = (tile_bucket // N_PAIRS) * EXPERTS_PER_GROUP
    e_lo = group0 + jnp.array([0, 0, 0, 1, 1, 2], jnp.int32)[pair]
    e_hi = group0 + jnp.array([1, 2, 3, 2, 3, 3], jnp.int32)[pair]
    return pos, fill, ntiles, e_lo, e_hi, used, nt


def kernel(x, mem, mix_norm, w_in, conv_w, hgrn_lb, hgrn_norm, w_out, xattn_norm, mem_norm,
           w_q, w_kv, w_o, ffn_norm, w_group, b_group, w_expert, b_expert, w_gate, w_up,
           w_down, final_norm):
    bsz, seq, d = x.shape
    assert d == D_MODEL and seq % SEQ_TILE == 0 and seq % ATTN_TILE == 0
    assert (bsz * seq) % MOE_TILE == 0 and (bsz * seq) % FINAL_TILE == 0
    assert (bsz * seq) % DISPATCH_TILE == 0 and bsz * seq >= 2 * DISPATCH_TILE
    assert 2 * DISPATCH_TILE >= MOE_TILE and mix_norm.shape[0] == 1
    bf = lambda w: w.astype(BF16)

    x1 = _mixer(x, mix_norm, bf(w_in[0]), conv_w[0], hgrn_lb, hgrn_norm, bf(w_out[0]))
    kv = _kv_proj(mem, mem_norm, bf(w_kv[0]))

    pad = LANES - N_GROUPS - N_EXPERTS
    w_router = jnp.concatenate(
        [w_group[0], w_expert[0], jnp.zeros((d, pad), F32)], axis=1)
    b_router = jnp.concatenate(
        [b_group[0], b_expert[0], jnp.zeros((pad,), F32)])[None, :]
    w_router_hi = bf(w_router)
    w_router_lo = bf(w_router - w_router_hi.astype(F32))
    w_router2 = jnp.concatenate([w_router_hi, w_router_lo], axis=1)
    w_q_scaled = bf(w_q[0] * (XATTN_HEAD_DIM ** -0.5))
    x2r, brk, counts = _attention(x1, kv, xattn_norm, w_q_scaled, bf(w_o[0]), ffn_norm,
                                  w_router2, b_router)

    pos, fill, ntiles, e_lo, e_hi, used, nt = _routing_tables(counts, brk)
    xs = _dispatch(pos, fill, ntiles, x2r, nt * MOE_TILE)
    y = _moe_sparse(e_lo, e_hi, used, xs, ffn_norm, w_router2, b_router, final_norm[None, :],
                    bf(w_gate[0]), bf(w_up[0]), bf(w_down[0]))
    return _final(pos, y).reshape(bsz, seq, d)
```

```python
import jax
import jax.numpy as jnp
from jax import lax
from jax.experimental import pallas as pl
from jax.experimental.pallas import tpu as pltpu

F32 = jnp.float32
BF16 = jnp.bfloat16

D_MODEL = 1024
CONV_WIDTH = 512
HGRN_WIDTH = 512
HGRN_HEADS = 4
HEAD_DIM = 128
N_PROJ_SLOTS = 7
PROJ_WIDTH = N_PROJ_SLOTS * 512
XATTN_HEADS = 4
XATTN_HEAD_DIM = 256
N_GROUPS = 4
EXPERTS_PER_GROUP = 4
N_EXPERTS = 16
D_EXPERT = 512
EPS = 1e-6

LANES = 128
SUBLANES = 8
CHUNK = 64
CHUNK_LEVELS = 6
HGRN_TILE = 4 * CHUNK
SEQ_TILE = 512
KV_TILE = 1024
ATTN_TILE = 1024
RANK_BLOCK = 256
DISPATCH_TILE = 1024
MOE_TILE = 512
FINAL_TILE = 1024
N_PAIRS = 6
N_BUCKETS = N_GROUPS * N_PAIRS
ROW_TILES = D_MODEL // LANES
DMA_QUEUES = 2
VMEM_LIMIT = 56 * 1024 * 1024


def _rms(x, g):
    return x * lax.rsqrt(jnp.mean(x * x, axis=-1, keepdims=True) + EPS) * g


def _dot(a, b):
    return jnp.dot(a, b, preferred_element_type=F32)


def _dot_nt(a, b):
    return lax.dot_general(a, b, (((1,), (1,)), ((), ())), preferred_element_type=F32)


def _dot_tn(a, b):
    return lax.dot_general(a, b, (((0,), (0,)), ((), ())), preferred_element_type=F32)


def _roll_rows(x, shift):
    return pltpu.roll(x, shift % x.shape[0], axis=0)


def _level_exponents(logf2, use_level):
    n = CHUNK // SUBLANES
    sub = lax.broadcasted_iota(jnp.int32, (SUBLANES, logf2.shape[1]), 0)
    roll = lambda x, s: pltpu.roll(x, s % SUBLANES, axis=0)
    r = [logf2[SUBLANES * j:SUBLANES * (j + 1), :] for j in range(n)]
    for lvl in range(1, CHUNK_LEVELS + 1):
        half = 1 << (lvl - 1)
        g = [None] * n
        if half < SUBLANES:
            second = (sub & half) != 0
            for j in range(n):
                last = r[j]
                w = 1
                while w < half:
                    last = jnp.where((sub & w) != 0, last, roll(last, -w))
                    w *= 2
                tot = jnp.where(second, roll(last, half), last)
                g[j] = jnp.where(second, r[j], tot - r[j])
                r[j] = jnp.where(second, r[j] + tot, r[j])
        else:
            hv = half // SUBLANES
            for j0 in range(0, n, 2 * hv):
                mid = r[j0 + hv - 1]
                tot = jnp.broadcast_to(mid[SUBLANES - 1:SUBLANES, :], mid.shape)
                for j in range(j0, j0 + hv):
                    g[j] = tot - r[j]
                for j in range(j0 + hv, j0 + 2 * hv):
                    g[j] = r[j]
                    r[j] = r[j] + tot
        use_level(lvl, jnp.concatenate(g, axis=0))
    return jnp.concatenate(r, axis=0)


def _sigmoid(x):
    return 0.5 * jnp.tanh(0.5 * x) + 0.5


def _split_levels():
    import numpy as np
    t = np.arange(CHUNK)[:, None]
    s = np.arange(CHUNK)[None, :]
    msb = np.floor(np.log2(np.maximum(t ^ s, 1))).astype(np.int32) + 1
    return np.where(s < t, msb, np.where(s == t, 0, -1)).astype(np.int32)


def _hgrn_chunk(q, z, v, lb, levels):
    half_th = 0.5 * jnp.tanh(0.5 * z)
    one_m_lb = 1.0 - lb
    logf2 = jnp.log2(lb + one_m_lb * (0.5 + half_th))
    k = one_m_lb * (0.5 - half_th)

    qb = q.astype(BF16)
    kb = k.astype(BF16)
    scores = [jnp.where(levels == 0, jnp.sum(q * k, axis=-1, keepdims=True), 0.0)]

    def use_level(lvl, g):
        decay = jnp.exp2(g).astype(BF16)
        scores[0] = jnp.where(levels == lvl, _dot_nt(qb * decay, kb * decay), scores[0])

    b2 = _level_exponents(logf2, use_level)
    a = scores[0]
    b2_last = b2[CHUNK - 1:CHUNK, :]

    qe = (q * jnp.exp2(b2)).astype(BF16)
    kd = (k * jnp.exp2(b2_last - b2)).astype(BF16)
    return a, qe, kd, v.astype(BF16), b2_last


def _hgrn_tile(chunks, st):
    (a0, qe0, kd0, v0, bl0), (a1, qe1, kd1, v1, bl1), (a2, qe2, kd2, v2, bl2), \
        (a3, qe3, kd3, v3, bl3) = chunks
    scale = lambda x, log2_decay: x * jnp.exp2(log2_decay).astype(BF16)
    bf = lambda x: x.astype(BF16)
    cum1 = bl0
    cum2 = cum1 + bl1
    cum3 = cum2 + bl2
    cum4 = cum3 + bl3
    s10 = bf(_dot_nt(qe1, kd0))
    s32 = bf(_dot_nt(qe3, kd2))
    s8 = bf(_dot_nt(jnp.concatenate([qe2, scale(qe3, bl2)], axis=0),
                    jnp.concatenate([scale(kd0, bl1), kd1], axis=0)))
    v01 = jnp.concatenate([v0, v1], axis=0)
    stb = bf(st)
    o0 = _dot(bf(a0), v0) + _dot_nt(qe0, stb)
    o1 = _dot(bf(a1), v1) + _dot(s10, v0) + _dot_nt(scale(qe1, cum1), stb)
    o2 = _dot(bf(a2), v2) + _dot(s8[:CHUNK], v01) + _dot_nt(scale(qe2, cum2), stb)
    o3 = (_dot(bf(a3), v3) + _dot(s32, v2) + _dot(s8[CHUNK:], v01)
          + _dot_nt(scale(qe3, cum3), stb))
    kd_all = jnp.concatenate(
        [scale(kd0, cum4 - cum1), scale(kd1, cum4 - cum2), scale(kd2, bl3), kd3], axis=0)
    v_all = jnp.concatenate([v0, v1, v2, v3], axis=0)
    st_new = st * jnp.exp2(cum4) + _dot_tn(v_all, kd_all)
    return [o0, o1, o2, o3], st_new


def _mixer_kernel(x_ref, gmix_ref, win_ref, convw_ref, lbraw_ref, hnorm_ref, wout_ref, lvl_ref,
                  o_ref, p_scr, y_scr, st_scr, tail_scr):
    j = pl.program_id(1)

    @pl.when(j == 0)
    def _():
        st_scr[...] = jnp.zeros_like(st_scr)
        tail_scr[...] = jnp.zeros_like(tail_scr)

    x = x_ref[0]
    h = _rms(x, gmix_ref[...]).astype(BF16)
    p_scr[...] = _dot(h, win_ref[...])

    ts = x.shape[0]
    cb = p_scr[:, 0:CONV_WIDTH]
    u = p_scr[:, CONV_WIDTH:2 * CONV_WIDTH] * p_scr[:, 2 * CONV_WIDTH:3 * CONV_WIDTH]
    row = lax.broadcasted_iota(jnp.int32, u.shape, 0)
    prev1 = tail_scr[7:8, :]
    prev2 = tail_scr[6:7, :]
    u1 = jnp.where(row == 0, prev1, _roll_rows(u, 1))
    u2 = jnp.where(row == 0, prev2, jnp.where(row == 1, prev1, _roll_rows(u, 2)))
    cw = convw_ref[...]
    conv = u2 * cw[0:1, :] + u1 * cw[1:2, :] + u * cw[2:3, :]
    y_scr[:, 0:CONV_WIDTH] = (cb * conv).astype(BF16)
    tail_scr[...] = u[ts - 8:ts, :]

    raw = lbraw_ref[...]
    mx = jnp.max(raw, axis=0, keepdims=True)
    ex = jnp.exp(raw - mx)
    lb_all = ex[0:1, :] / jnp.sum(ex, axis=0, keepdims=True)
    hn = hnorm_ref[...]

    for hd in range(HGRN_HEADS):
        lo = hd * HEAD_DIM
        sl = slice(lo, lo + HEAD_DIM)
        col = lambda slot: slice(slot * 512 + lo, slot * 512 + lo + HEAD_DIM)
        for t0 in range(0, ts, HGRN_TILE):
            chunks = []
            for r0 in range(t0, t0 + HGRN_TILE, CHUNK):
                rows = slice(r0, r0 + CHUNK)
                chunks.append(_hgrn_chunk(p_scr[rows, col(3)], p_scr[rows, col(4)],
                                          p_scr[rows, col(5)], lb_all[:, sl], lvl_ref[...]))
            outs, st_new = _hgrn_tile(chunks, st_scr[hd])
            st_scr[hd] = st_new
            for c, o in enumerate(outs):
                rows = slice(t0 + c * CHUNK, t0 + (c + 1) * CHUNK)
                g = p_scr[rows, col(6)]
                o = o * lax.rsqrt(jnp.mean(o * o, axis=-1, keepdims=True) + EPS) * hn[:, sl]
                y_scr[rows, CONV_WIDTH + lo:CONV_WIDTH + lo + HEAD_DIM] = \
                    (o * (g * _sigmoid(g))).astype(BF16)

    o_ref[0] = x + _dot(y_scr[...], wout_ref[...])


def _mixer(x, mix_norm, w_in, conv_w, hgrn_lb, hgrn_norm, w_out):
    bsz, seq, d = x.shape
    ts = SEQ_TILE
    const = lambda b, j: (0, 0)
    return pl.pallas_call(
        _mixer_kernel,
        grid=(bsz, seq // ts),
        in_specs=[
            pl.BlockSpec((1, ts, d), lambda b, j: (b, j, 0)),
            pl.BlockSpec((1, d), const),
            pl.BlockSpec((d, PROJ_WIDTH), const),
            pl.BlockSpec((3, CONV_WIDTH), const),
            pl.BlockSpec((2, HGRN_WIDTH), const),
            pl.BlockSpec((1, HGRN_WIDTH), const),
            pl.BlockSpec((d, d), const),
            pl.BlockSpec((CHUNK, CHUNK), const),
        ],
        out_specs=pl.BlockSpec((1, ts, d), lambda b, j: (b, j, 0)),
        out_shape=jax.ShapeDtypeStruct((bsz, seq, d), F32),
        scratch_shapes=[
            pltpu.VMEM((ts, PROJ_WIDTH), F32),
            pltpu.VMEM((ts, d), BF16),
            pltpu.VMEM((HGRN_HEADS, HEAD_DIM, HEAD_DIM), F32),
            pltpu.VMEM((8, CONV_WIDTH), F32),
        ],
        compiler_params=pltpu.CompilerParams(
            dimension_semantics=("arbitrary", "arbitrary"), vmem_limit_bytes=VMEM_LIMIT),
        name="mixer",
    )(x, mix_norm, w_in, conv_w, hgrn_lb, hgrn_norm, w_out, jnp.asarray(_split_levels()))


def _kv_kernel(m_ref, g_ref, w_ref, o_ref):
    h = _rms(m_ref[...], g_ref[...]).astype(BF16)
    o_ref[...] = _dot(h, w_ref[...]).astype(BF16)


def _kv_proj(mem, mem_norm, w_kv):
    bsz, mlen, d = mem.shape
    rows = bsz * mlen
    tk = min(KV_TILE, rows)
    assert rows % tk == 0
    const = lambda i: (0, 0)
    kv = pl.pallas_call(
        _kv_kernel,
        grid=(rows // tk,),
        in_specs=[
            pl.BlockSpec((tk, d), lambda i: (i, 0)),
            pl.BlockSpec((1, d), const),
            pl.BlockSpec((d, 2 * d), const),
        ],
        out_specs=pl.BlockSpec((tk, 2 * d), lambda i: (i, 0)),
        out_shape=jax.ShapeDtypeStruct((rows, 2 * d), BF16),
        compiler_params=pltpu.CompilerParams(
            dimension_semantics=("arbitrary",), vmem_limit_bytes=VMEM_LIMIT),
        name="kv_proj",
    )(mem.reshape(rows, d), mem_norm, w_kv)
    return kv.reshape(bsz, mlen, 2 * d)


def _first_argmax(vals, lane):
    mx = jnp.max(vals, axis=-1, keepdims=True)
    idx = jnp.min(jnp.where(vals == mx, lane, float(LANES)), axis=-1, keepdims=True)
    return mx, idx


def _router_logits(h3, wr_ref, br_ref):
    h3_hi = h3.astype(BF16)
    h3_lo = (h3 - h3_hi.astype(F32)).astype(BF16)
    two = _dot(h3_hi, wr_ref[...])
    return (two[:, :LANES] + two[:, LANES:]) + _dot(h3_lo, wr_ref[:, :LANES]) + br_ref[...]


def _route(logits, running, tri):
    lane = lax.broadcasted_iota(jnp.int32, logits.shape, 1).astype(F32)
    neg = jnp.float32(-jnp.inf)
    gl = jnp.where(lane < N_GROUPS, logits, neg)
    _, gidx = _first_argmax(gl, lane)
    base = N_GROUPS + EXPERTS_PER_GROUP * gidx
    el = jnp.where((lane >= base) & (lane < base + EXPERTS_PER_GROUP), logits, neg)
    _, i1 = _first_argmax(el, lane)
    _, i2 = _first_argmax(jnp.where(lane == i1, neg, el), lane)
    lo = jnp.minimum(i1, i2) - base
    hi = jnp.maximum(i1, i2) - base
    bucket = gidx * N_PAIRS + (lo * (7.0 - lo)) * 0.5 + hi - lo - 1.0

    onehot = lane == bucket
    ranks = []
    for r0 in range(0, logits.shape[0], RANK_BLOCK):
        oh = onehot[r0:r0 + RANK_BLOCK]
        before = _dot(tri, oh.astype(BF16))
        ranks.append(jnp.sum(jnp.where(oh, before + running, 0.0), axis=-1, keepdims=True))
        running = running + jnp.sum(oh.astype(F32), axis=0, keepdims=True)
    rank = jnp.concatenate(ranks, axis=0)
    info = jnp.where(lane == 0, bucket, jnp.where(lane == 1, rank, 0.0))
    return info, running


def _gates(logits, e_lo, e_hi):
    lane = lax.broadcasted_iota(jnp.int32, logits.shape, 1)
    neg = jnp.float32(-jnp.inf)
    gl = jnp.where(lane < N_GROUPS, logits, neg)
    g_p = 1.0 / jnp.sum(jnp.exp(gl - jnp.max(gl, axis=-1, keepdims=True)), axis=-1, keepdims=True)
    l_lo = jnp.sum(jnp.where(lane == N_GROUPS + e_lo, logits, 0.0), axis=-1, keepdims=True)
    l_hi = jnp.sum(jnp.where(lane == N_GROUPS + e_hi, logits, 0.0), axis=-1, keepdims=True)
    m = jnp.maximum(l_lo, l_hi)
    p_lo = jnp.exp(l_lo - m)
    p_hi = jnp.exp(l_hi - m)
    inv = g_p / (p_lo + p_hi)
    return p_lo * inv, p_hi * inv


def _attn_kernel(x_ref, kv_ref, gx_ref, wq_ref, wo_ref, gf_ref, wr_ref, br_ref, tri_ref,
                 x2r_ref, brk_ref, counts_ref, o_scr, cnt_scr):
    @pl.when((pl.program_id(0) == 0) & (pl.program_id(1) == 0))
    def _():
        cnt_scr[...] = jnp.zeros_like(cnt_scr)

    x = x_ref[0]
    tq = x.shape[0]
    h = _rms(x, gx_ref[...]).astype(BF16)
    q = _dot(h, wq_ref[...])
    for hd in range(XATTN_HEADS):
        lo = hd * XATTN_HEAD_DIM
        qh = q[:, lo:lo + XATTN_HEAD_DIM].astype(BF16)
        kh = kv_ref[0, :, lo:lo + XATTN_HEAD_DIM]
        vh = kv_ref[0, :, D_MODEL + lo:D_MODEL + lo + XATTN_HEAD_DIM]
        s = _dot_nt(qh, kh)
        p = jnp.exp(s - jnp.max(s, axis=-1, keepdims=True))
        inv = 1.0 / jnp.sum(p, axis=-1, keepdims=True)
        o_scr[:, lo:lo + XATTN_HEAD_DIM] = (_dot(p.astype(BF16), vh) * inv).astype(BF16)
    x2 = x + _dot(o_scr[...], wo_ref[...])
    for s in range(ROW_TILES):
        x2r_ref[pl.ds(s, tq, stride=ROW_TILES), :] = x2[:, s * LANES:(s + 1) * LANES]
    logits = _router_logits(_rms(x2, gf_ref[...]), wr_ref, br_ref)
    info, running = _route(logits, cnt_scr[...], tri_ref[...])
    cnt_scr[...] = running
    counts_ref[...] = running
    brk_ref[...] = info.T[0:8, :].astype(jnp.int32)


def _attention(x1, kv, xattn_norm, w_q, w_o, ffn_norm, w_router, b_router):
    bsz, seq, d = x1.shape
    mlen = kv.shape[1]
    tq = ATTN_TILE
    nj = seq // tq
    const = lambda b, j: (0, 0)
    tile = lambda b, j: (b, j, 0)
    tri = jnp.tri(RANK_BLOCK, RANK_BLOCK, -1, dtype=BF16)
    return pl.pallas_call(
        _attn_kernel,
        grid=(bsz, nj),
        in_specs=[
            pl.BlockSpec((1, tq, d), tile),
            pl.BlockSpec((1, mlen, 2 * d), lambda b, j: (b, 0, 0)),
            pl.BlockSpec((1, d), const),
            pl.BlockSpec((d, d), const),
            pl.BlockSpec((d, d), const),
            pl.BlockSpec((1, d), const),
            pl.BlockSpec((d, 2 * LANES), const),
            pl.BlockSpec((1, LANES), const),
            pl.BlockSpec((RANK_BLOCK, RANK_BLOCK), const),
        ],
        out_specs=[
            pl.BlockSpec((tq * ROW_TILES, LANES), lambda b, j: (b * nj + j, 0)),
            pl.BlockSpec((8, tq), lambda b, j: (0, b * nj + j)),
            pl.BlockSpec((1, LANES), const),
        ],
        out_shape=[
            jax.ShapeDtypeStruct((bsz * seq * ROW_TILES, LANES), F32),
            jax.ShapeDtypeStruct((8, bsz * seq), jnp.int32),
            jax.ShapeDtypeStruct((1, LANES), F32),
        ],
        scratch_shapes=[pltpu.VMEM((tq, d), BF16), pltpu.VMEM((1, LANES), F32)],
        compiler_params=pltpu.CompilerParams(
            dimension_semantics=("arbitrary", "arbitrary"), vmem_limit_bytes=VMEM_LIMIT),
        name="xattn_router",
    )(x1, kv, xattn_norm, w_q, w_o, ffn_norm, w_router, b_router, tri)


def _item_copy(hbm, hbm_item, buf, buf_item, rows_per_item, sem, to_hbm):
    h = hbm.at[pl.ds(pl.multiple_of(hbm_item * rows_per_item, rows_per_item), rows_per_item)]
    b = buf.at[pl.ds(pl.multiple_of(buf_item * rows_per_item, rows_per_item), rows_per_item)]
    return pltpu.make_async_copy(b, h, sem) if to_hbm else pltpu.make_async_copy(h, b, sem)


def _items_wait(hbm, buf, buf_item, n_items, rows_per_item, sem, to_hbm):
    n = n_items * rows_per_item
    h = hbm.at[pl.ds(0, n)]
    b = buf.at[pl.ds(pl.multiple_of(buf_item * rows_per_item, rows_per_item), n)]
    (pltpu.make_async_copy(b, h, sem) if to_hbm else pltpu.make_async_copy(h, b, sem)).wait()


def _dispatch_kernel(pos_ref, fill_ref, ntiles_ref, x2r_ref, xs_hbm, buf, sem):
    i = pl.program_id(0)
    nt = pl.num_programs(0)
    td = DISPATCH_TILE
    rpi = ROW_TILES
    slot = i % 2
    base = slot * td

    @pl.when(i >= 2)
    def _():
        _items_wait(xs_hbm, buf, base, td, rpi, sem.at[slot], True)

    buf[pl.ds(pl.multiple_of(base * rpi, td * rpi), td * rpi), :] = x2r_ref[...]

    def send(r2, carry):
        for k in range(DMA_QUEUES):
            r = r2 * DMA_QUEUES + k
            _item_copy(xs_hbm, pos_ref[i * td + r], buf, base + r, rpi, sem.at[slot],
                       True).start(priority=k)
        return carry

    lax.fori_loop(0, td // DMA_QUEUES, send, 0, unroll=4)

    @pl.when(i == nt - 1)
    def _():
        _items_wait(xs_hbm, buf, base, td, rpi, sem.at[slot], True)
        _items_wait(xs_hbm, buf, (1 - slot) * td, td, rpi, sem.at[1 - slot], True)
        buf[...] = jnp.zeros_like(buf)
        zsem = sem.at[2]
        bits = range(MOE_TILE.bit_length() - 2, -1, -1)

        def pad_copy(b, k):
            first = fill_ref[b]
            n = fill_ref[N_BUCKETS + b] - first
            done = (n >> (k + 1)) << (k + 1)
            rows = (1 << k) * rpi
            cp = pltpu.make_async_copy(
                buf.at[pl.ds(0, rows)],
                xs_hbm.at[pl.ds(pl.multiple_of((first + done) * rpi, rpi), rows)], zsem)
            return ((n >> k) & 1) == 1, cp

        for wait in (False, True):
            for b in range(N_BUCKETS):
                for k in bits:
                    present, cp = pad_copy(b, k)

                    @pl.when(present)
                    def _():
                        cp.wait() if wait else cp.start()

        n_tiles_total = xs_hbm.shape[0] // (MOE_TILE * rpi)

        def tile_copy(tile):
            return pltpu.make_async_copy(
                buf.at[pl.ds(0, MOE_TILE * rpi)],
                xs_hbm.at[pl.ds(pl.multiple_of(tile * MOE_TILE * rpi, MOE_TILE * rpi),
                                MOE_TILE * rpi)], zsem)

        def fill_tile(tile, carry):
            tile_copy(tile).start()
            return carry

        def drain_tile(tile, carry):
            tile_copy(tile).wait()
            return carry

        lax.fori_loop(ntiles_ref[0], n_tiles_total, fill_tile, 0)
        lax.fori_loop(ntiles_ref[0], n_tiles_total, drain_tile, 0)


def _dispatch(pos, fill, ntiles, x2r, n_sorted):
    t = x2r.shape[0] // ROW_TILES
    td = DISPATCH_TILE
    grid_spec = pltpu.PrefetchScalarGridSpec(
        num_scalar_prefetch=3,
        grid=(t // td,),
        in_specs=[pl.BlockSpec((td * ROW_TILES, LANES), lambda i, *_: (i, 0))],
        out_specs=pl.BlockSpec(memory_space=pl.ANY),
        scratch_shapes=[
            pltpu.VMEM((2 * td * ROW_TILES, LANES), F32),
            pltpu.SemaphoreType.DMA((3,)),
        ],
    )
    return pl.pallas_call(
        _dispatch_kernel,
        grid_spec=grid_spec,
        out_shape=jax.ShapeDtypeStruct((n_sorted * ROW_TILES, LANES), F32),
        compiler_params=pltpu.CompilerParams(
            dimension_semantics=("arbitrary",), vmem_limit_bytes=VMEM_LIMIT),
        name="moe_dispatch",
    )(pos, fill, ntiles, x2r)


def _silu(x):
    return x * _sigmoid(x)


def _moe_kernel(elo_ref, ehi_ref, used_ref, xtile_ref,
                xs_ref, gf_ref, wr_ref, br_ref, gfin_ref,
                wg_lo, wu_lo, wd_lo, wg_hi, wu_hi, wd_hi, y_ref):
    del xtile_ref
    i = pl.program_id(0)
    tm = MOE_TILE

    @pl.when(used_ref[i] == 1)
    def _():
        x2 = jnp.concatenate(
            [xs_ref[pl.ds(s, tm, stride=ROW_TILES), :] for s in range(ROW_TILES)], axis=1)
        h3 = _rms(x2, gf_ref[...])
        g_lo, g_hi = _gates(_router_logits(h3, wr_ref, br_ref), elo_ref[i], ehi_ref[i])
        x = h3.astype(BF16)

        def expert(wg, wu, wd):
            hid = (_silu(_dot(x, wg[0])) * _dot(x, wu[0])).astype(BF16)
            return _dot(hid, wd[0])

        moe = g_lo * expert(wg_lo, wu_lo, wd_lo) + g_hi * expert(wg_hi, wu_hi, wd_hi)
        out = _rms(x2 + moe, gfin_ref[...])
        for s in range(ROW_TILES):
            y_ref[pl.ds(s, tm, stride=ROW_TILES), :] = out[:, s * LANES:(s + 1) * LANES]

    @pl.when(used_ref[i] == 0)
    def _():
        y_ref[...] = jnp.zeros_like(y_ref)


def _moe_sparse(e_lo, e_hi, used, xs, ffn_norm, w_router, b_router, final_norm,
                w_gate, w_up, w_down):
    tm = MOE_TILE
    nt = used.shape[0]
    d = D_MODEL
    lo = lambda i, elo, ehi, used, xt: (elo[i], 0, 0)
    hi = lambda i, elo, ehi, used, xt: (ehi[i], 0, 0)
    const = lambda i, *_: (0, 0)
    steps = jnp.arange(nt, dtype=jnp.int32)
    xs_tile = jnp.where(used == 1, steps, jnp.maximum(jnp.sum(used) - 1, 0))
    grid_spec = pltpu.PrefetchScalarGridSpec(
        num_scalar_prefetch=4,
        grid=(nt,),
        in_specs=[
            pl.BlockSpec((tm * ROW_TILES, LANES), lambda i, elo, ehi, used, xt: (xt[i], 0)),
            pl.BlockSpec((1, d), const),
            pl.BlockSpec((d, 2 * LANES), const),
            pl.BlockSpec((1, LANES), const),
            pl.BlockSpec((1, d), const),
            pl.BlockSpec((1, d, D_EXPERT), lo),
            pl.BlockSpec((1, d, D_EXPERT), lo),
            pl.BlockSpec((1, D_EXPERT, d), lo),
            pl.BlockSpec((1, d, D_EXPERT), hi),
            pl.BlockSpec((1, d, D_EXPERT), hi),
            pl.BlockSpec((1, D_EXPERT, d), hi),
        ],
        out_specs=pl.BlockSpec((tm * ROW_TILES, LANES), lambda i, *_: (i, 0)),
    )
    return pl.pallas_call(
        _moe_kernel,
        grid_spec=grid_spec,
        out_shape=jax.ShapeDtypeStruct((nt * tm * ROW_TILES, LANES), F32),
        compiler_params=pltpu.CompilerParams(
            dimension_semantics=("arbitrary",), vmem_limit_bytes=VMEM_LIMIT),
        name="moe_sparse",
    )(e_lo, e_hi, used, xs_tile, xs, ffn_norm, w_router, b_router, final_norm,
      w_gate, w_up, w_down, w_gate, w_up, w_down)


def _final_kernel(pos_ref, y_hbm, o_ref, ybuf, sem):
    i = pl.program_id(0)
    nt = pl.num_programs(0)
    tf = FINAL_TILE

    def start(tile, slot):
        def fetch(r2, carry):
            for k in range(DMA_QUEUES):
                r = r2 * DMA_QUEUES + k
                _item_copy(y_hbm, pos_ref[tile * tf + r], ybuf, slot * tf + r, ROW_TILES,
                           sem.at[slot], False).start(priority=k)
            return carry

        lax.fori_loop(0, tf // DMA_QUEUES, fetch, 0, unroll=4)

    @pl.when(i == 0)
    def _():
        start(0, 0)

    @pl.when(i + 1 < nt)
    def _():
        start(i + 1, (i + 1) % 2)

    slot = i % 2
    _items_wait(y_hbm, ybuf, slot * tf, tf, ROW_TILES, sem.at[slot], False)
    row0 = pl.multiple_of(slot * tf * ROW_TILES, ROW_TILES)
    for s in range(ROW_TILES):
        o_ref[:, s * LANES:(s + 1) * LANES] = ybuf[pl.ds(row0 + s, tf, stride=ROW_TILES), :]


def _final(pos, y):
    t = pos.shape[0]
    d = D_MODEL
    tf = FINAL_TILE
    grid_spec = pltpu.PrefetchScalarGridSpec(
        num_scalar_prefetch=1,
        grid=(t // tf,),
        in_specs=[pl.BlockSpec(memory_space=pl.ANY)],
        out_specs=pl.BlockSpec((tf, d), lambda i, *_: (i, 0)),
        scratch_shapes=[
            pltpu.VMEM((2 * tf * ROW_TILES, LANES), F32),
            pltpu.SemaphoreType.DMA((2,)),
        ],
    )
    return pl.pallas_call(
        _final_kernel,
        grid_spec=grid_spec,
        out_shape=jax.ShapeDtypeStruct((t, d), F32),
        compiler_params=pltpu.CompilerParams(
            dimension_semantics=("arbitrary",), vmem_limit_bytes=VMEM_LIMIT),
        name="moe_unpermute",
    )(pos, y)


def _positions_kernel(starts_ref, brk_ref, pos_ref):
    bucket = brk_ref[0:1, :]
    pos = brk_ref[1:2, :]
    for b in range(N_BUCKETS):
        pos = pos + jnp.where(bucket == b, starts_ref[b], 0)
    pos_ref[...] = pos


def _positions(starts, brk):
    t = brk.shape[1]
    grid_spec = pltpu.PrefetchScalarGridSpec(
        num_scalar_prefetch=1,
        grid=(1,),
        in_specs=[pl.BlockSpec(brk.shape, lambda i, starts: (0, 0))],
        out_specs=pl.BlockSpec((1, t), lambda i, starts: (0, 0)),
    )
    return pl.pallas_call(
        _positions_kernel,
        grid_spec=grid_spec,
        out_shape=jax.ShapeDtypeStruct((1, t), jnp.int32),
        name="moe_positions",
    )(starts, brk).reshape(t)


def _routing_tables(counts, brk):
    tm = MOE_TILE
    t = brk.shape[1]
    nt = t // tm + N_BUCKETS
    cnt = counts[0, :N_BUCKETS].astype(jnp.int32)
    padded = ((cnt + tm - 1) // tm) * tm
    ends = jnp.cumsum(padded)
    starts = ends - padded
    pos = _positions(starts, brk)
    fill = jnp.concatenate([starts + cnt, ends])
    ntiles = ends[-1:] // tm
    tile_start = jnp.arange(nt, dtype=jnp.int32) * tm
    tile_bucket = jnp.sum((ends[None, :] <= tile_start[:, None]).astype(jnp.int32), axis=1)
    used = (tile_bucket < N_BUCKETS).astype(jnp.int32)
    tile_bucket = jnp.minimum(tile_bucket, N_BUCKETS - 1)
    pair = tile_bucket % N_PAIRS
    group0 = (tile_bucket // N_PAIRS) * EXPERTS_PER_GROUP
    e_lo = group0 + jnp.array([0, 0, 0, 1, 1, 2], jnp.int32)[pair]
    e_hi = group0 + jnp.array([1, 2, 3, 2, 3, 3], jnp.int32)[pair]
    return pos, fill, ntiles, e_lo, e_hi, used, nt


def kernel(x, mem, mix_norm, w_in, conv_w, hgrn_lb, hgrn_norm, w_out, xattn_norm, mem_norm,
           w_q, w_kv, w_o, ffn_norm, w_group, b_group, w_expert, b_expert, w_gate, w_up,
           w_down, final_norm):
    bsz, seq, d = x.shape
    assert d == D_MODEL and seq % SEQ_TILE == 0 and seq % ATTN_TILE == 0
    assert (bsz * seq) % MOE_TILE == 0 and (bsz * seq) % FINAL_TILE == 0
    assert (bsz * seq) % DISPATCH_TILE == 0 and bsz * seq >= 2 * DISPATCH_TILE
    assert 2 * DISPATCH_TILE >= MOE_TILE and mix_norm.shape[0] == 1
    bf = lambda w: w.astype(BF16)

    x1 = _mixer(x, mix_norm, bf(w_in[0]), conv_w[0], hgrn_lb, hgrn_norm, bf(w_out[0]))
    kv = _kv_proj(mem, mem_norm, bf(w_kv[0]))

    pad = LANES - N_GROUPS - N_EXPERTS
    w_router = jnp.concatenate(
        [w_group[0], w_expert[0], jnp.zeros((d, pad), F32)], axis=1)
    b_router = jnp.concatenate(
        [b_group[0], b_expert[0], jnp.zeros((pad,), F32)])[None, :]
    w_router_hi = bf(w_router)
    w_router_lo = bf(w_router - w_router_hi.astype(F32))
    w_router2 = jnp.concatenate([w_router_hi, w_router_lo], axis=1)
    w_q_scaled = bf(w_q[0] * (XATTN_HEAD_DIM ** -0.5))
    x2r, brk, counts = _attention(x1, kv, xattn_norm, w_q_scaled, bf(w_o[0]), ffn_norm,
                                  w_router2, b_router)

    pos, fill, ntiles, e_lo, e_hi, used, nt = _routing_tables(counts, brk)
    xs = _dispatch(pos, fill, ntiles, x2r, nt * MOE_TILE)
    y = _moe_sparse(e_lo, e_hi, used, xs, ffn_norm, w_router2, b_router, final_norm[None, :],
                    bf(w_gate[0]), bf(w_up[0]), bf(w_down[0]))
    return _final(pos, y).reshape(bsz, seq, d)
```

```python
import jax
import jax.numpy as jnp
from jax import lax
from jax.experimental import pallas as pl
from jax.experimental.pallas import tpu as pltpu

F32 = jnp.float32
BF16 = jnp.bfloat16

D_MODEL = 1024
CONV_WIDTH = 512
HGRN_WIDTH = 512
HGRN_HEADS = 4
HEAD_DIM = 128
N_PROJ_SLOTS = 7
PROJ_WIDTH = N_PROJ_SLOTS * 512
XATTN_HEADS = 4
XATTN_HEAD_DIM = 256
N_GROUPS = 4
EXPERTS_PER_GROUP = 4
N_EXPERTS = 16
D_EXPERT = 512
EPS = 1e-6

LANES = 128
SUBLANES = 8
CHUNK = 64
CHUNK_LEVELS = 6
HGRN_TILE = 4 * CHUNK
SEQ_TILE = 512
KV_TILE = 1024
ATTN_TILE = 1024
RANK_BLOCK = 256
DISPATCH_TILE = 2048
MOE_TILE = 512
FINAL_TILE = 2048
N_PAIRS = 6
N_BUCKETS = N_GROUPS * N_PAIRS
ROW_TILES = D_MODEL // LANES
DMA_QUEUES = 2
VMEM_LIMIT = 56 * 1024 * 1024


def _rms(x, g):
    return x * lax.rsqrt(jnp.mean(x * x, axis=-1, keepdims=True) + EPS) * g


def _dot(a, b):
    return jnp.dot(a, b, preferred_element_type=F32)


def _dot_nt(a, b):
    return lax.dot_general(a, b, (((1,), (1,)), ((), ())), preferred_element_type=F32)


def _dot_tn(a, b):
    return lax.dot_general(a, b, (((0,), (0,)), ((), ())), preferred_element_type=F32)


def _roll_rows(x, shift):
    return pltpu.roll(x, shift % x.shape[0], axis=0)


def _level_exponents(logf2, use_level):
    n = CHUNK // SUBLANES
    sub = lax.broadcasted_iota(jnp.int32, (SUBLANES, logf2.shape[1]), 0)
    roll = lambda x, s: pltpu.roll(x, s % SUBLANES, axis=0)
    r = [logf2[SUBLANES * j:SUBLANES * (j + 1), :] for j in range(n)]
    for lvl in range(1, CHUNK_LEVELS + 1):
        half = 1 << (lvl - 1)
        g = [None] * n
        if half < SUBLANES:
            second = (sub & half) != 0
            for j in range(n):
                last = r[j]
                w = 1
                while w < half:
                    last = jnp.where((sub & w) != 0, last, roll(last, -w))
                    w *= 2
                tot = jnp.where(second, roll(last, half), last)
                g[j] = jnp.where(second, r[j], tot - r[j])
                r[j] = jnp.where(second, r[j] + tot, r[j])
        else:
            hv = half // SUBLANES
            for j0 in range(0, n, 2 * hv):
                mid = r[j0 + hv - 1]
                tot = jnp.broadcast_to(mid[SUBLANES - 1:SUBLANES, :], mid.shape)
                for j in range(j0, j0 + hv):
                    g[j] = tot - r[j]
                for j in range(j0 + hv, j0 + 2 * hv):
                    g[j] = r[j]
                    r[j] = r[j] + tot
        use_level(lvl, jnp.concatenate(g, axis=0))
    return jnp.concatenate(r, axis=0)


def _sigmoid(x):
    return 0.5 * jnp.tanh(0.5 * x) + 0.5


def _split_levels():
    import numpy as np
    t = np.arange(CHUNK)[:, None]
    s = np.arange(CHUNK)[None, :]
    msb = np.floor(np.log2(np.maximum(t ^ s, 1))).astype(np.int32) + 1
    return np.where(s < t, msb, np.where(s == t, 0, -1)).astype(np.int32)


def _hgrn_chunk(q, z, v, lb, levels):
    half_th = 0.5 * jnp.tanh(0.5 * z)
    one_m_lb = 1.0 - lb
    logf2 = jnp.log2(lb + one_m_lb * (0.5 + half_th))
    k = one_m_lb * (0.5 - half_th)

    qb = q.astype(BF16)
    kb = k.astype(BF16)
    scores = [jnp.where(levels == 0, jnp.sum(q * k, axis=-1, keepdims=True), 0.0)]

    def use_level(lvl, g):
        decay = jnp.exp2(g).astype(BF16)
        scores[0] = jnp.where(levels == lvl, _dot_nt(qb * decay, kb * decay), scores[0])

    b2 = _level_exponents(logf2, use_level)
    a = scores[0]
    b2_last = b2[CHUNK - 1:CHUNK, :]

    qe = (q * jnp.exp2(b2)).astype(BF16)
    kd = (k * jnp.exp2(b2_last - b2)).astype(BF16)
    return a, qe, kd, v.astype(BF16), b2_last


def _hgrn_tile(chunks, st):
    (a0, qe0, kd0, v0, bl0), (a1, qe1, kd1, v1, bl1), (a2, qe2, kd2, v2, bl2), \
        (a3, qe3, kd3, v3, bl3) = chunks
    scale = lambda x, log2_decay: x * jnp.exp2(log2_decay).astype(BF16)
    bf = lambda x: x.astype(BF16)
    cum1 = bl0
    cum2 = cum1 + bl1
    cum3 = cum2 + bl2
    cum4 = cum3 + bl3
    s10 = bf(_dot_nt(qe1, kd0))
    s32 = bf(_dot_nt(qe3, kd2))
    s8 = bf(_dot_nt(jnp.concatenate([qe2, scale(qe3, bl2)], axis=0),
                    jnp.concatenate([scale(kd0, bl1), kd1], axis=0)))
    v01 = jnp.concatenate([v0, v1], axis=0)
    stb = bf(st)
    o0 = _dot(bf(a0), v0) + _dot_nt(qe0, stb)
    o1 = _dot(bf(a1), v1) + _dot(s10, v0) + _dot_nt(scale(qe1, cum1), stb)
    o2 = _dot(bf(a2), v2) + _dot(s8[:CHUNK], v01) + _dot_nt(scale(qe2, cum2), stb)
    o3 = (_dot(bf(a3), v3) + _dot(s32, v2) + _dot(s8[CHUNK:], v01)
          + _dot_nt(scale(qe3, cum3), stb))
    kd_all = jnp.concatenate(
        [scale(kd0, cum4 - cum1), scale(kd1, cum4 - cum2), scale(kd2, bl3), kd3], axis=0)
    v_all = jnp.concatenate([v0, v1, v2, v3], axis=0)
    st_new = st * jnp.exp2(cum4) + _dot_tn(v_all, kd_all)
    return [o0, o1, o2, o3], st_new


def _mixer_kernel(x_ref, gmix_ref, win_ref, convw_ref, lbraw_ref, hnorm_ref, wout_ref, lvl_ref,
                  o_ref, p_scr, y_scr, st_scr, tail_scr):
    j = pl.program_id(1)

    @pl.when(j == 0)
    def _():
        st_scr[...] = jnp.zeros_like(st_scr)
        tail_scr[...] = jnp.zeros_like(tail_scr)

    x = x_ref[0]
    h = _rms(x, gmix_ref[...]).astype(BF16)
    p_scr[...] = _dot(h, win_ref[...])

    ts = x.shape[0]
    cb = p_scr[:, 0:CONV_WIDTH]
    u = p_scr[:, CONV_WIDTH:2 * CONV_WIDTH] * p_scr[:, 2 * CONV_WIDTH:3 * CONV_WIDTH]
    row = lax.broadcasted_iota(jnp.int32, u.shape, 0)
    prev1 = tail_scr[7:8, :]
    prev2 = tail_scr[6:7, :]
    u1 = jnp.where(row == 0, prev1, _roll_rows(u, 1))
    u2 = jnp.where(row == 0, prev2, jnp.where(row == 1, prev1, _roll_rows(u, 2)))
    cw = convw_ref[...]
    conv = u2 * cw[0:1, :] + u1 * cw[1:2, :] + u * cw[2:3, :]
    y_scr[:, 0:CONV_WIDTH] = (cb * conv).astype(BF16)
    tail_scr[...] = u[ts - 8:ts, :]

    raw = lbraw_ref[...]
    mx = jnp.max(raw, axis=0, keepdims=True)
    ex = jnp.exp(raw - mx)
    lb_all = ex[0:1, :] / jnp.sum(ex, axis=0, keepdims=True)
    hn = hnorm_ref[...]

    for hd in range(HGRN_HEADS):
        lo = hd * HEAD_DIM
        sl = slice(lo, lo + HEAD_DIM)
        col = lambda slot: slice(slot * 512 + lo, slot * 512 + lo + HEAD_DIM)
        for t0 in range(0, ts, HGRN_TILE):
            chunks = []
            for r0 in range(t0, t0 + HGRN_TILE, CHUNK):
                rows = slice(r0, r0 + CHUNK)
                chunks.append(_hgrn_chunk(p_scr[rows, col(3)], p_scr[rows, col(4)],
                                          p_scr[rows, col(5)], lb_all[:, sl], lvl_ref[...]))
            outs, st_new = _hgrn_tile(chunks, st_scr[hd])
            st_scr[hd] = st_new
            for c, o in enumerate(outs):
                rows = slice(t0 + c * CHUNK, t0 + (c + 1) * CHUNK)
                g = p_scr[rows, col(6)]
                o = o * lax.rsqrt(jnp.mean(o * o, axis=-1, keepdims=True) + EPS) * hn[:, sl]
                y_scr[rows, CONV_WIDTH + lo:CONV_WIDTH + lo + HEAD_DIM] = \
                    (o * (g * _sigmoid(g))).astype(BF16)

    o_ref[0] = x + _dot(y_scr[...], wout_ref[...])


def _mixer(x, mix_norm, w_in, conv_w, hgrn_lb, hgrn_norm, w_out):
    bsz, seq, d = x.shape
    ts = SEQ_TILE
    const = lambda b, j: (0, 0)
    return pl.pallas_call(
        _mixer_kernel,
        grid=(bsz, seq // ts),
        in_specs=[
            pl.BlockSpec((1, ts, d), lambda b, j: (b, j, 0)),
            pl.BlockSpec((1, d), const),
            pl.BlockSpec((d, PROJ_WIDTH), const),
            pl.BlockSpec((3, CONV_WIDTH), const),
            pl.BlockSpec((2, HGRN_WIDTH), const),
            pl.BlockSpec((1, HGRN_WIDTH), const),
            pl.BlockSpec((d, d), const),
            pl.BlockSpec((CHUNK, CHUNK), const),
        ],
        out_specs=pl.BlockSpec((1, ts, d), lambda b, j: (b, j, 0)),
        out_shape=jax.ShapeDtypeStruct((bsz, seq, d), F32),
        scratch_shapes=[
            pltpu.VMEM((ts, PROJ_WIDTH), F32),
            pltpu.VMEM((ts, d), BF16),
            pltpu.VMEM((HGRN_HEADS, HEAD_DIM, HEAD_DIM), F32),
            pltpu.VMEM((8, CONV_WIDTH), F32),
        ],
        compiler_params=pltpu.CompilerParams(
            dimension_semantics=("arbitrary", "arbitrary"), vmem_limit_bytes=VMEM_LIMIT),
        name="mixer",
    )(x, mix_norm, w_in, conv_w, hgrn_lb, hgrn_norm, w_out, jnp.asarray(_split_levels()))


def _kv_kernel(m_ref, g_ref, w_ref, o_ref):
    h = _rms(m_ref[...], g_ref[...]).astype(BF16)
    o_ref[...] = _dot(h, w_ref[...]).astype(BF16)


def _kv_proj(mem, mem_norm, w_kv):
    bsz, mlen, d = mem.shape
    rows = bsz * mlen
    tk = min(KV_TILE, rows)
    assert rows % tk == 0
    const = lambda i: (0, 0)
    kv = pl.pallas_call(
        _kv_kernel,
        grid=(rows // tk,),
        in_specs=[
            pl.BlockSpec((tk, d), lambda i: (i, 0)),
            pl.BlockSpec((1, d), const),
            pl.BlockSpec((d, 2 * d), const),
        ],
        out_specs=pl.BlockSpec((tk, 2 * d), lambda i: (i, 0)),
        out_shape=jax.ShapeDtypeStruct((rows, 2 * d), BF16),
        compiler_params=pltpu.CompilerParams(
            dimension_semantics=("arbitrary",), vmem_limit_bytes=VMEM_LIMIT),
        name="kv_proj",
    )(mem.reshape(rows, d), mem_norm, w_kv)
    return kv.reshape(bsz, mlen, 2 * d)


def _first_argmax(vals, lane):
    mx = jnp.max(vals, axis=-1, keepdims=True)
    idx = jnp.min(jnp.where(vals == mx, lane, float(LANES)), axis=-1, keepdims=True)
    return mx, idx


def _router_logits(h3, wr_ref, br_ref):
    h3_hi = h3.astype(BF16)
    h3_lo = (h3 - h3_hi.astype(F32)).astype(BF16)
    two = _dot(h3_hi, wr_ref[...])
    return (two[:, :LANES] + two[:, LANES:]) + _dot(h3_lo, wr_ref[:, :LANES]) + br_ref[...]


def _route(logits, running, tri):
    lane = lax.broadcasted_iota(jnp.int32, logits.shape, 1).astype(F32)
    neg = jnp.float32(-jnp.inf)
    gl = jnp.where(lane < N_GROUPS, logits, neg)
    _, gidx = _first_argmax(gl, lane)
    base = N_GROUPS + EXPERTS_PER_GROUP * gidx
    el = jnp.where((lane >= base) & (lane < base + EXPERTS_PER_GROUP), logits, neg)
    _, i1 = _first_argmax(el, lane)
    _, i2 = _first_argmax(jnp.where(lane == i1, neg, el), lane)
    lo = jnp.minimum(i1, i2) - base
    hi = jnp.maximum(i1, i2) - base
    bucket = gidx * N_PAIRS + (lo * (7.0 - lo)) * 0.5 + hi - lo - 1.0

    onehot = lane == bucket
    ranks = []
    for r0 in range(0, logits.shape[0], RANK_BLOCK):
        oh = onehot[r0:r0 + RANK_BLOCK]
        before = _dot(tri, oh.astype(BF16))
        ranks.append(jnp.sum(jnp.where(oh, before + running, 0.0), axis=-1, keepdims=True))
        running = running + jnp.sum(oh.astype(F32), axis=0, keepdims=True)
    rank = jnp.concatenate(ranks, axis=0)
    info = jnp.where(lane == 0, bucket, jnp.where(lane == 1, rank, 0.0))
    return info, running


def _gates(logits, e_lo, e_hi):
    lane = lax.broadcasted_iota(jnp.int32, logits.shape, 1)
    neg = jnp.float32(-jnp.inf)
    gl = jnp.where(lane < N_GROUPS, logits, neg)
    g_p = 1.0 / jnp.sum(jnp.exp(gl - jnp.max(gl, axis=-1, keepdims=True)), axis=-1, keepdims=True)
    l_lo = jnp.sum(jnp.where(lane == N_GROUPS + e_lo, logits, 0.0), axis=-1, keepdims=True)
    l_hi = jnp.sum(jnp.where(lane == N_GROUPS + e_hi, logits, 0.0), axis=-1, keepdims=True)
    m = jnp.maximum(l_lo, l_hi)
    p_lo = jnp.exp(l_lo - m)
    p_hi = jnp.exp(l_hi - m)
    inv = g_p / (p_lo + p_hi)
    return p_lo * inv, p_hi * inv


def _attn_kernel(x_ref, kv_ref, gx_ref, wq_ref, wo_ref, gf_ref, wr_ref, br_ref, tri_ref,
                 x2r_ref, brk_ref, counts_ref, o_scr, cnt_scr):
    @pl.when((pl.program_id(0) == 0) & (pl.program_id(1) == 0))
    def _():
        cnt_scr[...] = jnp.zeros_like(cnt_scr)

    x = x_ref[0]
    tq = x.shape[0]
    h = _rms(x, gx_ref[...]).astype(BF16)
    q = _dot(h, wq_ref[...])
    for hd in range(XATTN_HEADS):
        lo = hd * XATTN_HEAD_DIM
        qh = q[:, lo:lo + XATTN_HEAD_DIM].astype(BF16)
        kh = kv_ref[0, :, lo:lo + XATTN_HEAD_DIM]
        vh = kv_ref[0, :, D_MODEL + lo:D_MODEL + lo + XATTN_HEAD_DIM]
        s = _dot_nt(qh, kh)
        p = jnp.exp(s - jnp.max(s, axis=-1, keepdims=True))
        inv = 1.0 / jnp.sum(p, axis=-1, keepdims=True)
        o_scr[:, lo:lo + XATTN_HEAD_DIM] = (_dot(p.astype(BF16), vh) * inv).astype(BF16)
    x2 = x + _dot(o_scr[...], wo_ref[...])
    for s in range(ROW_TILES):
        x2r_ref[pl.ds(s, tq, stride=ROW_TILES), :] = x2[:, s * LANES:(s + 1) * LANES]
    logits = _router_logits(_rms(x2, gf_ref[...]), wr_ref, br_ref)
    info, running = _route(logits, cnt_scr[...], tri_ref[...])
    cnt_scr[...] = running
    counts_ref[...] = running
    brk_ref[...] = info.T[0:8, :].astype(jnp.int32)


def _attention(x1, kv, xattn_norm, w_q, w_o, ffn_norm, w_router, b_router):
    bsz, seq, d = x1.shape
    mlen = kv.shape[1]
    tq = ATTN_TILE
    nj = seq // tq
    const = lambda b, j: (0, 0)
    tile = lambda b, j: (b, j, 0)
    tri = jnp.tri(RANK_BLOCK, RANK_BLOCK, -1, dtype=BF16)
    return pl.pallas_call(
        _attn_kernel,
        grid=(bsz, nj),
        in_specs=[
            pl.BlockSpec((1, tq, d), tile),
            pl.BlockSpec((1, mlen, 2 * d), lambda b, j: (b, 0, 0)),
            pl.BlockSpec((1, d), const),
            pl.BlockSpec((d, d), const),
            pl.BlockSpec((d, d), const),
            pl.BlockSpec((1, d), const),
            pl.BlockSpec((d, 2 * LANES), const),
            pl.BlockSpec((1, LANES), const),
            pl.BlockSpec((RANK_BLOCK, RANK_BLOCK), const),
        ],
        out_specs=[
            pl.BlockSpec((tq * ROW_TILES, LANES), lambda b, j: (b * nj + j, 0)),
            pl.BlockSpec((8, tq), lambda b, j: (0, b * nj + j)),
            pl.BlockSpec((1, LANES), const),
        ],
        out_shape=[
            jax.ShapeDtypeStruct((bsz * seq * ROW_TILES, LANES), F32),
            jax.ShapeDtypeStruct((8, bsz * seq), jnp.int32),
            jax.ShapeDtypeStruct((1, LANES), F32),
        ],
        scratch_shapes=[pltpu.VMEM((tq, d), BF16), pltpu.VMEM((1, LANES), F32)],
        compiler_params=pltpu.CompilerParams(
            dimension_semantics=("arbitrary", "arbitrary"), vmem_limit_bytes=VMEM_LIMIT),
        name="xattn_router",
    )(x1, kv, xattn_norm, w_q, w_o, ffn_norm, w_router, b_router, tri)


def _item_copy(hbm, hbm_item, buf, buf_item, rows_per_item, sem, to_hbm):
    h = hbm.at[pl.ds(pl.multiple_of(hbm_item * rows_per_item, rows_per_item), rows_per_item)]
    b = buf.at[pl.ds(pl.multiple_of(buf_item * rows_per_item, rows_per_item), rows_per_item)]
    return pltpu.make_async_copy(b, h, sem) if to_hbm else pltpu.make_async_copy(h, b, sem)


def _items_wait(hbm, buf, buf_item, n_items, rows_per_item, sem, to_hbm):
    n = n_items * rows_per_item
    h = hbm.at[pl.ds(0, n)]
    b = buf.at[pl.ds(pl.multiple_of(buf_item * rows_per_item, rows_per_item), n)]
    (pltpu.make_async_copy(b, h, sem) if to_hbm else pltpu.make_async_copy(h, b, sem)).wait()


def _dispatch_kernel(pos_ref, fill_ref, ntiles_ref, x2r_ref, xs_hbm, buf, sem):
    i = pl.program_id(0)
    nt = pl.num_programs(0)
    td = DISPATCH_TILE
    rpi = ROW_TILES
    slot = i % 2
    base = slot * td

    @pl.when(i >= 2)
    def _():
        _items_wait(xs_hbm, buf, base, td, rpi, sem.at[slot], True)

    buf[pl.ds(pl.multiple_of(base * rpi, td * rpi), td * rpi), :] = x2r_ref[...]

    def send(r2, carry):
        for k in range(DMA_QUEUES):
            r = r2 * DMA_QUEUES + k
            _item_copy(xs_hbm, pos_ref[i * td + r], buf, base + r, rpi, sem.at[slot],
                       True).start(priority=k)
        return carry

    lax.fori_loop(0, td // DMA_QUEUES, send, 0, unroll=4)

    @pl.when(i == nt - 1)
    def _():
        _items_wait(xs_hbm, buf, base, td, rpi, sem.at[slot], True)
        _items_wait(xs_hbm, buf, (1 - slot) * td, td, rpi, sem.at[1 - slot], True)
        buf[...] = jnp.zeros_like(buf)
        zsem = sem.at[2]
        bits = range(MOE_TILE.bit_length() - 2, -1, -1)

        def pad_copy(b, k):
            first = fill_ref[b]
            n = fill_ref[N_BUCKETS + b] - first
            done = (n >> (k + 1)) << (k + 1)
            rows = (1 << k) * rpi
            cp = pltpu.make_async_copy(
                buf.at[pl.ds(0, rows)],
                xs_hbm.at[pl.ds(pl.multiple_of((first + done) * rpi, rpi), rows)], zsem)
            return ((n >> k) & 1) == 1, cp

        for wait in (False, True):
            for b in range(N_BUCKETS):
                for k in bits:
                    present, cp = pad_copy(b, k)

                    @pl.when(present)
                    def _():
                        cp.wait() if wait else cp.start()

        n_tiles_total = xs_hbm.shape[0] // (MOE_TILE * rpi)

        def tile_copy(tile):
            return pltpu.make_async_copy(
                buf.at[pl.ds(0, MOE_TILE * rpi)],
                xs_hbm.at[pl.ds(pl.multiple_of(tile * MOE_TILE * rpi, MOE_TILE * rpi),
                                MOE_TILE * rpi)], zsem)

        def fill_tile(tile, carry):
            tile_copy(tile).start()
            return carry

        def drain_tile(tile, carry):
            tile_copy(tile).wait()
            return carry

        lax.fori_loop(ntiles_ref[0], n_tiles_total, fill_tile, 0)
        lax.fori_loop(ntiles_ref[0], n_tiles_total, drain_tile, 0)


def _dispatch(pos, fill, ntiles, x2r, n_sorted):
    t = x2r.shape[0] // ROW_TILES
    td = DISPATCH_TILE
    grid_spec = pltpu.PrefetchScalarGridSpec(
        num_scalar_prefetch=3,
        grid=(t // td,),
        in_specs=[pl.BlockSpec((td * ROW_TILES, LANES), lambda i, *_: (i, 0))],
        out_specs=pl.BlockSpec(memory_space=pl.ANY),
        scratch_shapes=[
            pltpu.VMEM((2 * td * ROW_TILES, LANES), F32),
            pltpu.SemaphoreType.DMA((3,)),
        ],
    )
    return pl.pallas_call(
        _dispatch_kernel,
        grid_spec=grid_spec,
        out_shape=jax.ShapeDtypeStruct((n_sorted * ROW_TILES, LANES), F32),
        compiler_params=pltpu.CompilerParams(
            dimension_semantics=("arbitrary",), vmem_limit_bytes=VMEM_LIMIT),
        name="moe_dispatch",
    )(pos, fill, ntiles, x2r)


def _silu(x):
    return x * _sigmoid(x)


def _moe_kernel(elo_ref, ehi_ref, used_ref, xtile_ref,
                xs_ref, gf_ref, wr_ref, br_ref, gfin_ref,
                wg_lo, wu_lo, wd_lo, wg_hi, wu_hi, wd_hi, y_ref):
    del xtile_ref
    i = pl.program_id(0)
    tm = MOE_TILE

    @pl.when(used_ref[i] == 1)
    def _():
        x2 = jnp.concatenate(
            [xs_ref[pl.ds(s, tm, stride=ROW_TILES), :] for s in range(ROW_TILES)], axis=1)
        h3 = _rms(x2, gf_ref[...])
        g_lo, g_hi = _gates(_router_logits(h3, wr_ref, br_ref), elo_ref[i], ehi_ref[i])
        x = h3.astype(BF16)

        def expert(wg, wu, wd):
            hid = (_silu(_dot(x, wg[0])) * _dot(x, wu[0])).astype(BF16)
            return _dot(hid, wd[0])

        moe = g_lo * expert(wg_lo, wu_lo, wd_lo) + g_hi * expert(wg_hi, wu_hi, wd_hi)
        out = _rms(x2 + moe, gfin_ref[...])
        for s in range(ROW_TILES):
            y_ref[pl.ds(s, tm, stride=ROW_TILES), :] = out[:, s * LANES:(s + 1) * LANES]

    @pl.when(used_ref[i] == 0)
    def _():
        y_ref[...] = jnp.zeros_like(y_ref)


def _moe_sparse(e_lo, e_hi, used, xs, ffn_norm, w_router, b_router, final_norm,
                w_gate, w_up, w_down):
    tm = MOE_TILE
    nt = used.shape[0]
    d = D_MODEL
    lo = lambda i, elo, ehi, used, xt: (elo[i], 0, 0)
    hi = lambda i, elo, ehi, used, xt: (ehi[i], 0, 0)
    const = lambda i, *_: (0, 0)
    steps = jnp.arange(nt, dtype=jnp.int32)
    xs_tile = jnp.where(used == 1, steps, jnp.maximum(jnp.sum(used) - 1, 0))
    grid_spec = pltpu.PrefetchScalarGridSpec(
        num_scalar_prefetch=4,
        grid=(nt,),
        in_specs=[
            pl.BlockSpec((tm * ROW_TILES, LANES), lambda i, elo, ehi, used, xt: (xt[i], 0)),
            pl.BlockSpec((1, d), const),
            pl.BlockSpec((d, 2 * LANES), const),
            pl.BlockSpec((1, LANES), const),
            pl.BlockSpec((1, d), const),
            pl.BlockSpec((1, d, D_EXPERT), lo),
            pl.BlockSpec((1, d, D_EXPERT), lo),
            pl.BlockSpec((1, D_EXPERT, d), lo),
            pl.BlockSpec((1, d, D_EXPERT), hi),
            pl.BlockSpec((1, d, D_EXPERT), hi),
            pl.BlockSpec((1, D_EXPERT, d), hi),
        ],
        out_specs=pl.BlockSpec((tm * ROW_TILES, LANES), lambda i, *_: (i, 0)),
    )
    return pl.pallas_call(
        _moe_kernel,
        grid_spec=grid_spec,
        out_shape=jax.ShapeDtypeStruct((nt * tm * ROW_TILES, LANES), F32),
        compiler_params=pltpu.CompilerParams(
            dimension_semantics=("arbitrary",), vmem_limit_bytes=VMEM_LIMIT),
        name="moe_sparse",
    )(e_lo, e_hi, used, xs_tile, xs, ffn_norm, w_router, b_router, final_norm,
      w_gate, w_up, w_down, w_gate, w_up, w_down)


def _final_kernel(pos_ref, y_hbm, o_ref, ybuf, sem):
    i = pl.program_id(0)
    nt = pl.num_programs(0)
    tf = FINAL_TILE

    def start(tile, slot):
        def fetch(r2, carry):
            for k in range(DMA_QUEUES):
                r = r2 * DMA_QUEUES + k
                _item_copy(y_hbm, pos_ref[tile * tf + r], ybuf, slot * tf + r, ROW_TILES,
                           sem.at[slot], False).start(priority=k)
            return carry

        lax.fori_loop(0, tf // DMA_QUEUES, fetch, 0, unroll=4)

    @pl.when(i == 0)
    def _():
        start(0, 0)

    @pl.when(i + 1 < nt)
    def _():
        start(i + 1, (i + 1) % 2)

    slot = i % 2
    _items_wait(y_hbm, ybuf, slot * tf, tf, ROW_TILES, sem.at[slot], False)
    row0 = pl.multiple_of(slot * tf * ROW_TILES, ROW_TILES)
    for s in range(ROW_TILES):
        o_ref[:, s * LANES:(s + 1) * LANES] = ybuf[pl.ds(row0 + s, tf, stride=ROW_TILES), :]


def _final(pos, y):
    t = pos.shape[0]
    d = D_MODEL
    tf = FINAL_TILE
    grid_spec = pltpu.PrefetchScalarGridSpec(
        num_scalar_prefetch=1,
        grid=(t // tf,),
        in_specs=[pl.BlockSpec(memory_space=pl.ANY)],
        out_specs=pl.BlockSpec((tf, d), lambda i, *_: (i, 0)),
        scratch_shapes=[
            pltpu.VMEM((2 * tf * ROW_TILES, LANES), F32),
            pltpu.SemaphoreType.DMA((2,)),
        ],
    )
    return pl.pallas_call(
        _final_kernel,
        grid_spec=grid_spec,
        out_shape=jax.ShapeDtypeStruct((t, d), F32),
        compiler_params=pltpu.CompilerParams(
            dimension_semantics=("arbitrary",), vmem_limit_bytes=VMEM_LIMIT),
        name="moe_unpermute",
    )(pos, y)


def _positions_kernel(starts_ref, brk_ref, pos_ref):
    bucket = brk_ref[0:1, :]
    pos = brk_ref[1:2, :]
    for b in range(N_BUCKETS):
        pos = pos + jnp.where(bucket == b, starts_ref[b], 0)
    pos_ref[...] = pos


def _positions(starts, brk):
    t = brk.shape[1]
    grid_spec = pltpu.PrefetchScalarGridSpec(
        num_scalar_prefetch=1,
        grid=(1,),
        in_specs=[pl.BlockSpec(brk.shape, lambda i, starts: (0, 0))],
        out_specs=pl.BlockSpec((1, t), lambda i, starts: (0, 0)),
    )
    return pl.pallas_call(
        _positions_kernel,
        grid_spec=grid_spec,
        out_shape=jax.ShapeDtypeStruct((1, t), jnp.int32),
        name="moe_positions",
    )(starts, brk).reshape(t)


def _routing_tables(counts, brk):
    tm = MOE_TILE
    t = brk.shape[1]
    nt = t // tm + N_BUCKETS
    cnt = counts[0, :N_BUCKETS].astype(jnp.int32)
    padded = ((cnt + tm - 1) // tm) * tm
    ends = jnp.cumsum(padded)
    starts = ends - padded
    pos = _positions(starts, brk)
    fill = jnp.concatenate([starts + cnt, ends])
    ntiles = ends[-1:] // tm
    tile_start = jnp.arange(nt, dtype=jnp.int32) * tm
    tile_bucket = jnp.sum((ends[None, :] <= tile_start[:, None]).astype(jnp.int32), axis=1)
    used = (tile_bucket < N_BUCKETS).astype(jnp.int32)
    tile_bucket = jnp.minimum(tile_bucket, N_BUCKETS - 1)
    pair = tile_bucket % N_PAIRS
    group0 = (tile_bucket // N_PAIRS) * EXPERTS_PER_GROUP
    e_lo = group0 + jnp.array([0, 0, 0, 1, 1, 2], jnp.int32)[pair]
    e_hi = group0 + jnp.array([1, 2, 3, 2, 3, 3], jnp.int32)[pair]
    return pos, fill, ntiles, e_lo, e_hi, used, nt


def kernel(x, mem, mix_norm, w_in, conv_w, hgrn_lb, hgrn_norm, w_out, xattn_norm, mem_norm,
           w_q, w_kv, w_o, ffn_norm, w_group, b_group, w_expert, b_expert, w_gate, w_up,
           w_down, final_norm):
    bsz, seq, d = x.shape
    assert d == D_MODEL and seq % SEQ_TILE == 0 and seq % ATTN_TILE == 0
    assert (bsz * seq) % MOE_TILE == 0 and (bsz * seq) % FINAL_TILE == 0
    assert (bsz * seq) % DISPATCH_TILE == 0 and bsz * seq >= 2 * DISPATCH_TILE
    assert 2 * DISPATCH_TILE >= MOE_TILE and mix_norm.shape[0] == 1
    bf = lambda w: w.astype(BF16)

    x1 = _mixer(x, mix_norm, bf(w_in[0]), conv_w[0], hgrn_lb, hgrn_norm, bf(w_out[0]))
    kv = _kv_proj(mem, mem_norm, bf(w_kv[0]))

    pad = LANES - N_GROUPS - N_EXPERTS
    w_router = jnp.concatenate(
        [w_group[0], w_expert[0], jnp.zeros((d, pad), F32)], axis=1)
    b_router = jnp.concatenate(
        [b_group[0], b_expert[0], jnp.zeros((pad,), F32)])[None, :]
    w_router_hi = bf(w_router)
    w_router_lo = bf(w_router - w_router_hi.astype(F32))
    w_router2 = jnp.concatenate([w_router_hi, w_router_lo], axis=1)
    w_q_scaled = bf(w_q[0] * (XATTN_HEAD_DIM ** -0.5))
    x2r, brk, counts = _attention(x1, kv, xattn_norm, w_q_scaled, bf(w_o[0]), ffn_norm,
                                  w_router2, b_router)

    pos, fill, ntiles, e_lo, e_hi, used, nt = _routing_tables(counts, brk)
    xs = _dispatch(pos, fill, ntiles, x2r, nt * MOE_TILE)
    y = _moe_sparse(e_lo, e_hi, used, xs, ffn_norm, w_router2, b_router, final_norm[None, :],
                    bf(w_gate[0]), bf(w_up[0]), bf(w_down[0]))
    return _final(pos, y).reshape(bsz, seq, d)
```

```python
import jax
import jax.numpy as jnp
from jax import lax
from jax.experimental import pallas as pl
from jax.experimental.pallas import tpu as pltpu

F32 = jnp.float32
BF16 = jnp.bfloat16

D_MODEL = 1024
CONV_WIDTH = 512
HGRN_WIDTH = 512
HGRN_HEADS = 4
HEAD_DIM = 128
N_PROJ_SLOTS = 7
PROJ_WIDTH = N_PROJ_SLOTS * 512
XATTN_HEADS = 4
XATTN_HEAD_DIM = 256
N_GROUPS = 4
EXPERTS_PER_GROUP = 4
N_EXPERTS = 16
D_EXPERT = 512
EPS = 1e-6

LANES = 128
SUBLANES = 8
CHUNK = 64
CHUNK_LEVELS = 6
HGRN_TILE = 4 * CHUNK
SEQ_TILE = 1024
KV_TILE = 1024
ATTN_TILE = 1024
RANK_BLOCK = 256
DISPATCH_TILE = 1024
MOE_TILE = 512
FINAL_TILE = 1024
N_PAIRS = 6
N_BUCKETS = N_GROUPS * N_PAIRS
ROW_TILES = D_MODEL // LANES
DMA_QUEUES = 2
VMEM_LIMIT = 56 * 1024 * 1024


def _rms(x, g):
    return x * lax.rsqrt(jnp.mean(x * x, axis=-1, keepdims=True) + EPS) * g


def _dot(a, b):
    return jnp.dot(a, b, preferred_element_type=F32)


def _dot_nt(a, b):
    return lax.dot_general(a, b, (((1,), (1,)), ((), ())), preferred_element_type=F32)


def _dot_tn(a, b):
    return lax.dot_general(a, b, (((0,), (0,)), ((), ())), preferred_element_type=F32)


def _roll_rows(x, shift):
    return pltpu.roll(x, shift % x.shape[0], axis=0)


def _level_exponents(logf2, use_level):
    n = CHUNK // SUBLANES
    sub = lax.broadcasted_iota(jnp.int32, (SUBLANES, logf2.shape[1]), 0)
    roll = lambda x, s: pltpu.roll(x, s % SUBLANES, axis=0)
    r = [logf2[SUBLANES * j:SUBLANES * (j + 1), :] for j in range(n)]
    for lvl in range(1, CHUNK_LEVELS + 1):
        half = 1 << (lvl - 1)
        g = [None] * n
        if half < SUBLANES:
            second = (sub & half) != 0
            for j in range(n):
                last = r[j]
                w = 1
                while w < half:
                    last = jnp.where((sub & w) != 0, last, roll(last, -w))
                    w *= 2
                tot = jnp.where(second, roll(last, half), last)
                g[j] = jnp.where(second, r[j], tot - r[j])
                r[j] = jnp.where(second, r[j] + tot, r[j])
        else:
            hv = half // SUBLANES
            for j0 in range(0, n, 2 * hv):
                mid = r[j0 + hv - 1]
                tot = jnp.broadcast_to(mid[SUBLANES - 1:SUBLANES, :], mid.shape)
                for j in range(j0, j0 + hv):
                    g[j] = tot - r[j]
                for j in range(j0 + hv, j0 + 2 * hv):
                    g[j] = r[j]
                    r[j] = r[j] + tot
        use_level(lvl, jnp.concatenate(g, axis=0))
    return jnp.concatenate(r, axis=0)


def _sigmoid(x):
    return 0.5 * jnp.tanh(0.5 * x) + 0.5


def _split_levels():
    import numpy as np
    t = np.arange(CHUNK)[:, None]
    s = np.arange(CHUNK)[None, :]
    msb = np.floor(np.log2(np.maximum(t ^ s, 1))).astype(np.int32) + 1
    return np.where(s < t, msb, np.where(s == t, 0, -1)).astype(np.int32)


def _hgrn_chunk(q, z, v, lb, levels):
    half_th = 0.5 * jnp.tanh(0.5 * z)
    one_m_lb = 1.0 - lb
    logf2 = jnp.log2(lb + one_m_lb * (0.5 + half_th))
    k = one_m_lb * (0.5 - half_th)

    qb = q.astype(BF16)
    kb = k.astype(BF16)
    scores = [jnp.where(levels == 0, jnp.sum(q * k, axis=-1, keepdims=True), 0.0)]

    def use_level(lvl, g):
        decay = jnp.exp2(g).astype(BF16)
        scores[0] = jnp.where(levels == lvl, _dot_nt(qb * decay, kb * decay), scores[0])

    b2 = _level_exponents(logf2, use_level)
    a = scores[0]
    b2_last = b2[CHUNK - 1:CHUNK, :]

    qe = (q * jnp.exp2(b2)).astype(BF16)
    kd = (k * jnp.exp2(b2_last - b2)).astype(BF16)
    return a, qe, kd, v.astype(BF16), b2_last


def _hgrn_tile(chunks, st):
    (a0, qe0, kd0, v0, bl0), (a1, qe1, kd1, v1, bl1), (a2, qe2, kd2, v2, bl2), \
        (a3, qe3, kd3, v3, bl3) = chunks
    scale = lambda x, log2_decay: x * jnp.exp2(log2_decay).astype(BF16)
    bf = lambda x: x.astype(BF16)
    cum1 = bl0
    cum2 = cum1 + bl1
    cum3 = cum2 + bl2
    cum4 = cum3 + bl3
    s10 = bf(_dot_nt(qe1, kd0))
    s32 = bf(_dot_nt(qe3, kd2))
    s8 = bf(_dot_nt(jnp.concatenate([qe2, scale(qe3, bl2)], axis=0),
                    jnp.concatenate([scale(kd0, bl1), kd1], axis=0)))
    v01 = jnp.concatenate([v0, v1], axis=0)
    stb = bf(st)
    o0 = _dot(bf(a0), v0) + _dot_nt(qe0, stb)
    o1 = _dot(bf(a1), v1) + _dot(s10, v0) + _dot_nt(scale(qe1, cum1), stb)
    o2 = _dot(bf(a2), v2) + _dot(s8[:CHUNK], v01) + _dot_nt(scale(qe2, cum2), stb)
    o3 = (_dot(bf(a3), v3) + _dot(s32, v2) + _dot(s8[CHUNK:], v01)
          + _dot_nt(scale(qe3, cum3), stb))
    kd_all = jnp.concatenate(
        [scale(kd0, cum4 - cum1), scale(kd1, cum4 - cum2), scale(kd2, bl3), kd3], axis=0)
    v_all = jnp.concatenate([v0, v1, v2, v3], axis=0)
    st_new = st * jnp.exp2(cum4) + _dot_tn(v_all, kd_all)
    return [o0, o1, o2, o3], st_new


def _mixer_kernel(x_ref, gmix_ref, win_ref, convw_ref, lbraw_ref, hnorm_ref, wout_ref, lvl_ref,
                  o_ref, p_scr, y_scr, st_scr, tail_scr):
    j = pl.program_id(1)

    @pl.when(j == 0)
    def _():
        st_scr[...] = jnp.zeros_like(st_scr)
        tail_scr[...] = jnp.zeros_like(tail_scr)

    x = x_ref[0]
    h = _rms(x, gmix_ref[...]).astype(BF16)
    p_scr[...] = _dot(h, win_ref[...])

    ts = x.shape[0]
    cb = p_scr[:, 0:CONV_WIDTH]
    u = p_scr[:, CONV_WIDTH:2 * CONV_WIDTH] * p_scr[:, 2 * CONV_WIDTH:3 * CONV_WIDTH]
    row = lax.broadcasted_iota(jnp.int32, u.shape, 0)
    prev1 = tail_scr[7:8, :]
    prev2 = tail_scr[6:7, :]
    u1 = jnp.where(row == 0, prev1, _roll_rows(u, 1))
    u2 = jnp.where(row == 0, prev2, jnp.where(row == 1, prev1, _roll_rows(u, 2)))
    cw = convw_ref[...]
    conv = u2 * cw[0:1, :] + u1 * cw[1:2, :] + u * cw[2:3, :]
    y_scr[:, 0:CONV_WIDTH] = (cb * conv).astype(BF16)
    tail_scr[...] = u[ts - 8:ts, :]

    raw = lbraw_ref[...]
    mx = jnp.max(raw, axis=0, keepdims=True)
    ex = jnp.exp(raw - mx)
    lb_all = ex[0:1, :] / jnp.sum(ex, axis=0, keepdims=True)
    hn = hnorm_ref[...]

    for hd in range(HGRN_HEADS):
        lo = hd * HEAD_DIM
        sl = slice(lo, lo + HEAD_DIM)
        col = lambda slot: slice(slot * 512 + lo, slot * 512 + lo + HEAD_DIM)
        for t0 in range(0, ts, HGRN_TILE):
            chunks = []
            for r0 in range(t0, t0 + HGRN_TILE, CHUNK):
                rows = slice(r0, r0 + CHUNK)
                chunks.append(_hgrn_chunk(p_scr[rows, col(3)], p_scr[rows, col(4)],
                                          p_scr[rows, col(5)], lb_all[:, sl], lvl_ref[...]))
            outs, st_new = _hgrn_tile(chunks, st_scr[hd])
            st_scr[hd] = st_new
            for c, o in enumerate(outs):
                rows = slice(t0 + c * CHUNK, t0 + (c + 1) * CHUNK)
                g = p_scr[rows, col(6)]
                o = o * lax.rsqrt(jnp.mean(o * o, axis=-1, keepdims=True) + EPS) * hn[:, sl]
                y_scr[rows, CONV_WIDTH + lo:CONV_WIDTH + lo + HEAD_DIM] = \
                    (o * (g * _sigmoid(g))).astype(BF16)

    o_ref[0] = x + _dot(y_scr[...], wout_ref[...])


def _mixer(x, mix_norm, w_in, conv_w, hgrn_lb, hgrn_norm, w_out):
    bsz, seq, d = x.shape
    ts = SEQ_TILE
    const = lambda b, j: (0, 0)
    return pl.pallas_call(
        _mixer_kernel,
        grid=(bsz, seq // ts),
        in_specs=[
            pl.BlockSpec((1, ts, d), lambda b, j: (b, j, 0)),
            pl.BlockSpec((1, d), const),
            pl.BlockSpec((d, PROJ_WIDTH), const),
            pl.BlockSpec((3, CONV_WIDTH), const),
            pl.BlockSpec((2, HGRN_WIDTH), const),
            pl.BlockSpec((1, HGRN_WIDTH), const),
            pl.BlockSpec((d, d), const),
            pl.BlockSpec((CHUNK, CHUNK), const),
        ],
        out_specs=pl.BlockSpec((1, ts, d), lambda b, j: (b, j, 0)),
        out_shape=jax.ShapeDtypeStruct((bsz, seq, d), F32),
        scratch_shapes=[
            pltpu.VMEM((ts, PROJ_WIDTH), F32),
            pltpu.VMEM((ts, d), BF16),
            pltpu.VMEM((HGRN_HEADS, HEAD_DIM, HEAD_DIM), F32),
            pltpu.VMEM((8, CONV_WIDTH), F32),
        ],
        compiler_params=pltpu.CompilerParams(
            dimension_semantics=("arbitrary", "arbitrary"), vmem_limit_bytes=VMEM_LIMIT),
        name="mixer",
    )(x, mix_norm, w_in, conv_w, hgrn_lb, hgrn_norm, w_out, jnp.asarray(_split_levels()))


def _kv_kernel(m_ref, g_ref, w_ref, o_ref):
    h = _rms(m_ref[...], g_ref[...]).astype(BF16)
    o_ref[...] = _dot(h, w_ref[...]).astype(BF16)


def _kv_proj(mem, mem_norm, w_kv):
    bsz, mlen, d = mem.shape
    rows = bsz * mlen
    tk = min(KV_TILE, rows)
    assert rows % tk == 0
    const = lambda i: (0, 0)
    kv = pl.pallas_call(
        _kv_kernel,
        grid=(rows // tk,),
        in_specs=[
            pl.BlockSpec((tk, d), lambda i: (i, 0)),
            pl.BlockSpec((1, d), const),
            pl.BlockSpec((d, 2 * d), const),
        ],
        out_specs=pl.BlockSpec((tk, 2 * d), lambda i: (i, 0)),
        out_shape=jax.ShapeDtypeStruct((rows, 2 * d), BF16),
        compiler_params=pltpu.CompilerParams(
            dimension_semantics=("arbitrary",), vmem_limit_bytes=VMEM_LIMIT),
        name="kv_proj",
    )(mem.reshape(rows, d), mem_norm, w_kv)
    return kv.reshape(bsz, mlen, 2 * d)


def _first_argmax(vals, lane):
    mx = jnp.max(vals, axis=-1, keepdims=True)
    idx = jnp.min(jnp.where(vals == mx, lane, float(LANES)), axis=-1, keepdims=True)
    return mx, idx


def _router_logits(h3, wr_ref, br_ref):
    h3_hi = h3.astype(BF16)
    h3_lo = (h3 - h3_hi.astype(F32)).astype(BF16)
    two = _dot(h3_hi, wr_ref[...])
    return (two[:, :LANES] + two[:, LANES:]) + _dot(h3_lo, wr_ref[:, :LANES]) + br_ref[...]


def _route(logits, running, tri):
    lane = lax.broadcasted_iota(jnp.int32, logits.shape, 1).astype(F32)
    neg = jnp.float32(-jnp.inf)
    gl = jnp.where(lane < N_GROUPS, logits, neg)
    _, gidx = _first_argmax(gl, lane)
    base = N_GROUPS + EXPERTS_PER_GROUP * gidx
    el = jnp.where((lane >= base) & (lane < base + EXPERTS_PER_GROUP), logits, neg)
    _, i1 = _first_argmax(el, lane)
    _, i2 = _first_argmax(jnp.where(lane == i1, neg, el), lane)
    lo = jnp.minimum(i1, i2) - base
    hi = jnp.maximum(i1, i2) - base
    bucket = gidx * N_PAIRS + (lo * (7.0 - lo)) * 0.5 + hi - lo - 1.0

    onehot = lane == bucket
    ranks = []
    for r0 in range(0, logits.shape[0], RANK_BLOCK):
        oh = onehot[r0:r0 + RANK_BLOCK]
        before = _dot(tri, oh.astype(BF16))
        ranks.append(jnp.sum(jnp.where(oh, before + running, 0.0), axis=-1, keepdims=True))
        running = running + jnp.sum(oh.astype(F32), axis=0, keepdims=True)
    rank = jnp.concatenate(ranks, axis=0)
    info = jnp.where(lane == 0, bucket, jnp.where(lane == 1, rank, 0.0))
    return info, running


def _gates(logits, e_lo, e_hi):
    lane = lax.broadcasted_iota(jnp.int32, logits.shape, 1)
    neg = jnp.float32(-jnp.inf)
    gl = jnp.where(lane < N_GROUPS, logits, neg)
    g_p = 1.0 / jnp.sum(jnp.exp(gl - jnp.max(gl, axis=-1, keepdims=True)), axis=-1, keepdims=True)
    l_lo = jnp.sum(jnp.where(lane == N_GROUPS + e_lo, logits, 0.0), axis=-1, keepdims=True)
    l_hi = jnp.sum(jnp.where(lane == N_GROUPS + e_hi, logits, 0.0), axis=-1, keepdims=True)
    m = jnp.maximum(l_lo, l_hi)
    p_lo = jnp.exp(l_lo - m)
    p_hi = jnp.exp(l_hi - m)
    inv = g_p / (p_lo + p_hi)
    return p_lo * inv, p_hi * inv


def _attn_kernel(x_ref, kv_ref, gx_ref, wq_ref, wo_ref, gf_ref, wr_ref, br_ref, tri_ref,
                 x2r_ref, brk_ref, counts_ref, o_scr, cnt_scr):
    @pl.when((pl.program_id(0) == 0) & (pl.program_id(1) == 0))
    def _():
        cnt_scr[...] = jnp.zeros_like(cnt_scr)

    x = x_ref[0]
    tq = x.shape[0]
    h = _rms(x, gx_ref[...]).astype(BF16)
    q = _dot(h, wq_ref[...])
    for hd in range(XATTN_HEADS):
        lo = hd * XATTN_HEAD_DIM
        qh = q[:, lo:lo + XATTN_HEAD_DIM].astype(BF16)
        kh = kv_ref[0, :, lo:lo + XATTN_HEAD_DIM]
        vh = kv_ref[0, :, D_MODEL + lo:D_MODEL + lo + XATTN_HEAD_DIM]
        s = _dot_nt(qh, kh)
        p = jnp.exp(s - jnp.max(s, axis=-1, keepdims=True))
        inv = 1.0 / jnp.sum(p, axis=-1, keepdims=True)
        o_scr[:, lo:lo + XATTN_HEAD_DIM] = (_dot(p.astype(BF16), vh) * inv).astype(BF16)
    x2 = x + _dot(o_scr[...], wo_ref[...])
    for s in range(ROW_TILES):
        x2r_ref[pl.ds(s, tq, stride=ROW_TILES), :] = x2[:, s * LANES:(s + 1) * LANES]
    logits = _router_logits(_rms(x2, gf_ref[...]), wr_ref, br_ref)
    info, running = _route(logits, cnt_scr[...], tri_ref[...])
    cnt_scr[...] = running
    counts_ref[...] = running
    brk_ref[...] = info.T[0:8, :].astype(jnp.int32)


def _attention(x1, kv, xattn_norm, w_q, w_o, ffn_norm, w_router, b_router):
    bsz, seq, d = x1.shape
    mlen = kv.shape[1]
    tq = ATTN_TILE
    nj = seq // tq
    const = lambda b, j: (0, 0)
    tile = lambda b, j: (b, j, 0)
    tri = jnp.tri(RANK_BLOCK, RANK_BLOCK, -1, dtype=BF16)
    return pl.pallas_call(
        _attn_kernel,
        grid=(bsz, nj),
        in_specs=[
            pl.BlockSpec((1, tq, d), tile),
            pl.BlockSpec((1, mlen, 2 * d), lambda b, j: (b, 0, 0)),
            pl.BlockSpec((1, d), const),
            pl.BlockSpec((d, d), const),
            pl.BlockSpec((d, d), const),
            pl.BlockSpec((1, d), const),
            pl.BlockSpec((d, 2 * LANES), const),
            pl.BlockSpec((1, LANES), const),
            pl.BlockSpec((RANK_BLOCK, RANK_BLOCK), const),
        ],
        out_specs=[
            pl.BlockSpec((tq * ROW_TILES, LANES), lambda b, j: (b * nj + j, 0)),
            pl.BlockSpec((8, tq), lambda b, j: (0, b * nj + j)),
            pl.BlockSpec((1, LANES), const),
        ],
        out_shape=[
            jax.ShapeDtypeStruct((bsz * seq * ROW_TILES, LANES), F32),
            jax.ShapeDtypeStruct((8, bsz * seq), jnp.int32),
            jax.ShapeDtypeStruct((1, LANES), F32),
        ],
        scratch_shapes=[pltpu.VMEM((tq, d), BF16), pltpu.VMEM((1, LANES), F32)],
        compiler_params=pltpu.CompilerParams(
            dimension_semantics=("arbitrary", "arbitrary"), vmem_limit_bytes=VMEM_LIMIT),
        name="xattn_router",
    )(x1, kv, xattn_norm, w_q, w_o, ffn_norm, w_router, b_router, tri)


def _item_copy(hbm, hbm_item, buf, buf_item, rows_per_item, sem, to_hbm):
    h = hbm.at[pl.ds(pl.multiple_of(hbm_item * rows_per_item, rows_per_item), rows_per_item)]
    b = buf.at[pl.ds(pl.multiple_of(buf_item * rows_per_item, rows_per_item), rows_per_item)]
    return pltpu.make_async_copy(b, h, sem) if to_hbm else pltpu.make_async_copy(h, b, sem)


def _items_wait(hbm, buf, buf_item, n_items, rows_per_item, sem, to_hbm):
    n = n_items * rows_per_item
    h = hbm.at[pl.ds(0, n)]
    b = buf.at[pl.ds(pl.multiple_of(buf_item * rows_per_item, rows_per_item), n)]
    (pltpu.make_async_copy(b, h, sem) if to_hbm else pltpu.make_async_copy(h, b, sem)).wait()


def _dispatch_kernel(pos_ref, fill_ref, ntiles_ref, x2r_ref, xs_hbm, buf, sem):
    i = pl.program_id(0)
    nt = pl.num_programs(0)
    td = DISPATCH_TILE
    rpi = ROW_TILES
    slot = i % 2
    base = slot * td

    @pl.when(i >= 2)
    def _():
        _items_wait(xs_hbm, buf, base, td, rpi, sem.at[slot], True)

    buf[pl.ds(pl.multiple_of(base * rpi, td * rpi), td * rpi), :] = x2r_ref[...]

    def send(r2, carry):
        for k in range(DMA_QUEUES):
            r = r2 * DMA_QUEUES + k
            _item_copy(xs_hbm, pos_ref[i * td + r], buf, base + r, rpi, sem.at[slot],
                       True).start(priority=k)
        return carry

    lax.fori_loop(0, td // DMA_QUEUES, send, 0, unroll=4)

    @pl.when(i == nt - 1)
    def _():
        _items_wait(xs_hbm, buf, base, td, rpi, sem.at[slot], True)
        _items_wait(xs_hbm, buf, (1 - slot) * td, td, rpi, sem.at[1 - slot], True)
        buf[...] = jnp.zeros_like(buf)
        zsem = sem.at[2]
        bits = range(MOE_TILE.bit_length() - 2, -1, -1)

        def pad_copy(b, k):
            first = fill_ref[b]
            n = fill_ref[N_BUCKETS + b] - first
            done = (n >> (k + 1)) << (k + 1)
            rows = (1 << k) * rpi
            cp = pltpu.make_async_copy(
                buf.at[pl.ds(0, rows)],
                xs_hbm.at[pl.ds(pl.multiple_of((first + done) * rpi, rpi), rows)], zsem)
            return ((n >> k) & 1) == 1, cp

        for wait in (False, True):
            for b in range(N_BUCKETS):
                for k in bits:
                    present, cp = pad_copy(b, k)

                    @pl.when(present)
                    def _():
                        cp.wait() if wait else cp.start()

        n_tiles_total = xs_hbm.shape[0] // (MOE_TILE * rpi)

        def tile_copy(tile):
            return pltpu.make_async_copy(
                buf.at[pl.ds(0, MOE_TILE * rpi)],
                xs_hbm.at[pl.ds(pl.multiple_of(tile * MOE_TILE * rpi, MOE_TILE * rpi),
                                MOE_TILE * rpi)], zsem)

        def fill_tile(tile, carry):
            tile_copy(tile).start()
            return carry

        def drain_tile(tile, carry):
            tile_copy(tile).wait()
            return carry

        lax.fori_loop(ntiles_ref[0], n_tiles_total, fill_tile, 0)
        lax.fori_loop(ntiles_ref[0], n_tiles_total, drain_tile, 0)


def _dispatch(pos, fill, ntiles, x2r, n_sorted):
    t = x2r.shape[0] // ROW_TILES
    td = DISPATCH_TILE
    grid_spec = pltpu.PrefetchScalarGridSpec(
        num_scalar_prefetch=3,
        grid=(t // td,),
        in_specs=[pl.BlockSpec((td * ROW_TILES, LANES), lambda i, *_: (i, 0))],
        out_specs=pl.BlockSpec(memory_space=pl.ANY),
        scratch_shapes=[
            pltpu.VMEM((2 * td * ROW_TILES, LANES), F32),
            pltpu.SemaphoreType.DMA((3,)),
        ],
    )
    return pl.pallas_call(
        _dispatch_kernel,
        grid_spec=grid_spec,
        out_shape=jax.ShapeDtypeStruct((n_sorted * ROW_TILES, LANES), F32),
        compiler_params=pltpu.CompilerParams(
            dimension_semantics=("arbitrary",), vmem_limit_bytes=VMEM_LIMIT),
        name="moe_dispatch",
    )(pos, fill, ntiles, x2r)


def _silu(x):
    return x * _sigmoid(x)


def _moe_kernel(elo_ref, ehi_ref, used_ref, xtile_ref,
                xs_ref, gf_ref, wr_ref, br_ref, gfin_ref,
                wg_lo, wu_lo, wd_lo, wg_hi, wu_hi, wd_hi, y_ref):
    del xtile_ref
    i = pl.program_id(0)
    tm = MOE_TILE

    @pl.when(used_ref[i] == 1)
    def _():
        x2 = jnp.concatenate(
            [xs_ref[pl.ds(s, tm, stride=ROW_TILES), :] for s in range(ROW_TILES)], axis=1)
        h3 = _rms(x2, gf_ref[...])
        g_lo, g_hi = _gates(_router_logits(h3, wr_ref, br_ref), elo_ref[i], ehi_ref[i])
        x = h3.astype(BF16)

        def expert(wg, wu, wd):
            hid = (_silu(_dot(x, wg[0])) * _dot(x, wu[0])).astype(BF16)
            return _dot(hid, wd[0])

        moe = g_lo * expert(wg_lo, wu_lo, wd_lo) + g_hi * expert(wg_hi, wu_hi, wd_hi)
        out = _rms(x2 + moe, gfin_ref[...])
        for s in range(ROW_TILES):
            y_ref[pl.ds(s, tm, stride=ROW_TILES), :] = out[:, s * LANES:(s + 1) * LANES]

    @pl.when(used_ref[i] == 0)
    def _():
        y_ref[...] = jnp.zeros_like(y_ref)


def _moe_sparse(e_lo, e_hi, used, xs, ffn_norm, w_router, b_router, final_norm,
                w_gate, w_up, w_down):
    tm = MOE_TILE
    nt = used.shape[0]
    d = D_MODEL
    lo = lambda i, elo, ehi, used, xt: (elo[i], 0, 0)
    hi = lambda i, elo, ehi, used, xt: (ehi[i], 0, 0)
    const = lambda i, *_: (0, 0)
    steps = jnp.arange(nt, dtype=jnp.int32)
    xs_tile = jnp.where(used == 1, steps, jnp.maximum(jnp.sum(used) - 1, 0))
    grid_spec = pltpu.PrefetchScalarGridSpec(
        num_scalar_prefetch=4,
        grid=(nt,),
        in_specs=[
            pl.BlockSpec((tm * ROW_TILES, LANES), lambda i, elo, ehi, used, xt: (xt[i], 0)),
            pl.BlockSpec((1, d), const),
            pl.BlockSpec((d, 2 * LANES), const),
            pl.BlockSpec((1, LANES), const),
            pl.BlockSpec((1, d), const),
            pl.BlockSpec((1, d, D_EXPERT), lo),
            pl.BlockSpec((1, d, D_EXPERT), lo),
            pl.BlockSpec((1, D_EXPERT, d), lo),
            pl.BlockSpec((1, d, D_EXPERT), hi),
            pl.BlockSpec((1, d, D_EXPERT), hi),
            pl.BlockSpec((1, D_EXPERT, d), hi),
        ],
        out_specs=pl.BlockSpec((tm * ROW_TILES, LANES), lambda i, *_: (i, 0)),
    )
    return pl.pallas_call(
        _moe_kernel,
        grid_spec=grid_spec,
        out_shape=jax.ShapeDtypeStruct((nt * tm * ROW_TILES, LANES), F32),
        compiler_params=pltpu.CompilerParams(
            dimension_semantics=("arbitrary",), vmem_limit_bytes=VMEM_LIMIT),
        name="moe_sparse",
    )(e_lo, e_hi, used, xs_tile, xs, ffn_norm, w_router, b_router, final_norm,
      w_gate, w_up, w_down, w_gate, w_up, w_down)


def _final_kernel(pos_ref, y_hbm, o_ref, ybuf, sem):
    i = pl.program_id(0)
    nt = pl.num_programs(0)
    tf = FINAL_TILE

    def start(tile, slot):
        def fetch(r2, carry):
            for k in range(DMA_QUEUES):
                r = r2 * DMA_QUEUES + k
                _item_copy(y_hbm, pos_ref[tile * tf + r], ybuf, slot * tf + r, ROW_TILES,
                           sem.at[slot], False).start(priority=k)
            return carry

        lax.fori_loop(0, tf // DMA_QUEUES, fetch, 0, unroll=4)

    @pl.when(i == 0)
    def _():
        start(0, 0)

    @pl.when(i + 1 < nt)
    def _():
        start(i + 1, (i + 1) % 2)

    slot = i % 2
    _items_wait(y_hbm, ybuf, slot * tf, tf, ROW_TILES, sem.at[slot], False)
    row0 = pl.multiple_of(slot * tf * ROW_TILES, ROW_TILES)
    for s in range(ROW_TILES):
        o_ref[:, s * LANES:(s + 1) * LANES] = ybuf[pl.ds(row0 + s, tf, stride=ROW_TILES), :]


def _final(pos, y):
    t = pos.shape[0]
    d = D_MODEL
    tf = FINAL_TILE
    grid_spec = pltpu.PrefetchScalarGridSpec(
        num_scalar_prefetch=1,
        grid=(t // tf,),
        in_specs=[pl.BlockSpec(memory_space=pl.ANY)],
        out_specs=pl.BlockSpec((tf, d), lambda i, *_: (i, 0)),
        scratch_shapes=[
            pltpu.VMEM((2 * tf * ROW_TILES, LANES), F32),
            pltpu.SemaphoreType.DMA((2,)),
        ],
    )
    return pl.pallas_call(
        _final_kernel,
        grid_spec=grid_spec,
        out_shape=jax.ShapeDtypeStruct((t, d), F32),
        compiler_params=pltpu.CompilerParams(
            dimension_semantics=("arbitrary",), vmem_limit_bytes=VMEM_LIMIT),
        name="moe_unpermute",
    )(pos, y)


def _positions_kernel(starts_ref, brk_ref, pos_ref):
    bucket = brk_ref[0:1, :]
    pos = brk_ref[1:2, :]
    for b in range(N_BUCKETS):
        pos = pos + jnp.where(bucket == b, starts_ref[b], 0)
    pos_ref[...] = pos


def _positions(starts, brk):
    t = brk.shape[1]
    grid_spec = pltpu.PrefetchScalarGridSpec(
        num_scalar_prefetch=1,
        grid=(1,),
        in_specs=[pl.BlockSpec(brk.shape, lambda i, starts: (0, 0))],
        out_specs=pl.BlockSpec((1, t), lambda i, starts: (0, 0)),
    )
    return pl.pallas_call(
        _positions_kernel,
        grid_spec=grid_spec,
        out_shape=jax.ShapeDtypeStruct((1, t), jnp.int32),
        name="moe_positions",
    )(starts, brk).reshape(t)


def _routing_tables(counts, brk):
    tm = MOE_TILE
    t = brk.shape[1]
    nt = t // tm + N_BUCKETS
    cnt = counts[0, :N_BUCKETS].astype(jnp.int32)
    padded = ((cnt + tm - 1) // tm) * tm
    ends = jnp.cumsum(padded)
    starts = ends - padded
    pos = _positions(starts, brk)
    fill = jnp.concatenate([starts + cnt, ends])
    ntiles = ends[-1:] // tm
    tile_start = jnp.arange(nt, dtype=jnp.int32) * tm
    tile_bucket = jnp.sum((ends[None, :] <= tile_start[:, None]).astype(jnp.int32), axis=1)
    used = (tile_bucket < N_BUCKETS).astype(jnp.int32)
    tile_bucket = jnp.minimum(tile_bucket, N_BUCKETS - 1)
    pair = tile_bucket % N_PAIRS
    group0 = (tile_bucket // N_PAIRS) * EXPERTS_PER_GROUP
    e_lo = group0 + jnp.array([0, 0, 0, 1, 1, 2], jnp.int32)[pair]
    e_hi = group0 + jnp.array([1, 2, 3, 2, 3, 3], jnp.int32)[pair]
    return pos, fill, ntiles, e_lo, e_hi, used, nt


def kernel(x, mem, mix_norm, w_in, conv_w, hgrn_lb, hgrn_norm, w_out, xattn_norm, mem_norm,
           w_q, w_kv, w_o, ffn_norm, w_group, b_group, w_expert, b_expert, w_gate, w_up,
           w_down, final_norm):
    bsz, seq, d = x.shape
    assert d == D_MODEL and seq % SEQ_TILE == 0 and seq % ATTN_TILE == 0
    assert (bsz * seq) % MOE_TILE == 0 and (bsz * seq) % FINAL_TILE == 0
    assert (bsz * seq) % DISPATCH_TILE == 0 and bsz * seq >= 2 * DISPATCH_TILE
    assert 2 * DISPATCH_TILE >= MOE_TILE and mix_norm.shape[0] == 1
    bf = lambda w: w.astype(BF16)

    x1 = _mixer(x, mix_norm, bf(w_in[0]), conv_w[0], hgrn_lb, hgrn_norm, bf(w_out[0]))
    kv = _kv_proj(mem, mem_norm, bf(w_kv[0]))

    pad = LANES - N_GROUPS - N_EXPERTS
    w_router = jnp.concatenate(
        [w_group[0], w_expert[0], jnp.zeros((d, pad), F32)], axis=1)
    b_router = jnp.concatenate(
        [b_group[0], b_expert[0], jnp.zeros((pad,), F32)])[None, :]
    w_router_hi = bf(w_router)
    w_router_lo = bf(w_router - w_router_hi.astype(F32))
    w_router2 = jnp.concatenate([w_router_hi, w_router_lo], axis=1)
    w_q_scaled = bf(w_q[0] * (XATTN_HEAD_DIM ** -0.5))
    x2r, brk, counts = _attention(x1, kv, xattn_norm, w_q_scaled, bf(w_o[0]), ffn_norm,
                                  w_router2, b_router)

    pos, fill, ntiles, e_lo, e_hi, used, nt = _routing_tables(counts, brk)
    xs = _dispatch(pos, fill, ntiles, x2r, nt * MOE_TILE)
    y = _moe_sparse(e_lo, e_hi, used, xs, ffn_norm, w_router2, b_router, final_norm[None, :],
                    bf(w_gate[0]), bf(w_up[0]), bf(w_down[0]))
    return _final(pos, y).reshape(bsz, seq, d)
```

```python
import jax
import jax.numpy as jnp
from jax import lax
from jax.experimental import pallas as pl
from jax.experimental.pallas import tpu as pltpu

F32 = jnp.float32
BF16 = jnp.bfloat16

D_MODEL = 1024
CONV_WIDTH = 512
HGRN_WIDTH = 512
HGRN_HEADS = 4
HEAD_DIM = 128
N_PROJ_SLOTS = 7
PROJ_WIDTH = N_PROJ_SLOTS * 512
XATTN_HEADS = 4
XATTN_HEAD_DIM = 256
N_GROUPS = 4
EXPERTS_PER_GROUP = 4
N_EXPERTS = 16
D_EXPERT = 512
EPS = 1e-6

LANES = 128
SUBLANES = 8
CHUNK = 64
CHUNK_LEVELS = 6
HGRN_TILE = 4 * CHUNK
SEQ_TILE = 1024
KV_TILE = 1024
ATTN_TILE = 1024
RANK_BLOCK = 256
DISPATCH_TILE = 1024
MOE_TILE = 512
FINAL_TILE = 1024
FINAL_SLOTS = 3
N_PAIRS = 6
N_BUCKETS = N_GROUPS * N_PAIRS
ROW_TILES = D_MODEL // LANES
DMA_QUEUES = 2
VMEM_LIMIT = 56 * 1024 * 1024


def _rms(x, g):
    return x * lax.rsqrt(jnp.mean(x * x, axis=-1, keepdims=True) + EPS) * g


def _dot(a, b):
    return jnp.dot(a, b, preferred_element_type=F32)


def _dot_nt(a, b):
    return lax.dot_general(a, b, (((1,), (1,)), ((), ())), preferred_element_type=F32)


def _dot_tn(a, b):
    return lax.dot_general(a, b, (((0,), (0,)), ((), ())), preferred_element_type=F32)


def _roll_rows(x, shift):
    return pltpu.roll(x, shift % x.shape[0], axis=0)


def _level_exponents(logf2, use_level):
    n = CHUNK // SUBLANES
    sub = lax.broadcasted_iota(jnp.int32, (SUBLANES, logf2.shape[1]), 0)
    roll = lambda x, s: pltpu.roll(x, s % SUBLANES, axis=0)
    r = [logf2[SUBLANES * j:SUBLANES * (j + 1), :] for j in range(n)]
    for lvl in range(1, CHUNK_LEVELS + 1):
        half = 1 << (lvl - 1)
        g = [None] * n
        if half < SUBLANES:
            second = (sub & half) != 0
            for j in range(n):
                last = r[j]
                w = 1
                while w < half:
                    last = jnp.where((sub & w) != 0, last, roll(last, -w))
                    w *= 2
                tot = jnp.where(second, roll(last, half), last)
                g[j] = jnp.where(second, r[j], tot - r[j])
                r[j] = jnp.where(second, r[j] + tot, r[j])
        else:
            hv = half // SUBLANES
            for j0 in range(0, n, 2 * hv):
                mid = r[j0 + hv - 1]
                tot = jnp.broadcast_to(mid[SUBLANES - 1:SUBLANES, :], mid.shape)
                for j in range(j0, j0 + hv):
                    g[j] = tot - r[j]
                for j in range(j0 + hv, j0 + 2 * hv):
                    g[j] = r[j]
                    r[j] = r[j] + tot
        use_level(lvl, jnp.concatenate(g, axis=0))
    return jnp.concatenate(r, axis=0)


def _sigmoid(x):
    return 0.5 * jnp.tanh(0.5 * x) + 0.5


def _split_levels():
    import numpy as np
    t = np.arange(CHUNK)[:, None]
    s = np.arange(CHUNK)[None, :]
    msb = np.floor(np.log2(np.maximum(t ^ s, 1))).astype(np.int32) + 1
    return np.where(s < t, msb, np.where(s == t, 0, -1)).astype(np.int32)


def _hgrn_chunk(q, z, v, lb, levels):
    half_th = 0.5 * jnp.tanh(0.5 * z)
    one_m_lb = 1.0 - lb
    logf2 = jnp.log2(lb + one_m_lb * (0.5 + half_th))
    k = one_m_lb * (0.5 - half_th)

    qb = q.astype(BF16)
    kb = k.astype(BF16)
    scores = [jnp.where(levels == 0, jnp.sum(q * k, axis=-1, keepdims=True), 0.0)]

    def use_level(lvl, g):
        decay = jnp.exp2(g).astype(BF16)
        scores[0] = jnp.where(levels == lvl, _dot_nt(qb * decay, kb * decay), scores[0])

    b2 = _level_exponents(logf2, use_level)
    a = scores[0]
    b2_last = b2[CHUNK - 1:CHUNK, :]

    qe = (q * jnp.exp2(b2)).astype(BF16)
    kd = (k * jnp.exp2(b2_last - b2)).astype(BF16)
    return a, qe, kd, v.astype(BF16), b2_last


def _hgrn_tile(chunks, st):
    (a0, qe0, kd0, v0, bl0), (a1, qe1, kd1, v1, bl1), (a2, qe2, kd2, v2, bl2), \
        (a3, qe3, kd3, v3, bl3) = chunks
    scale = lambda x, log2_decay: x * jnp.exp2(log2_decay).astype(BF16)
    bf = lambda x: x.astype(BF16)
    cum1 = bl0
    cum2 = cum1 + bl1
    cum3 = cum2 + bl2
    cum4 = cum3 + bl3
    s10 = bf(_dot_nt(qe1, kd0))
    s32 = bf(_dot_nt(qe3, kd2))
    s8 = bf(_dot_nt(jnp.concatenate([qe2, scale(qe3, bl2)], axis=0),
                    jnp.concatenate([scale(kd0, bl1), kd1], axis=0)))
    v01 = jnp.concatenate([v0, v1], axis=0)
    stb = bf(st)
    o0 = _dot(bf(a0), v0) + _dot_nt(qe0, stb)
    o1 = _dot(bf(a1), v1) + _dot(s10, v0) + _dot_nt(scale(qe1, cum1), stb)
    o2 = _dot(bf(a2), v2) + _dot(s8[:CHUNK], v01) + _dot_nt(scale(qe2, cum2), stb)
    o3 = (_dot(bf(a3), v3) + _dot(s32, v2) + _dot(s8[CHUNK:], v01)
          + _dot_nt(scale(qe3, cum3), stb))
    kd_all = jnp.concatenate(
        [scale(kd0, cum4 - cum1), scale(kd1, cum4 - cum2), scale(kd2, bl3), kd3], axis=0)
    v_all = jnp.concatenate([v0, v1, v2, v3], axis=0)
    st_new = st * jnp.exp2(cum4) + _dot_tn(v_all, kd_all)
    return [o0, o1, o2, o3], st_new


def _mixer_kernel(x_ref, gmix_ref, win_ref, convw_ref, lbraw_ref, hnorm_ref, wout_ref, lvl_ref,
                  o_ref, p_scr, y_scr, st_scr, tail_scr):
    j = pl.program_id(1)

    @pl.when(j == 0)
    def _():
        st_scr[...] = jnp.zeros_like(st_scr)
        tail_scr[...] = jnp.zeros_like(tail_scr)

    x = x_ref[0]
    h = _rms(x, gmix_ref[...]).astype(BF16)
    p_scr[...] = _dot(h, win_ref[...])

    ts = x.shape[0]
    cb = p_scr[:, 0:CONV_WIDTH]
    u = p_scr[:, CONV_WIDTH:2 * CONV_WIDTH] * p_scr[:, 2 * CONV_WIDTH:3 * CONV_WIDTH]
    row = lax.broadcasted_iota(jnp.int32, u.shape, 0)
    prev1 = tail_scr[7:8, :]
    prev2 = tail_scr[6:7, :]
    u1 = jnp.where(row == 0, prev1, _roll_rows(u, 1))
    u2 = jnp.where(row == 0, prev2, jnp.where(row == 1, prev1, _roll_rows(u, 2)))
    cw = convw_ref[...]
    conv = u2 * cw[0:1, :] + u1 * cw[1:2, :] + u * cw[2:3, :]
    y_scr[:, 0:CONV_WIDTH] = (cb * conv).astype(BF16)
    tail_scr[...] = u[ts - 8:ts, :]

    raw = lbraw_ref[...]
    mx = jnp.max(raw, axis=0, keepdims=True)
    ex = jnp.exp(raw - mx)
    lb_all = ex[0:1, :] / jnp.sum(ex, axis=0, keepdims=True)
    hn = hnorm_ref[...]

    for hd in range(HGRN_HEADS):
        lo = hd * HEAD_DIM
        sl = slice(lo, lo + HEAD_DIM)
        col = lambda slot: slice(slot * 512 + lo, slot * 512 + lo + HEAD_DIM)
        for t0 in range(0, ts, HGRN_TILE):
            chunks = []
            for r0 in range(t0, t0 + HGRN_TILE, CHUNK):
                rows = slice(r0, r0 + CHUNK)
                chunks.append(_hgrn_chunk(p_scr[rows, col(3)], p_scr[rows, col(4)],
                                          p_scr[rows, col(5)], lb_all[:, sl], lvl_ref[...]))
            outs, st_new = _hgrn_tile(chunks, st_scr[hd])
            st_scr[hd] = st_new
            for c, o in enumerate(outs):
                rows = slice(t0 + c * CHUNK, t0 + (c + 1) * CHUNK)
                g = p_scr[rows, col(6)]
                o = o * lax.rsqrt(jnp.mean(o * o, axis=-1, keepdims=True) + EPS) * hn[:, sl]
                y_scr[rows, CONV_WIDTH + lo:CONV_WIDTH + lo + HEAD_DIM] = \
                    (o * (g * _sigmoid(g))).astype(BF16)

    o_ref[0] = x + _dot(y_scr[...], wout_ref[...])


def _mixer(x, mix_norm, w_in, conv_w, hgrn_lb, hgrn_norm, w_out):
    bsz, seq, d = x.shape
    ts = SEQ_TILE
    const = lambda b, j: (0, 0)
    return pl.pallas_call(
        _mixer_kernel,
        grid=(bsz, seq // ts),
        in_specs=[
            pl.BlockSpec((1, ts, d), lambda b, j: (b, j, 0)),
            pl.BlockSpec((1, d), const),
            pl.BlockSpec((d, PROJ_WIDTH), const),
            pl.BlockSpec((3, CONV_WIDTH), const),
            pl.BlockSpec((2, HGRN_WIDTH), const),
            pl.BlockSpec((1, HGRN_WIDTH), const),
            pl.BlockSpec((d, d), const),
            pl.BlockSpec((CHUNK, CHUNK), const),
        ],
        out_specs=pl.BlockSpec((1, ts, d), lambda b, j: (b, j, 0)),
        out_shape=jax.ShapeDtypeStruct((bsz, seq, d), F32),
        scratch_shapes=[
            pltpu.VMEM((ts, PROJ_WIDTH), F32),
            pltpu.VMEM((ts, d), BF16),
            pltpu.VMEM((HGRN_HEADS, HEAD_DIM, HEAD_DIM), F32),
            pltpu.VMEM((8, CONV_WIDTH), F32),
        ],
        compiler_params=pltpu.CompilerParams(
            dimension_semantics=("arbitrary", "arbitrary"), vmem_limit_bytes=VMEM_LIMIT),
        name="mixer",
    )(x, mix_norm, w_in, conv_w, hgrn_lb, hgrn_norm, w_out, jnp.asarray(_split_levels()))


def _kv_kernel(m_ref, g_ref, w_ref, o_ref):
    h = _rms(m_ref[...], g_ref[...]).astype(BF16)
    o_ref[...] = _dot(h, w_ref[...]).astype(BF16)


def _kv_proj(mem, mem_norm, w_kv):
    bsz, mlen, d = mem.shape
    rows = bsz * mlen
    tk = min(KV_TILE, rows)
    assert rows % tk == 0
    const = lambda i: (0, 0)
    kv = pl.pallas_call(
        _kv_kernel,
        grid=(rows // tk,),
        in_specs=[
            pl.BlockSpec((tk, d), lambda i: (i, 0)),
            pl.BlockSpec((1, d), const),
            pl.BlockSpec((d, 2 * d), const),
        ],
        out_specs=pl.BlockSpec((tk, 2 * d), lambda i: (i, 0)),
        out_shape=jax.ShapeDtypeStruct((rows, 2 * d), BF16),
        compiler_params=pltpu.CompilerParams(
            dimension_semantics=("arbitrary",), vmem_limit_bytes=VMEM_LIMIT),
        name="kv_proj",
    )(mem.reshape(rows, d), mem_norm, w_kv)
    return kv.reshape(bsz, mlen, 2 * d)


def _first_argmax(vals, lane):
    mx = jnp.max(vals, axis=-1, keepdims=True)
    idx = jnp.min(jnp.where(vals == mx, lane, float(LANES)), axis=-1, keepdims=True)
    return mx, idx


def _router_logits(h3, wr_ref, br_ref):
    h3_hi = h3.astype(BF16)
    h3_lo = (h3 - h3_hi.astype(F32)).astype(BF16)
    two = _dot(h3_hi, wr_ref[...])
    return (two[:, :LANES] + two[:, LANES:]) + _dot(h3_lo, wr_ref[:, :LANES]) + br_ref[...]


def _route(logits, running, tri):
    lane = lax.broadcasted_iota(jnp.int32, logits.shape, 1).astype(F32)
    neg = jnp.float32(-jnp.inf)
    gl = jnp.where(lane < N_GROUPS, logits, neg)
    _, gidx = _first_argmax(gl, lane)
    base = N_GROUPS + EXPERTS_PER_GROUP * gidx
    el = jnp.where((lane >= base) & (lane < base + EXPERTS_PER_GROUP), logits, neg)
    _, i1 = _first_argmax(el, lane)
    _, i2 = _first_argmax(jnp.where(lane == i1, neg, el), lane)
    lo = jnp.minimum(i1, i2) - base
    hi = jnp.maximum(i1, i2) - base
    bucket = gidx * N_PAIRS + (lo * (7.0 - lo)) * 0.5 + hi - lo - 1.0

    onehot = lane == bucket
    ranks = []
    for r0 in range(0, logits.shape[0], RANK_BLOCK):
        oh = onehot[r0:r0 + RANK_BLOCK]
        before = _dot(tri, oh.astype(BF16))
        ranks.append(jnp.sum(jnp.where(oh, before + running, 0.0), axis=-1, keepdims=True))
        running = running + jnp.sum(oh.astype(F32), axis=0, keepdims=True)
    rank = jnp.concatenate(ranks, axis=0)
    info = jnp.where(lane == 0, bucket, jnp.where(lane == 1, rank, 0.0))
    return info, running


def _gates(logits, e_lo, e_hi):
    lane = lax.broadcasted_iota(jnp.int32, logits.shape, 1)
    neg = jnp.float32(-jnp.inf)
    gl = jnp.where(lane < N_GROUPS, logits, neg)
    g_p = 1.0 / jnp.sum(jnp.exp(gl - jnp.max(gl, axis=-1, keepdims=True)), axis=-1, keepdims=True)
    l_lo = jnp.sum(jnp.where(lane == N_GROUPS + e_lo, logits, 0.0), axis=-1, keepdims=True)
    l_hi = jnp.sum(jnp.where(lane == N_GROUPS + e_hi, logits, 0.0), axis=-1, keepdims=True)
    m = jnp.maximum(l_lo, l_hi)
    p_lo = jnp.exp(l_lo - m)
    p_hi = jnp.exp(l_hi - m)
    inv = g_p / (p_lo + p_hi)
    return p_lo * inv, p_hi * inv


def _attn_kernel(x_ref, kv_ref, gx_ref, wq_ref, wo_ref, gf_ref, wr_ref, br_ref, tri_ref,
                 x2r_ref, brk_ref, counts_ref, o_scr, cnt_scr):
    @pl.when((pl.program_id(0) == 0) & (pl.program_id(1) == 0))
    def _():
        cnt_scr[...] = jnp.zeros_like(cnt_scr)

    x = x_ref[0]
    tq = x.shape[0]
    h = _rms(x, gx_ref[...]).astype(BF16)
    q = _dot(h, wq_ref[...])
    for hd in range(XATTN_HEADS):
        lo = hd * XATTN_HEAD_DIM
        qh = q[:, lo:lo + XATTN_HEAD_DIM].astype(BF16)
        kh = kv_ref[0, :, lo:lo + XATTN_HEAD_DIM]
        vh = kv_ref[0, :, D_MODEL + lo:D_MODEL + lo + XATTN_HEAD_DIM]
        s = _dot_nt(qh, kh)
        p = jnp.exp(s - jnp.max(s, axis=-1, keepdims=True))
        inv = 1.0 / jnp.sum(p, axis=-1, keepdims=True)
        o_scr[:, lo:lo + XATTN_HEAD_DIM] = (_dot(p.astype(BF16), vh) * inv).astype(BF16)
    x2 = x + _dot(o_scr[...], wo_ref[...])
    for s in range(ROW_TILES):
        x2r_ref[pl.ds(s, tq, stride=ROW_TILES), :] = x2[:, s * LANES:(s + 1) * LANES]
    logits = _router_logits(_rms(x2, gf_ref[...]), wr_ref, br_ref)
    info, running = _route(logits, cnt_scr[...], tri_ref[...])
    cnt_scr[...] = running
    counts_ref[...] = running
    brk_ref[...] = info.T[0:8, :].astype(jnp.int32)


def _attention(x1, kv, xattn_norm, w_q, w_o, ffn_norm, w_router, b_router):
    bsz, seq, d = x1.shape
    mlen = kv.shape[1]
    tq = ATTN_TILE
    nj = seq // tq
    const = lambda b, j: (0, 0)
    tile = lambda b, j: (b, j, 0)
    tri = jnp.tri(RANK_BLOCK, RANK_BLOCK, -1, dtype=BF16)
    return pl.pallas_call(
        _attn_kernel,
        grid=(bsz, nj),
        in_specs=[
            pl.BlockSpec((1, tq, d), tile),
            pl.BlockSpec((1, mlen, 2 * d), lambda b, j: (b, 0, 0)),
            pl.BlockSpec((1, d), const),
            pl.BlockSpec((d, d), const),
            pl.BlockSpec((d, d), const),
            pl.BlockSpec((1, d), const),
            pl.BlockSpec((d, 2 * LANES), const),
            pl.BlockSpec((1, LANES), const),
            pl.BlockSpec((RANK_BLOCK, RANK_BLOCK), const),
        ],
        out_specs=[
            pl.BlockSpec((tq * ROW_TILES, LANES), lambda b, j: (b * nj + j, 0)),
            pl.BlockSpec((8, tq), lambda b, j: (0, b * nj + j)),
            pl.BlockSpec((1, LANES), const),
        ],
        out_shape=[
            jax.ShapeDtypeStruct((bsz * seq * ROW_TILES, LANES), F32),
            jax.ShapeDtypeStruct((8, bsz * seq), jnp.int32),
            jax.ShapeDtypeStruct((1, LANES), F32),
        ],
        scratch_shapes=[pltpu.VMEM((tq, d), BF16), pltpu.VMEM((1, LANES), F32)],
        compiler_params=pltpu.CompilerParams(
            dimension_semantics=("arbitrary", "arbitrary"), vmem_limit_bytes=VMEM_LIMIT),
        name="xattn_router",
    )(x1, kv, xattn_norm, w_q, w_o, ffn_norm, w_router, b_router, tri)


def _item_copy(hbm, hbm_item, buf, buf_item, rows_per_item, sem, to_hbm):
    h = hbm.at[pl.ds(pl.multiple_of(hbm_item * rows_per_item, rows_per_item), rows_per_item)]
    b = buf.at[pl.ds(pl.multiple_of(buf_item * rows_per_item, rows_per_item), rows_per_item)]
    return pltpu.make_async_copy(b, h, sem) if to_hbm else pltpu.make_async_copy(h, b, sem)


def _items_wait(hbm, buf, buf_item, n_items, rows_per_item, sem, to_hbm):
    n = n_items * rows_per_item
    h = hbm.at[pl.ds(0, n)]
    b = buf.at[pl.ds(pl.multiple_of(buf_item * rows_per_item, rows_per_item), n)]
    (pltpu.make_async_copy(b, h, sem) if to_hbm else pltpu.make_async_copy(h, b, sem)).wait()


def _dispatch_kernel(pos_ref, fill_ref, ntiles_ref, x2r_ref, xs_hbm, buf, sem):
    i = pl.program_id(0)
    nt = pl.num_programs(0)
    td = DISPATCH_TILE
    rpi = ROW_TILES
    slot = i % 2
    base = slot * td

    @pl.when(i >= 2)
    def _():
        _items_wait(xs_hbm, buf, base, td, rpi, sem.at[slot], True)

    buf[pl.ds(pl.multiple_of(base * rpi, td * rpi), td * rpi), :] = x2r_ref[...]

    def send(r2, carry):
        for k in range(DMA_QUEUES):
            r = r2 * DMA_QUEUES + k
            _item_copy(xs_hbm, pos_ref[i * td + r], buf, base + r, rpi, sem.at[slot],
                       True).start(priority=k)
        return carry

    lax.fori_loop(0, td // DMA_QUEUES, send, 0, unroll=4)

    @pl.when(i == nt - 1)
    def _():
        _items_wait(xs_hbm, buf, base, td, rpi, sem.at[slot], True)
        _items_wait(xs_hbm, buf, (1 - slot) * td, td, rpi, sem.at[1 - slot], True)
        buf[...] = jnp.zeros_like(buf)
        zsem = sem.at[2]
        bits = range(MOE_TILE.bit_length() - 2, -1, -1)

        def pad_copy(b, k):
            first = fill_ref[b]
            n = fill_ref[N_BUCKETS + b] - first
            done = (n >> (k + 1)) << (k + 1)
            rows = (1 << k) * rpi
            cp = pltpu.make_async_copy(
                buf.at[pl.ds(0, rows)],
                xs_hbm.at[pl.ds(pl.multiple_of((first + done) * rpi, rpi), rows)], zsem)
            return ((n >> k) & 1) == 1, cp

        for wait in (False, True):
            for b in range(N_BUCKETS):
                for k in bits:
                    present, cp = pad_copy(b, k)

                    @pl.when(present)
                    def _():
                        cp.wait() if wait else cp.start()

        n_tiles_total = xs_hbm.shape[0] // (MOE_TILE * rpi)

        def tile_copy(tile):
            return pltpu.make_async_copy(
                buf.at[pl.ds(0, MOE_TILE * rpi)],
                xs_hbm.at[pl.ds(pl.multiple_of(tile * MOE_TILE * rpi, MOE_TILE * rpi),
                                MOE_TILE * rpi)], zsem)

        def fill_tile(tile, carry):
            tile_copy(tile).start()
            return carry

        def drain_tile(tile, carry):
            tile_copy(tile).wait()
            return carry

        lax.fori_loop(ntiles_ref[0], n_tiles_total, fill_tile, 0)
        lax.fori_loop(ntiles_ref[0], n_tiles_total, drain_tile, 0)


def _dispatch(pos, fill, ntiles, x2r, n_sorted):
    t = x2r.shape[0] // ROW_TILES
    td = DISPATCH_TILE
    grid_spec = pltpu.PrefetchScalarGridSpec(
        num_scalar_prefetch=3,
        grid=(t // td,),
        in_specs=[pl.BlockSpec((td * ROW_TILES, LANES), lambda i, *_: (i, 0))],
        out_specs=pl.BlockSpec(memory_space=pl.ANY),
        scratch_shapes=[
            pltpu.VMEM((2 * td * ROW_TILES, LANES), F32),
            pltpu.SemaphoreType.DMA((3,)),
        ],
    )
    return pl.pallas_call(
        _dispatch_kernel,
        grid_spec=grid_spec,
        out_shape=jax.ShapeDtypeStruct((n_sorted * ROW_TILES, LANES), F32),
        compiler_params=pltpu.CompilerParams(
            dimension_semantics=("arbitrary",), vmem_limit_bytes=VMEM_LIMIT),
        name="moe_dispatch",
    )(pos, fill, ntiles, x2r)


def _silu(x):
    return x * _sigmoid(x)


def _moe_kernel(elo_ref, ehi_ref, used_ref, xtile_ref,
                xs_ref, gf_ref, wr_ref, br_ref, gfin_ref,
                wg_lo, wu_lo, wd_lo, wg_hi, wu_hi, wd_hi, y_ref):
    del xtile_ref
    i = pl.program_id(0)
    tm = MOE_TILE

    @pl.when(used_ref[i] == 1)
    def _():
        x2 = jnp.concatenate(
            [xs_ref[pl.ds(s, tm, stride=ROW_TILES), :] for s in range(ROW_TILES)], axis=1)
        h3 = _rms(x2, gf_ref[...])
        g_lo, g_hi = _gates(_router_logits(h3, wr_ref, br_ref), elo_ref[i], ehi_ref[i])
        x = h3.astype(BF16)

        def expert(wg, wu, wd):
            hid = (_silu(_dot(x, wg[0])) * _dot(x, wu[0])).astype(BF16)
            return _dot(hid, wd[0])

        moe = g_lo * expert(wg_lo, wu_lo, wd_lo) + g_hi * expert(wg_hi, wu_hi, wd_hi)
        out = _rms(x2 + moe, gfin_ref[...])
        for s in range(ROW_TILES):
            y_ref[pl.ds(s, tm, stride=ROW_TILES), :] = out[:, s * LANES:(s + 1) * LANES]

    @pl.when(used_ref[i] == 0)
    def _():
        y_ref[...] = jnp.zeros_like(y_ref)


def _moe_sparse(e_lo, e_hi, used, xs, ffn_norm, w_router, b_router, final_norm,
                w_gate, w_up, w_down):
    tm = MOE_TILE
    nt = used.shape[0]
    d = D_MODEL
    lo = lambda i, elo, ehi, used, xt: (elo[i], 0, 0)
    hi = lambda i, elo, ehi, used, xt: (ehi[i], 0, 0)
    const = lambda i, *_: (0, 0)
    steps = jnp.arange(nt, dtype=jnp.int32)
    xs_tile = jnp.where(used == 1, steps, jnp.maximum(jnp.sum(used) - 1, 0))
    grid_spec = pltpu.PrefetchScalarGridSpec(
        num_scalar_prefetch=4,
        grid=(nt,),
        in_specs=[
            pl.BlockSpec((tm * ROW_TILES, LANES), lambda i, elo, ehi, used, xt: (xt[i], 0)),
            pl.BlockSpec((1, d), const),
            pl.BlockSpec((d, 2 * LANES), const),
            pl.BlockSpec((1, LANES), const),
            pl.BlockSpec((1, d), const),
            pl.BlockSpec((1, d, D_EXPERT), lo),
            pl.BlockSpec((1, d, D_EXPERT), lo),
            pl.BlockSpec((1, D_EXPERT, d), lo),
            pl.BlockSpec((1, d, D_EXPERT), hi),
            pl.BlockSpec((1, d, D_EXPERT), hi),
            pl.BlockSpec((1, D_EXPERT, d), hi),
        ],
        out_specs=pl.BlockSpec((tm * ROW_TILES, LANES), lambda i, *_: (i, 0)),
    )
    return pl.pallas_call(
        _moe_kernel,
        grid_spec=grid_spec,
        out_shape=jax.ShapeDtypeStruct((nt * tm * ROW_TILES, LANES), F32),
        compiler_params=pltpu.CompilerParams(
            dimension_semantics=("arbitrary",), vmem_limit_bytes=VMEM_LIMIT),
        name="moe_sparse",
    )(e_lo, e_hi, used, xs_tile, xs, ffn_norm, w_router, b_router, final_norm,
      w_gate, w_up, w_down, w_gate, w_up, w_down)


def _final_kernel(pos_ref, y_hbm, o_ref, ybuf, sem):
    i = pl.program_id(0)
    nt = pl.num_programs(0)
    tf = FINAL_TILE
    slot = i % FINAL_SLOTS

    def fetch(tile, tslot, r8, k):
        r = r8 * SUBLANES + k
        _item_copy(y_hbm, pos_ref[tile * tf + r], ybuf, tslot * tf + r, ROW_TILES,
                   sem.at[tslot], False).start(priority=k % DMA_QUEUES)

    def start(tile, tslot):
        def body(r8, carry):
            for k in range(SUBLANES):
                fetch(tile, tslot, r8, k)
            return carry

        lax.fori_loop(0, tf // SUBLANES, body, 0)

    def relayout(r8):
        src = pl.multiple_of((slot * tf + r8 * SUBLANES) * ROW_TILES, SUBLANES * ROW_TILES)
        dst = pl.ds(pl.multiple_of(r8 * SUBLANES, SUBLANES), SUBLANES)
        for s in range(ROW_TILES):
            o_ref[dst, s * LANES:(s + 1) * LANES] = \
                ybuf[pl.ds(src + s, SUBLANES, stride=ROW_TILES), :]

    @pl.when(i == 0)
    def _():
        start(0, 0)
        if FINAL_SLOTS > 2:
            @pl.when(nt > 1)
            def _():
                start(1, 1)

    _items_wait(y_hbm, ybuf, slot * tf, tf, ROW_TILES, sem.at[slot], False)

    @pl.when(i + 2 < nt)
    def _():
        nslot = (i + 2) % FINAL_SLOTS

        def body(r8, carry):
            relayout(r8)
            for k in range(SUBLANES):
                fetch(i + 2, nslot, r8, k)
            return carry

        lax.fori_loop(0, tf // SUBLANES, body, 0)

    @pl.when(i + 2 >= nt)
    def _():
        def body(r8, carry):
            relayout(r8)
            return carry

        lax.fori_loop(0, tf // SUBLANES, body, 0, unroll=8)


def _final(pos, y):
    t = pos.shape[0]
    d = D_MODEL
    tf = FINAL_TILE
    grid_spec = pltpu.PrefetchScalarGridSpec(
        num_scalar_prefetch=1,
        grid=(t // tf,),
        in_specs=[pl.BlockSpec(memory_space=pl.ANY)],
        out_specs=pl.BlockSpec((tf, d), lambda i, *_: (i, 0)),
        scratch_shapes=[
            pltpu.VMEM((FINAL_SLOTS * tf * ROW_TILES, LANES), F32),
            pltpu.SemaphoreType.DMA((FINAL_SLOTS,)),
        ],
    )
    return pl.pallas_call(
        _final_kernel,
        grid_spec=grid_spec,
        out_shape=jax.ShapeDtypeStruct((t, d), F32),
        compiler_params=pltpu.CompilerParams(
            dimension_semantics=("arbitrary",), vmem_limit_bytes=VMEM_LIMIT),
        name="moe_unpermute",
    )(pos, y)


def _positions_kernel(starts_ref, brk_ref, pos_ref):
    bucket = brk_ref[0:1, :]
    pos = brk_ref[1:2, :]
    for b in range(N_BUCKETS):
        pos = pos + jnp.where(bucket == b, starts_ref[b], 0)
    pos_ref[...] = pos


def _positions(starts, brk):
    t = brk.shape[1]
    grid_spec = pltpu.PrefetchScalarGridSpec(
        num_scalar_prefetch=1,
        grid=(1,),
        in_specs=[pl.BlockSpec(brk.shape, lambda i, starts: (0, 0))],
        out_specs=pl.BlockSpec((1, t), lambda i, starts: (0, 0)),
    )
    return pl.pallas_call(
        _positions_kernel,
        grid_spec=grid_spec,
        out_shape=jax.ShapeDtypeStruct((1, t), jnp.int32),
        name="moe_positions",
    )(starts, brk).reshape(t)


def _routing_tables(counts, brk):
    tm = MOE_TILE
    t = brk.shape[1]
    nt = t // tm + N_BUCKETS
    cnt = counts[0, :N_BUCKETS].astype(jnp.int32)
    padded = ((cnt + tm - 1) // tm) * tm
    ends = jnp.cumsum(padded)
    starts = ends - padded
    pos = _positions(starts, brk)
    fill = jnp.concatenate([starts + cnt, ends])
    ntiles = ends[-1:] // tm
    tile_start = jnp.arange(nt, dtype=jnp.int32) * tm
    tile_bucket = jnp.sum((ends[None, :] <= tile_start[:, None]).astype(jnp.int32), axis=1)
    used = (tile_bucket < N_BUCKETS).astype(jnp.int32)
    tile_bucket = jnp.minimum(tile_bucket, N_BUCKETS - 1)
    pair = tile_bucket % N_PAIRS
    group0 = (tile_bucket // N_PAIRS) * EXPERTS_PER_GROUP
    e_lo = group0 + jnp.array([0, 0, 0, 1, 1, 2], jnp.int32)[pair]
    e_hi = group0 + jnp.array([1, 2, 3, 2, 3, 3], jnp.int32)[pair]
    return pos, fill, ntiles, e_lo, e_hi, used, nt


def kernel(x, mem, mix_norm, w_in, conv_w, hgrn_lb, hgrn_norm, w_out, xattn_norm, mem_norm,
           w_q, w_kv, w_o, ffn_norm, w_group, b_group, w_expert, b_expert, w_gate, w_up,
           w_down, final_norm):
    bsz, seq, d = x.shape
    assert d == D_MODEL and seq % SEQ_TILE == 0 and seq % ATTN_TILE == 0
    assert (bsz * seq) % MOE_TILE == 0 and (bsz * seq) % FINAL_TILE == 0
    assert (bsz * seq) % DISPATCH_TILE == 0 and bsz * seq >= 2 * DISPATCH_TILE
    assert 2 * DISPATCH_TILE >= MOE_TILE and mix_norm.shape[0] == 1
    bf = lambda w: w.astype(BF16)

    x1 = _mixer(x, mix_norm, bf(w_in[0]), conv_w[0], hgrn_lb, hgrn_norm, bf(w_out[0]))
    kv = _kv_proj(mem, mem_norm, bf(w_kv[0]))

    pad = LANES - N_GROUPS - N_EXPERTS
    w_router = jnp.concatenate(
        [w_group[0], w_expert[0], jnp.zeros((d, pad), F32)], axis=1)
    b_router = jnp.concatenate(
        [b_group[0], b_expert[0], jnp.zeros((pad,), F32)])[None, :]
    w_router_hi = bf(w_router)
    w_router_lo = bf(w_router - w_router_hi.astype(F32))
    w_router2 = jnp.concatenate([w_router_hi, w_router_lo], axis=1)
    w_q_scaled = bf(w_q[0] * (XATTN_HEAD_DIM ** -0.5))
    x2r, brk, counts = _attention(x1, kv, xattn_norm, w_q_scaled, bf(w_o[0]), ffn_norm,
                                  w_router2, b_router)

    pos, fill, ntiles, e_lo, e_hi, used, nt = _routing_tables(counts, brk)
    xs = _dispatch(pos, fill, ntiles, x2r, nt * MOE_TILE)
    y = _moe_sparse(e_lo, e_hi, used, xs, ffn_norm, w_router2, b_router, final_norm[None, :],
                    bf(w_gate[0]), bf(w_up[0]), bf(w_down[0]))
    return _final(pos, y).reshape(bsz, seq, d)
```
